```python
import jax
import jax.numpy as jnp
from jax import lax
import numpy as np

D_MODEL = 2048
BATCH = 2
SEQ = 4096
DEPTH = 1

HEAD_DIM = 128
HA = 8
HA_KV = 2
H_IDX = 16
D_IDX = 64
TOPK_MAX = 256
HB = 8
ROPE_THETA = 500000.0
ROT_A = HEAD_DIM // 4
ROT_IDX = D_IDX // 4
Q_BLOCK = 128
D_FF = -(-8 * D_MODEL // (3 * 256)) * 256
D_PLE = 256
EPS = 1e-6
NEG_INF = float('-inf')

SPLIT_SIZES = (
    HA * HEAD_DIM,
    HA_KV * HEAD_DIM,
    HA_KV * HEAD_DIM,
    H_IDX * D_IDX,
    D_IDX,
    H_IDX,
    HB * HEAD_DIM,
    HB * HEAD_DIM,
    HB * HEAD_DIM,
    HB,
    D_MODEL,
    D_MODEL,
)
N_IN = sum(SPLIT_SIZES)

kernel_name = 'hybrid_dsa_fox_gated_block'


def rmsnorm(x, g):
    xf = x.astype(jnp.float32)
    xf = xf * lax.rsqrt(jnp.mean(xf * xf, axis=-1, keepdims=True) + EPS)
    return (xf * g.astype(jnp.float32)).astype(x.dtype)


def partial_rope(x, pos, rot):
    half = rot // 2
    inv_freq = ROPE_THETA ** (-jnp.arange(half, dtype=jnp.float32) / half)
    ang = pos.astype(jnp.float32)[:, :, None] * inv_freq
    cos = jnp.cos(ang)[:, :, None, :]
    sin = jnp.sin(ang)[:, :, None, :]
    x1 = x[..., :half].astype(jnp.float32)
    x2 = x[..., half:rot].astype(jnp.float32)
    rotated = jnp.concatenate([x1 * cos - x2 * sin, x2 * cos + x1 * sin], axis=-1).astype(x.dtype)
    return jnp.concatenate([rotated, x[..., rot:]], axis=-1)


def to_blocks(a):
    b, t = a.shape[:2]
    return jnp.moveaxis(a.reshape(b, t // Q_BLOCK, Q_BLOCK, *a.shape[2:]), 1, 0)


def from_blocks(a):
    a = jnp.moveaxis(a, 0, 1)
    return a.reshape(a.shape[0], a.shape[1] * a.shape[2], *a.shape[3:])


def dsa_sparse_attention(q, k, v, q_idx, k_idx, w_idx, top_k):
    t_len = q.shape[1]
    group = HA // HA_KV
    scale = HEAD_DIM ** -0.5
    key_pos = jnp.arange(t_len)
    t_blocks = key_pos.reshape(t_len // Q_BLOCK, Q_BLOCK)
    gather = jax.vmap(lambda table, idx: table[idx])

    def block(args):
        qb, qib, wib, tb = args
        nb = qb.shape[0]
        dots = jnp.einsum('bqhd,bsd->bqhs', qib, k_idx).astype(jnp.float32)
        score = jnp.einsum('bqh,bqhs->bqs', wib.astype(jnp.float32), jax.nn.relu(dots))
        causal = key_pos[None, :] <= tb[:, None]
        score = jnp.where(causal[None], score, NEG_INF)
        _, sel = lax.top_k(score, top_k)
        valid = sel <= tb[None, :, None]
        k_sel = gather(k, sel)
        v_sel = gather(v, sel)
        qg = qb.reshape(nb, Q_BLOCK, HA_KV, group, HEAD_DIM)
        logits = jnp.einsum('bqngd,bqsnd->bqngs', qg, k_sel).astype(jnp.float32) * scale
        logits = jnp.where(valid[:, :, None, None, :], logits, NEG_INF)
        prob = jax.nn.softmax(logits, axis=-1).astype(v.dtype)
        out = jnp.einsum('bqngs,bqsnd->bqngd', prob, v_sel)
        return out.reshape(nb, Q_BLOCK, HA * HEAD_DIM)

    out = lax.map(block, (to_blocks(q), to_blocks(q_idx), to_blocks(w_idx), t_blocks))
    return from_blocks(out)


def forgetting_attention(q, k, v, log_fcum):
    t_len = q.shape[1]
    scale = HEAD_DIM ** -0.5
    key_pos = jnp.arange(t_len)
    t_blocks = key_pos.reshape(t_len // Q_BLOCK, Q_BLOCK)
    c_keys = jnp.swapaxes(log_fcum, 1, 2)

    def block(args):
        qb, cb, tb = args
        nb = qb.shape[0]
        logits = jnp.einsum('bqhd,bshd->bhqs', qb, k).astype(jnp.float32) * scale
        logits = logits + jnp.swapaxes(cb, 1, 2)[..., None] - c_keys[:, :, None, :]
        causal = key_pos[None, :] <= tb[:, None]
        logits = jnp.where(causal[None, None], logits, NEG_INF)
        prob = jax.nn.softmax(logits, axis=-1).astype(v.dtype)
        out = jnp.einsum('bhqs,bshd->bqhd', prob, v)
        return out.reshape(nb, Q_BLOCK, HB * HEAD_DIM)

    out = lax.map(block, (to_blocks(q), to_blocks(log_fcum), t_blocks))
    return from_blocks(out)


def setup_inputs(seed: int = 0) -> dict:
    key = jax.random.key(seed)
    ks = jax.random.split(key, 17)
    f32 = jnp.float32

    def normal(k, shape, fan_in):
        return jax.random.normal(k, shape, f32) * fan_in ** -0.5

    def gain(k, shape):
        return 1.0 + 0.05 * jax.random.normal(k, shape, f32)

    wa = HA * HEAD_DIM
    wb = HB * HEAD_DIM
    return {
        'x': jax.random.normal(ks[0], (BATCH, SEQ, D_MODEL), f32),
        'p': jax.random.normal(ks[1], (DEPTH, BATCH, SEQ, D_PLE), f32),
        'positions': jnp.broadcast_to(jnp.arange(SEQ, dtype=jnp.int32), (BATCH, SEQ)),
        'g_mix': gain(ks[2], (DEPTH, D_MODEL)),
        'w_in': normal(ks[3], (DEPTH, D_MODEL, N_IN), D_MODEL),
        'b_f': 1.0 + 0.5 * jax.random.normal(ks[4], (DEPTH, HB), f32),
        'w_o_a': normal(ks[5], (DEPTH, wa, D_MODEL), wa),
        'w_o_b': normal(ks[6], (DEPTH, wb, D_MODEL), wb),
        'w_out': normal(ks[7], (DEPTH, D_MODEL, D_MODEL), D_MODEL),
        'g_ffn': gain(ks[8], (DEPTH, D_MODEL)),
        'w_ffn_gate': normal(ks[9], (DEPTH, D_MODEL, D_FF), D_MODEL),
        'w_ffn_up': normal(ks[10], (DEPTH, D_MODEL, D_FF), D_MODEL),
        'w_ffn_down': normal(ks[11], (DEPTH, D_FF, D_MODEL), D_FF),
        'g_ple': gain(ks[12], (DEPTH, D_MODEL)),
        'w_ple_gate': normal(ks[13], (DEPTH, D_MODEL, D_MODEL), D_MODEL),
        'w_ple_proj': normal(ks[14], (DEPTH, D_PLE, D_MODEL), D_PLE),
        'g_final': gain(ks[15], (D_MODEL,)),
    }


def reference(x, p, positions, g_mix, w_in, b_f, w_o_a, w_o_b, w_out, g_ffn,
              w_ffn_gate, w_ffn_up, w_ffn_down, g_ple, w_ple_gate, w_ple_proj, g_final):
    b, t_len, _ = x.shape
    top_k = min(TOPK_MAX, t_len // 4)
    split_at = [int(s) for s in np.cumsum(SPLIT_SIZES)[:-1]]
    for i in range(DEPTH):
        h = rmsnorm(x, g_mix[i])
        proj = h @ w_in[i]
        (q_a, k_a, v_a, q_i, k_i, w_i, q_b, k_b, v_b, f_b, gate_a, gate_b) = jnp.split(proj, split_at, axis=-1)

        q_a = partial_rope(q_a.reshape(b, t_len, HA, HEAD_DIM), positions, ROT_A)
        k_a = partial_rope(k_a.reshape(b, t_len, HA_KV, HEAD_DIM), positions, ROT_A)
        v_a = v_a.reshape(b, t_len, HA_KV, HEAD_DIM)
        q_i = partial_rope(q_i.reshape(b, t_len, H_IDX, D_IDX), positions, ROT_IDX)
        k_i = partial_rope(k_i[:, :, None, :], positions, ROT_IDX)[:, :, 0, :]
        w_i = w_i * (H_IDX ** -0.5 * D_IDX ** -0.5)
        out_a = dsa_sparse_attention(q_a, k_a, v_a, q_i, k_i, w_i, top_k)

        log_f = jax.nn.log_sigmoid(f_b.astype(jnp.float32) + b_f[i].astype(jnp.float32))
        log_fcum = jnp.cumsum(log_f, axis=1)
        out_b = forgetting_attention(q_b.reshape(b, t_len, HB, HEAD_DIM),
                                     k_b.reshape(b, t_len, HB, HEAD_DIM),
                                     v_b.reshape(b, t_len, HB, HEAD_DIM), log_fcum)

        y_a = out_a @ w_o_a[i]
        y_b = out_b @ w_o_b[i]
        mixed = jax.nn.sigmoid(gate_a) * y_a + jax.nn.sigmoid(gate_b) * y_b
        x = x + mixed @ w_out[i]

        h = rmsnorm(x, g_ffn[i])
        x = x + (jax.nn.silu(h @ w_ffn_gate[i]) * (h @ w_ffn_up[i])) @ w_ffn_down[i]

        h = rmsnorm(x, g_ple[i])
        x = x + jax.nn.sigmoid(h @ w_ple_gate[i]) * (p[i] @ w_ple_proj[i])
    return rmsnorm(x, g_final)
```

```python
import functools

import jax
import jax.numpy as jnp
import numpy as np
from jax import lax
from jax.experimental import pallas as pl
from jax.experimental.pallas import tpu as pltpu

F32 = jnp.float32
BF16 = jnp.bfloat16

D_MODEL = 2048
HEAD_DIM = 128
HA = 8
HA_KV = 2
H_IDX = 16
D_IDX = 64
TOPK_MAX = 256
HB = 8
ROPE_THETA = 500000.0
ROT_A = HEAD_DIM // 4
ROT_IDX = D_IDX // 4
Q_BLOCK = 128
D_FF = -(-8 * D_MODEL // (3 * 256)) * 256
D_PLE = 256
EPS = 1e-6

SPLIT_SIZES = (HA * HEAD_DIM, HA_KV * HEAD_DIM, HA_KV * HEAD_DIM, H_IDX * D_IDX, D_IDX, H_IDX,
               HB * HEAD_DIM, HB * HEAD_DIM, HB * HEAD_DIM, HB, D_MODEL, D_MODEL)

OFF_GA = 0
OFF_GB = OFF_GA + D_MODEL
OFF_QA = OFF_GB + D_MODEL
OFF_QI = OFF_QA + HA * HEAD_DIM
OFF_KA = OFF_QI + H_IDX * D_IDX
OFF_VA = OFF_KA + HA_KV * HEAD_DIM
OFF_QB = OFF_VA + HA_KV * HEAD_DIM
OFF_KB = OFF_QB + HB * HEAD_DIM
OFF_VB = OFF_KB + HB * HEAD_DIM
N_MAIN = OFF_VB + HB * HEAD_DIM
LANE_KI = 0
LANE_WI = D_IDX
LANE_FB = D_IDX + H_IDX

TN_IN = 512
NEG = -1e30
VMEM_LIMIT = 56 * 1024 * 1024
ATTN_SCALE = HEAD_DIM ** -0.5


def _cparams(sem):
    return pltpu.CompilerParams(dimension_semantics=sem, vmem_limit_bytes=VMEM_LIMIT)


def _rms(x, g):
    return x * lax.rsqrt(jnp.mean(x * x, axis=-1, keepdims=True) + EPS) * g


def _resident(shape):
    nd = len(shape)
    return pl.BlockSpec(shape, lambda *_: (0,) * nd, pipeline_mode=pl.Buffered(1))


def _rope128(a, c, sa, sb, shift):
    outs = []
    for q in range(a.shape[1] // 128):
        aq = a[:, q * 128:(q + 1) * 128]
        outs.append(aq * c + pltpu.roll(aq, 128 - shift, 1) * sa + pltpu.roll(aq, shift, 1) * sb)
    return outs[0] if len(outs) == 1 else jnp.concatenate(outs, axis=1)


def _inproj_kernel(x_ref, pos_ref, g_ref, w_ref, ws_ref, bf_ref, inva_ref, invi_ref,
                   o_ref, kie_ref, kio_ref, wlf_ref,
                   h_scr, ca_scr, saa_scr, sab_scr, ci_scr, sia_scr, sib_scr):
    j = pl.program_id(1)
    tm = x_ref.shape[0]

    @pl.when(j == 0)
    def _():
        hb = _rms(x_ref[...], g_ref[...]).astype(BF16)
        h_scr[...] = hb
        pos = pos_ref[...].astype(F32)
        lane = lax.broadcasted_iota(jnp.int32, (tm, 128), 1)
        ang = pos * inva_ref[...]
        sn = jnp.sin(ang)
        ca_scr[...] = jnp.cos(ang)
        saa_scr[...] = jnp.where(lane < ROT_A // 2, -sn, 0.0)
        sab_scr[...] = jnp.where(lane >= ROT_A // 2, sn, 0.0)
        ang = pos * invi_ref[...]
        sn = jnp.sin(ang)
        lane_i = lane & (D_IDX - 1)
        ci = jnp.cos(ang)
        sia = jnp.where(lane_i < ROT_IDX // 2, -sn, 0.0)
        sib = jnp.where(lane_i >= ROT_IDX // 2, sn, 0.0)
        ci_scr[...] = ci
        sia_scr[...] = sia
        sib_scr[...] = sib
        small = jnp.dot(hb, ws_ref[...], preferred_element_type=F32)
        kr = _rope128(small, ci, sia, sib, ROT_IDX // 2)
        ke = jnp.where(lane < D_IDX, kr, 0.0)
        kie_ref[...] = ke.astype(BF16)
        kio_ref[...] = pltpu.roll(ke, D_IDX, 1).astype(BF16)
        is_w = (lane >= LANE_WI) & (lane < LANE_WI + H_IDX)
        is_f = (lane >= LANE_FB) & (lane < LANE_FB + HB)
        logf = jax.nn.log_sigmoid(small + bf_ref[...])
        wlf_ref[...] = jnp.where(is_w, small * (H_IDX ** -0.5 * D_IDX ** -0.5), jnp.where(is_f, logf, 0.0))

    acc = jnp.dot(h_scr[...], w_ref[...], preferred_element_type=F32)
    t = TN_IN
    j_gate_end = OFF_QA // t
    j_qa_end = OFF_QI // t
    j_qi_end = OFF_KA // t
    j_kv = OFF_KA // t
    j_qb = OFF_QB // t
    j_qb_end = OFF_KB // t

    @pl.when(j < j_gate_end)
    def _():
        o_ref[...] = jax.nn.sigmoid(acc).astype(BF16)

    @pl.when((j >= j_gate_end) & (j < j_qa_end))
    def _():
        y = _rope128(acc, ca_scr[...], saa_scr[...], sab_scr[...], ROT_A // 2)
        o_ref[...] = (y * ATTN_SCALE).astype(BF16)

    @pl.when((j >= j_qa_end) & (j < j_qi_end))
    def _():
        o_ref[...] = _rope128(acc, ci_scr[...], sia_scr[...], sib_scr[...], ROT_IDX // 2).astype(BF16)

    @pl.when(j == j_kv)
    def _():
        nk = HA_KV * HEAD_DIM
        y = _rope128(acc[:, :nk], ca_scr[...], saa_scr[...], sab_scr[...], ROT_A // 2)
        o_ref[...] = jnp.concatenate([y, acc[:, nk:]], axis=1).astype(BF16)

    @pl.when((j >= j_qb) & (j < j_qb_end))
    def _():
        o_ref[...] = (acc * ATTN_SCALE).astype(BF16)

    @pl.when(j >= j_qb_end)
    def _():
        o_ref[...] = acc.astype(BF16)


def _inproj(x2, pos2, g, w_main, w_small, bf_row, inv_a, inv_i, tm):
    m = x2.shape[0]
    assert OFF_VA - OFF_KA == HA_KV * HEAD_DIM and OFF_QB - OFF_KA == TN_IN
    grid = (m // tm, N_MAIN // TN_IN)
    row128 = pl.BlockSpec((1, 128), lambda i, j: (0, 0))
    return pl.pallas_call(
        _inproj_kernel,
        grid=grid,
        in_specs=[
            pl.BlockSpec((tm, D_MODEL), lambda i, j: (i, 0)),
            pl.BlockSpec((tm, 1), lambda i, j: (i, 0)),
            pl.BlockSpec((1, D_MODEL), lambda i, j: (0, 0)),
            pl.BlockSpec((D_MODEL, TN_IN), lambda i, j: (0, j)),
            pl.BlockSpec((D_MODEL, 128), lambda i, j: (0, 0)),
            row128, row128, row128,
        ],
        out_specs=[
            pl.BlockSpec((tm, TN_IN), lambda i, j: (i, j)),
            pl.BlockSpec((tm, 128), lambda i, j: (i, 0)),
            pl.BlockSpec((tm, 128), lambda i, j: (i, 0)),
            pl.BlockSpec((tm, 128), lambda i, j: (i, 0)),
        ],
        out_shape=[
            jax.ShapeDtypeStruct((m, N_MAIN), BF16),
            jax.ShapeDtypeStruct((m, 128), BF16),
            jax.ShapeDtypeStruct((m, 128), BF16),
            jax.ShapeDtypeStruct((m, 128), F32),
        ],
        scratch_shapes=[pltpu.VMEM((tm, D_MODEL), BF16)] + [pltpu.VMEM((tm, 128), F32)] * 6,
        compiler_params=_cparams(("arbitrary", "arbitrary")),
        name="inproj",
    )(x2, pos2, g, w_main, w_small, bf_row, inv_a, inv_i)


def _cumsum_kernel(wlf_ref, o_ref, carry_scr, *, tc):
    @pl.when(pl.program_id(1) == 0)
    def _():
        carry_scr[...] = jnp.zeros_like(carry_scr)

    lf = wlf_ref[...].T[LANE_FB:LANE_FB + 16, :]
    r = lax.broadcasted_iota(jnp.int32, (tc, tc), 0)
    c = lax.broadcasted_iota(jnp.int32, (tc, tc), 1)
    tri = jnp.where(r <= c, 1.0, 0.0).astype(BF16)
    hi = lf.astype(BF16)
    r1 = lf - hi.astype(F32)
    mid = r1.astype(BF16)
    lo = (r1 - mid.astype(F32)).astype(BF16)
    cs = (jnp.dot(hi, tri, preferred_element_type=F32) + jnp.dot(mid, tri, preferred_element_type=F32)
          + jnp.dot(lo, tri, preferred_element_type=F32)) + carry_scr[:, 0:1]
    o_ref[...] = cs[0:HB, :]
    carry_scr[...] = jnp.broadcast_to(cs[:, tc - 1:tc], carry_scr.shape)


def _cumsum(wlf, b, t_len):
    tc = 256
    nt = t_len // tc
    return pl.pallas_call(
        functools.partial(_cumsum_kernel, tc=tc),
        grid=(b, nt),
        in_specs=[pl.BlockSpec((tc, 128), lambda bi, ti: (bi * nt + ti, 0))],
        out_specs=pl.BlockSpec((None, HB, tc), lambda bi, ti: (bi, 0, ti)),
        out_shape=jax.ShapeDtypeStruct((b, HB, t_len), F32),
        scratch_shapes=[pltpu.VMEM((16, 128), F32)],
        compiler_params=_cparams(("arbitrary", "arbitrary")),
        name="fcumsum",
    )(wlf)


def _attn_a_kernel(qa_ref, qi_ref, ka_ref, va_ref, kie_ref, kio_ref, wlf_ref, o_ref,
                   sc_scr, m_scr, l_scr, acc_scr, *, tk, top_k, n_iter):
    i = pl.program_id(1)
    nq = Q_BLOCK
    group = HA // HA_KV
    nchunks = ((i + 1) * nq + tk - 1) // tk
    row_t = i * nq + lax.broadcasted_iota(jnp.int32, (nq, 1), 0)
    w = wlf_ref[...]

    def score_chunk(c, carry):
        rmin, rmax = carry
        off = pl.multiple_of(c * tk, tk)
        ke = kie_ref[pl.ds(off, tk), :]
        ko = kio_ref[pl.ds(off, tk), :]
        s = jnp.zeros((nq, tk), F32)
        for p in range(H_IDX // 2):
            qp = qi_ref[:, p * 128:(p + 1) * 128]
            de = lax.dot_general(qp, ke, (((1,), (1,)), ((), ())), preferred_element_type=F32)
            do = lax.dot_general(qp, ko, (((1,), (1,)), ((), ())), preferred_element_type=F32)
            s = s + jnp.maximum(de, 0.0) * w[:, LANE_WI + 2 * p:LANE_WI + 2 * p + 1]
            s = s + jnp.maximum(do, 0.0) * w[:, LANE_WI + 2 * p + 1:LANE_WI + 2 * p + 2]
        kpos = off + lax.broadcasted_iota(jnp.int32, (1, tk), 1)
        valid = kpos <= row_t
        sc_scr[:, pl.ds(off, tk)] = jnp.where(valid, s, NEG)
        rmin = jnp.minimum(rmin, jnp.min(jnp.where(valid, s, -NEG), axis=1, keepdims=True))
        rmax = jnp.maximum(rmax, jnp.max(jnp.where(valid, s, NEG), axis=1, keepdims=True))
        return rmin, rmax

    lo, hi = lax.fori_loop(0, nchunks, score_chunk,
                           (jnp.full((nq, 1), -NEG, F32), jnp.full((nq, 1), NEG, F32)))

    def bisect(_, carry):
        lo, hi = carry
        mid = 0.5 * (lo + hi)

        def count_chunk(c, part):
            off = pl.multiple_of(c * tk, tk)
            ind = jnp.where(sc_scr[:, pl.ds(off, tk)] >= mid, 1.0, 0.0)
            for q in range(tk // 128):
                part = part + ind[:, q * 128:(q + 1) * 128]
            return part

        part = lax.fori_loop(0, nchunks, count_chunk, jnp.zeros((nq, 128), F32))
        ge = jnp.sum(part, axis=1, keepdims=True) >= top_k
        return jnp.where(ge, mid, lo), jnp.where(ge, hi, mid)

    thr, _ = lax.fori_loop(0, n_iter, bisect, (lo, hi))

    for n in range(HA_KV):
        qg = jnp.concatenate(
            [qa_ref[:, (n * group + g) * HEAD_DIM:(n * group + g + 1) * HEAD_DIM] for g in range(group)], axis=0)
        m_scr[...] = jnp.full(m_scr.shape, NEG, F32)
        l_scr[...] = jnp.zeros(l_scr.shape, F32)
        acc_scr[...] = jnp.zeros(acc_scr.shape, F32)

        def att_chunk(c, carry):
            off = pl.multiple_of(c * tk, tk)
            k = ka_ref[pl.ds(off, tk), n * HEAD_DIM:(n + 1) * HEAD_DIM]
            v = va_ref[pl.ds(off, tk), n * HEAD_DIM:(n + 1) * HEAD_DIM]
            logits = lax.dot_general(qg, k, (((1,), (1,)), ((), ())), preferred_element_type=F32)
            sel = sc_scr[:, pl.ds(off, tk)] >= thr
            for g in range(group):
                rows = slice(g * nq, (g + 1) * nq)
                lg = jnp.where(sel, logits[rows], NEG)
                m_old = m_scr[rows]
                m_new = jnp.maximum(m_old, jnp.max(lg, axis=1, keepdims=True))
                alpha = jnp.exp(m_old - m_new)
                p = jnp.exp(lg - m_new)
                l_scr[rows] = alpha * l_scr[rows] + jnp.sum(p, axis=1, keepdims=True)
                m_scr[rows] = m_new
                acc_scr[rows] = alpha * acc_scr[rows] + jnp.dot(p.astype(BF16), v, preferred_element_type=F32)
            return carry

        lax.fori_loop(0, nchunks, att_chunk, 0)
        for g in range(group):
            rows = slice(g * nq, (g + 1) * nq)
            h = n * group + g
            o_ref[:, h * HEAD_DIM:(h + 1) * HEAD_DIM] = (acc_scr[rows] / l_scr[rows]).astype(BF16)


def _attn_a(proj, kie, kio, wlf, b, t_len, top_k):
    nb = t_len // Q_BLOCK
    wa = HA * HEAD_DIM
    wkv = HA_KV * HEAD_DIM
    tk = 256
    kern = functools.partial(_attn_a_kernel, tk=tk, top_k=top_k, n_iter=32)
    return pl.pallas_call(
        kern,
        grid=(b, nb),
        in_specs=[
            pl.BlockSpec((Q_BLOCK, wa), lambda bi, i: (bi * nb + i, OFF_QA // wa)),
            pl.BlockSpec((Q_BLOCK, H_IDX * D_IDX), lambda bi, i: (bi * nb + i, OFF_QI // (H_IDX * D_IDX))),
            pl.BlockSpec((t_len, wkv), lambda bi, i: (bi, OFF_KA // wkv)),
            pl.BlockSpec((t_len, wkv), lambda bi, i: (bi, OFF_VA // wkv)),
            pl.BlockSpec((t_len, 128), lambda bi, i: (bi, 0)),
            pl.BlockSpec((t_len, 128), lambda bi, i: (bi, 0)),
            pl.BlockSpec((Q_BLOCK, 128), lambda bi, i: (bi * nb + i, 0)),
        ],
        out_specs=pl.BlockSpec((Q_BLOCK, wa), lambda bi, i: (bi * nb + i, 0)),
        out_shape=jax.ShapeDtypeStruct((b * t_len, wa), BF16),
        scratch_shapes=[
            pltpu.VMEM((Q_BLOCK, t_len), F32),
            pltpu.VMEM((HA // HA_KV * Q_BLOCK, 1), F32),
            pltpu.VMEM((HA // HA_KV * Q_BLOCK, 1), F32),
            pltpu.VMEM((HA // HA_KV * Q_BLOCK, HEAD_DIM), F32),
        ],
        compiler_params=_cparams(("arbitrary", "arbitrary")),
        name="attn_a",
    )(proj, proj, proj, proj, kie, kio, wlf)


def _attn_b_kernel(q_ref, k_ref, v_ref, c_ref, o_ref, m_scr, l_scr, acc_scr, *, tq):
    qi = pl.program_id(2)
    q = q_ref[...]
    m_scr[...] = jnp.full(m_scr.shape, NEG, F32)
    l_scr[...] = jnp.zeros(l_scr.shape, F32)
    acc_scr[...] = jnp.zeros(acc_scr.shape, F32)
    c0 = c_ref[:, pl.ds(pl.multiple_of(qi * tq, tq), tq)][:, 0:1]

    def step(c, diagonal):
        off = pl.multiple_of(c * tq, tq)
        k = k_ref[pl.ds(off, tq), :]
        v = v_ref[pl.ds(off, tq), :]
        s = lax.dot_general(q, k, (((1,), (1,)), ((), ())), preferred_element_type=F32)
        s = s + (c0 - c_ref[:, pl.ds(off, tq)])
        if diagonal:
            row = lax.broadcasted_iota(jnp.int32, (tq, 1), 0)
            col = lax.broadcasted_iota(jnp.int32, (1, tq), 1)
            s = jnp.where(col <= row, s, NEG)
        m_old = m_scr[...]
        m_new = jnp.maximum(m_old, jnp.max(s, axis=1, keepdims=True))
        alpha = jnp.exp(m_old - m_new)
        p = jnp.exp(s - m_new)
        l_scr[...] = alpha * l_scr[...] + jnp.sum(p, axis=1, keepdims=True)
        m_scr[...] = m_new
        acc_scr[...] = alpha * acc_scr[...] + jnp.dot(p.astype(BF16), v, preferred_element_type=F32)

    def body(c, carry):
        step(c, False)
        return carry

    lax.fori_loop(0, qi, body, 0)
    step(qi, True)
    o_ref[...] = (acc_scr[...] / l_scr[...]).astype(BF16)


def _attn_b(proj, c4, b, t_len):
    tq = 512
    nq = t_len // tq
    return pl.pallas_call(
        functools.partial(_attn_b_kernel, tq=tq),
        grid=(b, HB, nq),
        in_specs=[
            pl.BlockSpec((tq, HEAD_DIM), lambda bi, h, i: (bi * nq + i, OFF_QB // HEAD_DIM + h)),
            pl.BlockSpec((t_len, HEAD_DIM), lambda bi, h, i: (bi, OFF_KB // HEAD_DIM + h)),
            pl.BlockSpec((t_len, HEAD_DIM), lambda bi, h, i: (bi, OFF_VB // HEAD_DIM + h)),
            pl.BlockSpec((None, None, 1, t_len), lambda bi, h, i: (bi, h, 0, 0)),
        ],
        out_specs=pl.BlockSpec((tq, HEAD_DIM), lambda bi, h, i: (bi * nq + i, h)),
        out_shape=jax.ShapeDtypeStruct((b * t_len, HB * HEAD_DIM), BF16),
        scratch_shapes=[
            pltpu.VMEM((tq, 1), F32),
            pltpu.VMEM((tq, 1), F32),
            pltpu.VMEM((tq, HEAD_DIM), F32),
        ],
        compiler_params=_cparams(("arbitrary", "arbitrary", "arbitrary")),
        name="attn_b",
    )(proj, proj, proj, c4)


def _merge_kernel(x_ref, oa_ref, ob_ref, ga_ref, gb_ref, woa_ref, wob_ref, wout_ref, o_ref):
    ya = jnp.dot(oa_ref[...], woa_ref[...], preferred_element_type=F32)
    yb = jnp.dot(ob_ref[...], wob_ref[...], preferred_element_type=F32)
    mixed = ga_ref[...].astype(F32) * ya + gb_ref[...].astype(F32) * yb
    o_ref[...] = x_ref[...] + jnp.dot(mixed.astype(BF16), wout_ref[...], preferred_element_type=F32)


def _merge(x2, out_a, out_b, proj, w_oa, w_ob, w_out, tm):
    m = x2.shape[0]
    wa = HA * HEAD_DIM
    wb = HB * HEAD_DIM
    return pl.pallas_call(
        _merge_kernel,
        grid=(m // tm,),
        in_specs=[
            pl.BlockSpec((tm, D_MODEL), lambda i: (i, 0)),
            pl.BlockSpec((tm, wa), lambda i: (i, 0)),
            pl.BlockSpec((tm, wb), lambda i: (i, 0)),
            pl.BlockSpec((tm, D_MODEL), lambda i: (i, OFF_GA // D_MODEL)),
            pl.BlockSpec((tm, D_MODEL), lambda i: (i, OFF_GB // D_MODEL)),
            _resident((wa, D_MODEL)),
            _resident((wb, D_MODEL)),
            _resident((D_MODEL, D_MODEL)),
        ],
        out_specs=pl.BlockSpec((tm, D_MODEL), lambda i: (i, 0)),
        out_shape=jax.ShapeDtypeStruct((m, D_MODEL), F32),
        compiler_params=_cparams(("arbitrary",)),
        name="merge",
    )(x2, out_a, out_b, proj, proj, w_oa, w_ob, w_out)


def _ffn_kernel(x_ref, g_ref, wg_ref, wu_ref, wd_ref, o_ref, h_scr, acc_scr):
    f = pl.program_id(1)

    @pl.when(f == 0)
    def _():
        h_scr[...] = _rms(x_ref[...], g_ref[...]).astype(BF16)
        acc_scr[...] = jnp.zeros(acc_scr.shape, F32)

    h = h_scr[...]
    gate = jnp.dot(h, wg_ref[...], preferred_element_type=F32)
    up = jnp.dot(h, wu_ref[...], preferred_element_type=F32)
    act = (jax.nn.silu(gate) * up).astype(BF16)
    acc_scr[...] += jnp.dot(act, wd_ref[...], preferred_element_type=F32)

    @pl.when(f == pl.num_programs(1) - 1)
    def _():
        o_ref[...] = x_ref[...] + acc_scr[...]


def _ffn(x2, g, w_gate, w_up, w_down, tm, tf):
    m = x2.shape[0]
    return pl.pallas_call(
        _ffn_kernel,
        grid=(m // tm, D_FF // tf),
        in_specs=[
            pl.BlockSpec((tm, D_MODEL), lambda i, f: (i, 0)),
            pl.BlockSpec((1, D_MODEL), lambda i, f: (0, 0)),
            pl.BlockSpec((D_MODEL, tf), lambda i, f: (0, f)),
            pl.BlockSpec((D_MODEL, tf), lambda i, f: (0, f)),
            pl.BlockSpec((tf, D_MODEL), lambda i, f: (f, 0)),
        ],
        out_specs=pl.BlockSpec((tm, D_MODEL), lambda i, f: (i, 0)),
        out_shape=jax.ShapeDtypeStruct((m, D_MODEL), F32),
        scratch_shapes=[pltpu.VMEM((tm, D_MODEL), BF16), pltpu.VMEM((tm, D_MODEL), F32)],
        compiler_params=_cparams(("arbitrary", "arbitrary")),
        name="ffn",
    )(x2, g, w_gate, w_up, w_down)


def _ple_kernel(x_ref, p_ref, g_ref, gf_ref, wg_ref, wp_ref, o_ref, *, final_norm):
    x = x_ref[...]
    h = _rms(x, g_ref[...]).astype(BF16)
    gate = jax.nn.sigmoid(jnp.dot(h, wg_ref[...], preferred_element_type=F32))
    emb = jnp.dot(p_ref[...].astype(BF16), wp_ref[...], preferred_element_type=F32)
    y = x + gate * emb
    o_ref[...] = _rms(y, gf_ref[...]) if final_norm else y


def _ple(x2, p2, g, g_final, w_gate, w_proj, tm, final_norm):
    m = x2.shape[0]
    return pl.pallas_call(
        functools.partial(_ple_kernel, final_norm=final_norm),
        grid=(m // tm,),
        in_specs=[
            pl.BlockSpec((tm, D_MODEL), lambda i: (i, 0)),
            pl.BlockSpec((tm, D_PLE), lambda i: (i, 0)),
            pl.BlockSpec((1, D_MODEL), lambda i: (0, 0)),
            pl.BlockSpec((1, D_MODEL), lambda i: (0, 0)),
            _resident((D_MODEL, D_MODEL)),
            _resident((D_PLE, D_MODEL)),
        ],
        out_specs=pl.BlockSpec((tm, D_MODEL), lambda i: (i, 0)),
        out_shape=jax.ShapeDtypeStruct((m, D_MODEL), F32),
        compiler_params=_cparams(("arbitrary",)),
        name="ple",
    )(x2, p2, g, g_final, w_gate, w_proj)


def _rope_lane_freqs(rot, period):
    half = rot // 2
    inv = ROPE_THETA ** (-jnp.arange(half, dtype=F32) / half)
    one = jnp.concatenate([inv, inv, jnp.zeros((period - rot,), F32)])
    return jnp.tile(one, 128 // period)[None, :]


def _pack_w_in(w):
    edges = [0] + [int(s) for s in np.cumsum(SPLIT_SIZES)]
    seg = [w[:, edges[k]:edges[k + 1]] for k in range(len(SPLIT_SIZES))]
    q_a, k_a, v_a, q_i, k_i, w_i, q_b, k_b, v_b, f_b, g_a, g_b = seg
    main = jnp.concatenate([g_a, g_b, q_a, q_i, k_a, v_a, q_b, k_b, v_b], axis=1).astype(BF16)
    pad = jnp.zeros((w.shape[0], 128 - D_IDX - H_IDX - HB), w.dtype)
    small = jnp.concatenate([k_i, w_i, f_b, pad], axis=1).astype(BF16)
    return main, small


def kernel(x, p, positions, g_mix, w_in, b_f, w_o_a, w_o_b, w_out, g_ffn, w_ffn_gate, w_ffn_up,
           w_ffn_down, g_ple, w_ple_gate, w_ple_proj, g_final):
    b, t_len, d = x.shape
    depth = w_in.shape[0]
    m = b * t_len
    top_k = min(TOPK_MAX, t_len // 4)
    inv_a = _rope_lane_freqs(ROT_A, HEAD_DIM)
    inv_i = _rope_lane_freqs(ROT_IDX, D_IDX)
    pos2 = positions.reshape(m, 1)
    x2 = x.reshape(m, d)
    tm_in = min(1024, m)
    tm = min(512, m)
    for i in range(depth):
        w_main, w_small = _pack_w_in(w_in[i])
        bf_row = jnp.zeros((1, 128), F32).at[0, LANE_FB:LANE_FB + HB].set(b_f[i].astype(F32))
        proj, kie, kio, wlf = _inproj(x2, pos2, g_mix[i][None, :], w_main, w_small, bf_row, inv_a, inv_i, tm_in)
        c4 = _cumsum(wlf, b, t_len).reshape(b, HB, 1, t_len)
        out_a = _attn_a(proj, kie, kio, wlf, b, t_len, top_k)
        out_b = _attn_b(proj, c4, b, t_len)
        x2 = _merge(x2, out_a, out_b, proj, w_o_a[i].astype(BF16), w_o_b[i].astype(BF16),
                    w_out[i].astype(BF16), tm)
        x2 = _ffn(x2, g_ffn[i][None, :], w_ffn_gate[i].astype(BF16), w_ffn_up[i].astype(BF16),
                  w_ffn_down[i].astype(BF16), tm, 512)
        x2 = _ple(x2, p[i].reshape(m, D_PLE), g_ple[i][None, :], g_final[None, :], w_ple_gate[i].astype(BF16),
                  w_ple_proj[i].astype(BF16), tm, final_norm=(i + 1 == depth))
    return x2.reshape(b, t_len, d)
```

```python
import functools

import jax
import jax.numpy as jnp
import numpy as np
from jax import lax
from jax.experimental import pallas as pl
from jax.experimental.pallas import tpu as pltpu

F32 = jnp.float32
BF16 = jnp.bfloat16

D_MODEL = 2048
HEAD_DIM = 128
HA = 8
HA_KV = 2
H_IDX = 16
D_IDX = 64
TOPK_MAX = 256
HB = 8
ROPE_THETA = 500000.0
ROT_A = HEAD_DIM // 4
ROT_IDX = D_IDX // 4
Q_BLOCK = 128
D_FF = -(-8 * D_MODEL // (3 * 256)) * 256
D_PLE = 256
EPS = 1e-6

SPLIT_SIZES = (HA * HEAD_DIM, HA_KV * HEAD_DIM, HA_KV * HEAD_DIM, H_IDX * D_IDX, D_IDX, H_IDX,
               HB * HEAD_DIM, HB * HEAD_DIM, HB * HEAD_DIM, HB, D_MODEL, D_MODEL)

OFF_GA = 0
OFF_GB = OFF_GA + D_MODEL
OFF_QA = OFF_GB + D_MODEL
OFF_QI = OFF_QA + HA * HEAD_DIM
OFF_KA = OFF_QI + H_IDX * D_IDX
OFF_VA = OFF_KA + HA_KV * HEAD_DIM
OFF_QB = OFF_VA + HA_KV * HEAD_DIM
OFF_KB = OFF_QB + HB * HEAD_DIM
OFF_VB = OFF_KB + HB * HEAD_DIM
N_MAIN = OFF_VB + HB * HEAD_DIM
LANE_KI = 0
LANE_WI = D_IDX
LANE_FB = D_IDX + H_IDX

TN_IN = 512
NEG = -1e30
VMEM_LIMIT = 56 * 1024 * 1024
ATTN_SCALE = HEAD_DIM ** -0.5


def _cparams(sem):
    return pltpu.CompilerParams(dimension_semantics=sem, vmem_limit_bytes=VMEM_LIMIT)


def _rms(x, g):
    return x * lax.rsqrt(jnp.mean(x * x, axis=-1, keepdims=True) + EPS) * g


def _resident(shape):
    nd = len(shape)
    return pl.BlockSpec(shape, lambda *_: (0,) * nd, pipeline_mode=pl.Buffered(1))


def _rope128(a, c, sa, sb, shift):
    outs = []
    for q in range(a.shape[1] // 128):
        aq = a[:, q * 128:(q + 1) * 128]
        outs.append(aq * c + pltpu.roll(aq, 128 - shift, 1) * sa + pltpu.roll(aq, shift, 1) * sb)
    return outs[0] if len(outs) == 1 else jnp.concatenate(outs, axis=1)


def _inproj_kernel(x_ref, pos_ref, g_ref, w_ref, ws_ref, bf_ref, inva_ref, invi_ref,
                   o_ref, kie_ref, kio_ref, wlf_ref,
                   h_scr, ca_scr, saa_scr, sab_scr, ci_scr, sia_scr, sib_scr):
    j = pl.program_id(1)
    tm = x_ref.shape[0]

    @pl.when(j == 0)
    def _():
        hb = _rms(x_ref[...], g_ref[...]).astype(BF16)
        h_scr[...] = hb
        pos = pos_ref[...].astype(F32)
        lane = lax.broadcasted_iota(jnp.int32, (tm, 128), 1)
        ang = pos * inva_ref[...]
        sn = jnp.sin(ang)
        ca_scr[...] = jnp.cos(ang)
        saa_scr[...] = jnp.where(lane < ROT_A // 2, -sn, 0.0)
        sab_scr[...] = jnp.where(lane >= ROT_A // 2, sn, 0.0)
        ang = pos * invi_ref[...]
        sn = jnp.sin(ang)
        lane_i = lane & (D_IDX - 1)
        ci = jnp.cos(ang)
        sia = jnp.where(lane_i < ROT_IDX // 2, -sn, 0.0)
        sib = jnp.where(lane_i >= ROT_IDX // 2, sn, 0.0)
        ci_scr[...] = ci
        sia_scr[...] = sia
        sib_scr[...] = sib
        small = jnp.dot(hb, ws_ref[...], preferred_element_type=F32)
        kr = _rope128(small, ci, sia, sib, ROT_IDX // 2)
        ke = jnp.where(lane < D_IDX, kr, 0.0)
        kie_ref[...] = ke.astype(BF16)
        kio_ref[...] = pltpu.roll(ke, D_IDX, 1).astype(BF16)
        is_w = (lane >= LANE_WI) & (lane < LANE_WI + H_IDX)
        is_f = (lane >= LANE_FB) & (lane < LANE_FB + HB)
        logf = jax.nn.log_sigmoid(small + bf_ref[...])
        wlf_ref[...] = jnp.where(is_w, small * (H_IDX ** -0.5 * D_IDX ** -0.5), jnp.where(is_f, logf, 0.0))

    acc = jnp.dot(h_scr[...], w_ref[...], preferred_element_type=F32)
    t = TN_IN
    j_gate_end = OFF_QA // t
    j_qa_end = OFF_QI // t
    j_qi_end = OFF_KA // t
    j_kv = OFF_KA // t
    j_qb = OFF_QB // t
    j_qb_end = OFF_KB // t

    @pl.when(j < j_gate_end)
    def _():
        o_ref[...] = jax.nn.sigmoid(acc).astype(BF16)

    @pl.when((j >= j_gate_end) & (j < j_qa_end))
    def _():
        y = _rope128(acc, ca_scr[...], saa_scr[...], sab_scr[...], ROT_A // 2)
        o_ref[...] = (y * ATTN_SCALE).astype(BF16)

    @pl.when((j >= j_qa_end) & (j < j_qi_end))
    def _():
        o_ref[...] = _rope128(acc, ci_scr[...], sia_scr[...], sib_scr[...], ROT_IDX // 2).astype(BF16)

    @pl.when(j == j_kv)
    def _():
        nk = HA_KV * HEAD_DIM
        y = _rope128(acc[:, :nk], ca_scr[...], saa_scr[...], sab_scr[...], ROT_A // 2)
        o_ref[...] = jnp.concatenate([y, acc[:, nk:]], axis=1).astype(BF16)

    @pl.when((j >= j_qb) & (j < j_qb_end))
    def _():
        o_ref[...] = (acc * ATTN_SCALE).astype(BF16)

    @pl.when(j >= j_qb_end)
    def _():
        o_ref[...] = acc.astype(BF16)


def _inproj(x2, pos2, g, w_main, w_small, bf_row, inv_a, inv_i, tm):
    m = x2.shape[0]
    assert OFF_VA - OFF_KA == HA_KV * HEAD_DIM and OFF_QB - OFF_KA == TN_IN
    grid = (m // tm, N_MAIN // TN_IN)
    row128 = pl.BlockSpec((1, 128), lambda i, j: (0, 0))
    return pl.pallas_call(
        _inproj_kernel,
        grid=grid,
        in_specs=[
            pl.BlockSpec((tm, D_MODEL), lambda i, j: (i, 0)),
            pl.BlockSpec((tm, 1), lambda i, j: (i, 0)),
            pl.BlockSpec((1, D_MODEL), lambda i, j: (0, 0)),
            pl.BlockSpec((D_MODEL, TN_IN), lambda i, j: (0, j)),
            pl.BlockSpec((D_MODEL, 128), lambda i, j: (0, 0)),
            row128, row128, row128,
        ],
        out_specs=[
            pl.BlockSpec((tm, TN_IN), lambda i, j: (i, j)),
            pl.BlockSpec((tm, 128), lambda i, j: (i, 0)),
            pl.BlockSpec((tm, 128), lambda i, j: (i, 0)),
            pl.BlockSpec((tm, 128), lambda i, j: (i, 0)),
        ],
        out_shape=[
            jax.ShapeDtypeStruct((m, N_MAIN), BF16),
            jax.ShapeDtypeStruct((m, 128), BF16),
            jax.ShapeDtypeStruct((m, 128), BF16),
            jax.ShapeDtypeStruct((m, 128), F32),
        ],
        scratch_shapes=[pltpu.VMEM((tm, D_MODEL), BF16)] + [pltpu.VMEM((tm, 128), F32)] * 6,
        compiler_params=_cparams(("arbitrary", "arbitrary")),
        name="inproj",
    )(x2, pos2, g, w_main, w_small, bf_row, inv_a, inv_i)


def _cumsum_kernel(wlf_ref, o_ref, carry_scr, *, tc):
    @pl.when(pl.program_id(1) == 0)
    def _():
        carry_scr[...] = jnp.zeros_like(carry_scr)

    lf = wlf_ref[...]
    r = lax.broadcasted_iota(jnp.int32, (tc, tc), 0)
    c = lax.broadcasted_iota(jnp.int32, (tc, tc), 1)
    tri = jnp.where(c <= r, 1.0, 0.0).astype(BF16)
    hi = lf.astype(BF16)
    r1 = lf - hi.astype(F32)
    mid = r1.astype(BF16)
    lo = (r1 - mid.astype(F32)).astype(BF16)
    cs = (jnp.dot(tri, hi, preferred_element_type=F32) + jnp.dot(tri, mid, preferred_element_type=F32)
          + jnp.dot(tri, lo, preferred_element_type=F32)) + carry_scr[0:1, :]
    carry_scr[...] = jnp.broadcast_to(cs[tc - 1:tc, :], carry_scr.shape)
    for h in range(HB):
        o_ref[h] = jnp.broadcast_to(cs[:, LANE_FB + h:LANE_FB + h + 1], (tc, 128))


def _cumsum(wlf, b, t_len):
    tc = 256
    nt = t_len // tc
    return pl.pallas_call(
        functools.partial(_cumsum_kernel, tc=tc),
        grid=(b, nt),
        in_specs=[pl.BlockSpec((tc, 128), lambda bi, ti: (bi * nt + ti, 0))],
        out_specs=pl.BlockSpec((None, HB, tc, 128), lambda bi, ti: (bi, 0, ti, 0)),
        out_shape=jax.ShapeDtypeStruct((b, HB, t_len, 128), F32),
        scratch_shapes=[pltpu.VMEM((8, 128), F32)],
        compiler_params=_cparams(("arbitrary", "arbitrary")),
        name="fcumsum",
    )(wlf)


def _attn_a_kernel(qa_ref, qi_ref, ka_ref, vt_ref, kie_ref, kio_ref, wlf_ref, o_ref,
                   sc_scr, qs_scr, acc_scr, *, tk, top_k, n_iter):
    i = pl.program_id(1)
    nq = Q_BLOCK
    group = HA // HA_KV
    gw = group * nq
    nchunks = ((i + 1) * nq + tk - 1) // tk
    q_pos = i * nq + lax.broadcasted_iota(jnp.int32, (1, nq), 1)
    wt = wlf_ref[...].T
    for p in range(H_IDX // 2):
        qs_scr[p * nq:(p + 1) * nq, :] = qi_ref[:, p * 128:(p + 1) * 128]

    def score_chunk(c, carry):
        rmin, rmax = carry
        off = pl.multiple_of(c * tk, tk)
        qs = qs_scr[...]
        de = lax.dot_general(kie_ref[pl.ds(off, tk), :], qs, (((1,), (1,)), ((), ())), preferred_element_type=F32)
        do = lax.dot_general(kio_ref[pl.ds(off, tk), :], qs, (((1,), (1,)), ((), ())), preferred_element_type=F32)
        s = jnp.zeros((tk, nq), F32)
        for p in range(H_IDX // 2):
            cols = slice(p * nq, (p + 1) * nq)
            s = s + jnp.maximum(de[:, cols], 0.0) * wt[LANE_WI + 2 * p:LANE_WI + 2 * p + 1, :]
            s = s + jnp.maximum(do[:, cols], 0.0) * wt[LANE_WI + 2 * p + 1:LANE_WI + 2 * p + 2, :]
        kpos = off + lax.broadcasted_iota(jnp.int32, (tk, 1), 0)
        valid = kpos <= q_pos
        sc_scr[pl.ds(off, tk), :] = jnp.where(valid, s, NEG)
        rmin = jnp.minimum(rmin, jnp.min(jnp.where(valid, s, -NEG), axis=0, keepdims=True))
        rmax = jnp.maximum(rmax, jnp.max(jnp.where(valid, s, NEG), axis=0, keepdims=True))
        return rmin, rmax

    lo, hi = lax.fori_loop(0, nchunks, score_chunk,
                           (jnp.full((1, nq), -NEG, F32), jnp.full((1, nq), NEG, F32)))

    def bisect(_, carry):
        lo, hi = carry
        mid = 0.5 * (lo + hi)

        def count_chunk(c, part):
            off = pl.multiple_of(c * tk, tk)
            ind = jnp.where(sc_scr[pl.ds(off, tk), :] >= mid, 1.0, 0.0)
            return part + jnp.sum(ind.reshape(tk // 64, 64, nq), axis=0)

        part = lax.fori_loop(0, nchunks, count_chunk, jnp.zeros((64, nq), F32))
        ge = jnp.sum(part, axis=0, keepdims=True) >= top_k
        return jnp.where(ge, mid, lo), jnp.where(ge, hi, mid)

    thr, _ = lax.fori_loop(0, n_iter, bisect, (lo, hi))

    for n in range(HA_KV):
        qg = jnp.concatenate(
            [qa_ref[:, (n * group + g) * HEAD_DIM:(n * group + g + 1) * HEAD_DIM] for g in range(group)], axis=0)
        acc_scr[...] = jnp.zeros(acc_scr.shape, F32)

        def att_chunk(c, carry):
            m_old, l_old = carry
            off = pl.multiple_of(c * tk, tk)
            k = ka_ref[pl.ds(off, tk), n * HEAD_DIM:(n + 1) * HEAD_DIM]
            vt = vt_ref[n * HEAD_DIM:(n + 1) * HEAD_DIM, pl.ds(off, tk)]
            lt = lax.dot_general(k, qg, (((1,), (1,)), ((), ())), preferred_element_type=F32)
            msk = jnp.where(sc_scr[pl.ds(off, tk), :] >= thr, 0.0, NEG)
            lt = lt + jnp.concatenate([msk] * group, axis=1)
            m_new = jnp.maximum(m_old, jnp.max(lt, axis=0, keepdims=True))
            alpha = jnp.exp(m_old - m_new)
            p = jnp.exp(lt - m_new)
            l_new = alpha * l_old + jnp.sum(p, axis=0, keepdims=True)
            acc_scr[...] = alpha * acc_scr[...] + jnp.dot(vt, p.astype(BF16), preferred_element_type=F32)
            return m_new, l_new

        _, l_fin = lax.fori_loop(0, nchunks, att_chunk,
                                 (jnp.full((1, gw), NEG, F32), jnp.zeros((1, gw), F32)))
        out_t = acc_scr[...] / l_fin
        for g in range(group):
            h = n * group + g
            o_ref[:, h * HEAD_DIM:(h + 1) * HEAD_DIM] = out_t[:, g * nq:(g + 1) * nq].T.astype(BF16)


def _attn_a(proj, vt_a, kie, kio, wlf, b, t_len, top_k):
    nb = t_len // Q_BLOCK
    wa = HA * HEAD_DIM
    wkv = HA_KV * HEAD_DIM
    tk = min(512, t_len)
    kern = functools.partial(_attn_a_kernel, tk=tk, top_k=top_k, n_iter=32)
    return pl.pallas_call(
        kern,
        grid=(b, nb),
        in_specs=[
            pl.BlockSpec((Q_BLOCK, wa), lambda bi, i: (bi * nb + i, OFF_QA // wa)),
            pl.BlockSpec((Q_BLOCK, H_IDX * D_IDX), lambda bi, i: (bi * nb + i, OFF_QI // (H_IDX * D_IDX))),
            pl.BlockSpec((t_len, wkv), lambda bi, i: (bi, OFF_KA // wkv)),
            pl.BlockSpec((None, wkv, t_len), lambda bi, i: (bi, 0, 0)),
            pl.BlockSpec((t_len, 128), lambda bi, i: (bi, 0)),
            pl.BlockSpec((t_len, 128), lambda bi, i: (bi, 0)),
            pl.BlockSpec((Q_BLOCK, 128), lambda bi, i: (bi * nb + i, 0)),
        ],
        out_specs=pl.BlockSpec((Q_BLOCK, wa), lambda bi, i: (bi * nb + i, 0)),
        out_shape=jax.ShapeDtypeStruct((b * t_len, wa), BF16),
        scratch_shapes=[
            pltpu.VMEM((t_len, Q_BLOCK), F32),
            pltpu.VMEM((H_IDX // 2 * Q_BLOCK, 128), BF16),
            pltpu.VMEM((HEAD_DIM, HA // HA_KV * Q_BLOCK), F32),
        ],
        compiler_params=_cparams(("arbitrary", "arbitrary")),
        name="attn_a",
    )(proj, proj, proj, vt_a, kie, kio, wlf)


def _attn_b_kernel(q_ref, k_ref, vt_ref, cb_ref, o_ref, acc_scr, *, tq):
    qi = pl.program_id(2)
    q = q_ref[...]
    acc_scr[...] = jnp.zeros(acc_scr.shape, F32)
    c0 = cb_ref[pl.ds(pl.multiple_of(qi * tq, tq), 1), :]

    def step(c, carry, diagonal):
        m_old, l_old = carry
        off = pl.multiple_of(c * tq, tq)
        k = k_ref[pl.ds(off, tq), :]
        vt = vt_ref[:, pl.ds(off, tq)]
        s = lax.dot_general(k, q, (((1,), (1,)), ((), ())), preferred_element_type=F32)
        bias = c0 - cb_ref[pl.ds(off, tq), :]
        s = s + jnp.concatenate([bias] * (tq // 128), axis=1)
        if diagonal:
            key = lax.broadcasted_iota(jnp.int32, (tq, 1), 0)
            qry = lax.broadcasted_iota(jnp.int32, (1, tq), 1)
            s = jnp.where(key <= qry, s, NEG)
        m_new = jnp.maximum(m_old, jnp.max(s, axis=0, keepdims=True))
        alpha = jnp.exp(m_old - m_new)
        p = jnp.exp(s - m_new)
        l_new = alpha * l_old + jnp.sum(p, axis=0, keepdims=True)
        acc_scr[...] = alpha * acc_scr[...] + jnp.dot(vt, p.astype(BF16), preferred_element_type=F32)
        return m_new, l_new

    carry = lax.fori_loop(0, qi, lambda c, cr: step(c, cr, False),
                          (jnp.full((1, tq), NEG, F32), jnp.zeros((1, tq), F32)))
    _, l_fin = step(qi, carry, True)
    out_t = acc_scr[...] / l_fin
    for g in range(tq // 128):
        o_ref[g * 128:(g + 1) * 128, :] = out_t[:, g * 128:(g + 1) * 128].T.astype(BF16)


def _attn_b(proj, vt_b, cb, b, t_len):
    tq = 512
    nq = t_len // tq
    return pl.pallas_call(
        functools.partial(_attn_b_kernel, tq=tq),
        grid=(b, HB, nq),
        in_specs=[
            pl.BlockSpec((tq, HEAD_DIM), lambda bi, h, i: (bi * nq + i, OFF_QB // HEAD_DIM + h)),
            pl.BlockSpec((t_len, HEAD_DIM), lambda bi, h, i: (bi, OFF_KB // HEAD_DIM + h)),
            pl.BlockSpec((None, HEAD_DIM, t_len), lambda bi, h, i: (bi, h, 0)),
            pl.BlockSpec((None, None, t_len, 128), lambda bi, h, i: (bi, h, 0, 0)),
        ],
        out_specs=pl.BlockSpec((tq, HEAD_DIM), lambda bi, h, i: (bi * nq + i, h)),
        out_shape=jax.ShapeDtypeStruct((b * t_len, HB * HEAD_DIM), BF16),
        scratch_shapes=[pltpu.VMEM((HEAD_DIM, tq), F32)],
        compiler_params=_cparams(("arbitrary", "arbitrary", "arbitrary")),
        name="attn_b",
    )(proj, proj, vt_b, cb)


def _merge_kernel(x_ref, oa_ref, ob_ref, ga_ref, gb_ref, woa_ref, wob_ref, wout_ref, o_ref):
    ya = jnp.dot(oa_ref[...], woa_ref[...], preferred_element_type=F32)
    yb = jnp.dot(ob_ref[...], wob_ref[...], preferred_element_type=F32)
    mixed = ga_ref[...].astype(F32) * ya + gb_ref[...].astype(F32) * yb
    o_ref[...] = x_ref[...] + jnp.dot(mixed.astype(BF16), wout_ref[...], preferred_element_type=F32)


def _merge(x2, out_a, out_b, proj, w_oa, w_ob, w_out, tm):
    m = x2.shape[0]
    wa = HA * HEAD_DIM
    wb = HB * HEAD_DIM
    return pl.pallas_call(
        _merge_kernel,
        grid=(m // tm,),
        in_specs=[
            pl.BlockSpec((tm, D_MODEL), lambda i: (i, 0)),
            pl.BlockSpec((tm, wa), lambda i: (i, 0)),
            pl.BlockSpec((tm, wb), lambda i: (i, 0)),
            pl.BlockSpec((tm, D_MODEL), lambda i: (i, OFF_GA // D_MODEL)),
            pl.BlockSpec((tm, D_MODEL), lambda i: (i, OFF_GB // D_MODEL)),
            _resident((wa, D_MODEL)),
            _resident((wb, D_MODEL)),
            _resident((D_MODEL, D_MODEL)),
        ],
        out_specs=pl.BlockSpec((tm, D_MODEL), lambda i: (i, 0)),
        out_shape=jax.ShapeDtypeStruct((m, D_MODEL), F32),
        compiler_params=_cparams(("arbitrary",)),
        name="merge",
    )(x2, out_a, out_b, proj, proj, w_oa, w_ob, w_out)


def _ffn_kernel(x_ref, g_ref, wg_ref, wu_ref, wd_ref, o_ref, h_scr, acc_scr):
    f = pl.program_id(1)

    @pl.when(f == 0)
    def _():
        h_scr[...] = _rms(x_ref[...], g_ref[...]).astype(BF16)
        acc_scr[...] = jnp.zeros(acc_scr.shape, F32)

    h = h_scr[...]
    gate = jnp.dot(h, wg_ref[...], preferred_element_type=F32)
    up = jnp.dot(h, wu_ref[...], preferred_element_type=F32)
    act = (jax.nn.silu(gate) * up).astype(BF16)
    acc_scr[...] += jnp.dot(act, wd_ref[...], preferred_element_type=F32)

    @pl.when(f == pl.num_programs(1) - 1)
    def _():
        o_ref[...] = x_ref[...] + acc_scr[...]


def _ffn(x2, g, w_gate, w_up, w_down, tm, tf):
    m = x2.shape[0]
    return pl.pallas_call(
        _ffn_kernel,
        grid=(m // tm, D_FF // tf),
        in_specs=[
            pl.BlockSpec((tm, D_MODEL), lambda i, f: (i, 0)),
            pl.BlockSpec((1, D_MODEL), lambda i, f: (0, 0)),
            pl.BlockSpec((D_MODEL, tf), lambda i, f: (0, f)),
            pl.BlockSpec((D_MODEL, tf), lambda i, f: (0, f)),
            pl.BlockSpec((tf, D_MODEL), lambda i, f: (f, 0)),
        ],
        out_specs=pl.BlockSpec((tm, D_MODEL), lambda i, f: (i, 0)),
        out_shape=jax.ShapeDtypeStruct((m, D_MODEL), F32),
        scratch_shapes=[pltpu.VMEM((tm, D_MODEL), BF16), pltpu.VMEM((tm, D_MODEL), F32)],
        compiler_params=_cparams(("arbitrary", "arbitrary")),
        name="ffn",
    )(x2, g, w_gate, w_up, w_down)


def _ple_kernel(x_ref, p_ref, g_ref, gf_ref, wg_ref, wp_ref, o_ref, *, final_norm):
    x = x_ref[...]
    h = _rms(x, g_ref[...]).astype(BF16)
    gate = jax.nn.sigmoid(jnp.dot(h, wg_ref[...], preferred_element_type=F32))
    emb = jnp.dot(p_ref[...].astype(BF16), wp_ref[...], preferred_element_type=F32)
    y = x + gate * emb
    o_ref[...] = _rms(y, gf_ref[...]) if final_norm else y


def _ple(x2, p2, g, g_final, w_gate, w_proj, tm, final_norm):
    m = x2.shape[0]
    return pl.pallas_call(
        functools.partial(_ple_kernel, final_norm=final_norm),
        grid=(m // tm,),
        in_specs=[
            pl.BlockSpec((tm, D_MODEL), lambda i: (i, 0)),
            pl.BlockSpec((tm, D_PLE), lambda i: (i, 0)),
            pl.BlockSpec((1, D_MODEL), lambda i: (0, 0)),
            pl.BlockSpec((1, D_MODEL), lambda i: (0, 0)),
            _resident((D_MODEL, D_MODEL)),
            _resident((D_PLE, D_MODEL)),
        ],
        out_specs=pl.BlockSpec((tm, D_MODEL), lambda i: (i, 0)),
        out_shape=jax.ShapeDtypeStruct((m, D_MODEL), F32),
        compiler_params=_cparams(("arbitrary",)),
        name="ple",
    )(x2, p2, g, g_final, w_gate, w_proj)


def _rope_lane_freqs(rot, period):
    half = rot // 2
    inv = ROPE_THETA ** (-jnp.arange(half, dtype=F32) / half)
    one = jnp.concatenate([inv, inv, jnp.zeros((period - rot,), F32)])
    return jnp.tile(one, 128 // period)[None, :]


def _pack_w_in(w):
    edges = [0] + [int(s) for s in np.cumsum(SPLIT_SIZES)]
    seg = [w[:, edges[k]:edges[k + 1]] for k in range(len(SPLIT_SIZES))]
    q_a, k_a, v_a, q_i, k_i, w_i, q_b, k_b, v_b, f_b, g_a, g_b = seg
    main = jnp.concatenate([g_a, g_b, q_a, q_i, k_a, v_a, q_b, k_b, v_b], axis=1).astype(BF16)
    pad = jnp.zeros((w.shape[0], 128 - D_IDX - H_IDX - HB), w.dtype)
    small = jnp.concatenate([k_i, w_i, f_b, pad], axis=1).astype(BF16)
    return main, small


def _keys_on_lanes(proj, off, width, b, t_len):
    return proj[:, off:off + width].reshape(b, t_len, width).transpose(0, 2, 1)


def kernel(x, p, positions, g_mix, w_in, b_f, w_o_a, w_o_b, w_out, g_ffn, w_ffn_gate, w_ffn_up,
           w_ffn_down, g_ple, w_ple_gate, w_ple_proj, g_final):
    b, t_len, d = x.shape
    depth = w_in.shape[0]
    m = b * t_len
    top_k = min(TOPK_MAX, t_len // 4)
    inv_a = _rope_lane_freqs(ROT_A, HEAD_DIM)
    inv_i = _rope_lane_freqs(ROT_IDX, D_IDX)
    pos2 = positions.reshape(m, 1)
    x2 = x.reshape(m, d)
    tm_in = min(1024, m)
    tm = min(512, m)
    for i in range(depth):
        w_main, w_small = _pack_w_in(w_in[i])
        bf_row = jnp.zeros((1, 128), F32).at[0, LANE_FB:LANE_FB + HB].set(b_f[i].astype(F32))
        proj, kie, kio, wlf = _inproj(x2, pos2, g_mix[i][None, :], w_main, w_small, bf_row, inv_a, inv_i, tm_in)
        cb = _cumsum(wlf, b, t_len)
        vt_a = _keys_on_lanes(proj, OFF_VA, HA_KV * HEAD_DIM, b, t_len)
        vt_b = _keys_on_lanes(proj, OFF_VB, HB * HEAD_DIM, b, t_len)
        out_a = _attn_a(proj, vt_a, kie, kio, wlf, b, t_len, top_k)
        out_b = _attn_b(proj, vt_b, cb, b, t_len)
        x2 = _merge(x2, out_a, out_b, proj, w_o_a[i].astype(BF16), w_o_b[i].astype(BF16),
                    w_out[i].astype(BF16), tm)
        x2 = _ffn(x2, g_ffn[i][None, :], w_ffn_gate[i].astype(BF16), w_ffn_up[i].astype(BF16),
                  w_ffn_down[i].astype(BF16), tm, 512)
        x2 = _ple(x2, p[i].reshape(m, D_PLE), g_ple[i][None, :], g_final[None, :], w_ple_gate[i].astype(BF16),
                  w_ple_proj[i].astype(BF16), tm, final_norm=(i + 1 == depth))
    return x2.reshape(b, t_len, d)
```

```python
import functools

import jax
import jax.numpy as jnp
import numpy as np
from jax import lax
from jax.experimental import pallas as pl
from jax.experimental.pallas import tpu as pltpu

F32 = jnp.float32
BF16 = jnp.bfloat16

D_MODEL = 2048
HEAD_DIM = 128
HA = 8
HA_KV = 2
H_IDX = 16
D_IDX = 64
TOPK_MAX = 256
HB = 8
ROPE_THETA = 500000.0
ROT_A = HEAD_DIM // 4
ROT_IDX = D_IDX // 4
Q_BLOCK = 128
D_FF = -(-8 * D_MODEL // (3 * 256)) * 256
D_PLE = 256
EPS = 1e-6

SPLIT_SIZES = (HA * HEAD_DIM, HA_KV * HEAD_DIM, HA_KV * HEAD_DIM, H_IDX * D_IDX, D_IDX, H_IDX,
               HB * HEAD_DIM, HB * HEAD_DIM, HB * HEAD_DIM, HB, D_MODEL, D_MODEL)

OFF_GA = 0
OFF_GB = OFF_GA + D_MODEL
OFF_QA = OFF_GB + D_MODEL
OFF_QI = OFF_QA + HA * HEAD_DIM
OFF_KA = OFF_QI + H_IDX * D_IDX
OFF_VA = OFF_KA + HA_KV * HEAD_DIM
OFF_QB = OFF_VA + HA_KV * HEAD_DIM
OFF_KB = OFF_QB + HB * HEAD_DIM
OFF_VB = OFF_KB + HB * HEAD_DIM
N_MAIN = OFF_VB + HB * HEAD_DIM
LANE_KI = 0
LANE_WI = D_IDX
LANE_FB = D_IDX + H_IDX

TN_IN = 512
NEG = -1e30
VMEM_LIMIT = 56 * 1024 * 1024
LOG2E = 1.4426950408889634
ATTN_SCALE = HEAD_DIM ** -0.5 * LOG2E
VT_ROWS = HEAD_DIM + 16
BISECT_ROUND = 4
BISECT_MAX_ITERS = 48


def _cparams(sem):
    return pltpu.CompilerParams(dimension_semantics=sem, vmem_limit_bytes=VMEM_LIMIT)


def _rms(x, g):
    return x * lax.rsqrt(jnp.mean(x * x, axis=-1, keepdims=True) + EPS) * g


def _resident(shape):
    nd = len(shape)
    return pl.BlockSpec(shape, lambda *_: (0,) * nd, pipeline_mode=pl.Buffered(1))


def _rope128(a, c, sa, sb, shift):
    outs = []
    for q in range(a.shape[1] // 128):
        aq = a[:, q * 128:(q + 1) * 128]
        outs.append(aq * c + pltpu.roll(aq, 128 - shift, 1) * sa + pltpu.roll(aq, shift, 1) * sb)
    return outs[0] if len(outs) == 1 else jnp.concatenate(outs, axis=1)


def _inproj_kernel(x_ref, pos_ref, g_ref, w_ref, ws_ref, bf_ref, inva_ref, invi_ref,
                   o_ref, kie_ref, kio_ref, wlf_ref,
                   h_scr, ca_scr, saa_scr, sab_scr, ci_scr, sia_scr, sib_scr):
    j = pl.program_id(1)
    tm = x_ref.shape[0]

    @pl.when(j == 0)
    def _():
        hb = _rms(x_ref[...], g_ref[...]).astype(BF16)
        h_scr[...] = hb
        pos = pos_ref[...].astype(F32)
        lane = lax.broadcasted_iota(jnp.int32, (tm, 128), 1)
        ang = pos * inva_ref[...]
        sn = jnp.sin(ang)
        ca_scr[...] = jnp.cos(ang)
        saa_scr[...] = jnp.where(lane < ROT_A // 2, -sn, 0.0)
        sab_scr[...] = jnp.where(lane >= ROT_A // 2, sn, 0.0)
        ang = pos * invi_ref[...]
        sn = jnp.sin(ang)
        lane_i = lane & (D_IDX - 1)
        ci = jnp.cos(ang)
        sia = jnp.where(lane_i < ROT_IDX // 2, -sn, 0.0)
        sib = jnp.where(lane_i >= ROT_IDX // 2, sn, 0.0)
        ci_scr[...] = ci
        sia_scr[...] = sia
        sib_scr[...] = sib
        small = jnp.dot(hb, ws_ref[...], preferred_element_type=F32)
        kr = _rope128(small, ci, sia, sib, ROT_IDX // 2)
        ke = jnp.where(lane < D_IDX, kr, 0.0)
        kie_ref[...] = ke.astype(BF16)
        kio_ref[...] = pltpu.roll(ke, D_IDX, 1).astype(BF16)
        is_w = (lane >= LANE_WI) & (lane < LANE_WI + H_IDX)
        is_f = (lane >= LANE_FB) & (lane < LANE_FB + HB)
        logf = jax.nn.log_sigmoid(small + bf_ref[...])
        wlf_ref[...] = jnp.where(is_w, small * (H_IDX ** -0.5 * D_IDX ** -0.5), jnp.where(is_f, logf, 0.0))

    acc = jnp.dot(h_scr[...], w_ref[...], preferred_element_type=F32)
    t = TN_IN
    j_gate_end = OFF_QA // t
    j_qa_end = OFF_QI // t
    j_qi_end = OFF_KA // t
    j_kv = OFF_KA // t
    j_qb = OFF_QB // t
    j_qb_end = OFF_KB // t

    @pl.when(j < j_gate_end)
    def _():
        o_ref[...] = jax.nn.sigmoid(acc).astype(BF16)

    @pl.when((j >= j_gate_end) & (j < j_qa_end))
    def _():
        y = _rope128(acc, ca_scr[...], saa_scr[...], sab_scr[...], ROT_A // 2)
        o_ref[...] = (y * ATTN_SCALE).astype(BF16)

    @pl.when((j >= j_qa_end) & (j < j_qi_end))
    def _():
        o_ref[...] = _rope128(acc, ci_scr[...], sia_scr[...], sib_scr[...], ROT_IDX // 2).astype(BF16)

    @pl.when(j == j_kv)
    def _():
        nk = HA_KV * HEAD_DIM
        y = _rope128(acc[:, :nk], ca_scr[...], saa_scr[...], sab_scr[...], ROT_A // 2)
        o_ref[...] = jnp.concatenate([y, acc[:, nk:]], axis=1).astype(BF16)

    @pl.when((j >= j_qb) & (j < j_qb_end))
    def _():
        o_ref[...] = (acc * ATTN_SCALE).astype(BF16)

    @pl.when(j >= j_qb_end)
    def _():
        o_ref[...] = acc.astype(BF16)


def _inproj(x2, pos2, g, w_main, w_small, bf_row, inv_a, inv_i, tm):
    m = x2.shape[0]
    assert OFF_VA - OFF_KA == HA_KV * HEAD_DIM and OFF_QB - OFF_KA == TN_IN
    grid = (m // tm, N_MAIN // TN_IN)
    row128 = pl.BlockSpec((1, 128), lambda i, j: (0, 0))
    return pl.pallas_call(
        _inproj_kernel,
        grid=grid,
        in_specs=[
            pl.BlockSpec((tm, D_MODEL), lambda i, j: (i, 0)),
            pl.BlockSpec((tm, 1), lambda i, j: (i, 0)),
            pl.BlockSpec((1, D_MODEL), lambda i, j: (0, 0)),
            pl.BlockSpec((D_MODEL, TN_IN), lambda i, j: (0, j)),
            pl.BlockSpec((D_MODEL, 128), lambda i, j: (0, 0)),
            row128, row128, row128,
        ],
        out_specs=[
            pl.BlockSpec((tm, TN_IN), lambda i, j: (i, j)),
            pl.BlockSpec((tm, 128), lambda i, j: (i, 0)),
            pl.BlockSpec((tm, 128), lambda i, j: (i, 0)),
            pl.BlockSpec((tm, 128), lambda i, j: (i, 0)),
        ],
        out_shape=[
            jax.ShapeDtypeStruct((m, N_MAIN), BF16),
            jax.ShapeDtypeStruct((m, 128), BF16),
            jax.ShapeDtypeStruct((m, 128), BF16),
            jax.ShapeDtypeStruct((m, 128), F32),
        ],
        scratch_shapes=[pltpu.VMEM((tm, D_MODEL), BF16)] + [pltpu.VMEM((tm, 128), F32)] * 6,
        compiler_params=_cparams(("arbitrary", "arbitrary")),
        name="inproj",
    )(x2, pos2, g, w_main, w_small, bf_row, inv_a, inv_i)


def _cumsum_kernel(wlf_ref, o_ref, carry_scr, *, tc):
    @pl.when(pl.program_id(1) == 0)
    def _():
        carry_scr[...] = jnp.zeros_like(carry_scr)

    lf = wlf_ref[...]
    r = lax.broadcasted_iota(jnp.int32, (tc, tc), 0)
    c = lax.broadcasted_iota(jnp.int32, (tc, tc), 1)
    tri = jnp.where(c <= r, 1.0, 0.0).astype(BF16)
    hi = lf.astype(BF16)
    r1 = lf - hi.astype(F32)
    mid = r1.astype(BF16)
    lo = (r1 - mid.astype(F32)).astype(BF16)
    cs = (jnp.dot(tri, hi, preferred_element_type=F32) + jnp.dot(tri, mid, preferred_element_type=F32)
          + jnp.dot(tri, lo, preferred_element_type=F32)) + carry_scr[0:1, :]
    carry_scr[...] = jnp.broadcast_to(cs[tc - 1:tc, :], carry_scr.shape)
    cs2 = cs * LOG2E
    for h in range(HB):
        o_ref[h] = jnp.broadcast_to(cs2[:, LANE_FB + h:LANE_FB + h + 1], (tc, 128))


def _cumsum(wlf, b, t_len):
    tc = 256
    nt = t_len // tc
    return pl.pallas_call(
        functools.partial(_cumsum_kernel, tc=tc),
        grid=(b, nt),
        in_specs=[pl.BlockSpec((tc, 128), lambda bi, ti: (bi * nt + ti, 0))],
        out_specs=pl.BlockSpec((None, HB, tc, 128), lambda bi, ti: (bi, 0, ti, 0)),
        out_shape=jax.ShapeDtypeStruct((b, HB, t_len, 128), F32),
        scratch_shapes=[pltpu.VMEM((8, 128), F32)],
        compiler_params=_cparams(("arbitrary", "arbitrary")),
        name="fcumsum",
    )(wlf)


def _attn_a_kernel(qa_ref, qi_ref, ka_ref, vt_ref, kie_ref, kio_ref, wlf_ref, o_ref,
                   sc_scr, qs_scr, qg_scr, lt_scr, acc_scr, *, tk, top_k, n_iter):
    i = pl.program_id(1)
    nq = Q_BLOCK
    group = HA // HA_KV
    gw = group * nq
    nchunks = ((i + 1) * nq + tk - 1) // tk
    q_pos = i * nq + lax.broadcasted_iota(jnp.int32, (1, nq), 1)
    wt = wlf_ref[...].T
    for p in range(H_IDX // 2):
        qs_scr[p * nq:(p + 1) * nq, :] = qi_ref[:, p * 128:(p + 1) * 128]

    def score_chunk(c, carry):
        rmin, rmax = carry
        off = pl.multiple_of(c * tk, tk)
        qs = qs_scr[...]
        de = lax.dot_general(kie_ref[pl.ds(off, tk), :], qs, (((1,), (1,)), ((), ())), preferred_element_type=F32)
        do = lax.dot_general(kio_ref[pl.ds(off, tk), :], qs, (((1,), (1,)), ((), ())), preferred_element_type=F32)
        s = jnp.zeros((tk, nq), F32)
        for p in range(H_IDX // 2):
            cols = slice(p * nq, (p + 1) * nq)
            s = s + jnp.maximum(de[:, cols], 0.0) * wt[LANE_WI + 2 * p:LANE_WI + 2 * p + 1, :]
            s = s + jnp.maximum(do[:, cols], 0.0) * wt[LANE_WI + 2 * p + 1:LANE_WI + 2 * p + 2, :]
        kpos = off + lax.broadcasted_iota(jnp.int32, (tk, 1), 0)
        valid = kpos <= q_pos
        sc_scr[pl.ds(off, tk), :] = jnp.where(valid, s, NEG)
        rmin = jnp.minimum(rmin, jnp.min(jnp.where(valid, s, -NEG), axis=0, keepdims=True))
        rmax = jnp.maximum(rmax, jnp.max(jnp.where(valid, s, NEG), axis=0, keepdims=True))
        return rmin, rmax

    lo, hi = lax.fori_loop(0, nchunks, score_chunk,
                           (jnp.full((1, nq), -NEG, F32), jnp.full((1, nq), NEG, F32)))

    def bisect(_, carry):
        lo, hi = carry
        mid = 0.5 * (lo + hi)

        def count_chunk(c, part):
            off = pl.multiple_of(c * tk, tk)
            ind = jnp.where(sc_scr[pl.ds(off, tk), :] >= mid, 1.0, 0.0)
            return part + jnp.sum(ind.reshape(tk // 64, 64, nq), axis=0)

        part = lax.fori_loop(0, nchunks, count_chunk, jnp.zeros((64, nq), F32))
        cnt = jnp.sum(part, axis=0, keepdims=True)
        ge = cnt >= top_k
        return jnp.where(ge, mid, lo), jnp.where(cnt == top_k, mid, jnp.where(ge, hi, mid))

    searching = q_pos >= top_k

    def bisect_round(state):
        it, lo, hi, _ = state
        lo, hi = lax.fori_loop(0, BISECT_ROUND, bisect, (lo, hi))
        pending = jnp.sum(jnp.where(searching & (lo < hi), 1.0, 0.0))
        return it + BISECT_ROUND, lo, hi, pending

    _, thr, _, _ = lax.while_loop(lambda st: (st[0] < n_iter) & (st[3] > 0.0), bisect_round,
                                  (jnp.int32(0), lo, hi, jnp.float32(1.0)))

    tka = tk // 2
    last = (nchunks - 1) * tk
    for n in range(HA_KV):
        for g in range(group):
            h = n * group + g
            qg_scr[n, g * nq:(g + 1) * nq, :] = qa_ref[:, h * HEAD_DIM:(h + 1) * HEAD_DIM]
    acc_scr[...] = jnp.zeros(acc_scr.shape, F32)

    def qk(off, slot):
        msk = jnp.where(sc_scr[pl.ds(off, tka), :] >= thr, 0.0, NEG)
        msk = jnp.concatenate([msk] * group, axis=1)
        cmax = []
        for n in range(HA_KV):
            k = ka_ref[pl.ds(off, tka), n * HEAD_DIM:(n + 1) * HEAD_DIM]
            s = lax.dot_general(k, qg_scr[n], (((1,), (1,)), ((), ())), preferred_element_type=F32) + msk
            lt_scr[n, slot] = s
            cmax.append(jnp.max(s, axis=0, keepdims=True))
        return tuple(cmax)

    def softmax_pv(off, slot, m_old, cmax):
        new = []
        for n in range(HA_KV):
            m_new = jnp.maximum(m_old[n], cmax[n])
            alpha = jnp.exp2(m_old[n] - m_new)
            p = jnp.exp2(lt_scr[n, slot] - m_new).astype(BF16)
            acc_scr[n] = alpha * acc_scr[n] + jnp.dot(vt_ref[n, :, pl.ds(off, tka)], p,
                                                      preferred_element_type=F32)
            new.append(m_new)
        return tuple(new)

    def pair(j, carry):
        m, cmax0 = carry
        off0 = pl.multiple_of(j * tk, tk)
        off1 = pl.multiple_of(off0 + tka, tka)
        off2 = pl.multiple_of(jnp.minimum(off0 + tk, last), tk)
        cmax1 = qk(off1, 1)
        m = softmax_pv(off0, 0, m, cmax0)
        cmax0 = qk(off2, 0)
        return softmax_pv(off1, 1, m, cmax1), cmax0

    lax.fori_loop(0, nchunks, pair, ((jnp.full((1, gw), NEG, F32),) * HA_KV, qk(0, 0)))
    for n in range(HA_KV):
        out_t = acc_scr[n, 0:HEAD_DIM, :] / acc_scr[n, HEAD_DIM:HEAD_DIM + 1, :]
        for g in range(group):
            h = n * group + g
            o_ref[:, h * HEAD_DIM:(h + 1) * HEAD_DIM] = out_t[:, g * nq:(g + 1) * nq].T.astype(BF16)


def _attn_a(proj, vt_a, kie, kio, wlf, b, t_len, top_k):
    nb = t_len // Q_BLOCK
    wa = HA * HEAD_DIM
    wkv = HA_KV * HEAD_DIM
    tk = min(512, t_len)
    gw = HA // HA_KV * Q_BLOCK
    kern = functools.partial(_attn_a_kernel, tk=tk, top_k=top_k, n_iter=BISECT_MAX_ITERS)
    return pl.pallas_call(
        kern,
        grid=(b, nb),
        in_specs=[
            pl.BlockSpec((Q_BLOCK, wa), lambda bi, i: (bi * nb + i, OFF_QA // wa)),
            pl.BlockSpec((Q_BLOCK, H_IDX * D_IDX), lambda bi, i: (bi * nb + i, OFF_QI // (H_IDX * D_IDX))),
            pl.BlockSpec((t_len, wkv), lambda bi, i: (bi, OFF_KA // wkv)),
            pl.BlockSpec((None, HA_KV, VT_ROWS, t_len), lambda bi, i: (bi, 0, 0, 0)),
            pl.BlockSpec((t_len, 128), lambda bi, i: (bi, 0)),
            pl.BlockSpec((t_len, 128), lambda bi, i: (bi, 0)),
            pl.BlockSpec((Q_BLOCK, 128), lambda bi, i: (bi * nb + i, 0)),
        ],
        out_specs=pl.BlockSpec((Q_BLOCK, wa), lambda bi, i: (bi * nb + i, 0)),
        out_shape=jax.ShapeDtypeStruct((b * t_len, wa), BF16),
        scratch_shapes=[
            pltpu.VMEM((t_len, Q_BLOCK), F32),
            pltpu.VMEM((H_IDX // 2 * Q_BLOCK, 128), BF16),
            pltpu.VMEM((HA_KV, gw, HEAD_DIM), BF16),
            pltpu.VMEM((HA_KV, 2, tk // 2, gw), F32),
            pltpu.VMEM((HA_KV, VT_ROWS, gw), F32),
        ],
        compiler_params=_cparams(("arbitrary", "arbitrary")),
        name="attn_a",
    )(proj, proj, proj, vt_a, kie, kio, wlf)


def _attn_b_kernel(q_ref, k_ref, vt_ref, cb_ref, o_ref, lt_scr, acc_scr, *, tq):
    qi = pl.program_id(2)
    tk = tq // 2
    q0 = pl.multiple_of(qi * tq, tq)
    acc_scr[...] = jnp.zeros(acc_scr.shape, F32)
    c0 = cb_ref[pl.ds(q0, 1), :]

    def qk(off, slot, key0=None):
        bias = c0 - cb_ref[pl.ds(off, tk), :]
        s = lax.dot_general(k_ref[pl.ds(off, tk), :], q_ref[...], (((1,), (1,)), ((), ())),
                            preferred_element_type=F32) + jnp.concatenate([bias] * (tq // 128), axis=1)
        if key0 is not None:
            key = key0 + lax.broadcasted_iota(jnp.int32, (tk, 1), 0)
            qry = lax.broadcasted_iota(jnp.int32, (1, tq), 1)
            s = jnp.where(key <= qry, s, NEG)
        lt_scr[slot] = s
        return jnp.max(s, axis=0, keepdims=True)

    def softmax_pv(off, slot, m_old, cmax):
        m_new = jnp.maximum(m_old, cmax)
        alpha = jnp.exp2(m_old - m_new)
        p = jnp.exp2(lt_scr[slot] - m_new).astype(BF16)
        acc_scr[...] = alpha * acc_scr[...] + jnp.dot(vt_ref[:, pl.ds(off, tk)], p, preferred_element_type=F32)
        return m_new

    q1 = pl.multiple_of(q0 + tk, tk)
    m = jnp.full((1, tq), NEG, F32)
    cmax0 = qk(q0, 0, key0=0)
    cmax1 = qk(q1, 1, key0=tk)
    m = softmax_pv(q0, 0, m, cmax0)
    cmax0 = qk(0, 0)
    m = softmax_pv(q1, 1, m, cmax1)
    last = jnp.maximum(qi - 1, 0) * tq

    def pair(j, carry):
        m, cmax0 = carry
        off0 = pl.multiple_of(j * tq, tq)
        off1 = pl.multiple_of(off0 + tk, tk)
        off2 = pl.multiple_of(jnp.minimum(off0 + tq, last), tq)
        cmax1 = qk(off1, 1)
        m = softmax_pv(off0, 0, m, cmax0)
        cmax0 = qk(off2, 0)
        return softmax_pv(off1, 1, m, cmax1), cmax0

    lax.fori_loop(0, qi, pair, (m, cmax0))
    out_t = acc_scr[0:HEAD_DIM, :] / acc_scr[HEAD_DIM:HEAD_DIM + 1, :]
    for g in range(tq // 128):
        o_ref[g * 128:(g + 1) * 128, :] = out_t[:, g * 128:(g + 1) * 128].T.astype(BF16)


def _attn_b(proj, vt_b, cb, b, t_len):
    tq = 512
    nq = t_len // tq
    return pl.pallas_call(
        functools.partial(_attn_b_kernel, tq=tq),
        grid=(b, HB, nq),
        in_specs=[
            pl.BlockSpec((tq, HEAD_DIM), lambda bi, h, i: (bi * nq + i, OFF_QB // HEAD_DIM + h)),
            pl.BlockSpec((t_len, HEAD_DIM), lambda bi, h, i: (bi, OFF_KB // HEAD_DIM + h)),
            pl.BlockSpec((None, None, VT_ROWS, t_len), lambda bi, h, i: (bi, h, 0, 0)),
            pl.BlockSpec((None, None, t_len, 128), lambda bi, h, i: (bi, h, 0, 0)),
        ],
        out_specs=pl.BlockSpec((tq, HEAD_DIM), lambda bi, h, i: (bi * nq + i, h)),
        out_shape=jax.ShapeDtypeStruct((b * t_len, HB * HEAD_DIM), BF16),
        scratch_shapes=[pltpu.VMEM((2, tq // 2, tq), F32), pltpu.VMEM((VT_ROWS, tq), F32)],
        compiler_params=_cparams(("arbitrary", "arbitrary", "arbitrary")),
        name="attn_b",
    )(proj, proj, vt_b, cb)


def _merge_kernel(x_ref, oa_ref, ob_ref, ga_ref, gb_ref, woa_ref, wob_ref, wout_ref, o_ref):
    ya = jnp.dot(oa_ref[...], woa_ref[...], preferred_element_type=F32)
    yb = jnp.dot(ob_ref[...], wob_ref[...], preferred_element_type=F32)
    mixed = ga_ref[...].astype(F32) * ya + gb_ref[...].astype(F32) * yb
    o_ref[...] = x_ref[...] + jnp.dot(mixed.astype(BF16), wout_ref[...], preferred_element_type=F32)


def _merge(x2, out_a, out_b, proj, w_oa, w_ob, w_out, tm):
    m = x2.shape[0]
    wa = HA * HEAD_DIM
    wb = HB * HEAD_DIM
    return pl.pallas_call(
        _merge_kernel,
        grid=(m // tm,),
        in_specs=[
            pl.BlockSpec((tm, D_MODEL), lambda i: (i, 0)),
            pl.BlockSpec((tm, wa), lambda i: (i, 0)),
            pl.BlockSpec((tm, wb), lambda i: (i, 0)),
            pl.BlockSpec((tm, D_MODEL), lambda i: (i, OFF_GA // D_MODEL)),
            pl.BlockSpec((tm, D_MODEL), lambda i: (i, OFF_GB // D_MODEL)),
            _resident((wa, D_MODEL)),
            _resident((wb, D_MODEL)),
            _resident((D_MODEL, D_MODEL)),
        ],
        out_specs=pl.BlockSpec((tm, D_MODEL), lambda i: (i, 0)),
        out_shape=jax.ShapeDtypeStruct((m, D_MODEL), F32),
        compiler_params=_cparams(("arbitrary",)),
        name="merge",
    )(x2, out_a, out_b, proj, proj, w_oa, w_ob, w_out)


def _ffn_kernel(x_ref, g_ref, wg_ref, wu_ref, wd_ref, o_ref, h_scr, acc_scr):
    f = pl.program_id(1)

    @pl.when(f == 0)
    def _():
        h_scr[...] = _rms(x_ref[...], g_ref[...]).astype(BF16)
        acc_scr[...] = jnp.zeros(acc_scr.shape, F32)

    h = h_scr[...]
    gate = jnp.dot(h, wg_ref[...], preferred_element_type=F32)
    up = jnp.dot(h, wu_ref[...], preferred_element_type=F32)
    act = (jax.nn.silu(gate) * up).astype(BF16)
    acc_scr[...] += jnp.dot(act, wd_ref[...], preferred_element_type=F32)

    @pl.when(f == pl.num_programs(1) - 1)
    def _():
        o_ref[...] = x_ref[...] + acc_scr[...]


def _ffn(x2, g, w_gate, w_up, w_down, tm, tf):
    m = x2.shape[0]
    return pl.pallas_call(
        _ffn_kernel,
        grid=(m // tm, D_FF // tf),
        in_specs=[
            pl.BlockSpec((tm, D_MODEL), lambda i, f: (i, 0)),
            pl.BlockSpec((1, D_MODEL), lambda i, f: (0, 0)),
            pl.BlockSpec((D_MODEL, tf), lambda i, f: (0, f)),
            pl.BlockSpec((D_MODEL, tf), lambda i, f: (0, f)),
            pl.BlockSpec((tf, D_MODEL), lambda i, f: (f, 0)),
        ],
        out_specs=pl.BlockSpec((tm, D_MODEL), lambda i, f: (i, 0)),
        out_shape=jax.ShapeDtypeStruct((m, D_MODEL), F32),
        scratch_shapes=[pltpu.VMEM((tm, D_MODEL), BF16), pltpu.VMEM((tm, D_MODEL), F32)],
        compiler_params=_cparams(("arbitrary", "arbitrary")),
        name="ffn",
    )(x2, g, w_gate, w_up, w_down)


def _ple_kernel(x_ref, p_ref, g_ref, gf_ref, wg_ref, wp_ref, o_ref, *, final_norm):
    x = x_ref[...]
    h = _rms(x, g_ref[...]).astype(BF16)
    gate = jax.nn.sigmoid(jnp.dot(h, wg_ref[...], preferred_element_type=F32))
    emb = jnp.dot(p_ref[...].astype(BF16), wp_ref[...], preferred_element_type=F32)
    y = x + gate * emb
    o_ref[...] = _rms(y, gf_ref[...]) if final_norm else y


def _ple(x2, p2, g, g_final, w_gate, w_proj, tm, final_norm):
    m = x2.shape[0]
    return pl.pallas_call(
        functools.partial(_ple_kernel, final_norm=final_norm),
        grid=(m // tm,),
        in_specs=[
            pl.BlockSpec((tm, D_MODEL), lambda i: (i, 0)),
            pl.BlockSpec((tm, D_PLE), lambda i: (i, 0)),
            pl.BlockSpec((1, D_MODEL), lambda i: (0, 0)),
            pl.BlockSpec((1, D_MODEL), lambda i: (0, 0)),
            _resident((D_MODEL, D_MODEL)),
            _resident((D_PLE, D_MODEL)),
        ],
        out_specs=pl.BlockSpec((tm, D_MODEL), lambda i: (i, 0)),
        out_shape=jax.ShapeDtypeStruct((m, D_MODEL), F32),
        compiler_params=_cparams(("arbitrary",)),
        name="ple",
    )(x2, p2, g, g_final, w_gate, w_proj)


def _rope_lane_freqs(rot, period):
    half = rot // 2
    inv = ROPE_THETA ** (-jnp.arange(half, dtype=F32) / half)
    one = jnp.concatenate([inv, inv, jnp.zeros((period - rot,), F32)])
    return jnp.tile(one, 128 // period)[None, :]


def _pack_w_in(w):
    edges = [0] + [int(s) for s in np.cumsum(SPLIT_SIZES)]
    seg = [w[:, edges[k]:edges[k + 1]] for k in range(len(SPLIT_SIZES))]
    q_a, k_a, v_a, q_i, k_i, w_i, q_b, k_b, v_b, f_b, g_a, g_b = seg
    main = jnp.concatenate([g_a, g_b, q_a, q_i, k_a, v_a, q_b, k_b, v_b], axis=1).astype(BF16)
    pad = jnp.zeros((w.shape[0], 128 - D_IDX - H_IDX - HB), w.dtype)
    small = jnp.concatenate([k_i, w_i, f_b, pad], axis=1).astype(BF16)
    return main, small


def _values_transposed(proj, off, heads, b, t_len):
    vt = proj[:, off:off + heads * HEAD_DIM].reshape(b, t_len, heads, HEAD_DIM).transpose(0, 2, 3, 1)
    ones = jnp.ones((b, heads, VT_ROWS - HEAD_DIM, t_len), vt.dtype)
    return jnp.concatenate([vt, ones], axis=2)


def kernel(x, p, positions, g_mix, w_in, b_f, w_o_a, w_o_b, w_out, g_ffn, w_ffn_gate, w_ffn_up,
           w_ffn_down, g_ple, w_ple_gate, w_ple_proj, g_final):
    b, t_len, d = x.shape
    depth = w_in.shape[0]
    m = b * t_len
    top_k = min(TOPK_MAX, t_len // 4)
    inv_a = _rope_lane_freqs(ROT_A, HEAD_DIM)
    inv_i = _rope_lane_freqs(ROT_IDX, D_IDX)
    pos2 = positions.reshape(m, 1)
    x2 = x.reshape(m, d)
    tm_in = min(1024, m)
    tm = min(512, m)
    for i in range(depth):
        w_main, w_small = _pack_w_in(w_in[i])
        bf_row = jnp.zeros((1, 128), F32).at[0, LANE_FB:LANE_FB + HB].set(b_f[i].astype(F32))
        proj, kie, kio, wlf = _inproj(x2, pos2, g_mix[i][None, :], w_main, w_small, bf_row, inv_a, inv_i, tm_in)
        cb = _cumsum(wlf, b, t_len)
        vt_a = _values_transposed(proj, OFF_VA, HA_KV, b, t_len)
        vt_b = _values_transposed(proj, OFF_VB, HB, b, t_len)
        out_a = _attn_a(proj, vt_a, kie, kio, wlf, b, t_len, top_k)
        out_b = _attn_b(proj, vt_b, cb, b, t_len)
        x2 = _merge(x2, out_a, out_b, proj, w_o_a[i].astype(BF16), w_o_b[i].astype(BF16),
                    w_out[i].astype(BF16), tm)
        x2 = _ffn(x2, g_ffn[i][None, :], w_ffn_gate[i].astype(BF16), w_ffn_up[i].astype(BF16),
                  w_ffn_down[i].astype(BF16), tm, 512)
        x2 = _ple(x2, p[i].reshape(m, D_PLE), g_ple[i][None, :], g_final[None, :], w_ple_gate[i].astype(BF16),
                  w_ple_proj[i].astype(BF16), tm, final_norm=(i + 1 == depth))
    return x2.reshape(b, t_len, d)
```

```python
import functools

import jax
import jax.numpy as jnp
import numpy as np
from jax import lax
from jax.experimental import pallas as pl
from jax.experimental.pallas import tpu as pltpu

F32 = jnp.float32
BF16 = jnp.bfloat16

D_MODEL = 2048
HEAD_DIM = 128
HA = 8
HA_KV = 2
H_IDX = 16
D_IDX = 64
TOPK_MAX = 256
HB = 8
ROPE_THETA = 500000.0
ROT_A = HEAD_DIM // 4
ROT_IDX = D_IDX // 4
Q_BLOCK = 128
D_FF = -(-8 * D_MODEL // (3 * 256)) * 256
D_PLE = 256
EPS = 1e-6

SPLIT_SIZES = (HA * HEAD_DIM, HA_KV * HEAD_DIM, HA_KV * HEAD_DIM, H_IDX * D_IDX, D_IDX, H_IDX,
               HB * HEAD_DIM, HB * HEAD_DIM, HB * HEAD_DIM, HB, D_MODEL, D_MODEL)

OFF_GA = 0
OFF_GB = OFF_GA + D_MODEL
OFF_QA = OFF_GB + D_MODEL
OFF_QI = OFF_QA + HA * HEAD_DIM
OFF_KA = OFF_QI + H_IDX * D_IDX
OFF_VA = OFF_KA + HA_KV * HEAD_DIM
OFF_QB = OFF_VA + HA_KV * HEAD_DIM
OFF_KB = OFF_QB + HB * HEAD_DIM
OFF_VB = OFF_KB + HB * HEAD_DIM
N_MAIN = OFF_VB + HB * HEAD_DIM
LANE_KI = 0
LANE_WI = D_IDX
LANE_FB = D_IDX + H_IDX

TN_IN = 512
MXU_N = 256
NEG = -1e30
VMEM_LIMIT = 56 * 1024 * 1024
LOG2E = 1.4426950408889634
ATTN_SCALE = HEAD_DIM ** -0.5 * LOG2E
VT_ROWS = HEAD_DIM + 16
BISECT_ROUND = 4
BISECT_MAX_ITERS = 48


def _cparams(sem):
    return pltpu.CompilerParams(dimension_semantics=sem, vmem_limit_bytes=VMEM_LIMIT)


def _rms(x, g):
    return x * lax.rsqrt(jnp.mean(x * x, axis=-1, keepdims=True) + EPS) * g


def _resident(shape):
    nd = len(shape)
    return pl.BlockSpec(shape, lambda *_: (0,) * nd, pipeline_mode=pl.Buffered(1))


def _rope128(a, c, sa, sb, shift):
    outs = []
    for q in range(a.shape[1] // 128):
        aq = a[:, q * 128:(q + 1) * 128]
        outs.append(aq * c + pltpu.roll(aq, 128 - shift, 1) * sa + pltpu.roll(aq, shift, 1) * sb)
    return outs[0] if len(outs) == 1 else jnp.concatenate(outs, axis=1)


def _inproj_kernel(x_ref, pos_ref, g_ref, w_ref, ws_ref, bf_ref, inv_ref,
                   o_ref, kie_ref, kio_ref, wlf_ref,
                   h_scr, ca_scr, saa_scr, sab_scr, ci_scr, sia_scr, sib_scr):
    j = pl.program_id(1)
    tm = x_ref.shape[0]

    @pl.when(j == 0)
    def _():
        hb = _rms(x_ref[...], g_ref[...]).astype(BF16)
        h_scr[...] = hb
        pos = pos_ref[...].astype(F32)
        lane = lax.broadcasted_iota(jnp.int32, (tm, 128), 1)
        ha, hi = ROT_A // 2, ROT_IDX // 2
        ang = pos * inv_ref[...]
        cs = jnp.cos(ang)
        sn = jnp.sin(ang)

        def at(tab, src, dst):
            return pltpu.roll(tab, (dst - src) % 128, 1)

        def lanes(lo, n):
            return (lane >= lo) & (lane < lo + n)

        ca_scr[...] = jnp.where(lanes(0, ha), cs, jnp.where(lanes(ha, ha), at(cs, 0, ha), 1.0))
        saa_scr[...] = jnp.where(lanes(0, ha), -sn, 0.0)
        sab_scr[...] = jnp.where(lanes(ha, ha), at(sn, 0, ha), 0.0)
        ci = jnp.ones((tm, 128), F32)
        sia = jnp.zeros((tm, 128), F32)
        sib = jnp.zeros((tm, 128), F32)
        for head0 in (0, D_IDX):
            ci = jnp.where(lanes(head0, hi), at(cs, ha, head0), ci)
            ci = jnp.where(lanes(head0 + hi, hi), at(cs, ha, head0 + hi), ci)
            sia = jnp.where(lanes(head0, hi), -at(sn, ha, head0), sia)
            sib = jnp.where(lanes(head0 + hi, hi), at(sn, ha, head0 + hi), sib)
        ci_scr[...] = ci
        sia_scr[...] = sia
        sib_scr[...] = sib
        small = jnp.dot(hb, ws_ref[...], preferred_element_type=F32)
        kr = _rope128(small, ci, sia, sib, ROT_IDX // 2)
        ke = jnp.where(lane < D_IDX, kr, 0.0)
        kie_ref[...] = ke.astype(BF16)
        kio_ref[...] = pltpu.roll(ke, D_IDX, 1).astype(BF16)
        is_w = (lane >= LANE_WI) & (lane < LANE_WI + H_IDX)
        is_f = (lane >= LANE_FB) & (lane < LANE_FB + HB)
        logf = jax.nn.log_sigmoid(small + bf_ref[...])
        wlf_ref[...] = jnp.where(is_w, small * (H_IDX ** -0.5 * D_IDX ** -0.5), jnp.where(is_f, logf, 0.0))

    t = TN_IN
    j_gate_end = OFF_QA // t
    j_qa_end = OFF_QI // t
    j_qi_end = OFF_KA // t
    j_kv = OFF_KA // t
    j_qb = OFF_QB // t
    j_qb_end = OFF_KB // t

    def project(epilogues):
        for c, epilogue in enumerate(epilogues):
            cols = slice(c * MXU_N, (c + 1) * MXU_N)
            acc = jnp.dot(h_scr[...], w_ref[:, cols], preferred_element_type=F32)
            o_ref[:, cols] = epilogue(acc).astype(BF16)

    def rope_a(acc):
        return _rope128(acc, ca_scr[...], saa_scr[...], sab_scr[...], ROT_A // 2)

    def rope_i(acc):
        return _rope128(acc, ci_scr[...], sia_scr[...], sib_scr[...], ROT_IDX // 2)

    groups = t // MXU_N

    @pl.when(j < j_gate_end)
    def _():
        project([jax.nn.sigmoid] * groups)

    @pl.when((j >= j_gate_end) & (j < j_qa_end))
    def _():
        project([lambda acc: rope_a(acc) * ATTN_SCALE] * groups)

    @pl.when((j >= j_qa_end) & (j < j_qi_end))
    def _():
        project([rope_i] * groups)

    @pl.when(j == j_kv)
    def _():
        project([rope_a, lambda acc: acc])

    @pl.when((j >= j_qb) & (j < j_qb_end))
    def _():
        project([lambda acc: acc * ATTN_SCALE] * groups)

    @pl.when(j >= j_qb_end)
    def _():
        project([lambda acc: acc] * groups)


def _inproj(x2, pos2, g, w_main, w_small, bf_row, inv_freq, tm):
    m = x2.shape[0]
    assert OFF_VA - OFF_KA == MXU_N and OFF_QB - OFF_KA == TN_IN
    grid = (m // tm, N_MAIN // TN_IN)
    row128 = pl.BlockSpec((1, 128), lambda i, j: (0, 0))
    return pl.pallas_call(
        _inproj_kernel,
        grid=grid,
        in_specs=[
            pl.BlockSpec((tm, D_MODEL), lambda i, j: (i, 0)),
            pl.BlockSpec((tm, 1), lambda i, j: (i, 0)),
            pl.BlockSpec((1, D_MODEL), lambda i, j: (0, 0)),
            pl.BlockSpec((D_MODEL, TN_IN), lambda i, j: (0, j)),
            pl.BlockSpec((D_MODEL, 128), lambda i, j: (0, 0)),
            row128, row128,
        ],
        out_specs=[
            pl.BlockSpec((tm, TN_IN), lambda i, j: (i, j)),
            pl.BlockSpec((tm, 128), lambda i, j: (i, 0)),
            pl.BlockSpec((tm, 128), lambda i, j: (i, 0)),
            pl.BlockSpec((tm, 128), lambda i, j: (i, 0)),
        ],
        out_shape=[
            jax.ShapeDtypeStruct((m, N_MAIN), BF16),
            jax.ShapeDtypeStruct((m, 128), BF16),
            jax.ShapeDtypeStruct((m, 128), BF16),
            jax.ShapeDtypeStruct((m, 128), F32),
        ],
        scratch_shapes=[pltpu.VMEM((tm, D_MODEL), BF16)] + [pltpu.VMEM((tm, 128), F32)] * 6,
        compiler_params=_cparams(("arbitrary", "arbitrary")),
        name="inproj",
    )(x2, pos2, g, w_main, w_small, bf_row, inv_freq)


def _cumsum_kernel(wlf_ref, o_ref, carry_scr, *, tc):
    @pl.when(pl.program_id(1) == 0)
    def _():
        carry_scr[...] = jnp.zeros_like(carry_scr)

    lf = wlf_ref[...]
    r = lax.broadcasted_iota(jnp.int32, (tc, tc), 0)
    c = lax.broadcasted_iota(jnp.int32, (tc, tc), 1)
    tri = jnp.where(c <= r, 1.0, 0.0).astype(BF16)
    hi = lf.astype(BF16)
    r1 = lf - hi.astype(F32)
    mid = r1.astype(BF16)
    lo = (r1 - mid.astype(F32)).astype(BF16)
    cs = (jnp.dot(tri, hi, preferred_element_type=F32) + jnp.dot(tri, mid, preferred_element_type=F32)
          + jnp.dot(tri, lo, preferred_element_type=F32)) + carry_scr[0:1, :]
    carry_scr[...] = jnp.broadcast_to(cs[tc - 1:tc, :], carry_scr.shape)
    cs2 = cs * LOG2E
    for h in range(HB):
        o_ref[h] = jnp.broadcast_to(cs2[:, LANE_FB + h:LANE_FB + h + 1], (tc, 128))


def _cumsum(wlf, b, t_len):
    tc = 256
    nt = t_len // tc
    return pl.pallas_call(
        functools.partial(_cumsum_kernel, tc=tc),
        grid=(b, nt),
        in_specs=[pl.BlockSpec((tc, 128), lambda bi, ti: (bi * nt + ti, 0))],
        out_specs=pl.BlockSpec((None, HB, tc, 128), lambda bi, ti: (bi, 0, ti, 0)),
        out_shape=jax.ShapeDtypeStruct((b, HB, t_len, 128), F32),
        scratch_shapes=[pltpu.VMEM((8, 128), F32)],
        compiler_params=_cparams(("arbitrary", "arbitrary")),
        name="fcumsum",
    )(wlf)


def _attn_a_kernel(qa_ref, qi_ref, ka_ref, vt_ref, kie_ref, kio_ref, wlf_ref, o_ref,
                   sc_scr, qs_scr, qg_scr, lt_scr, acc_scr, *, tk, top_k, n_iter):
    i = pl.program_id(1)
    nq = Q_BLOCK
    group = HA // HA_KV
    gw = group * nq
    nchunks = ((i + 1) * nq + tk - 1) // tk
    q_pos = i * nq + lax.broadcasted_iota(jnp.int32, (1, nq), 1)
    wt = wlf_ref[...].T
    for p in range(H_IDX // 2):
        qs_scr[p * nq:(p + 1) * nq, :] = qi_ref[:, p * 128:(p + 1) * 128]

    def score_chunk(c, carry):
        rmin, rmax = carry
        off = pl.multiple_of(c * tk, tk)
        qs = qs_scr[...]
        de = lax.dot_general(kie_ref[pl.ds(off, tk), :], qs, (((1,), (1,)), ((), ())), preferred_element_type=F32)
        do = lax.dot_general(kio_ref[pl.ds(off, tk), :], qs, (((1,), (1,)), ((), ())), preferred_element_type=F32)
        s = jnp.zeros((tk, nq), F32)
        for p in range(H_IDX // 2):
            cols = slice(p * nq, (p + 1) * nq)
            s = s + jnp.maximum(de[:, cols], 0.0) * wt[LANE_WI + 2 * p:LANE_WI + 2 * p + 1, :]
            s = s + jnp.maximum(do[:, cols], 0.0) * wt[LANE_WI + 2 * p + 1:LANE_WI + 2 * p + 2, :]
        kpos = off + lax.broadcasted_iota(jnp.int32, (tk, 1), 0)
        valid = kpos <= q_pos
        sc_scr[pl.ds(off, tk), :] = jnp.where(valid, s, NEG)
        rmin = jnp.minimum(rmin, jnp.min(jnp.where(valid, s, -NEG), axis=0, keepdims=True))
        rmax = jnp.maximum(rmax, jnp.max(jnp.where(valid, s, NEG), axis=0, keepdims=True))
        return rmin, rmax

    lo, hi = lax.fori_loop(0, nchunks, score_chunk,
                           (jnp.full((1, nq), -NEG, F32), jnp.full((1, nq), NEG, F32)))

    def bisect(_, carry):
        lo, hi = carry
        mid = 0.5 * (lo + hi)

        def count_chunk(c, part):
            off = pl.multiple_of(c * tk, tk)
            ind = jnp.where(sc_scr[pl.ds(off, tk), :] >= mid, 1.0, 0.0)
            return part + jnp.sum(ind.reshape(tk // 64, 64, nq), axis=0)

        part = lax.fori_loop(0, nchunks, count_chunk, jnp.zeros((64, nq), F32))
        cnt = jnp.sum(part, axis=0, keepdims=True)
        ge = cnt >= top_k
        return jnp.where(ge, mid, lo), jnp.where(cnt == top_k, mid, jnp.where(ge, hi, mid))

    searching = q_pos >= top_k

    def bisect_round(state):
        it, lo, hi, _ = state
        lo, hi = lax.fori_loop(0, BISECT_ROUND, bisect, (lo, hi))
        pending = jnp.sum(jnp.where(searching & (lo < hi), 1.0, 0.0))
        return it + BISECT_ROUND, lo, hi, pending

    _, thr, _, _ = lax.while_loop(lambda st: (st[0] < n_iter) & (st[3] > 0.0), bisect_round,
                                  (jnp.int32(0), lo, hi, jnp.float32(1.0)))

    tka = tk // 2
    last = (nchunks - 1) * tk
    for n in range(HA_KV):
        for g in range(group):
            h = n * group + g
            qg_scr[n, g * nq:(g + 1) * nq, :] = qa_ref[:, h * HEAD_DIM:(h + 1) * HEAD_DIM]
    acc_scr[...] = jnp.zeros(acc_scr.shape, F32)

    def qk(off, slot):
        msk = jnp.where(sc_scr[pl.ds(off, tka), :] >= thr, 0.0, NEG)
        msk = jnp.concatenate([msk] * group, axis=1)
        cmax = []
        for n in range(HA_KV):
            k = ka_ref[pl.ds(off, tka), n * HEAD_DIM:(n + 1) * HEAD_DIM]
            s = lax.dot_general(k, qg_scr[n], (((1,), (1,)), ((), ())), preferred_element_type=F32) + msk
            lt_scr[n, slot] = s
            cmax.append(jnp.max(s, axis=0, keepdims=True))
        return tuple(cmax)

    def softmax_pv(off, slot, m_old, cmax):
        new = []
        for n in range(HA_KV):
            m_new = jnp.maximum(m_old[n], cmax[n])
            alpha = jnp.exp2(m_old[n] - m_new)
            p = jnp.exp2(lt_scr[n, slot] - m_new).astype(BF16)
            acc_scr[n] = alpha * acc_scr[n] + jnp.dot(vt_ref[n, :, pl.ds(off, tka)], p,
                                                      preferred_element_type=F32)
            new.append(m_new)
        return tuple(new)

    def pair(j, carry):
        m, cmax0 = carry
        off0 = pl.multiple_of(j * tk, tk)
        off1 = pl.multiple_of(off0 + tka, tka)
        off2 = pl.multiple_of(jnp.minimum(off0 + tk, last), tk)
        cmax1 = qk(off1, 1)
        m = softmax_pv(off0, 0, m, cmax0)
        cmax0 = qk(off2, 0)
        return softmax_pv(off1, 1, m, cmax1), cmax0

    lax.fori_loop(0, nchunks, pair, ((jnp.full((1, gw), NEG, F32),) * HA_KV, qk(0, 0)))
    for n in range(HA_KV):
        out_t = acc_scr[n, 0:HEAD_DIM, :] / acc_scr[n, HEAD_DIM:HEAD_DIM + 1, :]
        for g in range(group):
            h = n * group + g
            o_ref[:, h * HEAD_DIM:(h + 1) * HEAD_DIM] = out_t[:, g * nq:(g + 1) * nq].T.astype(BF16)


def _attn_a(proj, vt_a, kie, kio, wlf, b, t_len, top_k):
    nb = t_len // Q_BLOCK
    wa = HA * HEAD_DIM
    wkv = HA_KV * HEAD_DIM
    tk = min(512, t_len)
    gw = HA // HA_KV * Q_BLOCK
    kern = functools.partial(_attn_a_kernel, tk=tk, top_k=top_k, n_iter=BISECT_MAX_ITERS)
    return pl.pallas_call(
        kern,
        grid=(b, nb),
        in_specs=[
            pl.BlockSpec((Q_BLOCK, wa), lambda bi, i: (bi * nb + i, OFF_QA // wa)),
            pl.BlockSpec((Q_BLOCK, H_IDX * D_IDX), lambda bi, i: (bi * nb + i, OFF_QI // (H_IDX * D_IDX))),
            pl.BlockSpec((t_len, wkv), lambda bi, i: (bi, OFF_KA // wkv)),
            pl.BlockSpec((None, HA_KV, VT_ROWS, t_len), lambda bi, i: (bi, 0, 0, 0)),
            pl.BlockSpec((t_len, 128), lambda bi, i: (bi, 0)),
            pl.BlockSpec((t_len, 128), lambda bi, i: (bi, 0)),
            pl.BlockSpec((Q_BLOCK, 128), lambda bi, i: (bi * nb + i, 0)),
        ],
        out_specs=pl.BlockSpec((Q_BLOCK, wa), lambda bi, i: (bi * nb + i, 0)),
        out_shape=jax.ShapeDtypeStruct((b * t_len, wa), BF16),
        scratch_shapes=[
            pltpu.VMEM((t_len, Q_BLOCK), F32),
            pltpu.VMEM((H_IDX // 2 * Q_BLOCK, 128), BF16),
            pltpu.VMEM((HA_KV, gw, HEAD_DIM), BF16),
            pltpu.VMEM((HA_KV, 2, tk // 2, gw), F32),
            pltpu.VMEM((HA_KV, VT_ROWS, gw), F32),
        ],
        compiler_params=_cparams(("arbitrary", "arbitrary")),
        name="attn_a",
    )(proj, proj, proj, vt_a, kie, kio, wlf)


def _attn_b_kernel(q_ref, k_ref, vt_ref, cb_ref, o_ref, lt_scr, acc_scr, *, tq):
    qi = pl.program_id(2)
    tk = tq // 2
    q0 = pl.multiple_of(qi * tq, tq)
    acc_scr[...] = jnp.zeros(acc_scr.shape, F32)
    c0 = cb_ref[pl.ds(q0, 1), :]

    def qk(off, slot, key0=None):
        bias = c0 - cb_ref[pl.ds(off, tk), :]
        s = lax.dot_general(k_ref[pl.ds(off, tk), :], q_ref[...], (((1,), (1,)), ((), ())),
                            preferred_element_type=F32) + jnp.concatenate([bias] * (tq // 128), axis=1)
        if key0 is not None:
            key = key0 + lax.broadcasted_iota(jnp.int32, (tk, 1), 0)
            qry = lax.broadcasted_iota(jnp.int32, (1, tq), 1)
            s = jnp.where(key <= qry, s, NEG)
        lt_scr[slot] = s
        return jnp.max(s, axis=0, keepdims=True)

    def softmax_pv(off, slot, m_old, cmax):
        m_new = jnp.maximum(m_old, cmax)
        alpha = jnp.exp2(m_old - m_new)
        p = jnp.exp2(lt_scr[slot] - m_new).astype(BF16)
        acc_scr[...] = alpha * acc_scr[...] + jnp.dot(vt_ref[:, pl.ds(off, tk)], p, preferred_element_type=F32)
        return m_new

    q1 = pl.multiple_of(q0 + tk, tk)
    m = jnp.full((1, tq), NEG, F32)
    cmax0 = qk(q0, 0, key0=0)
    cmax1 = qk(q1, 1, key0=tk)
    m = softmax_pv(q0, 0, m, cmax0)
    cmax0 = qk(0, 0)
    m = softmax_pv(q1, 1, m, cmax1)
    last = jnp.maximum(qi - 1, 0) * tq

    def pair(j, carry):
        m, cmax0 = carry
        off0 = pl.multiple_of(j * tq, tq)
        off1 = pl.multiple_of(off0 + tk, tk)
        off2 = pl.multiple_of(jnp.minimum(off0 + tq, last), tq)
        cmax1 = qk(off1, 1)
        m = softmax_pv(off0, 0, m, cmax0)
        cmax0 = qk(off2, 0)
        return softmax_pv(off1, 1, m, cmax1), cmax0

    lax.fori_loop(0, qi, pair, (m, cmax0))
    out_t = acc_scr[0:HEAD_DIM, :] / acc_scr[HEAD_DIM:HEAD_DIM + 1, :]
    for g in range(tq // 128):
        o_ref[g * 128:(g + 1) * 128, :] = out_t[:, g * 128:(g + 1) * 128].T.astype(BF16)


def _attn_b(proj, vt_b, cb, b, t_len):
    tq = 512
    nq = t_len // tq
    return pl.pallas_call(
        functools.partial(_attn_b_kernel, tq=tq),
        grid=(b, HB, nq),
        in_specs=[
            pl.BlockSpec((tq, HEAD_DIM), lambda bi, h, i: (bi * nq + i, OFF_QB // HEAD_DIM + h)),
            pl.BlockSpec((t_len, HEAD_DIM), lambda bi, h, i: (bi, OFF_KB // HEAD_DIM + h)),
            pl.BlockSpec((None, None, VT_ROWS, t_len), lambda bi, h, i: (bi, h, 0, 0)),
            pl.BlockSpec((None, None, t_len, 128), lambda bi, h, i: (bi, h, 0, 0)),
        ],
        out_specs=pl.BlockSpec((tq, HEAD_DIM), lambda bi, h, i: (bi * nq + i, h)),
        out_shape=jax.ShapeDtypeStruct((b * t_len, HB * HEAD_DIM), BF16),
        scratch_shapes=[pltpu.VMEM((2, tq // 2, tq), F32), pltpu.VMEM((VT_ROWS, tq), F32)],
        compiler_params=_cparams(("arbitrary", "arbitrary", "arbitrary")),
        name="attn_b",
    )(proj, proj, vt_b, cb)


def _merge_kernel(x_ref, oa_ref, ob_ref, ga_ref, gb_ref, woa_ref, wob_ref, wout_ref, o_ref):
    ya = jnp.dot(oa_ref[...], woa_ref[...], preferred_element_type=F32)
    yb = jnp.dot(ob_ref[...], wob_ref[...], preferred_element_type=F32)
    mixed = ga_ref[...].astype(F32) * ya + gb_ref[...].astype(F32) * yb
    o_ref[...] = x_ref[...] + jnp.dot(mixed.astype(BF16), wout_ref[...], preferred_element_type=F32)


def _merge(x2, out_a, out_b, proj, w_oa, w_ob, w_out, tm):
    m = x2.shape[0]
    wa = HA * HEAD_DIM
    wb = HB * HEAD_DIM
    return pl.pallas_call(
        _merge_kernel,
        grid=(m // tm,),
        in_specs=[
            pl.BlockSpec((tm, D_MODEL), lambda i: (i, 0)),
            pl.BlockSpec((tm, wa), lambda i: (i, 0)),
            pl.BlockSpec((tm, wb), lambda i: (i, 0)),
            pl.BlockSpec((tm, D_MODEL), lambda i: (i, OFF_GA // D_MODEL)),
            pl.BlockSpec((tm, D_MODEL), lambda i: (i, OFF_GB // D_MODEL)),
            _resident((wa, D_MODEL)),
            _resident((wb, D_MODEL)),
            _resident((D_MODEL, D_MODEL)),
        ],
        out_specs=pl.BlockSpec((tm, D_MODEL), lambda i: (i, 0)),
        out_shape=jax.ShapeDtypeStruct((m, D_MODEL), F32),
        compiler_params=_cparams(("arbitrary",)),
        name="merge",
    )(x2, out_a, out_b, proj, proj, w_oa, w_ob, w_out)


def _ffn_kernel(x_ref, g_ref, wg_ref, wu_ref, wd_ref, o_ref, h_scr, acc_scr):
    f = pl.program_id(1)

    @pl.when(f == 0)
    def _():
        h_scr[...] = _rms(x_ref[...], g_ref[...]).astype(BF16)
        acc_scr[...] = jnp.zeros(acc_scr.shape, F32)

    h = h_scr[...]
    gate = jnp.dot(h, wg_ref[...], preferred_element_type=F32)
    up = jnp.dot(h, wu_ref[...], preferred_element_type=F32)
    act = (jax.nn.silu(gate) * up).astype(BF16)
    acc_scr[...] += jnp.dot(act, wd_ref[...], preferred_element_type=F32)

    @pl.when(f == pl.num_programs(1) - 1)
    def _():
        o_ref[...] = x_ref[...] + acc_scr[...]


def _ffn(x2, g, w_gate, w_up, w_down, tm, tf):
    m = x2.shape[0]
    return pl.pallas_call(
        _ffn_kernel,
        grid=(m // tm, D_FF // tf),
        in_specs=[
            pl.BlockSpec((tm, D_MODEL), lambda i, f: (i, 0)),
            pl.BlockSpec((1, D_MODEL), lambda i, f: (0, 0)),
            pl.BlockSpec((D_MODEL, tf), lambda i, f: (0, f)),
            pl.BlockSpec((D_MODEL, tf), lambda i, f: (0, f)),
            pl.BlockSpec((tf, D_MODEL), lambda i, f: (f, 0)),
        ],
        out_specs=pl.BlockSpec((tm, D_MODEL), lambda i, f: (i, 0)),
        out_shape=jax.ShapeDtypeStruct((m, D_MODEL), F32),
        scratch_shapes=[pltpu.VMEM((tm, D_MODEL), BF16), pltpu.VMEM((tm, D_MODEL), F32)],
        compiler_params=_cparams(("arbitrary", "arbitrary")),
        name="ffn",
    )(x2, g, w_gate, w_up, w_down)


def _ple_kernel(x_ref, p_ref, g_ref, gf_ref, wg_ref, wp_ref, o_ref, *, final_norm):
    x = x_ref[...]
    h = _rms(x, g_ref[...]).astype(BF16)
    gate = jax.nn.sigmoid(jnp.dot(h, wg_ref[...], preferred_element_type=F32))
    emb = jnp.dot(p_ref[...].astype(BF16), wp_ref[...], preferred_element_type=F32)
    y = x + gate * emb
    o_ref[...] = _rms(y, gf_ref[...]) if final_norm else y


def _ple(x2, p2, g, g_final, w_gate, w_proj, tm, final_norm):
    m = x2.shape[0]
    return pl.pallas_call(
        functools.partial(_ple_kernel, final_norm=final_norm),
        grid=(m // tm,),
        in_specs=[
            pl.BlockSpec((tm, D_MODEL), lambda i: (i, 0)),
            pl.BlockSpec((tm, D_PLE), lambda i: (i, 0)),
            pl.BlockSpec((1, D_MODEL), lambda i: (0, 0)),
            pl.BlockSpec((1, D_MODEL), lambda i: (0, 0)),
            _resident((D_MODEL, D_MODEL)),
            _resident((D_PLE, D_MODEL)),
        ],
        out_specs=pl.BlockSpec((tm, D_MODEL), lambda i: (i, 0)),
        out_shape=jax.ShapeDtypeStruct((m, D_MODEL), F32),
        compiler_params=_cparams(("arbitrary",)),
        name="ple",
    )(x2, p2, g, g_final, w_gate, w_proj)


def _rope_lane_freqs():
    def inv(rot):
        half = rot // 2
        return ROPE_THETA ** (-jnp.arange(half, dtype=F32) / half)
    used = ROT_A // 2 + ROT_IDX // 2
    return jnp.concatenate([inv(ROT_A), inv(ROT_IDX), jnp.zeros((128 - used,), F32)])[None, :]


def _pack_w_in(w):
    w = w.astype(BF16)
    edges = [0] + [int(s) for s in np.cumsum(SPLIT_SIZES)]
    seg = [w[:, edges[k]:edges[k + 1]] for k in range(len(SPLIT_SIZES))]
    q_a, k_a, v_a, q_i, k_i, w_i, q_b, k_b, v_b, f_b, g_a, g_b = seg
    main = jnp.concatenate([g_a, g_b, q_a, q_i, k_a, v_a, q_b, k_b, v_b], axis=1)
    pad = jnp.zeros((w.shape[0], 128 - D_IDX - H_IDX - HB), w.dtype)
    small = jnp.concatenate([k_i, w_i, f_b, pad], axis=1)
    return main, small


def _values_transposed(proj, off, heads, b, t_len):
    vt = proj[:, off:off + heads * HEAD_DIM].reshape(b, t_len, heads, HEAD_DIM).transpose(0, 2, 3, 1)
    ones = jnp.ones((b, heads, VT_ROWS - HEAD_DIM, t_len), vt.dtype)
    return jnp.concatenate([vt, ones], axis=2)


def kernel(x, p, positions, g_mix, w_in, b_f, w_o_a, w_o_b, w_out, g_ffn, w_ffn_gate, w_ffn_up,
           w_ffn_down, g_ple, w_ple_gate, w_ple_proj, g_final):
    b, t_len, d = x.shape
    depth = w_in.shape[0]
    m = b * t_len
    top_k = min(TOPK_MAX, t_len // 4)
    inv_freq = _rope_lane_freqs()
    pos2 = positions.reshape(m, 1)
    x2 = x.reshape(m, d)
    tm_in = min(1024, m)
    tm = min(512, m)
    for i in range(depth):
        w_main, w_small = _pack_w_in(w_in[i])
        bf_row = jnp.zeros((1, 128), F32).at[0, LANE_FB:LANE_FB + HB].set(b_f[i].astype(F32))
        proj, kie, kio, wlf = _inproj(x2, pos2, g_mix[i][None, :], w_main, w_small, bf_row, inv_freq, tm_in)
        cb = _cumsum(wlf, b, t_len)
        vt_a = _values_transposed(proj, OFF_VA, HA_KV, b, t_len)
        vt_b = _values_transposed(proj, OFF_VB, HB, b, t_len)
        out_a = _attn_a(proj, vt_a, kie, kio, wlf, b, t_len, top_k)
        out_b = _attn_b(proj, vt_b, cb, b, t_len)
        x2 = _merge(x2, out_a, out_b, proj, w_o_a[i].astype(BF16), w_o_b[i].astype(BF16),
                    w_out[i].astype(BF16), tm)
        x2 = _ffn(x2, g_ffn[i][None, :], w_ffn_gate[i].astype(BF16), w_ffn_up[i].astype(BF16),
                  w_ffn_down[i].astype(BF16), tm, 512)
        x2 = _ple(x2, p[i].reshape(m, D_PLE), g_ple[i][None, :], g_final[None, :], w_ple_gate[i].astype(BF16),
                  w_ple_proj[i].astype(BF16), tm, final_norm=(i + 1 == depth))
    return x2.reshape(b, t_len, d)
```

```python
import functools

import jax
import jax.numpy as jnp
import numpy as np
from jax import lax
from jax.experimental import pallas as pl
from jax.experimental.pallas import tpu as pltpu

F32 = jnp.float32
BF16 = jnp.bfloat16

D_MODEL = 2048
HEAD_DIM = 128
HA = 8
HA_KV = 2
H_IDX = 16
D_IDX = 64
TOPK_MAX = 256
HB = 8
ROPE_THETA = 500000.0
ROT_A = HEAD_DIM // 4
ROT_IDX = D_IDX // 4
Q_BLOCK = 128
D_FF = -(-8 * D_MODEL // (3 * 256)) * 256
D_PLE = 256
EPS = 1e-6

SPLIT_SIZES = (HA * HEAD_DIM, HA_KV * HEAD_DIM, HA_KV * HEAD_DIM, H_IDX * D_IDX, D_IDX, H_IDX,
               HB * HEAD_DIM, HB * HEAD_DIM, HB * HEAD_DIM, HB, D_MODEL, D_MODEL)

OFF_GA = 0
OFF_GB = OFF_GA + D_MODEL
OFF_QA = OFF_GB + D_MODEL
OFF_QI = OFF_QA + HA * HEAD_DIM
OFF_KA = OFF_QI + H_IDX * D_IDX
OFF_VA = OFF_KA + HA_KV * HEAD_DIM
OFF_QB = OFF_VA + HA_KV * HEAD_DIM
OFF_KB = OFF_QB + HB * HEAD_DIM
OFF_VB = OFF_KB + HB * HEAD_DIM
N_MAIN = OFF_VB + HB * HEAD_DIM
LANE_KI = 0
LANE_WI = D_IDX
LANE_FB = D_IDX + H_IDX

TN_IN = 512
MXU_N = 256
NEG = -1e30
VMEM_LIMIT = 56 * 1024 * 1024
LOG2E = 1.4426950408889634
ATTN_SCALE = HEAD_DIM ** -0.5 * LOG2E
VT_ROWS = HEAD_DIM + 16
HEADS_PER_STEP_B = 2
BISECT_FIRST = 12
BISECT_ROUND = 4
BISECT_MAX_ITERS = 48


def _cparams(sem):
    return pltpu.CompilerParams(dimension_semantics=sem, vmem_limit_bytes=VMEM_LIMIT)


def _rms(x, g):
    return x * lax.rsqrt(jnp.mean(x * x, axis=-1, keepdims=True) + EPS) * g


def _resident(shape):
    nd = len(shape)
    return pl.BlockSpec(shape, lambda *_: (0,) * nd, pipeline_mode=pl.Buffered(1))


def _rope128(a, c, sa, sb, shift):
    outs = []
    for q in range(a.shape[1] // 128):
        aq = a[:, q * 128:(q + 1) * 128]
        outs.append(aq * c + pltpu.roll(aq, 128 - shift, 1) * sa + pltpu.roll(aq, shift, 1) * sb)
    return outs[0] if len(outs) == 1 else jnp.concatenate(outs, axis=1)


def _inproj_kernel(x_ref, pos_ref, g_ref, w_ref, ws_ref, bf_ref, inv_ref,
                   o_ref, kie_ref, kio_ref, wlf_ref,
                   h_scr, ca_scr, saa_scr, sab_scr, ci_scr, sia_scr, sib_scr):
    j = pl.program_id(1)
    tm = x_ref.shape[0]

    @pl.when(j == 0)
    def _():
        hb = _rms(x_ref[...], g_ref[...]).astype(BF16)
        h_scr[...] = hb
        pos = pos_ref[...].astype(F32)
        lane = lax.broadcasted_iota(jnp.int32, (tm, 128), 1)
        ha, hi = ROT_A // 2, ROT_IDX // 2
        ang = pos * inv_ref[...]
        cs = jnp.cos(ang)
        sn = jnp.sin(ang)

        def at(tab, src, dst):
            return pltpu.roll(tab, (dst - src) % 128, 1)

        def lanes(lo, n):
            return (lane >= lo) & (lane < lo + n)

        ca_scr[...] = jnp.where(lanes(0, ha), cs, jnp.where(lanes(ha, ha), at(cs, 0, ha), 1.0))
        saa_scr[...] = jnp.where(lanes(0, ha), -sn, 0.0)
        sab_scr[...] = jnp.where(lanes(ha, ha), at(sn, 0, ha), 0.0)
        ci = jnp.ones((tm, 128), F32)
        sia = jnp.zeros((tm, 128), F32)
        sib = jnp.zeros((tm, 128), F32)
        for head0 in (0, D_IDX):
            ci = jnp.where(lanes(head0, hi), at(cs, ha, head0), ci)
            ci = jnp.where(lanes(head0 + hi, hi), at(cs, ha, head0 + hi), ci)
            sia = jnp.where(lanes(head0, hi), -at(sn, ha, head0), sia)
            sib = jnp.where(lanes(head0 + hi, hi), at(sn, ha, head0 + hi), sib)
        ci_scr[...] = ci
        sia_scr[...] = sia
        sib_scr[...] = sib
        small = jnp.dot(hb, ws_ref[...], preferred_element_type=F32)
        kr = _rope128(small, ci, sia, sib, ROT_IDX // 2)
        ke = jnp.where(lane < D_IDX, kr, 0.0)
        kie_ref[...] = ke.astype(BF16)
        kio_ref[...] = pltpu.roll(ke, D_IDX, 1).astype(BF16)
        is_w = (lane >= LANE_WI) & (lane < LANE_WI + H_IDX)
        is_f = (lane >= LANE_FB) & (lane < LANE_FB + HB)
        logf = jax.nn.log_sigmoid(small + bf_ref[...])
        wlf_ref[...] = jnp.where(is_w, small * (H_IDX ** -0.5 * D_IDX ** -0.5), jnp.where(is_f, logf, 0.0))

    t = TN_IN
    j_gate_end = OFF_QA // t
    j_qa_end = OFF_QI // t
    j_qi_end = OFF_KA // t
    j_kv = OFF_KA // t
    j_qb = OFF_QB // t
    j_qb_end = OFF_KB // t

    def project(epilogues):
        for c, epilogue in enumerate(epilogues):
            cols = slice(c * MXU_N, (c + 1) * MXU_N)
            acc = jnp.dot(h_scr[...], w_ref[:, cols], preferred_element_type=F32)
            o_ref[:, cols] = epilogue(acc).astype(BF16)

    def rope_a(acc):
        return _rope128(acc, ca_scr[...], saa_scr[...], sab_scr[...], ROT_A // 2)

    def rope_i(acc):
        return _rope128(acc, ci_scr[...], sia_scr[...], sib_scr[...], ROT_IDX // 2)

    groups = t // MXU_N

    @pl.when(j < j_gate_end)
    def _():
        project([jax.nn.sigmoid] * groups)

    @pl.when((j >= j_gate_end) & (j < j_qa_end))
    def _():
        project([lambda acc: rope_a(acc) * ATTN_SCALE] * groups)

    @pl.when((j >= j_qa_end) & (j < j_qi_end))
    def _():
        project([rope_i] * groups)

    @pl.when(j == j_kv)
    def _():
        project([rope_a, lambda acc: acc])

    @pl.when((j >= j_qb) & (j < j_qb_end))
    def _():
        project([lambda acc: acc * ATTN_SCALE] * groups)

    @pl.when(j >= j_qb_end)
    def _():
        project([lambda acc: acc] * groups)


def _inproj(x2, pos2, g, w_main, w_small, bf_row, inv_freq, tm):
    m = x2.shape[0]
    assert OFF_VA - OFF_KA == MXU_N and OFF_QB - OFF_KA == TN_IN
    grid = (m // tm, N_MAIN // TN_IN)
    row128 = pl.BlockSpec((1, 128), lambda i, j: (0, 0))
    return pl.pallas_call(
        _inproj_kernel,
        grid=grid,
        in_specs=[
            pl.BlockSpec((tm, D_MODEL), lambda i, j: (i, 0)),
            pl.BlockSpec((tm, 1), lambda i, j: (i, 0)),
            pl.BlockSpec((1, D_MODEL), lambda i, j: (0, 0)),
            pl.BlockSpec((D_MODEL, TN_IN), lambda i, j: (0, j)),
            pl.BlockSpec((D_MODEL, 128), lambda i, j: (0, 0)),
            row128, row128,
        ],
        out_specs=[
            pl.BlockSpec((tm, TN_IN), lambda i, j: (i, j)),
            pl.BlockSpec((tm, 128), lambda i, j: (i, 0)),
            pl.BlockSpec((tm, 128), lambda i, j: (i, 0)),
            pl.BlockSpec((tm, 128), lambda i, j: (i, 0)),
        ],
        out_shape=[
            jax.ShapeDtypeStruct((m, N_MAIN), BF16),
            jax.ShapeDtypeStruct((m, 128), BF16),
            jax.ShapeDtypeStruct((m, 128), BF16),
            jax.ShapeDtypeStruct((m, 128), F32),
        ],
        scratch_shapes=[pltpu.VMEM((tm, D_MODEL), BF16)] + [pltpu.VMEM((tm, 128), F32)] * 6,
        compiler_params=_cparams(("arbitrary", "arbitrary")),
        name="inproj",
    )(x2, pos2, g, w_main, w_small, bf_row, inv_freq)


def _cumsum_kernel(wlf_ref, o_ref, carry_scr, *, tc):
    @pl.when(pl.program_id(1) == 0)
    def _():
        carry_scr[...] = jnp.zeros_like(carry_scr)

    lf = wlf_ref[...]
    r = lax.broadcasted_iota(jnp.int32, (tc, tc), 0)
    c = lax.broadcasted_iota(jnp.int32, (tc, tc), 1)
    tri = jnp.where(c <= r, 1.0, 0.0).astype(BF16)
    hi = lf.astype(BF16)
    r1 = lf - hi.astype(F32)
    mid = r1.astype(BF16)
    lo = (r1 - mid.astype(F32)).astype(BF16)
    cs = (jnp.dot(tri, hi, preferred_element_type=F32) + jnp.dot(tri, mid, preferred_element_type=F32)
          + jnp.dot(tri, lo, preferred_element_type=F32)) + carry_scr[0:1, :]
    carry_scr[...] = jnp.broadcast_to(cs[tc - 1:tc, :], carry_scr.shape)
    cs2 = cs * LOG2E
    for h in range(HB):
        o_ref[h] = jnp.broadcast_to(cs2[:, LANE_FB + h:LANE_FB + h + 1], (tc, 128))


def _cumsum(wlf, b, t_len):
    tc = 256
    nt = t_len // tc
    return pl.pallas_call(
        functools.partial(_cumsum_kernel, tc=tc),
        grid=(b, nt),
        in_specs=[pl.BlockSpec((tc, 128), lambda bi, ti: (bi * nt + ti, 0))],
        out_specs=pl.BlockSpec((None, HB, tc, 128), lambda bi, ti: (bi, 0, ti, 0)),
        out_shape=jax.ShapeDtypeStruct((b, HB, t_len, 128), F32),
        scratch_shapes=[pltpu.VMEM((8, 128), F32)],
        compiler_params=_cparams(("arbitrary", "arbitrary")),
        name="fcumsum",
    )(wlf)


def _attn_a_kernel(qa_ref, qi_ref, ka_ref, vt_ref, kie_ref, kio_ref, wlf_ref, o_ref,
                   sc_scr, qs_scr, qg_scr, lt_scr, acc_scr, *, tk, top_k, n_iter):
    i = pl.program_id(1)
    nq = Q_BLOCK
    group = HA // HA_KV
    gw = group * nq
    nchunks = ((i + 1) * nq + tk - 1) // tk
    q_pos = i * nq + lax.broadcasted_iota(jnp.int32, (1, nq), 1)
    wt = wlf_ref[...].T
    for p in range(H_IDX // 2):
        qs_scr[p * nq:(p + 1) * nq, :] = qi_ref[:, p * 128:(p + 1) * 128]

    def score_chunk(c, carry):
        rmin, rmax = carry
        off = pl.multiple_of(c * tk, tk)
        qs = qs_scr[...]
        de = lax.dot_general(kie_ref[pl.ds(off, tk), :], qs, (((1,), (1,)), ((), ())), preferred_element_type=F32)
        do = lax.dot_general(kio_ref[pl.ds(off, tk), :], qs, (((1,), (1,)), ((), ())), preferred_element_type=F32)
        s = jnp.zeros((tk, nq), F32)
        for p in range(H_IDX // 2):
            cols = slice(p * nq, (p + 1) * nq)
            s = s + jnp.maximum(de[:, cols], 0.0) * wt[LANE_WI + 2 * p:LANE_WI + 2 * p + 1, :]
            s = s + jnp.maximum(do[:, cols], 0.0) * wt[LANE_WI + 2 * p + 1:LANE_WI + 2 * p + 2, :]
        kpos = off + lax.broadcasted_iota(jnp.int32, (tk, 1), 0)
        valid = kpos <= q_pos
        sc_scr[pl.ds(off, tk), :] = jnp.where(valid, s, NEG)
        rmin = jnp.minimum(rmin, jnp.min(jnp.where(valid, s, -NEG), axis=0, keepdims=True))
        rmax = jnp.maximum(rmax, jnp.max(jnp.where(valid, s, NEG), axis=0, keepdims=True))
        return rmin, rmax

    lo, hi = lax.fori_loop(0, nchunks, score_chunk,
                           (jnp.full((1, nq), -NEG, F32), jnp.full((1, nq), NEG, F32)))

    def bisect(_, carry):
        lo, hi = carry
        mid = 0.5 * (lo + hi)

        def count_chunk(c, part):
            off = pl.multiple_of(c * tk, tk)
            ind = jnp.where(sc_scr[pl.ds(off, tk), :] >= mid, 1.0, 0.0)
            return part + jnp.sum(ind.reshape(tk // 64, 64, nq), axis=0)

        part = lax.fori_loop(0, nchunks, count_chunk, jnp.zeros((64, nq), F32))
        cnt = jnp.sum(part, axis=0, keepdims=True)
        ge = cnt >= top_k
        return jnp.where(ge, mid, lo), jnp.where(cnt == top_k, mid, jnp.where(ge, hi, mid))

    searching = q_pos >= top_k

    def bisect_round(state):
        it, lo, hi, _ = state
        lo, hi = lax.fori_loop(0, BISECT_ROUND, bisect, (lo, hi))
        pending = jnp.sum(jnp.where(searching & (lo < hi), 1.0, 0.0))
        return it + BISECT_ROUND, lo, hi, pending

    lo, hi = lax.fori_loop(0, BISECT_FIRST, bisect, (lo, hi))
    _, thr, _, _ = lax.while_loop(lambda st: (st[0] < n_iter) & (st[3] > 0.0), bisect_round,
                                  (jnp.int32(BISECT_FIRST), lo, hi, jnp.float32(1.0)))

    tka = tk // 2
    last = (nchunks - 1) * tk
    for n in range(HA_KV):
        for g in range(group):
            h = n * group + g
            qg_scr[n, g * nq:(g + 1) * nq, :] = qa_ref[:, h * HEAD_DIM:(h + 1) * HEAD_DIM]
    acc_scr[...] = jnp.zeros(acc_scr.shape, F32)

    def qk(off, slot):
        msk = jnp.where(sc_scr[pl.ds(off, tka), :] >= thr, 0.0, NEG)
        msk = jnp.concatenate([msk] * group, axis=1)
        cmax = []
        for n in range(HA_KV):
            k = ka_ref[pl.ds(off, tka), n * HEAD_DIM:(n + 1) * HEAD_DIM]
            s = lax.dot_general(k, qg_scr[n], (((1,), (1,)), ((), ())), preferred_element_type=F32) + msk
            lt_scr[n, slot] = s
            cmax.append(jnp.max(s, axis=0, keepdims=True))
        return tuple(cmax)

    def softmax_pv(off, slot, m_old, cmax):
        new = []
        for n in range(HA_KV):
            m_new = jnp.maximum(m_old[n], cmax[n])
            alpha = jnp.exp2(m_old[n] - m_new)
            p = jnp.exp2(lt_scr[n, slot] - m_new).astype(BF16)
            acc_scr[n] = alpha * acc_scr[n] + jnp.dot(vt_ref[n, :, pl.ds(off, tka)], p,
                                                      preferred_element_type=F32)
            new.append(m_new)
        return tuple(new)

    def pair(j, carry):
        m, cmax0 = carry
        off0 = pl.multiple_of(j * tk, tk)
        off1 = pl.multiple_of(off0 + tka, tka)
        off2 = pl.multiple_of(jnp.minimum(off0 + tk, last), tk)
        cmax1 = qk(off1, 1)
        m = softmax_pv(off0, 0, m, cmax0)
        cmax0 = qk(off2, 0)
        return softmax_pv(off1, 1, m, cmax1), cmax0

    lax.fori_loop(0, nchunks, pair, ((jnp.full((1, gw), NEG, F32),) * HA_KV, qk(0, 0)))
    for n in range(HA_KV):
        out_t = acc_scr[n, 0:HEAD_DIM, :] / acc_scr[n, HEAD_DIM:HEAD_DIM + 1, :]
        for g in range(group):
            h = n * group + g
            o_ref[:, h * HEAD_DIM:(h + 1) * HEAD_DIM] = out_t[:, g * nq:(g + 1) * nq].T.astype(BF16)


def _attn_a(proj, vt_a, kie, kio, wlf, b, t_len, top_k):
    nb = t_len // Q_BLOCK
    wa = HA * HEAD_DIM
    wkv = HA_KV * HEAD_DIM
    tk = min(512, t_len)
    gw = HA // HA_KV * Q_BLOCK
    kern = functools.partial(_attn_a_kernel, tk=tk, top_k=top_k, n_iter=BISECT_MAX_ITERS)
    return pl.pallas_call(
        kern,
        grid=(b, nb),
        in_specs=[
            pl.BlockSpec((Q_BLOCK, wa), lambda bi, i: (bi * nb + i, OFF_QA // wa)),
            pl.BlockSpec((Q_BLOCK, H_IDX * D_IDX), lambda bi, i: (bi * nb + i, OFF_QI // (H_IDX * D_IDX))),
            pl.BlockSpec((t_len, wkv), lambda bi, i: (bi, OFF_KA // wkv)),
            pl.BlockSpec((None, HA_KV, VT_ROWS, t_len), lambda bi, i: (bi, 0, 0, 0)),
            pl.BlockSpec((t_len, 128), lambda bi, i: (bi, 0)),
            pl.BlockSpec((t_len, 128), lambda bi, i: (bi, 0)),
            pl.BlockSpec((Q_BLOCK, 128), lambda bi, i: (bi * nb + i, 0)),
        ],
        out_specs=pl.BlockSpec((Q_BLOCK, wa), lambda bi, i: (bi * nb + i, 0)),
        out_shape=jax.ShapeDtypeStruct((b * t_len, wa), BF16),
        scratch_shapes=[
            pltpu.VMEM((t_len, Q_BLOCK), F32),
            pltpu.VMEM((H_IDX // 2 * Q_BLOCK, 128), BF16),
            pltpu.VMEM((HA_KV, gw, HEAD_DIM), BF16),
            pltpu.VMEM((HA_KV, 2, tk // 2, gw), F32),
            pltpu.VMEM((HA_KV, VT_ROWS, gw), F32),
        ],
        compiler_params=_cparams(("arbitrary", "arbitrary")),
        name="attn_a",
    )(proj, proj, proj, vt_a, kie, kio, wlf)


def _attn_b_kernel(q_ref, k_ref, vt_ref, cb_ref, o_ref, lt_scr, acc_scr, *, tq):
    qi = pl.program_id(2)
    tk = tq // 2
    heads = HEADS_PER_STEP_B
    q0 = pl.multiple_of(qi * tq, tq)
    acc_scr[...] = jnp.zeros(acc_scr.shape, F32)

    def qk(off, slot, key0=None):
        cmax = []
        for n in range(heads):
            cols = slice(n * HEAD_DIM, (n + 1) * HEAD_DIM)
            bias = cb_ref[n, pl.ds(q0, 1), :] - cb_ref[n, pl.ds(off, tk), :]
            s = lax.dot_general(k_ref[pl.ds(off, tk), cols], q_ref[:, cols], (((1,), (1,)), ((), ())),
                                preferred_element_type=F32) + jnp.concatenate([bias] * (tq // 128), axis=1)
            if key0 is not None:
                key = key0 + lax.broadcasted_iota(jnp.int32, (tk, 1), 0)
                qry = lax.broadcasted_iota(jnp.int32, (1, tq), 1)
                s = jnp.where(key <= qry, s, NEG)
            lt_scr[n, slot] = s
            cmax.append(jnp.max(s, axis=0, keepdims=True))
        return tuple(cmax)

    def softmax_pv(off, slot, m_old, cmax):
        new = []
        for n in range(heads):
            m_new = jnp.maximum(m_old[n], cmax[n])
            alpha = jnp.exp2(m_old[n] - m_new)
            p = jnp.exp2(lt_scr[n, slot] - m_new).astype(BF16)
            acc_scr[n] = alpha * acc_scr[n] + jnp.dot(vt_ref[n, :, pl.ds(off, tk)], p,
                                                      preferred_element_type=F32)
            new.append(m_new)
        return tuple(new)

    q1 = pl.multiple_of(q0 + tk, tk)
    m = (jnp.full((1, tq), NEG, F32),) * heads
    cmax0 = qk(q0, 0, key0=0)
    cmax1 = qk(q1, 1, key0=tk)
    m = softmax_pv(q0, 0, m, cmax0)
    cmax0 = qk(0, 0)
    m = softmax_pv(q1, 1, m, cmax1)
    last = jnp.maximum(qi - 1, 0) * tq

    def pair(j, carry):
        m, cmax0 = carry
        off0 = pl.multiple_of(j * tq, tq)
        off1 = pl.multiple_of(off0 + tk, tk)
        off2 = pl.multiple_of(jnp.minimum(off0 + tq, last), tq)
        cmax1 = qk(off1, 1)
        m = softmax_pv(off0, 0, m, cmax0)
        cmax0 = qk(off2, 0)
        return softmax_pv(off1, 1, m, cmax1), cmax0

    lax.fori_loop(0, qi, pair, (m, cmax0))
    for n in range(heads):
        out_t = acc_scr[n, 0:HEAD_DIM, :] / acc_scr[n, HEAD_DIM:HEAD_DIM + 1, :]
        for g in range(tq // 128):
            o_ref[g * 128:(g + 1) * 128, n * HEAD_DIM:(n + 1) * HEAD_DIM] = (
                out_t[:, g * 128:(g + 1) * 128].T.astype(BF16))


def _attn_b(proj, vt_b, cb, b, t_len):
    tq = 512
    nq = t_len // tq
    hps = HEADS_PER_STEP_B
    wh = hps * HEAD_DIM
    return pl.pallas_call(
        functools.partial(_attn_b_kernel, tq=tq),
        grid=(b, HB // hps, nq),
        in_specs=[
            pl.BlockSpec((tq, wh), lambda bi, h, i: (bi * nq + i, OFF_QB // wh + h)),
            pl.BlockSpec((t_len, wh), lambda bi, h, i: (bi, OFF_KB // wh + h)),
            pl.BlockSpec((None, hps, VT_ROWS, t_len), lambda bi, h, i: (bi, h, 0, 0)),
            pl.BlockSpec((None, hps, t_len, 128), lambda bi, h, i: (bi, h, 0, 0)),
        ],
        out_specs=pl.BlockSpec((tq, wh), lambda bi, h, i: (bi * nq + i, h)),
        out_shape=jax.ShapeDtypeStruct((b * t_len, HB * HEAD_DIM), BF16),
        scratch_shapes=[pltpu.VMEM((hps, 2, tq // 2, tq), F32), pltpu.VMEM((hps, VT_ROWS, tq), F32)],
        compiler_params=_cparams(("arbitrary", "arbitrary", "arbitrary")),
        name="attn_b",
    )(proj, proj, vt_b, cb)


def _merge_kernel(x_ref, oa_ref, ob_ref, ga_ref, gb_ref, woa_ref, wob_ref, wout_ref, o_ref):
    ya = jnp.dot(oa_ref[...], woa_ref[...], preferred_element_type=F32)
    yb = jnp.dot(ob_ref[...], wob_ref[...], preferred_element_type=F32)
    mixed = ga_ref[...].astype(F32) * ya + gb_ref[...].astype(F32) * yb
    o_ref[...] = x_ref[...] + jnp.dot(mixed.astype(BF16), wout_ref[...], preferred_element_type=F32)


def _merge(x2, out_a, out_b, proj, w_oa, w_ob, w_out, tm):
    m = x2.shape[0]
    wa = HA * HEAD_DIM
    wb = HB * HEAD_DIM
    return pl.pallas_call(
        _merge_kernel,
        grid=(m // tm,),
        in_specs=[
            pl.BlockSpec((tm, D_MODEL), lambda i: (i, 0)),
            pl.BlockSpec((tm, wa), lambda i: (i, 0)),
            pl.BlockSpec((tm, wb), lambda i: (i, 0)),
            pl.BlockSpec((tm, D_MODEL), lambda i: (i, OFF_GA // D_MODEL)),
            pl.BlockSpec((tm, D_MODEL), lambda i: (i, OFF_GB // D_MODEL)),
            _resident((wa, D_MODEL)),
            _resident((wb, D_MODEL)),
            _resident((D_MODEL, D_MODEL)),
        ],
        out_specs=pl.BlockSpec((tm, D_MODEL), lambda i: (i, 0)),
        out_shape=jax.ShapeDtypeStruct((m, D_MODEL), F32),
        compiler_params=_cparams(("arbitrary",)),
        name="merge",
    )(x2, out_a, out_b, proj, proj, w_oa, w_ob, w_out)


def _ffn_kernel(x_ref, g_ref, wg_ref, wu_ref, wd_ref, o_ref, h_scr, acc_scr):
    f = pl.program_id(1)

    @pl.when(f == 0)
    def _():
        h_scr[...] = _rms(x_ref[...], g_ref[...]).astype(BF16)
        acc_scr[...] = jnp.zeros(acc_scr.shape, F32)

    h = h_scr[...]
    gate = jnp.dot(h, wg_ref[...], preferred_element_type=F32)
    up = jnp.dot(h, wu_ref[...], preferred_element_type=F32)
    act = (jax.nn.silu(gate) * up).astype(BF16)
    acc_scr[...] += jnp.dot(act, wd_ref[...], preferred_element_type=F32)

    @pl.when(f == pl.num_programs(1) - 1)
    def _():
        o_ref[...] = x_ref[...] + acc_scr[...]


def _ffn(x2, g, w_gate, w_up, w_down, tm, tf):
    m = x2.shape[0]
    return pl.pallas_call(
        _ffn_kernel,
        grid=(m // tm, D_FF // tf),
        in_specs=[
            pl.BlockSpec((tm, D_MODEL), lambda i, f: (i, 0)),
            pl.BlockSpec((1, D_MODEL), lambda i, f: (0, 0)),
            pl.BlockSpec((D_MODEL, tf), lambda i, f: (0, f)),
            pl.BlockSpec((D_MODEL, tf), lambda i, f: (0, f)),
            pl.BlockSpec((tf, D_MODEL), lambda i, f: (f, 0)),
        ],
        out_specs=pl.BlockSpec((tm, D_MODEL), lambda i, f: (i, 0)),
        out_shape=jax.ShapeDtypeStruct((m, D_MODEL), F32),
        scratch_shapes=[pltpu.VMEM((tm, D_MODEL), BF16), pltpu.VMEM((tm, D_MODEL), F32)],
        compiler_params=_cparams(("arbitrary", "arbitrary")),
        name="ffn",
    )(x2, g, w_gate, w_up, w_down)


def _ple_kernel(x_ref, p_ref, g_ref, gf_ref, wg_ref, wp_ref, o_ref, *, final_norm):
    x = x_ref[...]
    h = _rms(x, g_ref[...]).astype(BF16)
    gate = jax.nn.sigmoid(jnp.dot(h, wg_ref[...], preferred_element_type=F32))
    emb = jnp.dot(p_ref[...].astype(BF16), wp_ref[...], preferred_element_type=F32)
    y = x + gate * emb
    o_ref[...] = _rms(y, gf_ref[...]) if final_norm else y


def _ple(x2, p2, g, g_final, w_gate, w_proj, tm, final_norm):
    m = x2.shape[0]
    return pl.pallas_call(
        functools.partial(_ple_kernel, final_norm=final_norm),
        grid=(m // tm,),
        in_specs=[
            pl.BlockSpec((tm, D_MODEL), lambda i: (i, 0)),
            pl.BlockSpec((tm, D_PLE), lambda i: (i, 0)),
            pl.BlockSpec((1, D_MODEL), lambda i: (0, 0)),
            pl.BlockSpec((1, D_MODEL), lambda i: (0, 0)),
            _resident((D_MODEL, D_MODEL)),
            _resident((D_PLE, D_MODEL)),
        ],
        out_specs=pl.BlockSpec((tm, D_MODEL), lambda i: (i, 0)),
        out_shape=jax.ShapeDtypeStruct((m, D_MODEL), F32),
        compiler_params=_cparams(("arbitrary",)),
        name="ple",
    )(x2, p2, g, g_final, w_gate, w_proj)


def _rope_lane_freqs():
    def inv(rot):
        half = rot // 2
        return ROPE_THETA ** (-jnp.arange(half, dtype=F32) / half)
    used = ROT_A // 2 + ROT_IDX // 2
    return jnp.concatenate([inv(ROT_A), inv(ROT_IDX), jnp.zeros((128 - used,), F32)])[None, :]


def _pack_w_in(w):
    w = w.astype(BF16)
    edges = [0] + [int(s) for s in np.cumsum(SPLIT_SIZES)]
    seg = [w[:, edges[k]:edges[k + 1]] for k in range(len(SPLIT_SIZES))]
    q_a, k_a, v_a, q_i, k_i, w_i, q_b, k_b, v_b, f_b, g_a, g_b = seg
    main = jnp.concatenate([g_a, g_b, q_a, q_i, k_a, v_a, q_b, k_b, v_b], axis=1)
    pad = jnp.zeros((w.shape[0], 128 - D_IDX - H_IDX - HB), w.dtype)
    small = jnp.concatenate([k_i, w_i, f_b, pad], axis=1)
    return main, small


def _values_transposed(proj, off, heads, b, t_len):
    vt = proj[:, off:off + heads * HEAD_DIM].reshape(b, t_len, heads, HEAD_DIM).transpose(0, 2, 3, 1)
    ones = jnp.ones((b, heads, VT_ROWS - HEAD_DIM, t_len), vt.dtype)
    return jnp.concatenate([vt, ones], axis=2)


def kernel(x, p, positions, g_mix, w_in, b_f, w_o_a, w_o_b, w_out, g_ffn, w_ffn_gate, w_ffn_up,
           w_ffn_down, g_ple, w_ple_gate, w_ple_proj, g_final):
    b, t_len, d = x.shape
    depth = w_in.shape[0]
    m = b * t_len
    top_k = min(TOPK_MAX, t_len // 4)
    inv_freq = _rope_lane_freqs()
    pos2 = positions.reshape(m, 1)
    x2 = x.reshape(m, d)
    tm_in = min(1024, m)
    tm = min(512, m)
    for i in range(depth):
        w_main, w_small = _pack_w_in(w_in[i])
        bf_row = jnp.zeros((1, 128), F32).at[0, LANE_FB:LANE_FB + HB].set(b_f[i].astype(F32))
        proj, kie, kio, wlf = _inproj(x2, pos2, g_mix[i][None, :], w_main, w_small, bf_row, inv_freq, tm_in)
        cb = _cumsum(wlf, b, t_len)
        vt_a = _values_transposed(proj, OFF_VA, HA_KV, b, t_len)
        vt_b = _values_transposed(proj, OFF_VB, HB, b, t_len)
        out_a = _attn_a(proj, vt_a, kie, kio, wlf, b, t_len, top_k)
        out_b = _attn_b(proj, vt_b, cb, b, t_len)
        x2 = _merge(x2, out_a, out_b, proj, w_o_a[i].astype(BF16), w_o_b[i].astype(BF16),
                    w_out[i].astype(BF16), tm)
        x2 = _ffn(x2, g_ffn[i][None, :], w_ffn_gate[i].astype(BF16), w_ffn_up[i].astype(BF16),
                  w_ffn_down[i].astype(BF16), tm, 512)
        x2 = _ple(x2, p[i].reshape(m, D_PLE), g_ple[i][None, :], g_final[None, :], w_ple_gate[i].astype(BF16),
                  w_ple_proj[i].astype(BF16), tm, final_norm=(i + 1 == depth))
    return x2.reshape(b, t_len, d)
```

```python
import functools

import jax
import jax.numpy as jnp
import numpy as np
from jax import lax
from jax.experimental import pallas as pl
from jax.experimental.pallas import tpu as pltpu

F32 = jnp.float32
BF16 = jnp.bfloat16

D_MODEL = 2048
HEAD_DIM = 128
HA = 8
HA_KV = 2
H_IDX = 16
D_IDX = 64
TOPK_MAX = 256
HB = 8
ROPE_THETA = 500000.0
ROT_A = HEAD_DIM // 4
ROT_IDX = D_IDX // 4
Q_BLOCK = 128
D_FF = -(-8 * D_MODEL // (3 * 256)) * 256
D_PLE = 256
EPS = 1e-6

SPLIT_SIZES = (HA * HEAD_DIM, HA_KV * HEAD_DIM, HA_KV * HEAD_DIM, H_IDX * D_IDX, D_IDX, H_IDX,
               HB * HEAD_DIM, HB * HEAD_DIM, HB * HEAD_DIM, HB, D_MODEL, D_MODEL)

OFF_GA = 0
OFF_GB = OFF_GA + D_MODEL
OFF_QA = OFF_GB + D_MODEL
OFF_QI = OFF_QA + HA * HEAD_DIM
OFF_KA = OFF_QI + H_IDX * D_IDX
OFF_VA = OFF_KA + HA_KV * HEAD_DIM
OFF_QB = OFF_VA + HA_KV * HEAD_DIM
OFF_KB = OFF_QB + HB * HEAD_DIM
OFF_VB = OFF_KB + HB * HEAD_DIM
N_MAIN = OFF_VB + HB * HEAD_DIM
LANE_KI = 0
LANE_WI = D_IDX
LANE_FB = D_IDX + H_IDX

TN_IN = 512
MXU_N = 256
NEG = -1e30
VMEM_LIMIT = 56 * 1024 * 1024
LOG2E = 1.4426950408889634
ATTN_SCALE = HEAD_DIM ** -0.5 * LOG2E
VT_ROWS = HEAD_DIM + 16
HEADS_PER_STEP_B = 4
BISECT_FIRST = 12
BISECT_ROUND = 4
BISECT_MAX_ITERS = 48


def _cparams(sem):
    return pltpu.CompilerParams(dimension_semantics=sem, vmem_limit_bytes=VMEM_LIMIT)


def _rms(x, g):
    return x * lax.rsqrt(jnp.mean(x * x, axis=-1, keepdims=True) + EPS) * g


def _resident(shape):
    nd = len(shape)
    return pl.BlockSpec(shape, lambda *_: (0,) * nd, pipeline_mode=pl.Buffered(1))


def _rope128(a, c, sa, sb, shift):
    outs = []
    for q in range(a.shape[1] // 128):
        aq = a[:, q * 128:(q + 1) * 128]
        outs.append(aq * c + pltpu.roll(aq, 128 - shift, 1) * sa + pltpu.roll(aq, shift, 1) * sb)
    return outs[0] if len(outs) == 1 else jnp.concatenate(outs, axis=1)


def _inproj_kernel(x_ref, pos_ref, g_ref, w_ref, ws_ref, bf_ref, inv_ref,
                   o_ref, kie_ref, kio_ref, wlf_ref,
                   h_scr, ca_scr, saa_scr, sab_scr, ci_scr, sia_scr, sib_scr):
    j = pl.program_id(1)
    tm = x_ref.shape[0]

    @pl.when(j == 0)
    def _():
        hb = _rms(x_ref[...], g_ref[...]).astype(BF16)
        h_scr[...] = hb
        pos = pos_ref[...].astype(F32)
        lane = lax.broadcasted_iota(jnp.int32, (tm, 128), 1)
        ha, hi = ROT_A // 2, ROT_IDX // 2
        ang = pos * inv_ref[...]
        cs = jnp.cos(ang)
        sn = jnp.sin(ang)

        def at(tab, src, dst):
            return pltpu.roll(tab, (dst - src) % 128, 1)

        def lanes(lo, n):
            return (lane >= lo) & (lane < lo + n)

        ca_scr[...] = jnp.where(lanes(0, ha), cs, jnp.where(lanes(ha, ha), at(cs, 0, ha), 1.0))
        saa_scr[...] = jnp.where(lanes(0, ha), -sn, 0.0)
        sab_scr[...] = jnp.where(lanes(ha, ha), at(sn, 0, ha), 0.0)
        ci = jnp.ones((tm, 128), F32)
        sia = jnp.zeros((tm, 128), F32)
        sib = jnp.zeros((tm, 128), F32)
        for head0 in (0, D_IDX):
            ci = jnp.where(lanes(head0, hi), at(cs, ha, head0), ci)
            ci = jnp.where(lanes(head0 + hi, hi), at(cs, ha, head0 + hi), ci)
            sia = jnp.where(lanes(head0, hi), -at(sn, ha, head0), sia)
            sib = jnp.where(lanes(head0 + hi, hi), at(sn, ha, head0 + hi), sib)
        ci_scr[...] = ci
        sia_scr[...] = sia
        sib_scr[...] = sib
        small = jnp.dot(hb, ws_ref[...], preferred_element_type=F32)
        kr = _rope128(small, ci, sia, sib, ROT_IDX // 2)
        ke = jnp.where(lane < D_IDX, kr, 0.0)
        kie_ref[...] = ke.astype(BF16)
        kio_ref[...] = pltpu.roll(ke, D_IDX, 1).astype(BF16)
        is_w = (lane >= LANE_WI) & (lane < LANE_WI + H_IDX)
        is_f = (lane >= LANE_FB) & (lane < LANE_FB + HB)
        logf = jax.nn.log_sigmoid(small + bf_ref[...])
        wlf_ref[...] = jnp.where(is_w, small * (H_IDX ** -0.5 * D_IDX ** -0.5), jnp.where(is_f, logf, 0.0))

    t = TN_IN
    j_gate_end = OFF_QA // t
    j_qa_end = OFF_QI // t
    j_qi_end = OFF_KA // t
    j_kv = OFF_KA // t
    j_qb = OFF_QB // t
    j_qb_end = OFF_KB // t

    def project(epilogues):
        for c, epilogue in enumerate(epilogues):
            cols = slice(c * MXU_N, (c + 1) * MXU_N)
            acc = jnp.dot(h_scr[...], w_ref[:, cols], preferred_element_type=F32)
            o_ref[:, cols] = epilogue(acc).astype(BF16)

    def rope_a(acc):
        return _rope128(acc, ca_scr[...], saa_scr[...], sab_scr[...], ROT_A // 2)

    def rope_i(acc):
        return _rope128(acc, ci_scr[...], sia_scr[...], sib_scr[...], ROT_IDX // 2)

    groups = t // MXU_N

    @pl.when(j < j_gate_end)
    def _():
        project([jax.nn.sigmoid] * groups)

    @pl.when((j >= j_gate_end) & (j < j_qa_end))
    def _():
        project([lambda acc: rope_a(acc) * ATTN_SCALE] * groups)

    @pl.when((j >= j_qa_end) & (j < j_qi_end))
    def _():
        project([rope_i] * groups)

    @pl.when(j == j_kv)
    def _():
        project([rope_a, lambda acc: acc])

    @pl.when((j >= j_qb) & (j < j_qb_end))
    def _():
        project([lambda acc: acc * ATTN_SCALE] * groups)

    @pl.when(j >= j_qb_end)
    def _():
        project([lambda acc: acc] * groups)


def _inproj(x2, pos2, g, w_main, w_small, bf_row, inv_freq, tm):
    m = x2.shape[0]
    assert OFF_VA - OFF_KA == MXU_N and OFF_QB - OFF_KA == TN_IN
    grid = (m // tm, N_MAIN // TN_IN)
    row128 = pl.BlockSpec((1, 128), lambda i, j: (0, 0))
    return pl.pallas_call(
        _inproj_kernel,
        grid=grid,
        in_specs=[
            pl.BlockSpec((tm, D_MODEL), lambda i, j: (i, 0)),
            pl.BlockSpec((tm, 1), lambda i, j: (i, 0)),
            pl.BlockSpec((1, D_MODEL), lambda i, j: (0, 0)),
            pl.BlockSpec((D_MODEL, TN_IN), lambda i, j: (0, j)),
            pl.BlockSpec((D_MODEL, 128), lambda i, j: (0, 0)),
            row128, row128,
        ],
        out_specs=[
            pl.BlockSpec((tm, TN_IN), lambda i, j: (i, j)),
            pl.BlockSpec((tm, 128), lambda i, j: (i, 0)),
            pl.BlockSpec((tm, 128), lambda i, j: (i, 0)),
            pl.BlockSpec((tm, 128), lambda i, j: (i, 0)),
        ],
        out_shape=[
            jax.ShapeDtypeStruct((m, N_MAIN), BF16),
            jax.ShapeDtypeStruct((m, 128), BF16),
            jax.ShapeDtypeStruct((m, 128), BF16),
            jax.ShapeDtypeStruct((m, 128), F32),
        ],
        scratch_shapes=[pltpu.VMEM((tm, D_MODEL), BF16)] + [pltpu.VMEM((tm, 128), F32)] * 6,
        compiler_params=_cparams(("arbitrary", "arbitrary")),
        name="inproj",
    )(x2, pos2, g, w_main, w_small, bf_row, inv_freq)


def _cumsum_kernel(wlf_ref, o_ref, carry_scr, *, tc):
    @pl.when(pl.program_id(1) == 0)
    def _():
        carry_scr[...] = jnp.zeros_like(carry_scr)

    lf = wlf_ref[...]
    r = lax.broadcasted_iota(jnp.int32, (tc, tc), 0)
    c = lax.broadcasted_iota(jnp.int32, (tc, tc), 1)
    tri = jnp.where(c <= r, 1.0, 0.0).astype(BF16)
    hi = lf.astype(BF16)
    r1 = lf - hi.astype(F32)
    mid = r1.astype(BF16)
    lo = (r1 - mid.astype(F32)).astype(BF16)
    cs = (jnp.dot(tri, hi, preferred_element_type=F32) + jnp.dot(tri, mid, preferred_element_type=F32)
          + jnp.dot(tri, lo, preferred_element_type=F32)) + carry_scr[0:1, :]
    carry_scr[...] = jnp.broadcast_to(cs[tc - 1:tc, :], carry_scr.shape)
    cs2 = cs * LOG2E
    for h in range(HB):
        o_ref[h] = jnp.broadcast_to(cs2[:, LANE_FB + h:LANE_FB + h + 1], (tc, 128))


def _cumsum(wlf, b, t_len):
    tc = 256
    nt = t_len // tc
    return pl.pallas_call(
        functools.partial(_cumsum_kernel, tc=tc),
        grid=(b, nt),
        in_specs=[pl.BlockSpec((tc, 128), lambda bi, ti: (bi * nt + ti, 0))],
        out_specs=pl.BlockSpec((None, HB, tc, 128), lambda bi, ti: (bi, 0, ti, 0)),
        out_shape=jax.ShapeDtypeStruct((b, HB, t_len, 128), F32),
        scratch_shapes=[pltpu.VMEM((8, 128), F32)],
        compiler_params=_cparams(("arbitrary", "arbitrary")),
        name="fcumsum",
    )(wlf)


def _attn_a_kernel(qa_ref, qi_ref, ka_ref, vt_ref, kie_ref, kio_ref, wlf_ref, o_ref,
                   sc_scr, qs_scr, qg_scr, lt_scr, acc_scr, *, tk, top_k, n_iter):
    i = pl.program_id(1)
    nq = Q_BLOCK
    group = HA // HA_KV
    gw = group * nq
    nchunks = ((i + 1) * nq + tk - 1) // tk
    q_pos = i * nq + lax.broadcasted_iota(jnp.int32, (1, nq), 1)
    wt = wlf_ref[...].T
    for p in range(H_IDX // 2):
        qs_scr[p * nq:(p + 1) * nq, :] = qi_ref[:, p * 128:(p + 1) * 128]

    def score_chunk(c, carry):
        rmin, rmax = carry
        off = pl.multiple_of(c * tk, tk)
        ke = kie_ref[pl.ds(off, tk), :]
        ko = kio_ref[pl.ds(off, tk), :]
        s = jnp.zeros((tk, nq), F32)
        pairs_per_dot = MXU_N // nq
        for pg in range(H_IDX // 2 // pairs_per_dot):
            qs = qs_scr[pg * MXU_N:(pg + 1) * MXU_N, :]
            de = lax.dot_general(ke, qs, (((1,), (1,)), ((), ())), preferred_element_type=F32)
            do = lax.dot_general(ko, qs, (((1,), (1,)), ((), ())), preferred_element_type=F32)
            for pp in range(pairs_per_dot):
                p = pg * pairs_per_dot + pp
                cols = slice(pp * nq, (pp + 1) * nq)
                s = s + jnp.maximum(de[:, cols], 0.0) * wt[LANE_WI + 2 * p:LANE_WI + 2 * p + 1, :]
                s = s + jnp.maximum(do[:, cols], 0.0) * wt[LANE_WI + 2 * p + 1:LANE_WI + 2 * p + 2, :]
        kpos = off + lax.broadcasted_iota(jnp.int32, (tk, 1), 0)
        valid = kpos <= q_pos
        sc_scr[pl.ds(off, tk), :] = jnp.where(valid, s, NEG)
        rmin = jnp.minimum(rmin, jnp.min(jnp.where(valid, s, -NEG), axis=0, keepdims=True))
        rmax = jnp.maximum(rmax, jnp.max(jnp.where(valid, s, NEG), axis=0, keepdims=True))
        return rmin, rmax

    lo, hi = lax.fori_loop(0, nchunks, score_chunk,
                           (jnp.full((1, nq), -NEG, F32), jnp.full((1, nq), NEG, F32)))

    def bisect(_, carry):
        lo, hi = carry
        mid = 0.5 * (lo + hi)

        def count_chunk(c, part):
            off = pl.multiple_of(c * tk, tk)
            ind = jnp.where(sc_scr[pl.ds(off, tk), :] >= mid, 1.0, 0.0)
            return part + jnp.sum(ind.reshape(tk // 64, 64, nq), axis=0)

        part = lax.fori_loop(0, nchunks, count_chunk, jnp.zeros((64, nq), F32))
        cnt = jnp.sum(part, axis=0, keepdims=True)
        ge = cnt >= top_k
        return jnp.where(ge, mid, lo), jnp.where(cnt == top_k, mid, jnp.where(ge, hi, mid))

    searching = q_pos >= top_k

    def bisect_round(state):
        it, lo, hi, _ = state
        lo, hi = lax.fori_loop(0, BISECT_ROUND, bisect, (lo, hi))
        pending = jnp.sum(jnp.where(searching & (lo < hi), 1.0, 0.0))
        return it + BISECT_ROUND, lo, hi, pending

    lo, hi = lax.fori_loop(0, BISECT_FIRST, bisect, (lo, hi))
    _, thr, _, _ = lax.while_loop(lambda st: (st[0] < n_iter) & (st[3] > 0.0), bisect_round,
                                  (jnp.int32(BISECT_FIRST), lo, hi, jnp.float32(1.0)))

    tka = tk // 2
    last = (nchunks - 1) * tk
    for n in range(HA_KV):
        for g in range(group):
            h = n * group + g
            qg_scr[n, g * nq:(g + 1) * nq, :] = qa_ref[:, h * HEAD_DIM:(h + 1) * HEAD_DIM]
    acc_scr[...] = jnp.zeros(acc_scr.shape, F32)

    def qk(off, slot):
        msk = jnp.where(sc_scr[pl.ds(off, tka), :] >= thr, 0.0, NEG)
        msk = jnp.concatenate([msk] * group, axis=1)
        cmax = []
        for n in range(HA_KV):
            k = ka_ref[pl.ds(off, tka), n * HEAD_DIM:(n + 1) * HEAD_DIM]
            s = lax.dot_general(k, qg_scr[n], (((1,), (1,)), ((), ())), preferred_element_type=F32) + msk
            lt_scr[n, slot] = s
            cmax.append(jnp.max(s, axis=0, keepdims=True))
        return tuple(cmax)

    def softmax_pv(off, slot, m_old, cmax):
        new = []
        for n in range(HA_KV):
            m_new = jnp.maximum(m_old[n], cmax[n])
            alpha = jnp.exp2(m_old[n] - m_new)
            p = jnp.exp2(lt_scr[n, slot] - m_new).astype(BF16)
            acc_scr[n] = alpha * acc_scr[n] + jnp.dot(vt_ref[n, :, pl.ds(off, tka)], p,
                                                      preferred_element_type=F32)
            new.append(m_new)
        return tuple(new)

    def pair(j, carry):
        m, cmax0 = carry
        off0 = pl.multiple_of(j * tk, tk)
        off1 = pl.multiple_of(off0 + tka, tka)
        off2 = pl.multiple_of(jnp.minimum(off0 + tk, last), tk)
        cmax1 = qk(off1, 1)
        m = softmax_pv(off0, 0, m, cmax0)
        cmax0 = qk(off2, 0)
        return softmax_pv(off1, 1, m, cmax1), cmax0

    lax.fori_loop(0, nchunks, pair, ((jnp.full((1, gw), NEG, F32),) * HA_KV, qk(0, 0)))
    for n in range(HA_KV):
        out_t = acc_scr[n, 0:HEAD_DIM, :] / acc_scr[n, HEAD_DIM:HEAD_DIM + 1, :]
        for g in range(group):
            h = n * group + g
            o_ref[:, h * HEAD_DIM:(h + 1) * HEAD_DIM] = out_t[:, g * nq:(g + 1) * nq].T.astype(BF16)


def _attn_a(proj, vt_a, kie, kio, wlf, b, t_len, top_k):
    nb = t_len // Q_BLOCK
    wa = HA * HEAD_DIM
    wkv = HA_KV * HEAD_DIM
    tk = min(512, t_len)
    gw = HA // HA_KV * Q_BLOCK
    kern = functools.partial(_attn_a_kernel, tk=tk, top_k=top_k, n_iter=BISECT_MAX_ITERS)
    return pl.pallas_call(
        kern,
        grid=(b, nb),
        in_specs=[
            pl.BlockSpec((Q_BLOCK, wa), lambda bi, i: (bi * nb + i, OFF_QA // wa)),
            pl.BlockSpec((Q_BLOCK, H_IDX * D_IDX), lambda bi, i: (bi * nb + i, OFF_QI // (H_IDX * D_IDX))),
            pl.BlockSpec((t_len, wkv), lambda bi, i: (bi, OFF_KA // wkv)),
            pl.BlockSpec((None, HA_KV, VT_ROWS, t_len), lambda bi, i: (bi, 0, 0, 0)),
            pl.BlockSpec((t_len, 128), lambda bi, i: (bi, 0)),
            pl.BlockSpec((t_len, 128), lambda bi, i: (bi, 0)),
            pl.BlockSpec((Q_BLOCK, 128), lambda bi, i: (bi * nb + i, 0)),
        ],
        out_specs=pl.BlockSpec((Q_BLOCK, wa), lambda bi, i: (bi * nb + i, 0)),
        out_shape=jax.ShapeDtypeStruct((b * t_len, wa), BF16),
        scratch_shapes=[
            pltpu.VMEM((t_len, Q_BLOCK), F32),
            pltpu.VMEM((H_IDX // 2 * Q_BLOCK, 128), BF16),
            pltpu.VMEM((HA_KV, gw, HEAD_DIM), BF16),
            pltpu.VMEM((HA_KV, 2, tk // 2, gw), F32),
            pltpu.VMEM((HA_KV, VT_ROWS, gw), F32),
        ],
        compiler_params=_cparams(("arbitrary", "arbitrary")),
        name="attn_a",
    )(proj, proj, proj, vt_a, kie, kio, wlf)


def _attn_b_kernel(q_ref, k_ref, vt_ref, cb_ref, o_ref, lt_scr, acc_scr, *, tq):
    qi = pl.program_id(2)
    tk = tq // 2
    heads = HEADS_PER_STEP_B
    q0 = pl.multiple_of(qi * tq, tq)
    acc_scr[...] = jnp.zeros(acc_scr.shape, F32)

    def qk(off, slot, key0=None):
        cmax = []
        for n in range(heads):
            cols = slice(n * HEAD_DIM, (n + 1) * HEAD_DIM)
            bias = cb_ref[n, pl.ds(q0, 1), :] - cb_ref[n, pl.ds(off, tk), :]
            s = lax.dot_general(k_ref[pl.ds(off, tk), cols], q_ref[:, cols], (((1,), (1,)), ((), ())),
                                preferred_element_type=F32) + jnp.concatenate([bias] * (tq // 128), axis=1)
            if key0 is not None:
                key = key0 + lax.broadcasted_iota(jnp.int32, (tk, 1), 0)
                qry = lax.broadcasted_iota(jnp.int32, (1, tq), 1)
                s = jnp.where(key <= qry, s, NEG)
            lt_scr[n, slot] = s
            cmax.append(jnp.max(s, axis=0, keepdims=True))
        return tuple(cmax)

    def softmax_pv(off, slot, m_old, cmax):
        new = []
        for n in range(heads):
            m_new = jnp.maximum(m_old[n], cmax[n])
            alpha = jnp.exp2(m_old[n] - m_new)
            p = jnp.exp2(lt_scr[n, slot] - m_new).astype(BF16)
            acc_scr[n] = alpha * acc_scr[n] + jnp.dot(vt_ref[n, :, pl.ds(off, tk)], p,
                                                      preferred_element_type=F32)
            new.append(m_new)
        return tuple(new)

    q1 = pl.multiple_of(q0 + tk, tk)
    m = (jnp.full((1, tq), NEG, F32),) * heads
    cmax0 = qk(q0, 0, key0=0)
    cmax1 = qk(q1, 1, key0=tk)
    m = softmax_pv(q0, 0, m, cmax0)
    cmax0 = qk(0, 0)
    m = softmax_pv(q1, 1, m, cmax1)
    last = jnp.maximum(qi - 1, 0) * tq

    def pair(j, carry):
        m, cmax0 = carry
        off0 = pl.multiple_of(j * tq, tq)
        off1 = pl.multiple_of(off0 + tk, tk)
        off2 = pl.multiple_of(jnp.minimum(off0 + tq, last), tq)
        cmax1 = qk(off1, 1)
        m = softmax_pv(off0, 0, m, cmax0)
        cmax0 = qk(off2, 0)
        return softmax_pv(off1, 1, m, cmax1), cmax0

    lax.fori_loop(0, qi, pair, (m, cmax0))
    for n in range(heads):
        out_t = acc_scr[n, 0:HEAD_DIM, :] / acc_scr[n, HEAD_DIM:HEAD_DIM + 1, :]
        for g in range(tq // 128):
            o_ref[g * 128:(g + 1) * 128, n * HEAD_DIM:(n + 1) * HEAD_DIM] = (
                out_t[:, g * 128:(g + 1) * 128].T.astype(BF16))


def _attn_b(proj, vt_b, cb, b, t_len):
    tq = 512
    nq = t_len // tq
    hps = HEADS_PER_STEP_B
    wh = hps * HEAD_DIM
    return pl.pallas_call(
        functools.partial(_attn_b_kernel, tq=tq),
        grid=(b, HB // hps, nq),
        in_specs=[
            pl.BlockSpec((tq, wh), lambda bi, h, i: (bi * nq + i, OFF_QB // wh + h)),
            pl.BlockSpec((t_len, wh), lambda bi, h, i: (bi, OFF_KB // wh + h)),
            pl.BlockSpec((None, hps, VT_ROWS, t_len), lambda bi, h, i: (bi, h, 0, 0)),
            pl.BlockSpec((None, hps, t_len, 128), lambda bi, h, i: (bi, h, 0, 0)),
        ],
        out_specs=pl.BlockSpec((tq, wh), lambda bi, h, i: (bi * nq + i, h)),
        out_shape=jax.ShapeDtypeStruct((b * t_len, HB * HEAD_DIM), BF16),
        scratch_shapes=[pltpu.VMEM((hps, 2, tq // 2, tq), F32), pltpu.VMEM((hps, VT_ROWS, tq), F32)],
        compiler_params=_cparams(("arbitrary", "arbitrary", "arbitrary")),
        name="attn_b",
    )(proj, proj, vt_b, cb)


def _merge_kernel(x_ref, oa_ref, ob_ref, ga_ref, gb_ref, woa_ref, wob_ref, wout_ref, o_ref):
    ya = jnp.dot(oa_ref[...], woa_ref[...], preferred_element_type=F32)
    yb = jnp.dot(ob_ref[...], wob_ref[...], preferred_element_type=F32)
    mixed = ga_ref[...].astype(F32) * ya + gb_ref[...].astype(F32) * yb
    o_ref[...] = x_ref[...] + jnp.dot(mixed.astype(BF16), wout_ref[...], preferred_element_type=F32)


def _merge(x2, out_a, out_b, proj, w_oa, w_ob, w_out, tm):
    m = x2.shape[0]
    wa = HA * HEAD_DIM
    wb = HB * HEAD_DIM
    return pl.pallas_call(
        _merge_kernel,
        grid=(m // tm,),
        in_specs=[
            pl.BlockSpec((tm, D_MODEL), lambda i: (i, 0)),
            pl.BlockSpec((tm, wa), lambda i: (i, 0)),
            pl.BlockSpec((tm, wb), lambda i: (i, 0)),
            pl.BlockSpec((tm, D_MODEL), lambda i: (i, OFF_GA // D_MODEL)),
            pl.BlockSpec((tm, D_MODEL), lambda i: (i, OFF_GB // D_MODEL)),
            _resident((wa, D_MODEL)),
            _resident((wb, D_MODEL)),
            _resident((D_MODEL, D_MODEL)),
        ],
        out_specs=pl.BlockSpec((tm, D_MODEL), lambda i: (i, 0)),
        out_shape=jax.ShapeDtypeStruct((m, D_MODEL), F32),
        compiler_params=_cparams(("arbitrary",)),
        name="merge",
    )(x2, out_a, out_b, proj, proj, w_oa, w_ob, w_out)


def _ffn_kernel(x_ref, g_ref, wg_ref, wu_ref, wd_ref, o_ref, h_scr):
    @pl.when(pl.program_id(1) == 0)
    def _():
        x = x_ref[...]
        h_scr[...] = _rms(x, g_ref[...]).astype(BF16)
        o_ref[...] = x

    h = h_scr[...]
    gate = jnp.dot(h, wg_ref[...], preferred_element_type=F32)
    up = jnp.dot(h, wu_ref[...], preferred_element_type=F32)
    act = (jax.nn.silu(gate) * up).astype(BF16)
    o_ref[...] += jnp.dot(act, wd_ref[...], preferred_element_type=F32)


def _ffn(x2, g, w_gate, w_up, w_down, tm, tf):
    m = x2.shape[0]
    return pl.pallas_call(
        _ffn_kernel,
        grid=(m // tm, D_FF // tf),
        in_specs=[
            pl.BlockSpec((tm, D_MODEL), lambda i, f: (i, 0)),
            pl.BlockSpec((1, D_MODEL), lambda i, f: (0, 0)),
            pl.BlockSpec((D_MODEL, tf), lambda i, f: (0, f)),
            pl.BlockSpec((D_MODEL, tf), lambda i, f: (0, f)),
            pl.BlockSpec((tf, D_MODEL), lambda i, f: (f, 0)),
        ],
        out_specs=pl.BlockSpec((tm, D_MODEL), lambda i, f: (i, 0)),
        out_shape=jax.ShapeDtypeStruct((m, D_MODEL), F32),
        scratch_shapes=[pltpu.VMEM((tm, D_MODEL), BF16)],
        compiler_params=_cparams(("arbitrary", "arbitrary")),
        name="ffn",
    )(x2, g, w_gate, w_up, w_down)


def _ple_kernel(x_ref, p_ref, g_ref, gf_ref, wg_ref, wp_ref, o_ref, *, final_norm):
    x = x_ref[...]
    h = _rms(x, g_ref[...]).astype(BF16)
    gate = jax.nn.sigmoid(jnp.dot(h, wg_ref[...], preferred_element_type=F32))
    emb = jnp.dot(p_ref[...].astype(BF16), wp_ref[...], preferred_element_type=F32)
    y = x + gate * emb
    o_ref[...] = _rms(y, gf_ref[...]) if final_norm else y


def _ple(x2, p2, g, g_final, w_gate, w_proj, tm, final_norm):
    m = x2.shape[0]
    return pl.pallas_call(
        functools.partial(_ple_kernel, final_norm=final_norm),
        grid=(m // tm,),
        in_specs=[
            pl.BlockSpec((tm, D_MODEL), lambda i: (i, 0)),
            pl.BlockSpec((tm, D_PLE), lambda i: (i, 0)),
            pl.BlockSpec((1, D_MODEL), lambda i: (0, 0)),
            pl.BlockSpec((1, D_MODEL), lambda i: (0, 0)),
            _resident((D_MODEL, D_MODEL)),
            _resident((D_PLE, D_MODEL)),
        ],
        out_specs=pl.BlockSpec((tm, D_MODEL), lambda i: (i, 0)),
        out_shape=jax.ShapeDtypeStruct((m, D_MODEL), F32),
        compiler_params=_cparams(("arbitrary",)),
        name="ple",
    )(x2, p2, g, g_final, w_gate, w_proj)


def _rope_lane_freqs():
    def inv(rot):
        half = rot // 2
        return ROPE_THETA ** (-jnp.arange(half, dtype=F32) / half)
    used = ROT_A // 2 + ROT_IDX // 2
    return jnp.concatenate([inv(ROT_A), inv(ROT_IDX), jnp.zeros((128 - used,), F32)])[None, :]


def _pack_w_in(w):
    w = w.astype(BF16)
    edges = [0] + [int(s) for s in np.cumsum(SPLIT_SIZES)]
    seg = [w[:, edges[k]:edges[k + 1]] for k in range(len(SPLIT_SIZES))]
    q_a, k_a, v_a, q_i, k_i, w_i, q_b, k_b, v_b, f_b, g_a, g_b = seg
    main = jnp.concatenate([g_a, g_b, q_a, q_i, k_a, v_a, q_b, k_b, v_b], axis=1)
    pad = jnp.zeros((w.shape[0], 128 - D_IDX - H_IDX - HB), w.dtype)
    small = jnp.concatenate([k_i, w_i, f_b, pad], axis=1)
    return main, small


def _values_transposed(proj, off, heads, b, t_len):
    vt = proj[:, off:off + heads * HEAD_DIM].reshape(b, t_len, heads, HEAD_DIM).transpose(0, 2, 3, 1)
    ones = jnp.ones((b, heads, VT_ROWS - HEAD_DIM, t_len), vt.dtype)
    return jnp.concatenate([vt, ones], axis=2)


def kernel(x, p, positions, g_mix, w_in, b_f, w_o_a, w_o_b, w_out, g_ffn, w_ffn_gate, w_ffn_up,
           w_ffn_down, g_ple, w_ple_gate, w_ple_proj, g_final):
    b, t_len, d = x.shape
    depth = w_in.shape[0]
    m = b * t_len
    top_k = min(TOPK_MAX, t_len // 4)
    inv_freq = _rope_lane_freqs()
    pos2 = positions.reshape(m, 1)
    x2 = x.reshape(m, d)
    tm_in = min(1024, m)
    tm = min(512, m)
    for i in range(depth):
        w_main, w_small = _pack_w_in(w_in[i])
        bf_row = jnp.zeros((1, 128), F32).at[0, LANE_FB:LANE_FB + HB].set(b_f[i].astype(F32))
        proj, kie, kio, wlf = _inproj(x2, pos2, g_mix[i][None, :], w_main, w_small, bf_row, inv_freq, tm_in)
        cb = _cumsum(wlf, b, t_len)
        vt_a = _values_transposed(proj, OFF_VA, HA_KV, b, t_len)
        vt_b = _values_transposed(proj, OFF_VB, HB, b, t_len)
        out_a = _attn_a(proj, vt_a, kie, kio, wlf, b, t_len, top_k)
        out_b = _attn_b(proj, vt_b, cb, b, t_len)
        x2 = _merge(x2, out_a, out_b, proj, w_o_a[i].astype(BF16), w_o_b[i].astype(BF16),
                    w_out[i].astype(BF16), tm)
        x2 = _ffn(x2, g_ffn[i][None, :], w_ffn_gate[i].astype(BF16), w_ffn_up[i].astype(BF16),
                  w_ffn_down[i].astype(BF16), tm_in, 512)
        x2 = _ple(x2, p[i].reshape(m, D_PLE), g_ple[i][None, :], g_final[None, :], w_ple_gate[i].astype(BF16),
                  w_ple_proj[i].astype(BF16), tm, final_norm=(i + 1 == depth))
    return x2.reshape(b, t_len, d)
```

```python
import functools

import jax
import jax.numpy as jnp
import numpy as np
from jax import lax
from jax.experimental import pallas as pl
from jax.experimental.pallas import tpu as pltpu

F32 = jnp.float32
BF16 = jnp.bfloat16

D_MODEL = 2048
HEAD_DIM = 128
HA = 8
HA_KV = 2
H_IDX = 16
D_IDX = 64
TOPK_MAX = 256
HB = 8
ROPE_THETA = 500000.0
ROT_A = HEAD_DIM // 4
ROT_IDX = D_IDX // 4
Q_BLOCK = 128
D_FF = -(-8 * D_MODEL // (3 * 256)) * 256
D_PLE = 256
EPS = 1e-6

SPLIT_SIZES = (HA * HEAD_DIM, HA_KV * HEAD_DIM, HA_KV * HEAD_DIM, H_IDX * D_IDX, D_IDX, H_IDX,
               HB * HEAD_DIM, HB * HEAD_DIM, HB * HEAD_DIM, HB, D_MODEL, D_MODEL)

OFF_GA = 0
OFF_GB = OFF_GA + D_MODEL
OFF_QA = OFF_GB + D_MODEL
OFF_KA = OFF_QA + HA * HEAD_DIM
OFF_VA = OFF_KA + HA_KV * HEAD_DIM
OFF_QI = OFF_VA + HA_KV * HEAD_DIM
OFF_QB = OFF_QI + H_IDX * D_IDX
OFF_KB = OFF_QB + HB * HEAD_DIM
OFF_VB = OFF_KB + HB * HEAD_DIM
N_MAIN = OFF_VB + HB * HEAD_DIM
LANE_KI = 0
LANE_WI = D_IDX
LANE_FB = D_IDX + H_IDX

TN_IN = 512
MXU_N = 256
NEG = -1e30
VMEM_LIMIT = 56 * 1024 * 1024
LOG2E = 1.4426950408889634
ATTN_SCALE = HEAD_DIM ** -0.5 * LOG2E
VT_ROWS = HEAD_DIM + 16
HEADS_PER_STEP_B = 4
BISECT_FIRST = 12
BISECT_ROUND = 4
BISECT_MAX_ITERS = 48


def _cparams(sem):
    return pltpu.CompilerParams(dimension_semantics=sem, vmem_limit_bytes=VMEM_LIMIT)


def _rms(x, g):
    return x * lax.rsqrt(jnp.mean(x * x, axis=-1, keepdims=True) + EPS) * g


def _resident(shape):
    nd = len(shape)
    return pl.BlockSpec(shape, lambda *_: (0,) * nd, pipeline_mode=pl.Buffered(1))


def _rope128(a, c, sa, sb, shift):
    outs = []
    for q in range(a.shape[1] // 128):
        aq = a[:, q * 128:(q + 1) * 128]
        outs.append(aq * c + pltpu.roll(aq, 128 - shift, 1) * sa + pltpu.roll(aq, shift, 1) * sb)
    return outs[0] if len(outs) == 1 else jnp.concatenate(outs, axis=1)


def _inproj_kernel(x_ref, pos_ref, g_ref, w_ref, ws_ref, bf_ref, inv_ref,
                   o_ref, kie_ref, kio_ref, wlf_ref, vta_ref, vtb_ref,
                   h_scr, ca_scr, saa_scr, sab_scr, ci_scr, sia_scr, sib_scr):
    j = pl.program_id(1)
    tm = x_ref.shape[0]

    @pl.when(j == 0)
    def _():
        hb = _rms(x_ref[...], g_ref[...]).astype(BF16)
        h_scr[...] = hb
        pos = pos_ref[...].astype(F32)
        lane = lax.broadcasted_iota(jnp.int32, (tm, 128), 1)
        ha, hi = ROT_A // 2, ROT_IDX // 2
        ang = pos * inv_ref[...]
        cs = jnp.cos(ang)
        sn = jnp.sin(ang)

        def at(tab, src, dst):
            return pltpu.roll(tab, (dst - src) % 128, 1)

        def lanes(lo, n):
            return (lane >= lo) & (lane < lo + n)

        ca_scr[...] = jnp.where(lanes(0, ha), cs, jnp.where(lanes(ha, ha), at(cs, 0, ha), 1.0))
        saa_scr[...] = jnp.where(lanes(0, ha), -sn, 0.0)
        sab_scr[...] = jnp.where(lanes(ha, ha), at(sn, 0, ha), 0.0)
        ci = jnp.ones((tm, 128), F32)
        sia = jnp.zeros((tm, 128), F32)
        sib = jnp.zeros((tm, 128), F32)
        for head0 in (0, D_IDX):
            ci = jnp.where(lanes(head0, hi), at(cs, ha, head0), ci)
            ci = jnp.where(lanes(head0 + hi, hi), at(cs, ha, head0 + hi), ci)
            sia = jnp.where(lanes(head0, hi), -at(sn, ha, head0), sia)
            sib = jnp.where(lanes(head0 + hi, hi), at(sn, ha, head0 + hi), sib)
        ci_scr[...] = ci
        sia_scr[...] = sia
        sib_scr[...] = sib
        small = jnp.dot(hb, ws_ref[...], preferred_element_type=F32)
        kr = _rope128(small, ci, sia, sib, ROT_IDX // 2)
        ke = jnp.where(lane < D_IDX, kr, 0.0)
        kie_ref[...] = ke.astype(BF16)
        kio_ref[...] = pltpu.roll(ke, D_IDX, 1).astype(BF16)
        is_w = (lane >= LANE_WI) & (lane < LANE_WI + H_IDX)
        is_f = (lane >= LANE_FB) & (lane < LANE_FB + HB)
        logf = jax.nn.log_sigmoid(small + bf_ref[...])
        wlf_ref[...] = jnp.where(is_w, small * (H_IDX ** -0.5 * D_IDX ** -0.5), jnp.where(is_f, logf, 0.0))

    t = TN_IN
    groups = t // MXU_N
    heads_per_group = MXU_N // HEAD_DIM

    def project(epilogues):
        for c, epilogue in enumerate(epilogues):
            cols = slice(c * MXU_N, (c + 1) * MXU_N)
            acc = jnp.dot(h_scr[...], w_ref[:, cols], preferred_element_type=F32)
            o_ref[:, cols] = epilogue(acc).astype(BF16)

    def rope_a(acc):
        return _rope128(acc, ca_scr[...], saa_scr[...], sab_scr[...], ROT_A // 2)

    def rope_i(acc):
        return _rope128(acc, ci_scr[...], sia_scr[...], sib_scr[...], ROT_IDX // 2)

    def values(vt_ref, c):
        def epilogue(acc):
            acc_t = acc.T
            for hh in range(heads_per_group):
                h = c * heads_per_group + hh
                vt_ref[h, 0:HEAD_DIM, :] = acc_t[hh * HEAD_DIM:(hh + 1) * HEAD_DIM, :].astype(BF16)
                vt_ref[h, HEAD_DIM:VT_ROWS, :] = jnp.ones((VT_ROWS - HEAD_DIM, tm), BF16)
            return acc
        return epilogue

    @pl.when(j < OFF_QA // t)
    def _():
        project([jax.nn.sigmoid] * groups)

    @pl.when((j >= OFF_QA // t) & (j < OFF_KA // t))
    def _():
        project([lambda acc: rope_a(acc) * ATTN_SCALE] * groups)

    @pl.when(j == OFF_KA // t)
    def _():
        project([rope_a, values(vta_ref, 0)])

    @pl.when((j >= OFF_QI // t) & (j < OFF_QB // t))
    def _():
        project([rope_i] * groups)

    @pl.when((j >= OFF_QB // t) & (j < OFF_KB // t))
    def _():
        project([lambda acc: acc * ATTN_SCALE] * groups)

    @pl.when((j >= OFF_KB // t) & (j < OFF_VB // t))
    def _():
        project([lambda acc: acc] * groups)

    @pl.when(j >= OFF_VB // t)
    def _():
        project([values(vtb_ref, c) for c in range(groups)])


def _inproj(x2, pos2, g, w_main, w_small, bf_row, inv_freq, tm, b, t_len):
    m = x2.shape[0]
    assert OFF_VA - OFF_KA == MXU_N and OFF_QI - OFF_KA == TN_IN
    assert HA_KV * HEAD_DIM == MXU_N
    grid = (m // tm, N_MAIN // TN_IN)
    nt = t_len // tm
    j_vb = OFF_VB // TN_IN
    vb_tiles = HB * HEAD_DIM // TN_IN
    heads_per_tile = TN_IN // HEAD_DIM
    row128 = pl.BlockSpec((1, 128), lambda i, j: (0, 0))
    return pl.pallas_call(
        _inproj_kernel,
        grid=grid,
        in_specs=[
            pl.BlockSpec((tm, D_MODEL), lambda i, j: (i, 0)),
            pl.BlockSpec((tm, 1), lambda i, j: (i, 0)),
            pl.BlockSpec((1, D_MODEL), lambda i, j: (0, 0)),
            pl.BlockSpec((D_MODEL, TN_IN), lambda i, j: (0, j)),
            pl.BlockSpec((D_MODEL, 128), lambda i, j: (0, 0)),
            row128, row128,
        ],
        out_specs=[
            pl.BlockSpec((tm, TN_IN), lambda i, j: (i, j)),
            pl.BlockSpec((tm, 128), lambda i, j: (i, 0)),
            pl.BlockSpec((tm, 128), lambda i, j: (i, 0)),
            pl.BlockSpec((tm, 128), lambda i, j: (i, 0)),
            pl.BlockSpec((None, HA_KV, VT_ROWS, tm), lambda i, j: (i // nt, 0, 0, i % nt)),
            pl.BlockSpec((None, heads_per_tile, VT_ROWS, tm),
                         lambda i, j: (i // nt, jnp.clip(j - j_vb, 0, vb_tiles - 1), 0, i % nt)),
        ],
        out_shape=[
            jax.ShapeDtypeStruct((m, N_MAIN), BF16),
            jax.ShapeDtypeStruct((m, 128), BF16),
            jax.ShapeDtypeStruct((m, 128), BF16),
            jax.ShapeDtypeStruct((m, 128), F32),
            jax.ShapeDtypeStruct((b, HA_KV, VT_ROWS, t_len), BF16),
            jax.ShapeDtypeStruct((b, HB, VT_ROWS, t_len), BF16),
        ],
        scratch_shapes=[pltpu.VMEM((tm, D_MODEL), BF16)] + [pltpu.VMEM((tm, 128), F32)] * 6,
        compiler_params=_cparams(("arbitrary", "arbitrary")),
        name="inproj",
    )(x2, pos2, g, w_main, w_small, bf_row, inv_freq)


def _cumsum_kernel(wlf_ref, o_ref, carry_scr, *, tc):
    @pl.when(pl.program_id(1) == 0)
    def _():
        carry_scr[...] = jnp.zeros_like(carry_scr)

    lf = wlf_ref[...]
    r = lax.broadcasted_iota(jnp.int32, (tc, tc), 0)
    c = lax.broadcasted_iota(jnp.int32, (tc, tc), 1)
    tri = jnp.where(c <= r, 1.0, 0.0).astype(BF16)
    hi = lf.astype(BF16)
    r1 = lf - hi.astype(F32)
    mid = r1.astype(BF16)
    lo = (r1 - mid.astype(F32)).astype(BF16)
    cs = (jnp.dot(tri, hi, preferred_element_type=F32) + jnp.dot(tri, mid, preferred_element_type=F32)
          + jnp.dot(tri, lo, preferred_element_type=F32)) + carry_scr[0:1, :]
    carry_scr[...] = jnp.broadcast_to(cs[tc - 1:tc, :], carry_scr.shape)
    cs2 = cs * LOG2E
    for h in range(HB):
        o_ref[h] = jnp.broadcast_to(cs2[:, LANE_FB + h:LANE_FB + h + 1], (tc, 128))


def _cumsum(wlf, b, t_len):
    tc = 256
    nt = t_len // tc
    return pl.pallas_call(
        functools.partial(_cumsum_kernel, tc=tc),
        grid=(b, nt),
        in_specs=[pl.BlockSpec((tc, 128), lambda bi, ti: (bi * nt + ti, 0))],
        out_specs=pl.BlockSpec((None, HB, tc, 128), lambda bi, ti: (bi, 0, ti, 0)),
        out_shape=jax.ShapeDtypeStruct((b, HB, t_len, 128), F32),
        scratch_shapes=[pltpu.VMEM((8, 128), F32)],
        compiler_params=_cparams(("arbitrary", "arbitrary")),
        name="fcumsum",
    )(wlf)


def _attn_a_kernel(qa_ref, qi0_ref, qi1_ref, ka_ref, vt_ref, kie_ref, kio_ref, wlf_ref, o_ref,
                   sc_scr, qs_scr, qg_scr, lt_scr, acc_scr, *, tk, top_k, n_iter):
    i = pl.program_id(1)
    nq = Q_BLOCK
    group = HA // HA_KV
    gw = group * nq
    nchunks = ((i + 1) * nq + tk - 1) // tk
    q_pos = i * nq + lax.broadcasted_iota(jnp.int32, (1, nq), 1)
    wt = wlf_ref[...].T
    half = H_IDX // 4
    for p in range(H_IDX // 2):
        src = qi0_ref if p < half else qi1_ref
        qs_scr[p * nq:(p + 1) * nq, :] = src[:, (p % half) * 128:(p % half + 1) * 128]

    def score_chunk(c, carry):
        rmin, rmax = carry
        off = pl.multiple_of(c * tk, tk)
        ke = kie_ref[pl.ds(off, tk), :]
        ko = kio_ref[pl.ds(off, tk), :]
        s = jnp.zeros((tk, nq), F32)
        pairs_per_dot = MXU_N // nq
        for pg in range(H_IDX // 2 // pairs_per_dot):
            qs = qs_scr[pg * MXU_N:(pg + 1) * MXU_N, :]
            de = lax.dot_general(ke, qs, (((1,), (1,)), ((), ())), preferred_element_type=F32)
            do = lax.dot_general(ko, qs, (((1,), (1,)), ((), ())), preferred_element_type=F32)
            for pp in range(pairs_per_dot):
                p = pg * pairs_per_dot + pp
                cols = slice(pp * nq, (pp + 1) * nq)
                s = s + jnp.maximum(de[:, cols], 0.0) * wt[LANE_WI + 2 * p:LANE_WI + 2 * p + 1, :]
                s = s + jnp.maximum(do[:, cols], 0.0) * wt[LANE_WI + 2 * p + 1:LANE_WI + 2 * p + 2, :]
        kpos = off + lax.broadcasted_iota(jnp.int32, (tk, 1), 0)
        valid = kpos <= q_pos
        sc_scr[pl.ds(off, tk), :] = jnp.where(valid, s, NEG)
        rmin = jnp.minimum(rmin, jnp.min(jnp.where(valid, s, -NEG), axis=0, keepdims=True))
        rmax = jnp.maximum(rmax, jnp.max(jnp.where(valid, s, NEG), axis=0, keepdims=True))
        return rmin, rmax

    lo, hi = lax.fori_loop(0, nchunks, score_chunk,
                           (jnp.full((1, nq), -NEG, F32), jnp.full((1, nq), NEG, F32)))

    def bisect(_, carry):
        lo, hi = carry
        mid = 0.5 * (lo + hi)

        def count_chunk(c, part):
            off = pl.multiple_of(c * tk, tk)
            ind = jnp.where(sc_scr[pl.ds(off, tk), :] >= mid, 1.0, 0.0)
            return part + jnp.sum(ind.reshape(tk // 64, 64, nq), axis=0)

        part = lax.fori_loop(0, nchunks, count_chunk, jnp.zeros((64, nq), F32))
        cnt = jnp.sum(part, axis=0, keepdims=True)
        ge = cnt >= top_k
        return jnp.where(ge, mid, lo), jnp.where(cnt == top_k, mid, jnp.where(ge, hi, mid))

    searching = q_pos >= top_k

    def bisect_round(state):
        it, lo, hi, _ = state
        lo, hi = lax.fori_loop(0, BISECT_ROUND, bisect, (lo, hi))
        pending = jnp.sum(jnp.where(searching & (lo < hi), 1.0, 0.0))
        return it + BISECT_ROUND, lo, hi, pending

    lo, hi = lax.fori_loop(0, BISECT_FIRST, bisect, (lo, hi))
    _, thr, _, _ = lax.while_loop(lambda st: (st[0] < n_iter) & (st[3] > 0.0), bisect_round,
                                  (jnp.int32(BISECT_FIRST), lo, hi, jnp.float32(1.0)))

    tka = tk // 2
    last = (nchunks - 1) * tk
    for n in range(HA_KV):
        for g in range(group):
            h = n * group + g
            qg_scr[n, g * nq:(g + 1) * nq, :] = qa_ref[:, h * HEAD_DIM:(h + 1) * HEAD_DIM]
    acc_scr[...] = jnp.zeros(acc_scr.shape, F32)

    def qk(off, slot):
        msk = jnp.where(sc_scr[pl.ds(off, tka), :] >= thr, 0.0, NEG)
        msk = jnp.concatenate([msk] * group, axis=1)
        cmax = []
        for n in range(HA_KV):
            k = ka_ref[pl.ds(off, tka), n * HEAD_DIM:(n + 1) * HEAD_DIM]
            s = lax.dot_general(k, qg_scr[n], (((1,), (1,)), ((), ())), preferred_element_type=F32) + msk
            lt_scr[n, slot] = s
            cmax.append(jnp.max(s, axis=0, keepdims=True))
        return tuple(cmax)

    def softmax_pv(off, slot, m_old, cmax):
        new = []
        for n in range(HA_KV):
            m_new = jnp.maximum(m_old[n], cmax[n])
            alpha = jnp.exp2(m_old[n] - m_new)
            p = jnp.exp2(lt_scr[n, slot] - m_new).astype(BF16)
            acc_scr[n] = alpha * acc_scr[n] + jnp.dot(vt_ref[n, :, pl.ds(off, tka)], p,
                                                      preferred_element_type=F32)
            new.append(m_new)
        return tuple(new)

    def pair(j, carry):
        m, cmax0 = carry
        off0 = pl.multiple_of(j * tk, tk)
        off1 = pl.multiple_of(off0 + tka, tka)
        off2 = pl.multiple_of(jnp.minimum(off0 + tk, last), tk)
        cmax1 = qk(off1, 1)
        m = softmax_pv(off0, 0, m, cmax0)
        cmax0 = qk(off2, 0)
        return softmax_pv(off1, 1, m, cmax1), cmax0

    lax.fori_loop(0, nchunks, pair, ((jnp.full((1, gw), NEG, F32),) * HA_KV, qk(0, 0)))
    for n in range(HA_KV):
        out_t = acc_scr[n, 0:HEAD_DIM, :] / acc_scr[n, HEAD_DIM:HEAD_DIM + 1, :]
        for g in range(group):
            h = n * group + g
            o_ref[:, h * HEAD_DIM:(h + 1) * HEAD_DIM] = out_t[:, g * nq:(g + 1) * nq].T.astype(BF16)


def _attn_a(proj, vt_a, kie, kio, wlf, b, t_len, top_k):
    nb = t_len // Q_BLOCK
    wa = HA * HEAD_DIM
    wkv = HA_KV * HEAD_DIM
    wqi = H_IDX * D_IDX // 2
    tk = min(512, t_len)
    gw = HA // HA_KV * Q_BLOCK
    kern = functools.partial(_attn_a_kernel, tk=tk, top_k=top_k, n_iter=BISECT_MAX_ITERS)
    return pl.pallas_call(
        kern,
        grid=(b, nb),
        in_specs=[
            pl.BlockSpec((Q_BLOCK, wa), lambda bi, i: (bi * nb + i, OFF_QA // wa)),
            pl.BlockSpec((Q_BLOCK, wqi), lambda bi, i: (bi * nb + i, OFF_QI // wqi)),
            pl.BlockSpec((Q_BLOCK, wqi), lambda bi, i: (bi * nb + i, OFF_QI // wqi + 1)),
            pl.BlockSpec((t_len, wkv), lambda bi, i: (bi, OFF_KA // wkv)),
            pl.BlockSpec((None, HA_KV, VT_ROWS, t_len), lambda bi, i: (bi, 0, 0, 0)),
            pl.BlockSpec((t_len, 128), lambda bi, i: (bi, 0)),
            pl.BlockSpec((t_len, 128), lambda bi, i: (bi, 0)),
            pl.BlockSpec((Q_BLOCK, 128), lambda bi, i: (bi * nb + i, 0)),
        ],
        out_specs=pl.BlockSpec((Q_BLOCK, wa), lambda bi, i: (bi * nb + i, 0)),
        out_shape=jax.ShapeDtypeStruct((b * t_len, wa), BF16),
        scratch_shapes=[
            pltpu.VMEM((t_len, Q_BLOCK), F32),
            pltpu.VMEM((H_IDX // 2 * Q_BLOCK, 128), BF16),
            pltpu.VMEM((HA_KV, gw, HEAD_DIM), BF16),
            pltpu.VMEM((HA_KV, 2, tk // 2, gw), F32),
            pltpu.VMEM((HA_KV, VT_ROWS, gw), F32),
        ],
        compiler_params=_cparams(("arbitrary", "arbitrary")),
        name="attn_a",
    )(proj, proj, proj, proj, vt_a, kie, kio, wlf)


def _attn_b_kernel(q_ref, k_ref, vt_ref, cb_ref, o_ref, lt_scr, acc_scr, *, tq):
    qi = pl.program_id(2)
    tk = tq // 2
    heads = HEADS_PER_STEP_B
    q0 = pl.multiple_of(qi * tq, tq)
    acc_scr[...] = jnp.zeros(acc_scr.shape, F32)

    def qk(off, slot, key0=None):
        cmax = []
        for n in range(heads):
            cols = slice(n * HEAD_DIM, (n + 1) * HEAD_DIM)
            bias = cb_ref[n, pl.ds(q0, 1), :] - cb_ref[n, pl.ds(off, tk), :]
            s = lax.dot_general(k_ref[pl.ds(off, tk), cols], q_ref[:, cols], (((1,), (1,)), ((), ())),
                                preferred_element_type=F32) + jnp.concatenate([bias] * (tq // 128), axis=1)
            if key0 is not None:
                key = key0 + lax.broadcasted_iota(jnp.int32, (tk, 1), 0)
                qry = lax.broadcasted_iota(jnp.int32, (1, tq), 1)
                s = jnp.where(key <= qry, s, NEG)
            lt_scr[n, slot] = s
            cmax.append(jnp.max(s, axis=0, keepdims=True))
        return tuple(cmax)

    def softmax_pv(off, slot, m_old, cmax):
        new = []
        for n in range(heads):
            m_new = jnp.maximum(m_old[n], cmax[n])
            alpha = jnp.exp2(m_old[n] - m_new)
            p = jnp.exp2(lt_scr[n, slot] - m_new).astype(BF16)
            acc_scr[n] = alpha * acc_scr[n] + jnp.dot(vt_ref[n, :, pl.ds(off, tk)], p,
                                                      preferred_element_type=F32)
            new.append(m_new)
        return tuple(new)

    q1 = pl.multiple_of(q0 + tk, tk)
    m = (jnp.full((1, tq), NEG, F32),) * heads
    cmax0 = qk(q0, 0, key0=0)
    cmax1 = qk(q1, 1, key0=tk)
    m = softmax_pv(q0, 0, m, cmax0)
    cmax0 = qk(0, 0)
    m = softmax_pv(q1, 1, m, cmax1)
    last = jnp.maximum(qi - 1, 0) * tq

    def pair(j, carry):
        m, cmax0 = carry
        off0 = pl.multiple_of(j * tq, tq)
        off1 = pl.multiple_of(off0 + tk, tk)
        off2 = pl.multiple_of(jnp.minimum(off0 + tq, last), tq)
        cmax1 = qk(off1, 1)
        m = softmax_pv(off0, 0, m, cmax0)
        cmax0 = qk(off2, 0)
        return softmax_pv(off1, 1, m, cmax1), cmax0

    lax.fori_loop(0, qi, pair, (m, cmax0))
    for n in range(heads):
        out_t = acc_scr[n, 0:HEAD_DIM, :] / acc_scr[n, HEAD_DIM:HEAD_DIM + 1, :]
        for g in range(tq // 128):
            o_ref[g * 128:(g + 1) * 128, n * HEAD_DIM:(n + 1) * HEAD_DIM] = (
                out_t[:, g * 128:(g + 1) * 128].T.astype(BF16))


def _attn_b(proj, vt_b, cb, b, t_len):
    tq = 512
    nq = t_len // tq
    hps = HEADS_PER_STEP_B
    wh = hps * HEAD_DIM
    return pl.pallas_call(
        functools.partial(_attn_b_kernel, tq=tq),
        grid=(b, HB // hps, nq),
        in_specs=[
            pl.BlockSpec((tq, wh), lambda bi, h, i: (bi * nq + i, OFF_QB // wh + h)),
            pl.BlockSpec((t_len, wh), lambda bi, h, i: (bi, OFF_KB // wh + h)),
            pl.BlockSpec((None, hps, VT_ROWS, t_len), lambda bi, h, i: (bi, h, 0, 0)),
            pl.BlockSpec((None, hps, t_len, 128), lambda bi, h, i: (bi, h, 0, 0)),
        ],
        out_specs=pl.BlockSpec((tq, wh), lambda bi, h, i: (bi * nq + i, h)),
        out_shape=jax.ShapeDtypeStruct((b * t_len, HB * HEAD_DIM), BF16),
        scratch_shapes=[pltpu.VMEM((hps, 2, tq // 2, tq), F32), pltpu.VMEM((hps, VT_ROWS, tq), F32)],
        compiler_params=_cparams(("arbitrary", "arbitrary", "arbitrary")),
        name="attn_b",
    )(proj, proj, vt_b, cb)


def _merge_kernel(x_ref, oa_ref, ob_ref, ga_ref, gb_ref, woa_ref, wob_ref, wout_ref, o_ref):
    ya = jnp.dot(oa_ref[...], woa_ref[...], preferred_element_type=F32)
    yb = jnp.dot(ob_ref[...], wob_ref[...], preferred_element_type=F32)
    mixed = ga_ref[...].astype(F32) * ya + gb_ref[...].astype(F32) * yb
    o_ref[...] = x_ref[...] + jnp.dot(mixed.astype(BF16), wout_ref[...], preferred_element_type=F32)


def _merge(x2, out_a, out_b, proj, w_oa, w_ob, w_out, tm):
    m = x2.shape[0]
    wa = HA * HEAD_DIM
    wb = HB * HEAD_DIM
    return pl.pallas_call(
        _merge_kernel,
        grid=(m // tm,),
        in_specs=[
            pl.BlockSpec((tm, D_MODEL), lambda i: (i, 0)),
            pl.BlockSpec((tm, wa), lambda i: (i, 0)),
            pl.BlockSpec((tm, wb), lambda i: (i, 0)),
            pl.BlockSpec((tm, D_MODEL), lambda i: (i, OFF_GA // D_MODEL)),
            pl.BlockSpec((tm, D_MODEL), lambda i: (i, OFF_GB // D_MODEL)),
            _resident((wa, D_MODEL)),
            _resident((wb, D_MODEL)),
            _resident((D_MODEL, D_MODEL)),
        ],
        out_specs=pl.BlockSpec((tm, D_MODEL), lambda i: (i, 0)),
        out_shape=jax.ShapeDtypeStruct((m, D_MODEL), F32),
        compiler_params=_cparams(("arbitrary",)),
        name="merge",
    )(x2, out_a, out_b, proj, proj, w_oa, w_ob, w_out)


def _ffn_kernel(x_ref, g_ref, wg_ref, wu_ref, wd_ref, o_ref, h_scr):
    @pl.when(pl.program_id(1) == 0)
    def _():
        x = x_ref[...]
        h_scr[...] = _rms(x, g_ref[...]).astype(BF16)
        o_ref[...] = x

    h = h_scr[...]
    gate = jnp.dot(h, wg_ref[...], preferred_element_type=F32)
    up = jnp.dot(h, wu_ref[...], preferred_element_type=F32)
    act = (jax.nn.silu(gate) * up).astype(BF16)
    o_ref[...] += jnp.dot(act, wd_ref[...], preferred_element_type=F32)


def _ffn(x2, g, w_gate, w_up, w_down, tm, tf):
    m = x2.shape[0]
    return pl.pallas_call(
        _ffn_kernel,
        grid=(m // tm, D_FF // tf),
        in_specs=[
            pl.BlockSpec((tm, D_MODEL), lambda i, f: (i, 0)),
            pl.BlockSpec((1, D_MODEL), lambda i, f: (0, 0)),
            pl.BlockSpec((D_MODEL, tf), lambda i, f: (0, f)),
            pl.BlockSpec((D_MODEL, tf), lambda i, f: (0, f)),
            pl.BlockSpec((tf, D_MODEL), lambda i, f: (f, 0)),
        ],
        out_specs=pl.BlockSpec((tm, D_MODEL), lambda i, f: (i, 0)),
        out_shape=jax.ShapeDtypeStruct((m, D_MODEL), F32),
        scratch_shapes=[pltpu.VMEM((tm, D_MODEL), BF16)],
        compiler_params=_cparams(("arbitrary", "arbitrary")),
        name="ffn",
    )(x2, g, w_gate, w_up, w_down)


def _ple_kernel(x_ref, p_ref, g_ref, gf_ref, wg_ref, wp_ref, o_ref, *, final_norm):
    x = x_ref[...]
    h = _rms(x, g_ref[...]).astype(BF16)
    gate = jax.nn.sigmoid(jnp.dot(h, wg_ref[...], preferred_element_type=F32))
    emb = jnp.dot(p_ref[...].astype(BF16), wp_ref[...], preferred_element_type=F32)
    y = x + gate * emb
    o_ref[...] = _rms(y, gf_ref[...]) if final_norm else y


def _ple(x2, p2, g, g_final, w_gate, w_proj, tm, final_norm):
    m = x2.shape[0]
    return pl.pallas_call(
        functools.partial(_ple_kernel, final_norm=final_norm),
        grid=(m // tm,),
        in_specs=[
            pl.BlockSpec((tm, D_MODEL), lambda i: (i, 0)),
            pl.BlockSpec((tm, D_PLE), lambda i: (i, 0)),
            pl.BlockSpec((1, D_MODEL), lambda i: (0, 0)),
            pl.BlockSpec((1, D_MODEL), lambda i: (0, 0)),
            _resident((D_MODEL, D_MODEL)),
            _resident((D_PLE, D_MODEL)),
        ],
        out_specs=pl.BlockSpec((tm, D_MODEL), lambda i: (i, 0)),
        out_shape=jax.ShapeDtypeStruct((m, D_MODEL), F32),
        compiler_params=_cparams(("arbitrary",)),
        name="ple",
    )(x2, p2, g, g_final, w_gate, w_proj)


def _rope_lane_freqs():
    def inv(rot):
        half = rot // 2
        return ROPE_THETA ** (-jnp.arange(half, dtype=F32) / half)
    used = ROT_A // 2 + ROT_IDX // 2
    return jnp.concatenate([inv(ROT_A), inv(ROT_IDX), jnp.zeros((128 - used,), F32)])[None, :]


def _pack_w_in(w):
    w = w.astype(BF16)
    edges = [0] + [int(s) for s in np.cumsum(SPLIT_SIZES)]
    seg = [w[:, edges[k]:edges[k + 1]] for k in range(len(SPLIT_SIZES))]
    q_a, k_a, v_a, q_i, k_i, w_i, q_b, k_b, v_b, f_b, g_a, g_b = seg
    main = jnp.concatenate([g_a, g_b, q_a, k_a, v_a, q_i, q_b, k_b, v_b], axis=1)
    pad = jnp.zeros((w.shape[0], 128 - D_IDX - H_IDX - HB), w.dtype)
    small = jnp.concatenate([k_i, w_i, f_b, pad], axis=1)
    return main, small


def kernel(x, p, positions, g_mix, w_in, b_f, w_o_a, w_o_b, w_out, g_ffn, w_ffn_gate, w_ffn_up,
           w_ffn_down, g_ple, w_ple_gate, w_ple_proj, g_final):
    b, t_len, d = x.shape
    depth = w_in.shape[0]
    m = b * t_len
    top_k = min(TOPK_MAX, t_len // 4)
    inv_freq = _rope_lane_freqs()
    pos2 = positions.reshape(m, 1)
    x2 = x.reshape(m, d)
    tm_in = min(1024, m)
    tm = min(512, m)
    for i in range(depth):
        w_main, w_small = _pack_w_in(w_in[i])
        bf_row = jnp.zeros((1, 128), F32).at[0, LANE_FB:LANE_FB + HB].set(b_f[i].astype(F32))
        proj, kie, kio, wlf, vt_a, vt_b = _inproj(x2, pos2, g_mix[i][None, :], w_main, w_small, bf_row,
                                                  inv_freq, min(tm_in, t_len), b, t_len)
        cb = _cumsum(wlf, b, t_len)
        out_a = _attn_a(proj, vt_a, kie, kio, wlf, b, t_len, top_k)
        out_b = _attn_b(proj, vt_b, cb, b, t_len)
        x2 = _merge(x2, out_a, out_b, proj, w_o_a[i].astype(BF16), w_o_b[i].astype(BF16),
                    w_out[i].astype(BF16), tm)
        x2 = _ffn(x2, g_ffn[i][None, :], w_ffn_gate[i].astype(BF16), w_ffn_up[i].astype(BF16),
                  w_ffn_down[i].astype(BF16), tm_in, 512)
        x2 = _ple(x2, p[i].reshape(m, D_PLE), g_ple[i][None, :], g_final[None, :], w_ple_gate[i].astype(BF16),
                  w_ple_proj[i].astype(BF16), tm, final_norm=(i + 1 == depth))
    return x2.reshape(b, t_len, d)
```

```python
import functools

import jax
import jax.numpy as jnp
import numpy as np
from jax import lax
from jax.experimental import pallas as pl
from jax.experimental.pallas import tpu as pltpu

F32 = jnp.float32
BF16 = jnp.bfloat16

D_MODEL = 2048
HEAD_DIM = 128
HA = 8
HA_KV = 2
H_IDX = 16
D_IDX = 64
TOPK_MAX = 256
HB = 8
ROPE_THETA = 500000.0
ROT_A = HEAD_DIM // 4
ROT_IDX = D_IDX // 4
Q_BLOCK = 128
D_FF = -(-8 * D_MODEL // (3 * 256)) * 256
D_PLE = 256
EPS = 1e-6

SPLIT_SIZES = (HA * HEAD_DIM, HA_KV * HEAD_DIM, HA_KV * HEAD_DIM, H_IDX * D_IDX, D_IDX, H_IDX,
               HB * HEAD_DIM, HB * HEAD_DIM, HB * HEAD_DIM, HB, D_MODEL, D_MODEL)

OFF_GA = 0
OFF_GB = OFF_GA + D_MODEL
OFF_QA = OFF_GB + D_MODEL
OFF_KA = OFF_QA + HA * HEAD_DIM
OFF_VA = OFF_KA + HA_KV * HEAD_DIM
OFF_QI = OFF_VA + HA_KV * HEAD_DIM
OFF_QB = OFF_QI + H_IDX * D_IDX
OFF_KB = OFF_QB + HB * HEAD_DIM
OFF_VB = OFF_KB + HB * HEAD_DIM
N_MAIN = OFF_VB + HB * HEAD_DIM
LANE_KI = 0
LANE_WI = D_IDX
LANE_FB = D_IDX + H_IDX

TN_IN = 512
MXU_N = 256
NEG = -1e30
VMEM_LIMIT = 56 * 1024 * 1024
LOG2E = 1.4426950408889634
ATTN_SCALE = HEAD_DIM ** -0.5 * LOG2E
VT_ROWS = HEAD_DIM + 16
HEADS_PER_STEP_B = 4
BISECT_FIRST = 12
BISECT_ROUND = 4
BISECT_MAX_ITERS = 48


def _cparams(sem):
    return pltpu.CompilerParams(dimension_semantics=sem, vmem_limit_bytes=VMEM_LIMIT)


def _rms(x, g):
    return x * lax.rsqrt(jnp.mean(x * x, axis=-1, keepdims=True) + EPS) * g


def _resident(shape):
    nd = len(shape)
    return pl.BlockSpec(shape, lambda *_: (0,) * nd, pipeline_mode=pl.Buffered(1))


def _rope128(a, c, sa, sb, shift):
    outs = []
    for q in range(a.shape[1] // 128):
        aq = a[:, q * 128:(q + 1) * 128]
        outs.append(aq * c + pltpu.roll(aq, 128 - shift, 1) * sa + pltpu.roll(aq, shift, 1) * sb)
    return outs[0] if len(outs) == 1 else jnp.concatenate(outs, axis=1)


def _inproj_kernel(x_ref, pos_ref, g_ref, w_ref, ws_ref, bf_ref, inv_ref,
                   o_ref, kie_ref, kio_ref, wlf_ref, vta_ref, vtb_ref,
                   h_scr, ca_scr, saa_scr, sab_scr, ci_scr, sia_scr, sib_scr):
    j = pl.program_id(1)
    tm = x_ref.shape[0]

    @pl.when(j == 0)
    def _():
        hb = _rms(x_ref[...], g_ref[...]).astype(BF16)
        h_scr[...] = hb
        pos = pos_ref[...].astype(F32)
        lane = lax.broadcasted_iota(jnp.int32, (tm, 128), 1)
        ha, hi = ROT_A // 2, ROT_IDX // 2
        ang = pos * inv_ref[...]
        cs = jnp.cos(ang)
        sn = jnp.sin(ang)

        def at(tab, src, dst):
            return pltpu.roll(tab, (dst - src) % 128, 1)

        def lanes(lo, n):
            return (lane >= lo) & (lane < lo + n)

        ca_scr[...] = jnp.where(lanes(0, ha), cs, jnp.where(lanes(ha, ha), at(cs, 0, ha), 1.0))
        saa_scr[...] = jnp.where(lanes(0, ha), -sn, 0.0)
        sab_scr[...] = jnp.where(lanes(ha, ha), at(sn, 0, ha), 0.0)
        ci = jnp.ones((tm, 128), F32)
        sia = jnp.zeros((tm, 128), F32)
        sib = jnp.zeros((tm, 128), F32)
        for head0 in (0, D_IDX):
            ci = jnp.where(lanes(head0, hi), at(cs, ha, head0), ci)
            ci = jnp.where(lanes(head0 + hi, hi), at(cs, ha, head0 + hi), ci)
            sia = jnp.where(lanes(head0, hi), -at(sn, ha, head0), sia)
            sib = jnp.where(lanes(head0 + hi, hi), at(sn, ha, head0 + hi), sib)
        ci_scr[...] = ci
        sia_scr[...] = sia
        sib_scr[...] = sib
        small = jnp.dot(hb, ws_ref[...], preferred_element_type=F32)
        kr = _rope128(small, ci, sia, sib, ROT_IDX // 2)
        ke = jnp.where(lane < D_IDX, kr, 0.0)
        kie_ref[...] = ke.astype(BF16)
        kio_ref[...] = pltpu.roll(ke, D_IDX, 1).astype(BF16)
        is_w = (lane >= LANE_WI) & (lane < LANE_WI + H_IDX)
        is_f = (lane >= LANE_FB) & (lane < LANE_FB + HB)
        logf = jax.nn.log_sigmoid(small + bf_ref[...])
        wlf_ref[...] = jnp.where(is_w, small * (H_IDX ** -0.5 * D_IDX ** -0.5), jnp.where(is_f, logf, 0.0))

    t = TN_IN
    groups = t // MXU_N
    heads_per_group = MXU_N // HEAD_DIM

    def project(epilogues):
        for c, epilogue in enumerate(epilogues):
            cols = slice(c * MXU_N, (c + 1) * MXU_N)
            acc = jnp.dot(h_scr[...], w_ref[:, cols], preferred_element_type=F32)
            o_ref[:, cols] = epilogue(acc).astype(BF16)

    def rope_a(acc):
        return _rope128(acc, ca_scr[...], saa_scr[...], sab_scr[...], ROT_A // 2)

    def rope_i(acc):
        return _rope128(acc, ci_scr[...], sia_scr[...], sib_scr[...], ROT_IDX // 2)

    def values(vt_ref, c):
        def epilogue(acc):
            acc_t = acc.T
            for hh in range(heads_per_group):
                h = c * heads_per_group + hh
                vt_ref[h, 0:HEAD_DIM, :] = acc_t[hh * HEAD_DIM:(hh + 1) * HEAD_DIM, :].astype(BF16)
                vt_ref[h, HEAD_DIM:VT_ROWS, :] = jnp.ones((VT_ROWS - HEAD_DIM, tm), BF16)
            return acc
        return epilogue

    @pl.when(j < OFF_QA // t)
    def _():
        project([jax.nn.sigmoid] * groups)

    @pl.when((j >= OFF_QA // t) & (j < OFF_KA // t))
    def _():
        project([lambda acc: rope_a(acc) * ATTN_SCALE] * groups)

    @pl.when(j == OFF_KA // t)
    def _():
        project([rope_a, values(vta_ref, 0)])

    @pl.when((j >= OFF_QI // t) & (j < OFF_QB // t))
    def _():
        project([rope_i] * groups)

    @pl.when((j >= OFF_QB // t) & (j < OFF_KB // t))
    def _():
        project([lambda acc: acc * ATTN_SCALE] * groups)

    @pl.when((j >= OFF_KB // t) & (j < OFF_VB // t))
    def _():
        project([lambda acc: acc] * groups)

    @pl.when(j >= OFF_VB // t)
    def _():
        project([values(vtb_ref, c) for c in range(groups)])


def _inproj(x2, pos2, g, w_main, w_small, bf_row, inv_freq, tm, b, t_len):
    m = x2.shape[0]
    assert OFF_VA - OFF_KA == MXU_N and OFF_QI - OFF_KA == TN_IN
    assert HA_KV * HEAD_DIM == MXU_N
    grid = (m // tm, N_MAIN // TN_IN)
    nt = t_len // tm
    j_vb = OFF_VB // TN_IN
    vb_tiles = HB * HEAD_DIM // TN_IN
    heads_per_tile = TN_IN // HEAD_DIM
    row128 = pl.BlockSpec((1, 128), lambda i, j: (0, 0))
    return pl.pallas_call(
        _inproj_kernel,
        grid=grid,
        in_specs=[
            pl.BlockSpec((tm, D_MODEL), lambda i, j: (i, 0)),
            pl.BlockSpec((tm, 1), lambda i, j: (i, 0)),
            pl.BlockSpec((1, D_MODEL), lambda i, j: (0, 0)),
            pl.BlockSpec((D_MODEL, TN_IN), lambda i, j: (0, j)),
            pl.BlockSpec((D_MODEL, 128), lambda i, j: (0, 0)),
            row128, row128,
        ],
        out_specs=[
            pl.BlockSpec((tm, TN_IN), lambda i, j: (i, j)),
            pl.BlockSpec((tm, 128), lambda i, j: (i, 0)),
            pl.BlockSpec((tm, 128), lambda i, j: (i, 0)),
            pl.BlockSpec((tm, 128), lambda i, j: (i, 0)),
            pl.BlockSpec((None, HA_KV, VT_ROWS, tm), lambda i, j: (i // nt, 0, 0, i % nt)),
            pl.BlockSpec((None, heads_per_tile, VT_ROWS, tm),
                         lambda i, j: (i // nt, jnp.clip(j - j_vb, 0, vb_tiles - 1), 0, i % nt)),
        ],
        out_shape=[
            jax.ShapeDtypeStruct((m, N_MAIN), BF16),
            jax.ShapeDtypeStruct((m, 128), BF16),
            jax.ShapeDtypeStruct((m, 128), BF16),
            jax.ShapeDtypeStruct((m, 128), F32),
            jax.ShapeDtypeStruct((b, HA_KV, VT_ROWS, t_len), BF16),
            jax.ShapeDtypeStruct((b, HB, VT_ROWS, t_len), BF16),
        ],
        scratch_shapes=[pltpu.VMEM((tm, D_MODEL), BF16)] + [pltpu.VMEM((tm, 128), F32)] * 6,
        compiler_params=_cparams(("arbitrary", "arbitrary")),
        name="inproj",
    )(x2, pos2, g, w_main, w_small, bf_row, inv_freq)


def _cumsum_kernel(wlf_ref, o_ref, carry_scr, *, tc):
    @pl.when(pl.program_id(1) == 0)
    def _():
        carry_scr[...] = jnp.zeros_like(carry_scr)

    lf = wlf_ref[...]
    r = lax.broadcasted_iota(jnp.int32, (tc, tc), 0)
    c = lax.broadcasted_iota(jnp.int32, (tc, tc), 1)
    tri = jnp.where(c <= r, 1.0, 0.0).astype(BF16)
    hi = lf.astype(BF16)
    r1 = lf - hi.astype(F32)
    mid = r1.astype(BF16)
    lo = (r1 - mid.astype(F32)).astype(BF16)
    cs = (jnp.dot(tri, hi, preferred_element_type=F32) + jnp.dot(tri, mid, preferred_element_type=F32)
          + jnp.dot(tri, lo, preferred_element_type=F32)) + carry_scr[0:1, :]
    carry_scr[...] = jnp.broadcast_to(cs[tc - 1:tc, :], carry_scr.shape)
    cs2 = cs * LOG2E
    for h in range(HB):
        o_ref[h] = jnp.broadcast_to(cs2[:, LANE_FB + h:LANE_FB + h + 1], (tc, 128))


def _cumsum(wlf, b, t_len):
    tc = 256
    nt = t_len // tc
    return pl.pallas_call(
        functools.partial(_cumsum_kernel, tc=tc),
        grid=(b, nt),
        in_specs=[pl.BlockSpec((tc, 128), lambda bi, ti: (bi * nt + ti, 0))],
        out_specs=pl.BlockSpec((None, HB, tc, 128), lambda bi, ti: (bi, 0, ti, 0)),
        out_shape=jax.ShapeDtypeStruct((b, HB, t_len, 128), F32),
        scratch_shapes=[pltpu.VMEM((8, 128), F32)],
        compiler_params=_cparams(("arbitrary", "arbitrary")),
        name="fcumsum",
    )(wlf)


def _attn_a_kernel(qa_ref, qi0_ref, qi1_ref, ka_ref, vt_ref, kie_ref, kio_ref, wlf_ref, o_ref,
                   sc_scr, qs_scr, qg_scr, lt_scr, acc_scr, *, tk, top_k, n_iter):
    i = pl.program_id(1)
    nq = Q_BLOCK
    group = HA // HA_KV
    gw = group * nq
    nchunks = ((i + 1) * nq + tk - 1) // tk
    q_pos = i * nq + lax.broadcasted_iota(jnp.int32, (1, nq), 1)
    wt = wlf_ref[...].T
    half = H_IDX // 4
    for p in range(H_IDX // 2):
        src = qi0_ref if p < half else qi1_ref
        qs_scr[p * nq:(p + 1) * nq, :] = src[:, (p % half) * 128:(p % half + 1) * 128]

    def score_chunk(c, carry):
        rmin, rmax = carry
        off = pl.multiple_of(c * tk, tk)
        ke = kie_ref[pl.ds(off, tk), :]
        ko = kio_ref[pl.ds(off, tk), :]
        s = jnp.zeros((tk, nq), F32)
        pairs_per_dot = MXU_N // nq
        for pg in range(H_IDX // 2 // pairs_per_dot):
            qs = qs_scr[pg * MXU_N:(pg + 1) * MXU_N, :]
            de = lax.dot_general(ke, qs, (((1,), (1,)), ((), ())), preferred_element_type=F32)
            do = lax.dot_general(ko, qs, (((1,), (1,)), ((), ())), preferred_element_type=F32)
            for pp in range(pairs_per_dot):
                p = pg * pairs_per_dot + pp
                cols = slice(pp * nq, (pp + 1) * nq)
                s = s + jnp.maximum(de[:, cols], 0.0) * wt[LANE_WI + 2 * p:LANE_WI + 2 * p + 1, :]
                s = s + jnp.maximum(do[:, cols], 0.0) * wt[LANE_WI + 2 * p + 1:LANE_WI + 2 * p + 2, :]
        kpos = off + lax.broadcasted_iota(jnp.int32, (tk, 1), 0)
        valid = kpos <= q_pos
        sc_scr[pl.ds(off, tk), :] = jnp.where(valid, s, NEG)
        rmin = jnp.minimum(rmin, jnp.min(jnp.where(valid, s, -NEG), axis=0, keepdims=True))
        rmax = jnp.maximum(rmax, jnp.max(jnp.where(valid, s, NEG), axis=0, keepdims=True))
        return rmin, rmax

    lo, hi = lax.fori_loop(0, nchunks, score_chunk,
                           (jnp.full((1, nq), -NEG, F32), jnp.full((1, nq), NEG, F32)))

    def bisect(_, carry):
        lo, hi = carry
        mid = 0.5 * (lo + hi)

        def count_chunk(c, part):
            off = pl.multiple_of(c * tk, tk)
            ind = jnp.where(sc_scr[pl.ds(off, tk), :] >= mid, 1.0, 0.0)
            return part + jnp.sum(ind.reshape(tk // 64, 64, nq), axis=0)

        part = lax.fori_loop(0, nchunks, count_chunk, jnp.zeros((64, nq), F32))
        cnt = jnp.sum(part, axis=0, keepdims=True)
        ge = cnt >= top_k
        return jnp.where(ge, mid, lo), jnp.where(cnt == top_k, mid, jnp.where(ge, hi, mid))

    searching = q_pos >= top_k

    def bisect_round(state):
        it, lo, hi, _ = state
        lo, hi = lax.fori_loop(0, BISECT_ROUND, bisect, (lo, hi))
        pending = jnp.sum(jnp.where(searching & (lo < hi), 1.0, 0.0))
        return it + BISECT_ROUND, lo, hi, pending

    lo, hi = lax.fori_loop(0, BISECT_FIRST, bisect, (lo, hi))
    _, thr, _, _ = lax.while_loop(lambda st: (st[0] < n_iter) & (st[3] > 0.0), bisect_round,
                                  (jnp.int32(BISECT_FIRST), lo, hi, jnp.float32(1.0)))

    tka = tk // 2
    last = (nchunks - 1) * tk
    for n in range(HA_KV):
        for g in range(group):
            h = n * group + g
            qg_scr[n, g * nq:(g + 1) * nq, :] = qa_ref[:, h * HEAD_DIM:(h + 1) * HEAD_DIM]
    acc_scr[...] = jnp.zeros(acc_scr.shape, F32)

    def qk(off, slot):
        msk = jnp.where(sc_scr[pl.ds(off, tka), :] >= thr, 0.0, NEG)
        msk = jnp.concatenate([msk] * group, axis=1)
        cmax = []
        for n in range(HA_KV):
            k = ka_ref[pl.ds(off, tka), n * HEAD_DIM:(n + 1) * HEAD_DIM]
            s = lax.dot_general(k, qg_scr[n], (((1,), (1,)), ((), ())), preferred_element_type=F32) + msk
            lt_scr[n, slot] = s
            cmax.append(jnp.max(s, axis=0, keepdims=True))
        return tuple(cmax)

    def softmax_pv(off, slot, m_old, cmax):
        new = []
        for n in range(HA_KV):
            m_new = jnp.maximum(m_old[n], cmax[n])
            alpha = jnp.exp2(m_old[n] - m_new)
            p = jnp.exp2(lt_scr[n, slot] - m_new).astype(BF16)
            acc_scr[n] = alpha * acc_scr[n] + jnp.dot(vt_ref[n, :, pl.ds(off, tka)], p,
                                                      preferred_element_type=F32)
            new.append(m_new)
        return tuple(new)

    def pair(j, carry):
        m, cmax0 = carry
        off0 = pl.multiple_of(j * tk, tk)
        off1 = pl.multiple_of(off0 + tka, tka)
        off2 = pl.multiple_of(jnp.minimum(off0 + tk, last), tk)
        cmax1 = qk(off1, 1)
        m = softmax_pv(off0, 0, m, cmax0)
        cmax0 = qk(off2, 0)
        return softmax_pv(off1, 1, m, cmax1), cmax0

    lax.fori_loop(0, nchunks, pair, ((jnp.full((1, gw), NEG, F32),) * HA_KV, qk(0, 0)))
    for n in range(HA_KV):
        out_t = acc_scr[n, 0:HEAD_DIM, :] / acc_scr[n, HEAD_DIM:HEAD_DIM + 1, :]
        for g in range(group):
            h = n * group + g
            o_ref[:, h * HEAD_DIM:(h + 1) * HEAD_DIM] = out_t[:, g * nq:(g + 1) * nq].T.astype(BF16)


def _attn_a(proj, vt_a, kie, kio, wlf, b, t_len, top_k):
    nb = t_len // Q_BLOCK
    wa = HA * HEAD_DIM
    wkv = HA_KV * HEAD_DIM
    wqi = H_IDX * D_IDX // 2
    tk = min(512, t_len)
    gw = HA // HA_KV * Q_BLOCK
    kern = functools.partial(_attn_a_kernel, tk=tk, top_k=top_k, n_iter=BISECT_MAX_ITERS)
    return pl.pallas_call(
        kern,
        grid=(b, nb),
        in_specs=[
            pl.BlockSpec((Q_BLOCK, wa), lambda bi, i: (bi * nb + i, OFF_QA // wa)),
            pl.BlockSpec((Q_BLOCK, wqi), lambda bi, i: (bi * nb + i, OFF_QI // wqi)),
            pl.BlockSpec((Q_BLOCK, wqi), lambda bi, i: (bi * nb + i, OFF_QI // wqi + 1)),
            pl.BlockSpec((t_len, wkv), lambda bi, i: (bi, OFF_KA // wkv)),
            pl.BlockSpec((None, HA_KV, VT_ROWS, t_len), lambda bi, i: (bi, 0, 0, 0)),
            pl.BlockSpec((t_len, 128), lambda bi, i: (bi, 0)),
            pl.BlockSpec((t_len, 128), lambda bi, i: (bi, 0)),
            pl.BlockSpec((Q_BLOCK, 128), lambda bi, i: (bi * nb + i, 0)),
        ],
        out_specs=pl.BlockSpec((Q_BLOCK, wa), lambda bi, i: (bi * nb + i, 0)),
        out_shape=jax.ShapeDtypeStruct((b * t_len, wa), BF16),
        scratch_shapes=[
            pltpu.VMEM((t_len, Q_BLOCK), F32),
            pltpu.VMEM((H_IDX // 2 * Q_BLOCK, 128), BF16),
            pltpu.VMEM((HA_KV, gw, HEAD_DIM), BF16),
            pltpu.VMEM((HA_KV, 2, tk // 2, gw), F32),
            pltpu.VMEM((HA_KV, VT_ROWS, gw), F32),
        ],
        compiler_params=_cparams(("arbitrary", "arbitrary")),
        name="attn_a",
    )(proj, proj, proj, proj, vt_a, kie, kio, wlf)


def _attn_b_kernel(q_ref, k_ref, vt_ref, cb_ref, o_ref, lt_scr, acc_scr, *, tq):
    qi = pl.program_id(2)
    tk = tq // 2
    heads = HEADS_PER_STEP_B
    q0 = pl.multiple_of(qi * tq, tq)
    acc_scr[...] = jnp.zeros(acc_scr.shape, F32)

    def qk(off, slot, key0=None):
        cmax = []
        for n in range(heads):
            cols = slice(n * HEAD_DIM, (n + 1) * HEAD_DIM)
            bias = cb_ref[n, pl.ds(q0, 1), :] - cb_ref[n, pl.ds(off, tk), :]
            s = lax.dot_general(k_ref[pl.ds(off, tk), cols], q_ref[:, cols], (((1,), (1,)), ((), ())),
                                preferred_element_type=F32) + jnp.concatenate([bias] * (tq // 128), axis=1)
            if key0 is not None:
                key = key0 + lax.broadcasted_iota(jnp.int32, (tk, 1), 0)
                qry = lax.broadcasted_iota(jnp.int32, (1, tq), 1)
                s = jnp.where(key <= qry, s, NEG)
            lt_scr[n, slot] = s
            cmax.append(jnp.max(s, axis=0, keepdims=True))
        return tuple(cmax)

    def softmax_pv(off, slot, m_old, cmax):
        new = []
        for n in range(heads):
            m_new = jnp.maximum(m_old[n], cmax[n])
            alpha = jnp.exp2(m_old[n] - m_new)
            p = jnp.exp2(lt_scr[n, slot] - m_new).astype(BF16)
            acc_scr[n] = alpha * acc_scr[n] + jnp.dot(vt_ref[n, :, pl.ds(off, tk)], p,
                                                      preferred_element_type=F32)
            new.append(m_new)
        return tuple(new)

    q1 = pl.multiple_of(q0 + tk, tk)
    m = (jnp.full((1, tq), NEG, F32),) * heads
    cmax0 = qk(q0, 0, key0=0)
    cmax1 = qk(q1, 1, key0=tk)
    m = softmax_pv(q0, 0, m, cmax0)
    cmax0 = qk(0, 0)
    m = softmax_pv(q1, 1, m, cmax1)
    last = jnp.maximum(qi - 1, 0) * tq

    def pair(j, carry):
        m, cmax0 = carry
        off0 = pl.multiple_of(j * tq, tq)
        off1 = pl.multiple_of(off0 + tk, tk)
        off2 = pl.multiple_of(jnp.minimum(off0 + tq, last), tq)
        cmax1 = qk(off1, 1)
        m = softmax_pv(off0, 0, m, cmax0)
        cmax0 = qk(off2, 0)
        return softmax_pv(off1, 1, m, cmax1), cmax0

    lax.fori_loop(0, qi, pair, (m, cmax0))
    for n in range(heads):
        out_t = acc_scr[n, 0:HEAD_DIM, :] / acc_scr[n, HEAD_DIM:HEAD_DIM + 1, :]
        for g in range(tq // 128):
            o_ref[g * 128:(g + 1) * 128, n * HEAD_DIM:(n + 1) * HEAD_DIM] = (
                out_t[:, g * 128:(g + 1) * 128].T.astype(BF16))


def _attn_b(proj, vt_b, cb, b, t_len):
    tq = 512
    nq = t_len // tq
    hps = HEADS_PER_STEP_B
    wh = hps * HEAD_DIM
    return pl.pallas_call(
        functools.partial(_attn_b_kernel, tq=tq),
        grid=(b, HB // hps, nq),
        in_specs=[
            pl.BlockSpec((tq, wh), lambda bi, h, i: (bi * nq + i, OFF_QB // wh + h)),
            pl.BlockSpec((t_len, wh), lambda bi, h, i: (bi, OFF_KB // wh + h)),
            pl.BlockSpec((None, hps, VT_ROWS, t_len), lambda bi, h, i: (bi, h, 0, 0)),
            pl.BlockSpec((None, hps, t_len, 128), lambda bi, h, i: (bi, h, 0, 0)),
        ],
        out_specs=pl.BlockSpec((tq, wh), lambda bi, h, i: (bi * nq + i, h)),
        out_shape=jax.ShapeDtypeStruct((b * t_len, HB * HEAD_DIM), BF16),
        scratch_shapes=[pltpu.VMEM((hps, 2, tq // 2, tq), F32), pltpu.VMEM((hps, VT_ROWS, tq), F32)],
        compiler_params=_cparams(("arbitrary", "arbitrary", "arbitrary")),
        name="attn_b",
    )(proj, proj, vt_b, cb)


def _merge_kernel(x_ref, oa_ref, ob_ref, ga_ref, gb_ref, woa_ref, wob_ref, wout_ref, o_ref):
    ya = jnp.dot(oa_ref[...], woa_ref[...], preferred_element_type=F32)
    yb = jnp.dot(ob_ref[...], wob_ref[...], preferred_element_type=F32)
    mixed = ga_ref[...].astype(F32) * ya + gb_ref[...].astype(F32) * yb
    o_ref[...] = x_ref[...] + jnp.dot(mixed.astype(BF16), wout_ref[...], preferred_element_type=F32)


def _merge(x2, out_a, out_b, proj, w_oa, w_ob, w_out, tm):
    m = x2.shape[0]
    wa = HA * HEAD_DIM
    wb = HB * HEAD_DIM
    return pl.pallas_call(
        _merge_kernel,
        grid=(m // tm,),
        in_specs=[
            pl.BlockSpec((tm, D_MODEL), lambda i: (i, 0)),
            pl.BlockSpec((tm, wa), lambda i: (i, 0)),
            pl.BlockSpec((tm, wb), lambda i: (i, 0)),
            pl.BlockSpec((tm, D_MODEL), lambda i: (i, OFF_GA // D_MODEL)),
            pl.BlockSpec((tm, D_MODEL), lambda i: (i, OFF_GB // D_MODEL)),
            _resident((wa, D_MODEL)),
            _resident((wb, D_MODEL)),
            _resident((D_MODEL, D_MODEL)),
        ],
        out_specs=pl.BlockSpec((tm, D_MODEL), lambda i: (i, 0)),
        out_shape=jax.ShapeDtypeStruct((m, D_MODEL), F32),
        compiler_params=_cparams(("arbitrary",)),
        name="merge",
    )(x2, out_a, out_b, proj, proj, w_oa, w_ob, w_out)


def _ffn_kernel(x_ref, g_ref, wg_ref, wu_ref, wd_ref, o_ref, h_scr):
    @pl.when(pl.program_id(1) == 0)
    def _():
        x = x_ref[...]
        h_scr[...] = _rms(x, g_ref[...]).astype(BF16)
        o_ref[...] = x

    h = h_scr[...]
    gate = jnp.dot(h, wg_ref[...], preferred_element_type=F32)
    up = jnp.dot(h, wu_ref[...], preferred_element_type=F32)
    act = (jax.nn.silu(gate) * up).astype(BF16)
    o_ref[...] += jnp.dot(act, wd_ref[...], preferred_element_type=F32)


def _ffn(x2, g, w_gate, w_up, w_down, tm, tf):
    m = x2.shape[0]
    return pl.pallas_call(
        _ffn_kernel,
        grid=(m // tm, D_FF // tf),
        in_specs=[
            pl.BlockSpec((tm, D_MODEL), lambda i, f: (i, 0)),
            pl.BlockSpec((1, D_MODEL), lambda i, f: (0, 0)),
            pl.BlockSpec((D_MODEL, tf), lambda i, f: (0, f)),
            pl.BlockSpec((D_MODEL, tf), lambda i, f: (0, f)),
            pl.BlockSpec((tf, D_MODEL), lambda i, f: (f, 0)),
        ],
        out_specs=pl.BlockSpec((tm, D_MODEL), lambda i, f: (i, 0)),
        out_shape=jax.ShapeDtypeStruct((m, D_MODEL), F32),
        scratch_shapes=[pltpu.VMEM((tm, D_MODEL), BF16)],
        compiler_params=_cparams(("arbitrary", "arbitrary")),
        name="ffn",
    )(x2, g, w_gate, w_up, w_down)


def _ple_kernel(x_ref, p_ref, g_ref, gf_ref, wg_ref, wp_ref, o_ref, *, final_norm):
    x = x_ref[...]
    h = _rms(x, g_ref[...]).astype(BF16)
    gate = jax.nn.sigmoid(jnp.dot(h, wg_ref[...], preferred_element_type=F32))
    emb = jnp.dot(p_ref[...].astype(BF16), wp_ref[...], preferred_element_type=F32)
    y = x + gate * emb
    o_ref[...] = _rms(y, gf_ref[...]) if final_norm else y


def _ple(x2, p2, g, g_final, w_gate, w_proj, tm, final_norm):
    m = x2.shape[0]
    return pl.pallas_call(
        functools.partial(_ple_kernel, final_norm=final_norm),
        grid=(m // tm,),
        in_specs=[
            pl.BlockSpec((tm, D_MODEL), lambda i: (i, 0)),
            pl.BlockSpec((tm, D_PLE), lambda i: (i, 0)),
            pl.BlockSpec((1, D_MODEL), lambda i: (0, 0)),
            pl.BlockSpec((1, D_MODEL), lambda i: (0, 0)),
            _resident((D_MODEL, D_MODEL)),
            _resident((D_PLE, D_MODEL)),
        ],
        out_specs=pl.BlockSpec((tm, D_MODEL), lambda i: (i, 0)),
        out_shape=jax.ShapeDtypeStruct((m, D_MODEL), F32),
        compiler_params=_cparams(("arbitrary",)),
        name="ple",
    )(x2, p2, g, g_final, w_gate, w_proj)


def _rope_lane_freqs():
    def inv(rot):
        half = rot // 2
        return ROPE_THETA ** (-jnp.arange(half, dtype=F32) / half)
    used = ROT_A // 2 + ROT_IDX // 2
    return jnp.concatenate([inv(ROT_A), inv(ROT_IDX), jnp.zeros((128 - used,), F32)])[None, :]


_SPLIT_EDGES = [0] + [int(s) for s in np.cumsum(SPLIT_SIZES)]
_PACK_RUNS = ((OFF_GA, _SPLIT_EDGES[10], 2 * D_MODEL),
              (OFF_QA, _SPLIT_EDGES[0], OFF_QB - OFF_QA),
              (OFF_QB, _SPLIT_EDGES[6], N_MAIN - OFF_QB))


def _repack_kernel(*refs):
    blocks, o_ref = refs[:-1], refs[-1]
    j = pl.program_id(0)
    nb = TN_IN // 128
    lane = lax.broadcasted_iota(jnp.int32, (o_ref.shape[0], 128), 1)
    for out0, src0, width in _PACK_RUNS:
        shift = src0 % 128

        @pl.when((j >= out0 // TN_IN) & (j < (out0 + width) // TN_IN))
        def _(shift=shift):
            if shift == 0:
                for k in range(nb):
                    o_ref[:, k * 128:(k + 1) * 128] = blocks[k][...].astype(BF16)
            else:
                rolled = [pltpu.roll(blk[...], 128 - shift, 1) for blk in blocks]
                for k in range(nb):
                    o_ref[:, k * 128:(k + 1) * 128] = jnp.where(lane < 128 - shift, rolled[k],
                                                                rolled[k + 1]).astype(BF16)


def _pack_w_in(w):
    rows = w.shape[0]
    nb = TN_IN // 128
    for out0, src0, width in _PACK_RUNS:
        assert out0 % TN_IN == 0 and width % TN_IN == 0

    def first_block(j):
        blk = jnp.int32(0)
        for out0, src0, width in _PACK_RUNS:
            blk = jnp.where(j >= out0 // TN_IN, src0 // 128 + nb * (j - out0 // TN_IN), blk)
        return blk

    assert list(_PACK_RUNS) == sorted(_PACK_RUNS)
    last_block = (w.shape[1] - 1) // 128
    in_specs = [pl.BlockSpec((rows, 128), (lambda j, d=d: (0, jnp.minimum(first_block(j) + d, last_block))))
                for d in range(nb + 1)]
    main = pl.pallas_call(
        _repack_kernel,
        grid=(N_MAIN // TN_IN,),
        in_specs=in_specs,
        out_specs=pl.BlockSpec((rows, TN_IN), lambda j: (0, j)),
        out_shape=jax.ShapeDtypeStruct((rows, N_MAIN), BF16),
        compiler_params=_cparams(("arbitrary",)),
        name="repack_w_in",
    )(*([w] * (nb + 1)))
    k_i, w_i, f_b = (w[:, _SPLIT_EDGES[k]:_SPLIT_EDGES[k + 1]] for k in (4, 5, 9))
    pad = jnp.zeros((rows, 128 - D_IDX - H_IDX - HB), w.dtype)
    small = jnp.concatenate([k_i, w_i, f_b, pad], axis=1).astype(BF16)
    return main, small


def kernel(x, p, positions, g_mix, w_in, b_f, w_o_a, w_o_b, w_out, g_ffn, w_ffn_gate, w_ffn_up,
           w_ffn_down, g_ple, w_ple_gate, w_ple_proj, g_final):
    b, t_len, d = x.shape
    depth = w_in.shape[0]
    m = b * t_len
    top_k = min(TOPK_MAX, t_len // 4)
    inv_freq = _rope_lane_freqs()
    pos2 = positions.reshape(m, 1)
    x2 = x.reshape(m, d)
    tm_in = min(1024, m)
    tm = min(512, m)
    for i in range(depth):
        w_main, w_small = _pack_w_in(w_in[i])
        bf_row = jnp.zeros((1, 128), F32).at[0, LANE_FB:LANE_FB + HB].set(b_f[i].astype(F32))
        proj, kie, kio, wlf, vt_a, vt_b = _inproj(x2, pos2, g_mix[i][None, :], w_main, w_small, bf_row,
                                                  inv_freq, min(tm_in, t_len), b, t_len)
        cb = _cumsum(wlf, b, t_len)
        out_a = _attn_a(proj, vt_a, kie, kio, wlf, b, t_len, top_k)
        out_b = _attn_b(proj, vt_b, cb, b, t_len)
        x2 = _merge(x2, out_a, out_b, proj, w_o_a[i].astype(BF16), w_o_b[i].astype(BF16),
                    w_out[i].astype(BF16), tm)
        x2 = _ffn(x2, g_ffn[i][None, :], w_ffn_gate[i].astype(BF16), w_ffn_up[i].astype(BF16),
                  w_ffn_down[i].astype(BF16), tm_in, 512)
        x2 = _ple(x2, p[i].reshape(m, D_PLE), g_ple[i][None, :], g_final[None, :], w_ple_gate[i].astype(BF16),
                  w_ple_proj[i].astype(BF16), tm, final_norm=(i + 1 == depth))
    return x2.reshape(b, t_len, d)
```

```python
import functools

import jax
import jax.numpy as jnp
import numpy as np
from jax import lax
from jax.experimental import pallas as pl
from jax.experimental.pallas import tpu as pltpu

F32 = jnp.float32
BF16 = jnp.bfloat16

D_MODEL = 2048
HEAD_DIM = 128
HA = 8
HA_KV = 2
H_IDX = 16
D_IDX = 64
TOPK_MAX = 256
HB = 8
ROPE_THETA = 500000.0
ROT_A = HEAD_DIM // 4
ROT_IDX = D_IDX // 4
Q_BLOCK = 128
D_FF = -(-8 * D_MODEL // (3 * 256)) * 256
D_PLE = 256
EPS = 1e-6

SPLIT_SIZES = (HA * HEAD_DIM, HA_KV * HEAD_DIM, HA_KV * HEAD_DIM, H_IDX * D_IDX, D_IDX, H_IDX,
               HB * HEAD_DIM, HB * HEAD_DIM, HB * HEAD_DIM, HB, D_MODEL, D_MODEL)

OFF_GA = 0
OFF_GB = OFF_GA + D_MODEL
OFF_QA = OFF_GB + D_MODEL
OFF_KA = OFF_QA + HA * HEAD_DIM
OFF_VA = OFF_KA + HA_KV * HEAD_DIM
OFF_QI = OFF_VA + HA_KV * HEAD_DIM
OFF_QB = OFF_QI + H_IDX * D_IDX
OFF_KB = OFF_QB + HB * HEAD_DIM
OFF_VB = OFF_KB + HB * HEAD_DIM
N_MAIN = OFF_VB + HB * HEAD_DIM
_SPLIT_EDGES = [0] + [int(s) for s in np.cumsum(SPLIT_SIZES)]
_PACK_RUNS = ((OFF_GA, _SPLIT_EDGES[10], 2 * D_MODEL),
              (OFF_QA, _SPLIT_EDGES[0], OFF_QB - OFF_QA),
              (OFF_QB, _SPLIT_EDGES[6], N_MAIN - OFF_QB))
LANE_KI = 0
LANE_WI = D_IDX
LANE_FB = D_IDX + H_IDX

TN_IN = 512
MXU_N = 256
NEG = -1e30
VMEM_LIMIT = 56 * 1024 * 1024
LOG2E = 1.4426950408889634
ATTN_SCALE = HEAD_DIM ** -0.5 * LOG2E
VT_ROWS = HEAD_DIM + 16
HEADS_PER_STEP_B = 4
BISECT_FIRST = 12
BISECT_ROUND = 4
BISECT_MAX_ITERS = 48


def _cparams(sem):
    return pltpu.CompilerParams(dimension_semantics=sem, vmem_limit_bytes=VMEM_LIMIT)


def _rms(x, g):
    return x * lax.rsqrt(jnp.mean(x * x, axis=-1, keepdims=True) + EPS) * g


def _resident(shape):
    nd = len(shape)
    return pl.BlockSpec(shape, lambda *_: (0,) * nd, pipeline_mode=pl.Buffered(1))


def _rope128(a, c, sa, sb, shift):
    outs = []
    for q in range(a.shape[1] // 128):
        aq = a[:, q * 128:(q + 1) * 128]
        outs.append(aq * c + pltpu.roll(aq, 128 - shift, 1) * sa + pltpu.roll(aq, shift, 1) * sb)
    return outs[0] if len(outs) == 1 else jnp.concatenate(outs, axis=1)


def _inproj_kernel(x_ref, pos_ref, g_ref, w_ref, ws_ref, bf_ref, inv_ref,
                   o_ref, kie_ref, kio_ref, wlf_ref, vta_ref, vtb_ref,
                   h_scr, ca_scr, saa_scr, sab_scr, ci_scr, sia_scr, sib_scr):
    j = pl.program_id(1)
    tm = x_ref.shape[0]

    @pl.when(j == 0)
    def _():
        hb = _rms(x_ref[...], g_ref[...]).astype(BF16)
        h_scr[...] = hb
        pos = pos_ref[...].astype(F32)
        lane = lax.broadcasted_iota(jnp.int32, (tm, 128), 1)
        ha, hi = ROT_A // 2, ROT_IDX // 2
        ang = pos * inv_ref[...]
        cs = jnp.cos(ang)
        sn = jnp.sin(ang)

        def at(tab, src, dst):
            return pltpu.roll(tab, (dst - src) % 128, 1)

        def lanes(lo, n):
            return (lane >= lo) & (lane < lo + n)

        ca_scr[...] = jnp.where(lanes(0, ha), cs, jnp.where(lanes(ha, ha), at(cs, 0, ha), 1.0))
        saa_scr[...] = jnp.where(lanes(0, ha), -sn, 0.0)
        sab_scr[...] = jnp.where(lanes(ha, ha), at(sn, 0, ha), 0.0)
        ci = jnp.ones((tm, 128), F32)
        sia = jnp.zeros((tm, 128), F32)
        sib = jnp.zeros((tm, 128), F32)
        for head0 in (0, D_IDX):
            ci = jnp.where(lanes(head0, hi), at(cs, ha, head0), ci)
            ci = jnp.where(lanes(head0 + hi, hi), at(cs, ha, head0 + hi), ci)
            sia = jnp.where(lanes(head0, hi), -at(sn, ha, head0), sia)
            sib = jnp.where(lanes(head0 + hi, hi), at(sn, ha, head0 + hi), sib)
        ci_scr[...] = ci
        sia_scr[...] = sia
        sib_scr[...] = sib
        small = lax.dot_general(hb, ws_ref[...].astype(BF16), (((1,), (1,)), ((), ())),
                                preferred_element_type=F32)
        kr = _rope128(small, ci, sia, sib, ROT_IDX // 2)
        ke = jnp.where(lane < D_IDX, kr, 0.0)
        kie_ref[...] = ke.astype(BF16)
        kio_ref[...] = pltpu.roll(ke, D_IDX, 1).astype(BF16)
        is_w = (lane >= LANE_WI) & (lane < LANE_WI + H_IDX)
        is_f = (lane >= LANE_FB) & (lane < LANE_FB + HB)
        logf = jax.nn.log_sigmoid(small + bf_ref[...])
        wlf_ref[...] = jnp.where(is_w, small * (H_IDX ** -0.5 * D_IDX ** -0.5), jnp.where(is_f, logf, 0.0))

    t = TN_IN
    groups = t // MXU_N
    heads_per_group = MXU_N // HEAD_DIM

    def project(epilogues):
        for c, epilogue in enumerate(epilogues):
            cols = slice(c * MXU_N, (c + 1) * MXU_N)
            w_t = w_ref[cols, :].astype(BF16)
            acc = lax.dot_general(h_scr[...], w_t, (((1,), (1,)), ((), ())), preferred_element_type=F32)
            o_ref[:, cols] = epilogue(acc).astype(BF16)

    def rope_a(acc):
        return _rope128(acc, ca_scr[...], saa_scr[...], sab_scr[...], ROT_A // 2)

    def rope_i(acc):
        return _rope128(acc, ci_scr[...], sia_scr[...], sib_scr[...], ROT_IDX // 2)

    def values(vt_ref, c):
        def epilogue(acc):
            acc_t = acc.T
            for hh in range(heads_per_group):
                h = c * heads_per_group + hh
                vt_ref[h, 0:HEAD_DIM, :] = acc_t[hh * HEAD_DIM:(hh + 1) * HEAD_DIM, :].astype(BF16)
                vt_ref[h, HEAD_DIM:VT_ROWS, :] = jnp.ones((VT_ROWS - HEAD_DIM, tm), BF16)
            return acc
        return epilogue

    @pl.when(j < OFF_QA // t)
    def _():
        project([jax.nn.sigmoid] * groups)

    @pl.when((j >= OFF_QA // t) & (j < OFF_KA // t))
    def _():
        project([lambda acc: rope_a(acc) * ATTN_SCALE] * groups)

    @pl.when(j == OFF_KA // t)
    def _():
        project([rope_a, values(vta_ref, 0)])

    @pl.when((j >= OFF_QI // t) & (j < OFF_QB // t))
    def _():
        project([rope_i] * groups)

    @pl.when((j >= OFF_QB // t) & (j < OFF_KB // t))
    def _():
        project([lambda acc: acc * ATTN_SCALE] * groups)

    @pl.when((j >= OFF_KB // t) & (j < OFF_VB // t))
    def _():
        project([lambda acc: acc] * groups)

    @pl.when(j >= OFF_VB // t)
    def _():
        project([values(vtb_ref, c) for c in range(groups)])


def _w_in_row(j):
    row = jnp.int32(0)
    for out0, src0, _ in _PACK_RUNS:
        row = jnp.where(j >= out0 // TN_IN, src0 + TN_IN * (j - out0 // TN_IN), row)
    return pl.multiple_of(row, 8)


def _inproj(x2, pos2, g, w_t, w_small, bf_row, inv_freq, tm, b, t_len):
    m = x2.shape[0]
    assert OFF_VA - OFF_KA == MXU_N and OFF_QI - OFF_KA == TN_IN
    assert HA_KV * HEAD_DIM == MXU_N
    for out0, src0, width in _PACK_RUNS:
        assert out0 % TN_IN == 0 and width % TN_IN == 0 and src0 % 8 == 0
    assert list(_PACK_RUNS) == sorted(_PACK_RUNS)
    grid = (m // tm, N_MAIN // TN_IN)
    nt = t_len // tm
    j_vb = OFF_VB // TN_IN
    vb_tiles = HB * HEAD_DIM // TN_IN
    heads_per_tile = TN_IN // HEAD_DIM
    row128 = pl.BlockSpec((1, 128), lambda i, j: (0, 0))
    return pl.pallas_call(
        _inproj_kernel,
        grid=grid,
        in_specs=[
            pl.BlockSpec((tm, D_MODEL), lambda i, j: (i, 0)),
            pl.BlockSpec((tm, 1), lambda i, j: (i, 0)),
            pl.BlockSpec((1, D_MODEL), lambda i, j: (0, 0)),
            pl.BlockSpec((pl.Element(TN_IN), pl.Element(D_MODEL)), lambda i, j: (_w_in_row(j), 0)),
            pl.BlockSpec((128, D_MODEL), lambda i, j: (0, 0)),
            row128, row128,
        ],
        out_specs=[
            pl.BlockSpec((tm, TN_IN), lambda i, j: (i, j)),
            pl.BlockSpec((tm, 128), lambda i, j: (i, 0)),
            pl.BlockSpec((tm, 128), lambda i, j: (i, 0)),
            pl.BlockSpec((tm, 128), lambda i, j: (i, 0)),
            pl.BlockSpec((None, HA_KV, VT_ROWS, tm), lambda i, j: (i // nt, 0, 0, i % nt)),
            pl.BlockSpec((None, heads_per_tile, VT_ROWS, tm),
                         lambda i, j: (i // nt, jnp.clip(j - j_vb, 0, vb_tiles - 1), 0, i % nt)),
        ],
        out_shape=[
            jax.ShapeDtypeStruct((m, N_MAIN), BF16),
            jax.ShapeDtypeStruct((m, 128), BF16),
            jax.ShapeDtypeStruct((m, 128), BF16),
            jax.ShapeDtypeStruct((m, 128), F32),
            jax.ShapeDtypeStruct((b, HA_KV, VT_ROWS, t_len), BF16),
            jax.ShapeDtypeStruct((b, HB, VT_ROWS, t_len), BF16),
        ],
        scratch_shapes=[pltpu.VMEM((tm, D_MODEL), BF16)] + [pltpu.VMEM((tm, 128), F32)] * 6,
        compiler_params=_cparams(("arbitrary", "arbitrary")),
        name="inproj",
    )(x2, pos2, g, w_t, w_small, bf_row, inv_freq)


def _cumsum_kernel(wlf_ref, o_ref, carry_scr, *, tc):
    @pl.when(pl.program_id(1) == 0)
    def _():
        carry_scr[...] = jnp.zeros_like(carry_scr)

    lf = wlf_ref[...]
    r = lax.broadcasted_iota(jnp.int32, (tc, tc), 0)
    c = lax.broadcasted_iota(jnp.int32, (tc, tc), 1)
    tri = jnp.where(c <= r, 1.0, 0.0).astype(BF16)
    hi = lf.astype(BF16)
    r1 = lf - hi.astype(F32)
    mid = r1.astype(BF16)
    lo = (r1 - mid.astype(F32)).astype(BF16)
    cs = (jnp.dot(tri, hi, preferred_element_type=F32) + jnp.dot(tri, mid, preferred_element_type=F32)
          + jnp.dot(tri, lo, preferred_element_type=F32)) + carry_scr[0:1, :]
    carry_scr[...] = jnp.broadcast_to(cs[tc - 1:tc, :], carry_scr.shape)
    cs2 = cs * LOG2E
    for h in range(HB):
        o_ref[h] = jnp.broadcast_to(cs2[:, LANE_FB + h:LANE_FB + h + 1], (tc, 128))


def _cumsum(wlf, b, t_len):
    tc = 256
    nt = t_len // tc
    return pl.pallas_call(
        functools.partial(_cumsum_kernel, tc=tc),
        grid=(b, nt),
        in_specs=[pl.BlockSpec((tc, 128), lambda bi, ti: (bi * nt + ti, 0))],
        out_specs=pl.BlockSpec((None, HB, tc, 128), lambda bi, ti: (bi, 0, ti, 0)),
        out_shape=jax.ShapeDtypeStruct((b, HB, t_len, 128), F32),
        scratch_shapes=[pltpu.VMEM((8, 128), F32)],
        compiler_params=_cparams(("arbitrary", "arbitrary")),
        name="fcumsum",
    )(wlf)


def _attn_a_kernel(qa_ref, qi0_ref, qi1_ref, ka_ref, vt_ref, kie_ref, kio_ref, wlf_ref, o_ref,
                   sc_scr, qs_scr, qg_scr, lt_scr, acc_scr, *, tk, top_k, n_iter):
    i = pl.program_id(1)
    nq = Q_BLOCK
    group = HA // HA_KV
    gw = group * nq
    nchunks = ((i + 1) * nq + tk - 1) // tk
    q_pos = i * nq + lax.broadcasted_iota(jnp.int32, (1, nq), 1)
    wt = wlf_ref[...].T
    half = H_IDX // 4
    for p in range(H_IDX // 2):
        src = qi0_ref if p < half else qi1_ref
        qs_scr[p * nq:(p + 1) * nq, :] = src[:, (p % half) * 128:(p % half + 1) * 128]

    def score_chunk(c, carry):
        rmin, rmax = carry
        off = pl.multiple_of(c * tk, tk)
        ke = kie_ref[pl.ds(off, tk), :]
        ko = kio_ref[pl.ds(off, tk), :]
        s = jnp.zeros((tk, nq), F32)
        pairs_per_dot = MXU_N // nq
        for pg in range(H_IDX // 2 // pairs_per_dot):
            qs = qs_scr[pg * MXU_N:(pg + 1) * MXU_N, :]
            de = lax.dot_general(ke, qs, (((1,), (1,)), ((), ())), preferred_element_type=F32)
            do = lax.dot_general(ko, qs, (((1,), (1,)), ((), ())), preferred_element_type=F32)
            for pp in range(pairs_per_dot):
                p = pg * pairs_per_dot + pp
                cols = slice(pp * nq, (pp + 1) * nq)
                s = s + jnp.maximum(de[:, cols], 0.0) * wt[LANE_WI + 2 * p:LANE_WI + 2 * p + 1, :]
                s = s + jnp.maximum(do[:, cols], 0.0) * wt[LANE_WI + 2 * p + 1:LANE_WI + 2 * p + 2, :]
        kpos = off + lax.broadcasted_iota(jnp.int32, (tk, 1), 0)
        valid = kpos <= q_pos
        sc_scr[pl.ds(off, tk), :] = jnp.where(valid, s, NEG)
        rmin = jnp.minimum(rmin, jnp.min(jnp.where(valid, s, -NEG), axis=0, keepdims=True))
        rmax = jnp.maximum(rmax, jnp.max(jnp.where(valid, s, NEG), axis=0, keepdims=True))
        return rmin, rmax

    lo, hi = lax.fori_loop(0, nchunks, score_chunk,
                           (jnp.full((1, nq), -NEG, F32), jnp.full((1, nq), NEG, F32)))

    def bisect(_, carry):
        lo, hi = carry
        mid = 0.5 * (lo + hi)

        def count_chunk(c, part):
            off = pl.multiple_of(c * tk, tk)
            ind = jnp.where(sc_scr[pl.ds(off, tk), :] >= mid, 1.0, 0.0)
            return part + jnp.sum(ind.reshape(tk // 64, 64, nq), axis=0)

        part = lax.fori_loop(0, nchunks, count_chunk, jnp.zeros((64, nq), F32))
        cnt = jnp.sum(part, axis=0, keepdims=True)
        ge = cnt >= top_k
        return jnp.where(ge, mid, lo), jnp.where(cnt == top_k, mid, jnp.where(ge, hi, mid))

    searching = q_pos >= top_k

    def bisect_round(state):
        it, lo, hi, _ = state
        lo, hi = lax.fori_loop(0, BISECT_ROUND, bisect, (lo, hi))
        pending = jnp.sum(jnp.where(searching & (lo < hi), 1.0, 0.0))
        return it + BISECT_ROUND, lo, hi, pending

    lo, hi = lax.fori_loop(0, BISECT_FIRST, bisect, (lo, hi))
    _, thr, _, _ = lax.while_loop(lambda st: (st[0] < n_iter) & (st[3] > 0.0), bisect_round,
                                  (jnp.int32(BISECT_FIRST), lo, hi, jnp.float32(1.0)))

    tka = tk // 2
    last = (nchunks - 1) * tk
    for n in range(HA_KV):
        for g in range(group):
            h = n * group + g
            qg_scr[n, g * nq:(g + 1) * nq, :] = qa_ref[:, h * HEAD_DIM:(h + 1) * HEAD_DIM]
    acc_scr[...] = jnp.zeros(acc_scr.shape, F32)

    def qk(off, slot):
        msk = jnp.where(sc_scr[pl.ds(off, tka), :] >= thr, 0.0, NEG)
        msk = jnp.concatenate([msk] * group, axis=1)
        cmax = []
        for n in range(HA_KV):
            k = ka_ref[pl.ds(off, tka), n * HEAD_DIM:(n + 1) * HEAD_DIM]
            s = lax.dot_general(k, qg_scr[n], (((1,), (1,)), ((), ())), preferred_element_type=F32) + msk
            lt_scr[n, slot] = s
            cmax.append(jnp.max(s, axis=0, keepdims=True))
        return tuple(cmax)

    def softmax_pv(off, slot, m_old, cmax):
        new = []
        for n in range(HA_KV):
            m_new = jnp.maximum(m_old[n], cmax[n])
            alpha = jnp.exp2(m_old[n] - m_new)
            p = jnp.exp2(lt_scr[n, slot] - m_new).astype(BF16)
            acc_scr[n] = alpha * acc_scr[n] + jnp.dot(vt_ref[n, :, pl.ds(off, tka)], p,
                                                      preferred_element_type=F32)
            new.append(m_new)
        return tuple(new)

    def pair(j, carry):
        m, cmax0 = carry
        off0 = pl.multiple_of(j * tk, tk)
        off1 = pl.multiple_of(off0 + tka, tka)
        off2 = pl.multiple_of(jnp.minimum(off0 + tk, last), tk)
        cmax1 = qk(off1, 1)
        m = softmax_pv(off0, 0, m, cmax0)
        cmax0 = qk(off2, 0)
        return softmax_pv(off1, 1, m, cmax1), cmax0

    lax.fori_loop(0, nchunks, pair, ((jnp.full((1, gw), NEG, F32),) * HA_KV, qk(0, 0)))
    for n in range(HA_KV):
        out_t = acc_scr[n, 0:HEAD_DIM, :] / acc_scr[n, HEAD_DIM:HEAD_DIM + 1, :]
        for g in range(group):
            h = n * group + g
            o_ref[:, h * HEAD_DIM:(h + 1) * HEAD_DIM] = out_t[:, g * nq:(g + 1) * nq].T.astype(BF16)


def _attn_a(proj, vt_a, kie, kio, wlf, b, t_len, top_k):
    nb = t_len // Q_BLOCK
    wa = HA * HEAD_DIM
    wkv = HA_KV * HEAD_DIM
    wqi = H_IDX * D_IDX // 2
    tk = min(512, t_len)
    gw = HA // HA_KV * Q_BLOCK
    kern = functools.partial(_attn_a_kernel, tk=tk, top_k=top_k, n_iter=BISECT_MAX_ITERS)
    return pl.pallas_call(
        kern,
        grid=(b, nb),
        in_specs=[
            pl.BlockSpec((Q_BLOCK, wa), lambda bi, i: (bi * nb + i, OFF_QA // wa)),
            pl.BlockSpec((Q_BLOCK, wqi), lambda bi, i: (bi * nb + i, OFF_QI // wqi)),
            pl.BlockSpec((Q_BLOCK, wqi), lambda bi, i: (bi * nb + i, OFF_QI // wqi + 1)),
            pl.BlockSpec((t_len, wkv), lambda bi, i: (bi, OFF_KA // wkv)),
            pl.BlockSpec((None, HA_KV, VT_ROWS, t_len), lambda bi, i: (bi, 0, 0, 0)),
            pl.BlockSpec((t_len, 128), lambda bi, i: (bi, 0)),
            pl.BlockSpec((t_len, 128), lambda bi, i: (bi, 0)),
            pl.BlockSpec((Q_BLOCK, 128), lambda bi, i: (bi * nb + i, 0)),
        ],
        out_specs=pl.BlockSpec((Q_BLOCK, wa), lambda bi, i: (bi * nb + i, 0)),
        out_shape=jax.ShapeDtypeStruct((b * t_len, wa), BF16),
        scratch_shapes=[
            pltpu.VMEM((t_len, Q_BLOCK), F32),
            pltpu.VMEM((H_IDX // 2 * Q_BLOCK, 128), BF16),
            pltpu.VMEM((HA_KV, gw, HEAD_DIM), BF16),
            pltpu.VMEM((HA_KV, 2, tk // 2, gw), F32),
            pltpu.VMEM((HA_KV, VT_ROWS, gw), F32),
        ],
        compiler_params=_cparams(("arbitrary", "arbitrary")),
        name="attn_a",
    )(proj, proj, proj, proj, vt_a, kie, kio, wlf)


def _attn_b_kernel(q_ref, k_ref, vt_ref, cb_ref, o_ref, lt_scr, acc_scr, *, tq):
    qi = pl.program_id(2)
    tk = tq // 2
    heads = HEADS_PER_STEP_B
    q0 = pl.multiple_of(qi * tq, tq)
    acc_scr[...] = jnp.zeros(acc_scr.shape, F32)

    def qk(off, slot, key0=None):
        cmax = []
        for n in range(heads):
            cols = slice(n * HEAD_DIM, (n + 1) * HEAD_DIM)
            bias = cb_ref[n, pl.ds(q0, 1), :] - cb_ref[n, pl.ds(off, tk), :]
            s = lax.dot_general(k_ref[pl.ds(off, tk), cols], q_ref[:, cols], (((1,), (1,)), ((), ())),
                                preferred_element_type=F32) + jnp.concatenate([bias] * (tq // 128), axis=1)
            if key0 is not None:
                key = key0 + lax.broadcasted_iota(jnp.int32, (tk, 1), 0)
                qry = lax.broadcasted_iota(jnp.int32, (1, tq), 1)
                s = jnp.where(key <= qry, s, NEG)
            lt_scr[n, slot] = s
            cmax.append(jnp.max(s, axis=0, keepdims=True))
        return tuple(cmax)

    def softmax_pv(off, slot, m_old, cmax):
        new = []
        for n in range(heads):
            m_new = jnp.maximum(m_old[n], cmax[n])
            alpha = jnp.exp2(m_old[n] - m_new)
            p = jnp.exp2(lt_scr[n, slot] - m_new).astype(BF16)
            acc_scr[n] = alpha * acc_scr[n] + jnp.dot(vt_ref[n, :, pl.ds(off, tk)], p,
                                                      preferred_element_type=F32)
            new.append(m_new)
        return tuple(new)

    q1 = pl.multiple_of(q0 + tk, tk)
    m = (jnp.full((1, tq), NEG, F32),) * heads
    cmax0 = qk(q0, 0, key0=0)
    cmax1 = qk(q1, 1, key0=tk)
    m = softmax_pv(q0, 0, m, cmax0)
    cmax0 = qk(0, 0)
    m = softmax_pv(q1, 1, m, cmax1)
    last = jnp.maximum(qi - 1, 0) * tq

    def pair(j, carry):
        m, cmax0 = carry
        off0 = pl.multiple_of(j * tq, tq)
        off1 = pl.multiple_of(off0 + tk, tk)
        off2 = pl.multiple_of(jnp.minimum(off0 + tq, last), tq)
        cmax1 = qk(off1, 1)
        m = softmax_pv(off0, 0, m, cmax0)
        cmax0 = qk(off2, 0)
        return softmax_pv(off1, 1, m, cmax1), cmax0

    lax.fori_loop(0, qi, pair, (m, cmax0))
    for n in range(heads):
        out_t = acc_scr[n, 0:HEAD_DIM, :] / acc_scr[n, HEAD_DIM:HEAD_DIM + 1, :]
        for g in range(tq // 128):
            o_ref[g * 128:(g + 1) * 128, n * HEAD_DIM:(n + 1) * HEAD_DIM] = (
                out_t[:, g * 128:(g + 1) * 128].T.astype(BF16))


def _attn_b(proj, vt_b, cb, b, t_len):
    tq = 512
    nq = t_len // tq
    hps = HEADS_PER_STEP_B
    wh = hps * HEAD_DIM
    return pl.pallas_call(
        functools.partial(_attn_b_kernel, tq=tq),
        grid=(b, HB // hps, nq),
        in_specs=[
            pl.BlockSpec((tq, wh), lambda bi, h, i: (bi * nq + i, OFF_QB // wh + h)),
            pl.BlockSpec((t_len, wh), lambda bi, h, i: (bi, OFF_KB // wh + h)),
            pl.BlockSpec((None, hps, VT_ROWS, t_len), lambda bi, h, i: (bi, h, 0, 0)),
            pl.BlockSpec((None, hps, t_len, 128), lambda bi, h, i: (bi, h, 0, 0)),
        ],
        out_specs=pl.BlockSpec((tq, wh), lambda bi, h, i: (bi * nq + i, h)),
        out_shape=jax.ShapeDtypeStruct((b * t_len, HB * HEAD_DIM), BF16),
        scratch_shapes=[pltpu.VMEM((hps, 2, tq // 2, tq), F32), pltpu.VMEM((hps, VT_ROWS, tq), F32)],
        compiler_params=_cparams(("arbitrary", "arbitrary", "arbitrary")),
        name="attn_b",
    )(proj, proj, vt_b, cb)


def _merge_kernel(x_ref, oa_ref, ob_ref, ga_ref, gb_ref, woa_ref, wob_ref, wout_ref, o_ref):
    ya = jnp.dot(oa_ref[...], woa_ref[...], preferred_element_type=F32)
    yb = jnp.dot(ob_ref[...], wob_ref[...], preferred_element_type=F32)
    mixed = ga_ref[...].astype(F32) * ya + gb_ref[...].astype(F32) * yb
    o_ref[...] = x_ref[...] + jnp.dot(mixed.astype(BF16), wout_ref[...], preferred_element_type=F32)


def _merge(x2, out_a, out_b, proj, w_oa, w_ob, w_out, tm):
    m = x2.shape[0]
    wa = HA * HEAD_DIM
    wb = HB * HEAD_DIM
    return pl.pallas_call(
        _merge_kernel,
        grid=(m // tm,),
        in_specs=[
            pl.BlockSpec((tm, D_MODEL), lambda i: (i, 0)),
            pl.BlockSpec((tm, wa), lambda i: (i, 0)),
            pl.BlockSpec((tm, wb), lambda i: (i, 0)),
            pl.BlockSpec((tm, D_MODEL), lambda i: (i, OFF_GA // D_MODEL)),
            pl.BlockSpec((tm, D_MODEL), lambda i: (i, OFF_GB // D_MODEL)),
            _resident((wa, D_MODEL)),
            _resident((wb, D_MODEL)),
            _resident((D_MODEL, D_MODEL)),
        ],
        out_specs=pl.BlockSpec((tm, D_MODEL), lambda i: (i, 0)),
        out_shape=jax.ShapeDtypeStruct((m, D_MODEL), F32),
        compiler_params=_cparams(("arbitrary",)),
        name="merge",
    )(x2, out_a, out_b, proj, proj, w_oa, w_ob, w_out)


def _ffn_kernel(x_ref, g_ref, wg_ref, wu_ref, wd_ref, o_ref, h_scr):
    @pl.when(pl.program_id(1) == 0)
    def _():
        x = x_ref[...]
        h_scr[...] = _rms(x, g_ref[...]).astype(BF16)
        o_ref[...] = x

    h = h_scr[...]
    gate = jnp.dot(h, wg_ref[...], preferred_element_type=F32)
    up = jnp.dot(h, wu_ref[...], preferred_element_type=F32)
    act = (jax.nn.silu(gate) * up).astype(BF16)
    o_ref[...] += jnp.dot(act, wd_ref[...], preferred_element_type=F32)


def _ffn(x2, g, w_gate, w_up, w_down, tm, tf):
    m = x2.shape[0]
    return pl.pallas_call(
        _ffn_kernel,
        grid=(m // tm, D_FF // tf),
        in_specs=[
            pl.BlockSpec((tm, D_MODEL), lambda i, f: (i, 0)),
            pl.BlockSpec((1, D_MODEL), lambda i, f: (0, 0)),
            pl.BlockSpec((D_MODEL, tf), lambda i, f: (0, f)),
            pl.BlockSpec((D_MODEL, tf), lambda i, f: (0, f)),
            pl.BlockSpec((tf, D_MODEL), lambda i, f: (f, 0)),
        ],
        out_specs=pl.BlockSpec((tm, D_MODEL), lambda i, f: (i, 0)),
        out_shape=jax.ShapeDtypeStruct((m, D_MODEL), F32),
        scratch_shapes=[pltpu.VMEM((tm, D_MODEL), BF16)],
        compiler_params=_cparams(("arbitrary", "arbitrary")),
        name="ffn",
    )(x2, g, w_gate, w_up, w_down)


def _ple_kernel(x_ref, p_ref, g_ref, gf_ref, wg_ref, wp_ref, o_ref, *, final_norm):
    x = x_ref[...]
    h = _rms(x, g_ref[...]).astype(BF16)
    gate = jax.nn.sigmoid(jnp.dot(h, wg_ref[...], preferred_element_type=F32))
    emb = jnp.dot(p_ref[...].astype(BF16), wp_ref[...], preferred_element_type=F32)
    y = x + gate * emb
    o_ref[...] = _rms(y, gf_ref[...]) if final_norm else y


def _ple(x2, p2, g, g_final, w_gate, w_proj, tm, final_norm):
    m = x2.shape[0]
    return pl.pallas_call(
        functools.partial(_ple_kernel, final_norm=final_norm),
        grid=(m // tm,),
        in_specs=[
            pl.BlockSpec((tm, D_MODEL), lambda i: (i, 0)),
            pl.BlockSpec((tm, D_PLE), lambda i: (i, 0)),
            pl.BlockSpec((1, D_MODEL), lambda i: (0, 0)),
            pl.BlockSpec((1, D_MODEL), lambda i: (0, 0)),
            _resident((D_MODEL, D_MODEL)),
            _resident((D_PLE, D_MODEL)),
        ],
        out_specs=pl.BlockSpec((tm, D_MODEL), lambda i: (i, 0)),
        out_shape=jax.ShapeDtypeStruct((m, D_MODEL), F32),
        compiler_params=_cparams(("arbitrary",)),
        name="ple",
    )(x2, p2, g, g_final, w_gate, w_proj)


def _rope_lane_freqs():
    def inv(rot):
        half = rot // 2
        return ROPE_THETA ** (-jnp.arange(half, dtype=F32) / half)
    used = ROT_A // 2 + ROT_IDX // 2
    return jnp.concatenate([inv(ROT_A), inv(ROT_IDX), jnp.zeros((128 - used,), F32)])[None, :]


def _w_in_views(w):
    w_t = jnp.swapaxes(w, 0, 1)
    rows = [w_t[_SPLIT_EDGES[k]:_SPLIT_EDGES[k + 1]] for k in (4, 5, 9)]
    pad = jnp.zeros((128 - D_IDX - H_IDX - HB, w_t.shape[1]), w_t.dtype)
    return w_t, jnp.concatenate(rows + [pad], axis=0)


def kernel(x, p, positions, g_mix, w_in, b_f, w_o_a, w_o_b, w_out, g_ffn, w_ffn_gate, w_ffn_up,
           w_ffn_down, g_ple, w_ple_gate, w_ple_proj, g_final):
    b, t_len, d = x.shape
    depth = w_in.shape[0]
    m = b * t_len
    top_k = min(TOPK_MAX, t_len // 4)
    inv_freq = _rope_lane_freqs()
    pos2 = positions.reshape(m, 1)
    x2 = x.reshape(m, d)
    tm_in = min(1024, m)
    tm = min(512, m)
    for i in range(depth):
        w_t, w_small = _w_in_views(w_in[i])
        bf_row = jnp.zeros((1, 128), F32).at[0, LANE_FB:LANE_FB + HB].set(b_f[i].astype(F32))
        proj, kie, kio, wlf, vt_a, vt_b = _inproj(x2, pos2, g_mix[i][None, :], w_t, w_small, bf_row,
                                                  inv_freq, min(tm_in, t_len), b, t_len)
        cb = _cumsum(wlf, b, t_len)
        out_a = _attn_a(proj, vt_a, kie, kio, wlf, b, t_len, top_k)
        out_b = _attn_b(proj, vt_b, cb, b, t_len)
        x2 = _merge(x2, out_a, out_b, proj, w_o_a[i].astype(BF16), w_o_b[i].astype(BF16),
                    w_out[i].astype(BF16), tm)
        x2 = _ffn(x2, g_ffn[i][None, :], w_ffn_gate[i].astype(BF16), w_ffn_up[i].astype(BF16),
                  w_ffn_down[i].astype(BF16), tm_in, 512)
        x2 = _ple(x2, p[i].reshape(m, D_PLE), g_ple[i][None, :], g_final[None, :], w_ple_gate[i].astype(BF16),
                  w_ple_proj[i].astype(BF16), tm, final_norm=(i + 1 == depth))
    return x2.reshape(b, t_len, d)
```

```python
import functools

import jax
import jax.numpy as jnp
import numpy as np
from jax import lax
from jax.experimental import pallas as pl
from jax.experimental.pallas import tpu as pltpu

F32 = jnp.float32
BF16 = jnp.bfloat16

D_MODEL = 2048
HEAD_DIM = 128
HA = 8
HA_KV = 2
H_IDX = 16
D_IDX = 64
TOPK_MAX = 256
HB = 8
ROPE_THETA = 500000.0
ROT_A = HEAD_DIM // 4
ROT_IDX = D_IDX // 4
Q_BLOCK = 128
D_FF = -(-8 * D_MODEL // (3 * 256)) * 256
D_PLE = 256
EPS = 1e-6

SPLIT_SIZES = (HA * HEAD_DIM, HA_KV * HEAD_DIM, HA_KV * HEAD_DIM, H_IDX * D_IDX, D_IDX, H_IDX,
               HB * HEAD_DIM, HB * HEAD_DIM, HB * HEAD_DIM, HB, D_MODEL, D_MODEL)

OFF_GA = 0
OFF_GB = OFF_GA + D_MODEL
OFF_QA = OFF_GB + D_MODEL
OFF_KA = OFF_QA + HA * HEAD_DIM
OFF_VA = OFF_KA + HA_KV * HEAD_DIM
OFF_QI = OFF_VA + HA_KV * HEAD_DIM
OFF_QB = OFF_QI + H_IDX * D_IDX
OFF_KB = OFF_QB + HB * HEAD_DIM
OFF_VB = OFF_KB + HB * HEAD_DIM
N_MAIN = OFF_VB + HB * HEAD_DIM
_SPLIT_EDGES = [0] + [int(s) for s in np.cumsum(SPLIT_SIZES)]
_PACK_RUNS = ((OFF_GA, _SPLIT_EDGES[10], 2 * D_MODEL),
              (OFF_QA, _SPLIT_EDGES[0], OFF_QB - OFF_QA),
              (OFF_QB, _SPLIT_EDGES[6], N_MAIN - OFF_QB))
LANE_KI = 0
LANE_WI = D_IDX
LANE_FB = D_IDX + H_IDX

TN_IN = 512
MXU_N = 256
NEG = -1e30
VMEM_LIMIT = 56 * 1024 * 1024
LOG2E = 1.4426950408889634
ATTN_SCALE = HEAD_DIM ** -0.5 * LOG2E
VT_ROWS = HEAD_DIM + 16
HEADS_PER_STEP_B = 4
BISECT_FIRST = 12
BISECT_ROUND = 4
BISECT_MAX_ITERS = 48


def _cparams(sem):
    return pltpu.CompilerParams(dimension_semantics=sem, vmem_limit_bytes=VMEM_LIMIT)


def _rms(x, g):
    return x * lax.rsqrt(jnp.mean(x * x, axis=-1, keepdims=True) + EPS) * g


def _resident(shape):
    nd = len(shape)
    return pl.BlockSpec(shape, lambda *_: (0,) * nd, pipeline_mode=pl.Buffered(1))


def _rope128(a, c, sa, sb, shift):
    outs = []
    for q in range(a.shape[1] // 128):
        aq = a[:, q * 128:(q + 1) * 128]
        outs.append(aq * c + pltpu.roll(aq, 128 - shift, 1) * sa + pltpu.roll(aq, shift, 1) * sb)
    return outs[0] if len(outs) == 1 else jnp.concatenate(outs, axis=1)


def _inproj_kernel(x_ref, pos_ref, g_ref, w_ref, ws_ref, bf_ref, inv_ref,
                   o_ref, kie_ref, kio_ref, wlf_ref, vta_ref, vtb_ref,
                   h_scr, ca_scr, saa_scr, sab_scr, ci_scr, sia_scr, sib_scr):
    j = pl.program_id(1)
    tm = x_ref.shape[0]

    @pl.when(j == 0)
    def _():
        hb = _rms(x_ref[...], g_ref[...]).astype(BF16)
        h_scr[...] = hb
        pos = pos_ref[...].astype(F32)
        lane = lax.broadcasted_iota(jnp.int32, (tm, 128), 1)
        ha, hi = ROT_A // 2, ROT_IDX // 2
        ang = pos * inv_ref[...]
        cs = jnp.cos(ang)
        sn = jnp.sin(ang)

        def at(tab, src, dst):
            return pltpu.roll(tab, (dst - src) % 128, 1)

        def lanes(lo, n):
            return (lane >= lo) & (lane < lo + n)

        ca_scr[...] = jnp.where(lanes(0, ha), cs, jnp.where(lanes(ha, ha), at(cs, 0, ha), 1.0))
        saa_scr[...] = jnp.where(lanes(0, ha), -sn, 0.0)
        sab_scr[...] = jnp.where(lanes(ha, ha), at(sn, 0, ha), 0.0)
        ci = jnp.ones((tm, 128), F32)
        sia = jnp.zeros((tm, 128), F32)
        sib = jnp.zeros((tm, 128), F32)
        for head0 in (0, D_IDX):
            ci = jnp.where(lanes(head0, hi), at(cs, ha, head0), ci)
            ci = jnp.where(lanes(head0 + hi, hi), at(cs, ha, head0 + hi), ci)
            sia = jnp.where(lanes(head0, hi), -at(sn, ha, head0), sia)
            sib = jnp.where(lanes(head0 + hi, hi), at(sn, ha, head0 + hi), sib)
        ci_scr[...] = ci
        sia_scr[...] = sia
        sib_scr[...] = sib
        small = lax.dot_general(hb, ws_ref[...].astype(BF16), (((1,), (1,)), ((), ())),
                                preferred_element_type=F32)
        kr = _rope128(small, ci, sia, sib, ROT_IDX // 2)
        ke = jnp.where(lane < D_IDX, kr, 0.0)
        kie_ref[...] = ke.astype(BF16)
        kio_ref[...] = pltpu.roll(ke, D_IDX, 1).astype(BF16)
        is_w = (lane >= LANE_WI) & (lane < LANE_WI + H_IDX)
        is_f = (lane >= LANE_FB) & (lane < LANE_FB + HB)
        logf = jax.nn.log_sigmoid(small + bf_ref[...])
        wlf_ref[...] = jnp.where(is_w, small * (H_IDX ** -0.5 * D_IDX ** -0.5), jnp.where(is_f, logf, 0.0))

    t = TN_IN
    groups = t // MXU_N
    heads_per_group = MXU_N // HEAD_DIM

    def project(epilogues):
        for c, epilogue in enumerate(epilogues):
            cols = slice(c * MXU_N, (c + 1) * MXU_N)
            w_t = w_ref[cols, :].astype(BF16)
            acc = lax.dot_general(h_scr[...], w_t, (((1,), (1,)), ((), ())), preferred_element_type=F32)
            o_ref[:, cols] = epilogue(acc).astype(BF16)

    def rope_a(acc):
        return _rope128(acc, ca_scr[...], saa_scr[...], sab_scr[...], ROT_A // 2)

    def rope_i(acc):
        return _rope128(acc, ci_scr[...], sia_scr[...], sib_scr[...], ROT_IDX // 2)

    def values(vt_ref, c):
        def epilogue(acc):
            acc_t = acc.T
            for hh in range(heads_per_group):
                h = c * heads_per_group + hh
                vt_ref[h, 0:HEAD_DIM, :] = acc_t[hh * HEAD_DIM:(hh + 1) * HEAD_DIM, :].astype(BF16)
                vt_ref[h, HEAD_DIM:VT_ROWS, :] = jnp.ones((VT_ROWS - HEAD_DIM, tm), BF16)
            return acc
        return epilogue

    @pl.when(j < OFF_QA // t)
    def _():
        project([jax.nn.sigmoid] * groups)

    @pl.when((j >= OFF_QA // t) & (j < OFF_KA // t))
    def _():
        project([lambda acc: rope_a(acc) * ATTN_SCALE] * groups)

    @pl.when(j == OFF_KA // t)
    def _():
        project([rope_a, values(vta_ref, 0)])

    @pl.when((j >= OFF_QI // t) & (j < OFF_QB // t))
    def _():
        project([rope_i] * groups)

    @pl.when((j >= OFF_QB // t) & (j < OFF_KB // t))
    def _():
        project([lambda acc: acc * ATTN_SCALE] * groups)

    @pl.when((j >= OFF_KB // t) & (j < OFF_VB // t))
    def _():
        project([lambda acc: acc] * groups)

    @pl.when(j >= OFF_VB // t)
    def _():
        project([values(vtb_ref, c) for c in range(groups)])


def _w_in_row(j):
    row = jnp.int32(0)
    for out0, src0, _ in _PACK_RUNS:
        row = jnp.where(j >= out0 // TN_IN, src0 + TN_IN * (j - out0 // TN_IN), row)
    return pl.multiple_of(row, 8)


def _inproj(x2, pos2, g, w_t, w_small, bf_row, inv_freq, tm, b, t_len):
    m = x2.shape[0]
    assert OFF_VA - OFF_KA == MXU_N and OFF_QI - OFF_KA == TN_IN
    assert HA_KV * HEAD_DIM == MXU_N
    for out0, src0, width in _PACK_RUNS:
        assert out0 % TN_IN == 0 and width % TN_IN == 0 and src0 % 8 == 0
    assert list(_PACK_RUNS) == sorted(_PACK_RUNS)
    grid = (m // tm, N_MAIN // TN_IN)
    nt = t_len // tm
    j_vb = OFF_VB // TN_IN
    vb_tiles = HB * HEAD_DIM // TN_IN
    heads_per_tile = TN_IN // HEAD_DIM
    row128 = pl.BlockSpec((1, 128), lambda i, j: (0, 0))
    return pl.pallas_call(
        _inproj_kernel,
        grid=grid,
        in_specs=[
            pl.BlockSpec((tm, D_MODEL), lambda i, j: (i, 0)),
            pl.BlockSpec((tm, 1), lambda i, j: (i, 0)),
            pl.BlockSpec((1, D_MODEL), lambda i, j: (0, 0)),
            pl.BlockSpec((pl.Element(TN_IN), pl.Element(D_MODEL)), lambda i, j: (_w_in_row(j), 0)),
            pl.BlockSpec((128, D_MODEL), lambda i, j: (0, 0)),
            row128, row128,
        ],
        out_specs=[
            pl.BlockSpec((tm, TN_IN), lambda i, j: (i, j)),
            pl.BlockSpec((tm, 128), lambda i, j: (i, 0)),
            pl.BlockSpec((tm, 128), lambda i, j: (i, 0)),
            pl.BlockSpec((tm, 128), lambda i, j: (i, 0)),
            pl.BlockSpec((None, HA_KV, VT_ROWS, tm), lambda i, j: (i // nt, 0, 0, i % nt)),
            pl.BlockSpec((None, heads_per_tile, VT_ROWS, tm),
                         lambda i, j: (i // nt, jnp.clip(j - j_vb, 0, vb_tiles - 1), 0, i % nt)),
        ],
        out_shape=[
            jax.ShapeDtypeStruct((m, N_MAIN), BF16),
            jax.ShapeDtypeStruct((m, 128), BF16),
            jax.ShapeDtypeStruct((m, 128), BF16),
            jax.ShapeDtypeStruct((m, 128), F32),
            jax.ShapeDtypeStruct((b, HA_KV, VT_ROWS, t_len), BF16),
            jax.ShapeDtypeStruct((b, HB, VT_ROWS, t_len), BF16),
        ],
        scratch_shapes=[pltpu.VMEM((tm, D_MODEL), BF16)] + [pltpu.VMEM((tm, 128), F32)] * 6,
        compiler_params=_cparams(("arbitrary", "arbitrary")),
        name="inproj",
    )(x2, pos2, g, w_t, w_small, bf_row, inv_freq)


def _cumsum_kernel(wlf_ref, o_ref, carry_scr, *, tc):
    @pl.when(pl.program_id(1) == 0)
    def _():
        carry_scr[...] = jnp.zeros_like(carry_scr)

    lf = wlf_ref[...]
    r = lax.broadcasted_iota(jnp.int32, (tc, tc), 0)
    c = lax.broadcasted_iota(jnp.int32, (tc, tc), 1)
    tri = jnp.where(c <= r, 1.0, 0.0).astype(BF16)
    hi = lf.astype(BF16)
    r1 = lf - hi.astype(F32)
    mid = r1.astype(BF16)
    lo = (r1 - mid.astype(F32)).astype(BF16)
    cs = (jnp.dot(tri, hi, preferred_element_type=F32) + jnp.dot(tri, mid, preferred_element_type=F32)
          + jnp.dot(tri, lo, preferred_element_type=F32)) + carry_scr[0:1, :]
    carry_scr[...] = jnp.broadcast_to(cs[tc - 1:tc, :], carry_scr.shape)
    cs2 = cs * LOG2E
    for h in range(HB):
        o_ref[h] = jnp.broadcast_to(cs2[:, LANE_FB + h:LANE_FB + h + 1], (tc, 128))


def _cumsum(wlf, b, t_len):
    tc = 256
    nt = t_len // tc
    return pl.pallas_call(
        functools.partial(_cumsum_kernel, tc=tc),
        grid=(b, nt),
        in_specs=[pl.BlockSpec((tc, 128), lambda bi, ti: (bi * nt + ti, 0))],
        out_specs=pl.BlockSpec((None, HB, tc, 128), lambda bi, ti: (bi, 0, ti, 0)),
        out_shape=jax.ShapeDtypeStruct((b, HB, t_len, 128), F32),
        scratch_shapes=[pltpu.VMEM((8, 128), F32)],
        compiler_params=_cparams(("arbitrary", "arbitrary")),
        name="fcumsum",
    )(wlf)


def _attn_a_kernel(qa_ref, qi0_ref, qi1_ref, ka_ref, vt_ref, kie_ref, kio_ref, wlf_ref, o_ref,
                   sc_scr, qs_scr, qg_scr, lt_scr, acc_scr, *, tk, top_k, n_iter):
    i = pl.program_id(1)
    nq = Q_BLOCK
    group = HA // HA_KV
    gw = group * nq
    nchunks = ((i + 1) * nq + tk - 1) // tk
    q_pos = i * nq + lax.broadcasted_iota(jnp.int32, (1, nq), 1)
    wt = wlf_ref[...].T
    half = H_IDX // 4
    for p in range(H_IDX // 2):
        src = qi0_ref if p < half else qi1_ref
        qs_scr[p * nq:(p + 1) * nq, :] = src[:, (p % half) * 128:(p % half + 1) * 128]

    def score_chunk(c, carry):
        rmin, rmax = carry
        off = pl.multiple_of(c * tk, tk)
        ke = kie_ref[pl.ds(off, tk), :]
        ko = kio_ref[pl.ds(off, tk), :]
        s = jnp.zeros((tk, nq), F32)
        pairs_per_dot = MXU_N // nq
        for pg in range(H_IDX // 2 // pairs_per_dot):
            qs = qs_scr[pg * MXU_N:(pg + 1) * MXU_N, :]
            de = lax.dot_general(ke, qs, (((1,), (1,)), ((), ())), preferred_element_type=F32)
            do = lax.dot_general(ko, qs, (((1,), (1,)), ((), ())), preferred_element_type=F32)
            for pp in range(pairs_per_dot):
                p = pg * pairs_per_dot + pp
                cols = slice(pp * nq, (pp + 1) * nq)
                s = s + jnp.maximum(de[:, cols], 0.0) * wt[LANE_WI + 2 * p:LANE_WI + 2 * p + 1, :]
                s = s + jnp.maximum(do[:, cols], 0.0) * wt[LANE_WI + 2 * p + 1:LANE_WI + 2 * p + 2, :]
        kpos = off + lax.broadcasted_iota(jnp.int32, (tk, 1), 0)
        valid = kpos <= q_pos
        sc_scr[pl.ds(off, tk), :] = jnp.where(valid, s, NEG)
        rmin = jnp.minimum(rmin, jnp.min(jnp.where(valid, s, -NEG), axis=0, keepdims=True))
        rmax = jnp.maximum(rmax, jnp.max(jnp.where(valid, s, NEG), axis=0, keepdims=True))
        return rmin, rmax

    lo, hi = lax.fori_loop(0, nchunks, score_chunk,
                           (jnp.full((1, nq), -NEG, F32), jnp.full((1, nq), NEG, F32)))

    def bisect(_, carry):
        lo, hi = carry
        mid = 0.5 * (lo + hi)

        def count_chunk(c, part):
            off = pl.multiple_of(c * tk, tk)
            ind = jnp.where(sc_scr[pl.ds(off, tk), :] >= mid, 1.0, 0.0)
            return part + jnp.sum(ind.reshape(tk // 64, 64, nq), axis=0)

        part = lax.fori_loop(0, nchunks, count_chunk, jnp.zeros((64, nq), F32))
        cnt = jnp.sum(part, axis=0, keepdims=True)
        ge = cnt >= top_k
        return jnp.where(ge, mid, lo), jnp.where(cnt == top_k, mid, jnp.where(ge, hi, mid))

    searching = q_pos >= top_k

    def bisect_round(state):
        it, lo, hi, _ = state
        lo, hi = lax.fori_loop(0, BISECT_ROUND, bisect, (lo, hi))
        pending = jnp.sum(jnp.where(searching & (lo < hi), 1.0, 0.0))
        return it + BISECT_ROUND, lo, hi, pending

    lo, hi = lax.fori_loop(0, BISECT_FIRST, bisect, (lo, hi))
    _, thr, _, _ = lax.while_loop(lambda st: (st[0] < n_iter) & (st[3] > 0.0), bisect_round,
                                  (jnp.int32(BISECT_FIRST), lo, hi, jnp.float32(1.0)))

    tka = tk // 2
    last = (nchunks - 1) * tk
    for n in range(HA_KV):
        for g in range(group):
            h = n * group + g
            qg_scr[n, g * nq:(g + 1) * nq, :] = qa_ref[:, h * HEAD_DIM:(h + 1) * HEAD_DIM]
    acc_scr[...] = jnp.zeros(acc_scr.shape, F32)

    def qk(off, slot):
        msk = jnp.where(sc_scr[pl.ds(off, tka), :] >= thr, 0.0, NEG)
        msk = jnp.concatenate([msk] * group, axis=1)
        cmax = []
        for n in range(HA_KV):
            k = ka_ref[pl.ds(off, tka), n * HEAD_DIM:(n + 1) * HEAD_DIM]
            s = lax.dot_general(k, qg_scr[n], (((1,), (1,)), ((), ())), preferred_element_type=F32) + msk
            lt_scr[n, slot] = s
            cmax.append(jnp.max(s, axis=0, keepdims=True))
        return tuple(cmax)

    def softmax_pv(off, slot, m_old, cmax):
        new = []
        for n in range(HA_KV):
            m_new = jnp.maximum(m_old[n], cmax[n])
            alpha = jnp.exp2(m_old[n] - m_new)
            p = jnp.exp2(lt_scr[n, slot] - m_new).astype(BF16)
            acc_scr[n] = alpha * acc_scr[n] + jnp.dot(vt_ref[n, :, pl.ds(off, tka)], p,
                                                      preferred_element_type=F32)
            new.append(m_new)
        return tuple(new)

    def pair(j, carry):
        m, cmax0 = carry
        off0 = pl.multiple_of(j * tk, tk)
        off1 = pl.multiple_of(off0 + tka, tka)
        off2 = pl.multiple_of(jnp.minimum(off0 + tk, last), tk)
        cmax1 = qk(off1, 1)
        m = softmax_pv(off0, 0, m, cmax0)
        cmax0 = qk(off2, 0)
        return softmax_pv(off1, 1, m, cmax1), cmax0

    lax.fori_loop(0, nchunks, pair, ((jnp.full((1, gw), NEG, F32),) * HA_KV, qk(0, 0)))
    for n in range(HA_KV):
        out_t = acc_scr[n, 0:HEAD_DIM, :] / acc_scr[n, HEAD_DIM:HEAD_DIM + 1, :]
        for g in range(group):
            h = n * group + g
            o_ref[:, h * HEAD_DIM:(h + 1) * HEAD_DIM] = out_t[:, g * nq:(g + 1) * nq].T.astype(BF16)


def _attn_a(proj, vt_a, kie, kio, wlf, b, t_len, top_k):
    nb = t_len // Q_BLOCK
    wa = HA * HEAD_DIM
    wkv = HA_KV * HEAD_DIM
    wqi = H_IDX * D_IDX // 2
    tk = min(512, t_len)
    gw = HA // HA_KV * Q_BLOCK
    kern = functools.partial(_attn_a_kernel, tk=tk, top_k=top_k, n_iter=BISECT_MAX_ITERS)
    return pl.pallas_call(
        kern,
        grid=(b, nb),
        in_specs=[
            pl.BlockSpec((Q_BLOCK, wa), lambda bi, i: (bi * nb + i, OFF_QA // wa)),
            pl.BlockSpec((Q_BLOCK, wqi), lambda bi, i: (bi * nb + i, OFF_QI // wqi)),
            pl.BlockSpec((Q_BLOCK, wqi), lambda bi, i: (bi * nb + i, OFF_QI // wqi + 1)),
            pl.BlockSpec((t_len, wkv), lambda bi, i: (bi, OFF_KA // wkv)),
            pl.BlockSpec((None, HA_KV, VT_ROWS, t_len), lambda bi, i: (bi, 0, 0, 0)),
            pl.BlockSpec((t_len, 128), lambda bi, i: (bi, 0)),
            pl.BlockSpec((t_len, 128), lambda bi, i: (bi, 0)),
            pl.BlockSpec((Q_BLOCK, 128), lambda bi, i: (bi * nb + i, 0)),
        ],
        out_specs=pl.BlockSpec((Q_BLOCK, wa), lambda bi, i: (bi * nb + i, 0)),
        out_shape=jax.ShapeDtypeStruct((b * t_len, wa), BF16),
        scratch_shapes=[
            pltpu.VMEM((t_len, Q_BLOCK), F32),
            pltpu.VMEM((H_IDX // 2 * Q_BLOCK, 128), BF16),
            pltpu.VMEM((HA_KV, gw, HEAD_DIM), BF16),
            pltpu.VMEM((HA_KV, 2, tk // 2, gw), F32),
            pltpu.VMEM((HA_KV, VT_ROWS, gw), F32),
        ],
        compiler_params=_cparams(("arbitrary", "arbitrary")),
        name="attn_a",
    )(proj, proj, proj, proj, vt_a, kie, kio, wlf)


def _attn_b_kernel(q_ref, k_ref, vt_ref, cb_ref, o_ref, lt_scr, acc_scr, *, tq):
    qi = pl.program_id(2)
    tk = tq // 2
    heads = HEADS_PER_STEP_B
    q0 = pl.multiple_of(qi * tq, tq)
    acc_scr[...] = jnp.zeros(acc_scr.shape, F32)

    def qk(off, slot, key0=None):
        cmax = []
        for n in range(heads):
            cols = slice(n * HEAD_DIM, (n + 1) * HEAD_DIM)
            bias = cb_ref[n, pl.ds(q0, 1), :] - cb_ref[n, pl.ds(off, tk), :]
            s = lax.dot_general(k_ref[pl.ds(off, tk), cols], q_ref[:, cols], (((1,), (1,)), ((), ())),
                                preferred_element_type=F32) + jnp.concatenate([bias] * (tq // 128), axis=1)
            if key0 is not None:
                key = key0 + lax.broadcasted_iota(jnp.int32, (tk, 1), 0)
                qry = lax.broadcasted_iota(jnp.int32, (1, tq), 1)
                s = jnp.where(key <= qry, s, NEG)
            lt_scr[n, slot] = s
            cmax.append(jnp.max(s, axis=0, keepdims=True))
        return tuple(cmax)

    def softmax_pv(off, slot, m_old, cmax):
        new = []
        for n in range(heads):
            m_new = jnp.maximum(m_old[n], cmax[n])
            alpha = jnp.exp2(m_old[n] - m_new)
            p = jnp.exp2(lt_scr[n, slot] - m_new).astype(BF16)
            acc_scr[n] = alpha * acc_scr[n] + jnp.dot(vt_ref[n, :, pl.ds(off, tk)], p,
                                                      preferred_element_type=F32)
            new.append(m_new)
        return tuple(new)

    q1 = pl.multiple_of(q0 + tk, tk)
    m = (jnp.full((1, tq), NEG, F32),) * heads
    cmax0 = qk(q0, 0, key0=0)
    cmax1 = qk(q1, 1, key0=tk)
    m = softmax_pv(q0, 0, m, cmax0)
    cmax0 = qk(0, 0)
    m = softmax_pv(q1, 1, m, cmax1)
    last = jnp.maximum(qi - 1, 0) * tq

    def pair(j, carry):
        m, cmax0 = carry
        off0 = pl.multiple_of(j * tq, tq)
        off1 = pl.multiple_of(off0 + tk, tk)
        off2 = pl.multiple_of(jnp.minimum(off0 + tq, last), tq)
        cmax1 = qk(off1, 1)
        m = softmax_pv(off0, 0, m, cmax0)
        cmax0 = qk(off2, 0)
        return softmax_pv(off1, 1, m, cmax1), cmax0

    lax.fori_loop(0, qi, pair, (m, cmax0))
    for n in range(heads):
        out_t = acc_scr[n, 0:HEAD_DIM, :] / acc_scr[n, HEAD_DIM:HEAD_DIM + 1, :]
        for g in range(tq // 128):
            o_ref[g * 128:(g + 1) * 128, n * HEAD_DIM:(n + 1) * HEAD_DIM] = (
                out_t[:, g * 128:(g + 1) * 128].T.astype(BF16))


def _attn_b(proj, vt_b, cb, b, t_len):
    tq = 512
    nq = t_len // tq
    hps = HEADS_PER_STEP_B
    wh = hps * HEAD_DIM
    return pl.pallas_call(
        functools.partial(_attn_b_kernel, tq=tq),
        grid=(b, HB // hps, nq),
        in_specs=[
            pl.BlockSpec((tq, wh), lambda bi, h, i: (bi * nq + i, OFF_QB // wh + h)),
            pl.BlockSpec((t_len, wh), lambda bi, h, i: (bi, OFF_KB // wh + h)),
            pl.BlockSpec((None, hps, VT_ROWS, t_len), lambda bi, h, i: (bi, h, 0, 0)),
            pl.BlockSpec((None, hps, t_len, 128), lambda bi, h, i: (bi, h, 0, 0)),
        ],
        out_specs=pl.BlockSpec((tq, wh), lambda bi, h, i: (bi * nq + i, h)),
        out_shape=jax.ShapeDtypeStruct((b * t_len, HB * HEAD_DIM), BF16),
        scratch_shapes=[pltpu.VMEM((hps, 2, tq // 2, tq), F32), pltpu.VMEM((hps, VT_ROWS, tq), F32)],
        compiler_params=_cparams(("arbitrary", "arbitrary", "arbitrary")),
        name="attn_b",
    )(proj, proj, vt_b, cb)


def _merge_kernel(x_ref, oa_ref, ob_ref, ga_ref, gb_ref, woa_ref, wob_ref, wout_ref, o_ref):
    ya = jnp.dot(oa_ref[...], woa_ref[...], preferred_element_type=F32)
    yb = jnp.dot(ob_ref[...], wob_ref[...], preferred_element_type=F32)
    mixed = ga_ref[...].astype(F32) * ya + gb_ref[...].astype(F32) * yb
    o_ref[...] = x_ref[...] + jnp.dot(mixed.astype(BF16), wout_ref[...], preferred_element_type=F32)


def _merge(x2, out_a, out_b, proj, w_oa, w_ob, w_out, tm):
    m = x2.shape[0]
    wa = HA * HEAD_DIM
    wb = HB * HEAD_DIM
    return pl.pallas_call(
        _merge_kernel,
        grid=(m // tm,),
        in_specs=[
            pl.BlockSpec((tm, D_MODEL), lambda i: (i, 0)),
            pl.BlockSpec((tm, wa), lambda i: (i, 0)),
            pl.BlockSpec((tm, wb), lambda i: (i, 0)),
            pl.BlockSpec((tm, D_MODEL), lambda i: (i, OFF_GA // D_MODEL)),
            pl.BlockSpec((tm, D_MODEL), lambda i: (i, OFF_GB // D_MODEL)),
            _resident((wa, D_MODEL)),
            _resident((wb, D_MODEL)),
            _resident((D_MODEL, D_MODEL)),
        ],
        out_specs=pl.BlockSpec((tm, D_MODEL), lambda i: (i, 0)),
        out_shape=jax.ShapeDtypeStruct((m, D_MODEL), F32),
        compiler_params=_cparams(("arbitrary",)),
        name="merge",
    )(x2, out_a, out_b, proj, proj, w_oa, w_ob, w_out)


def _ffn_kernel(x_ref, g_ref, wg_ref, wu_ref, wd_ref, o_ref, h_scr):
    @pl.when(pl.program_id(1) == 0)
    def _():
        x = x_ref[...]
        h_scr[...] = _rms(x, g_ref[...]).astype(BF16)
        o_ref[...] = x

    h = h_scr[...]
    gate = jnp.dot(h, wg_ref[...].astype(BF16), preferred_element_type=F32)
    up = jnp.dot(h, wu_ref[...].astype(BF16), preferred_element_type=F32)
    act = (jax.nn.silu(gate) * up).astype(BF16)
    o_ref[...] += jnp.dot(act, wd_ref[...].astype(BF16), preferred_element_type=F32)


def _ffn(x2, g, w_gate, w_up, w_down, tm, tf):
    m = x2.shape[0]
    return pl.pallas_call(
        _ffn_kernel,
        grid=(m // tm, D_FF // tf),
        in_specs=[
            pl.BlockSpec((tm, D_MODEL), lambda i, f: (i, 0)),
            pl.BlockSpec((1, D_MODEL), lambda i, f: (0, 0)),
            pl.BlockSpec((D_MODEL, tf), lambda i, f: (0, f)),
            pl.BlockSpec((D_MODEL, tf), lambda i, f: (0, f)),
            pl.BlockSpec((tf, D_MODEL), lambda i, f: (f, 0)),
        ],
        out_specs=pl.BlockSpec((tm, D_MODEL), lambda i, f: (i, 0)),
        out_shape=jax.ShapeDtypeStruct((m, D_MODEL), F32),
        scratch_shapes=[pltpu.VMEM((tm, D_MODEL), BF16)],
        compiler_params=_cparams(("arbitrary", "arbitrary")),
        name="ffn",
    )(x2, g, w_gate, w_up, w_down)


def _ple_kernel(x_ref, p_ref, g_ref, gf_ref, wg_ref, wp_ref, o_ref, *, final_norm):
    x = x_ref[...]
    h = _rms(x, g_ref[...]).astype(BF16)
    gate = jax.nn.sigmoid(jnp.dot(h, wg_ref[...], preferred_element_type=F32))
    emb = jnp.dot(p_ref[...].astype(BF16), wp_ref[...], preferred_element_type=F32)
    y = x + gate * emb
    o_ref[...] = _rms(y, gf_ref[...]) if final_norm else y


def _ple(x2, p2, g, g_final, w_gate, w_proj, tm, final_norm):
    m = x2.shape[0]
    return pl.pallas_call(
        functools.partial(_ple_kernel, final_norm=final_norm),
        grid=(m // tm,),
        in_specs=[
            pl.BlockSpec((tm, D_MODEL), lambda i: (i, 0)),
            pl.BlockSpec((tm, D_PLE), lambda i: (i, 0)),
            pl.BlockSpec((1, D_MODEL), lambda i: (0, 0)),
            pl.BlockSpec((1, D_MODEL), lambda i: (0, 0)),
            _resident((D_MODEL, D_MODEL)),
            _resident((D_PLE, D_MODEL)),
        ],
        out_specs=pl.BlockSpec((tm, D_MODEL), lambda i: (i, 0)),
        out_shape=jax.ShapeDtypeStruct((m, D_MODEL), F32),
        compiler_params=_cparams(("arbitrary",)),
        name="ple",
    )(x2, p2, g, g_final, w_gate, w_proj)


def _rope_lane_freqs():
    def inv(rot):
        half = rot // 2
        return ROPE_THETA ** (-jnp.arange(half, dtype=F32) / half)
    used = ROT_A // 2 + ROT_IDX // 2
    return jnp.concatenate([inv(ROT_A), inv(ROT_IDX), jnp.zeros((128 - used,), F32)])[None, :]


def _w_in_views(w):
    w_t = jnp.swapaxes(w, 0, 1)
    rows = [w_t[_SPLIT_EDGES[k]:_SPLIT_EDGES[k + 1]] for k in (4, 5, 9)]
    pad = jnp.zeros((128 - D_IDX - H_IDX - HB, w_t.shape[1]), w_t.dtype)
    return w_t, jnp.concatenate(rows + [pad], axis=0)


def kernel(x, p, positions, g_mix, w_in, b_f, w_o_a, w_o_b, w_out, g_ffn, w_ffn_gate, w_ffn_up,
           w_ffn_down, g_ple, w_ple_gate, w_ple_proj, g_final):
    b, t_len, d = x.shape
    depth = w_in.shape[0]
    m = b * t_len
    top_k = min(TOPK_MAX, t_len // 4)
    inv_freq = _rope_lane_freqs()
    pos2 = positions.reshape(m, 1)
    x2 = x.reshape(m, d)
    tm_in = min(1024, m)
    tm = min(512, m)
    for i in range(depth):
        w_t, w_small = _w_in_views(w_in[i])
        bf_row = jnp.zeros((1, 128), F32).at[0, LANE_FB:LANE_FB + HB].set(b_f[i].astype(F32))
        proj, kie, kio, wlf, vt_a, vt_b = _inproj(x2, pos2, g_mix[i][None, :], w_t, w_small, bf_row,
                                                  inv_freq, min(tm_in, t_len), b, t_len)
        cb = _cumsum(wlf, b, t_len)
        out_a = _attn_a(proj, vt_a, kie, kio, wlf, b, t_len, top_k)
        out_b = _attn_b(proj, vt_b, cb, b, t_len)
        x2 = _merge(x2, out_a, out_b, proj, w_o_a[i].astype(BF16), w_o_b[i].astype(BF16),
                    w_out[i].astype(BF16), tm)
        x2 = _ffn(x2, g_ffn[i][None, :], w_ffn_gate[i], w_ffn_up[i], w_ffn_down[i], tm_in, 256)
        x2 = _ple(x2, p[i].reshape(m, D_PLE), g_ple[i][None, :], g_final[None, :], w_ple_gate[i].astype(BF16),
                  w_ple_proj[i].astype(BF16), tm, final_norm=(i + 1 == depth))
    return x2.reshape(b, t_len, d)
```

```python
import functools

import jax
import jax.numpy as jnp
import numpy as np
from jax import lax
from jax.experimental import pallas as pl
from jax.experimental.pallas import tpu as pltpu

F32 = jnp.float32
BF16 = jnp.bfloat16

D_MODEL = 2048
HEAD_DIM = 128
HA = 8
HA_KV = 2
H_IDX = 16
D_IDX = 64
TOPK_MAX = 256
HB = 8
ROPE_THETA = 500000.0
ROT_A = HEAD_DIM // 4
ROT_IDX = D_IDX // 4
Q_BLOCK = 128
D_FF = -(-8 * D_MODEL // (3 * 256)) * 256
D_PLE = 256
EPS = 1e-6

SPLIT_SIZES = (HA * HEAD_DIM, HA_KV * HEAD_DIM, HA_KV * HEAD_DIM, H_IDX * D_IDX, D_IDX, H_IDX,
               HB * HEAD_DIM, HB * HEAD_DIM, HB * HEAD_DIM, HB, D_MODEL, D_MODEL)

OFF_GA = 0
OFF_GB = OFF_GA + D_MODEL
OFF_QA = OFF_GB + D_MODEL
OFF_KA = OFF_QA + HA * HEAD_DIM
OFF_VA = OFF_KA + HA_KV * HEAD_DIM
OFF_QI = OFF_VA + HA_KV * HEAD_DIM
TN_IN = 1024
MXU_N = 256
OFF_QB = -(-(OFF_QI + H_IDX * D_IDX) // TN_IN) * TN_IN
OFF_KB = OFF_QB + HB * HEAD_DIM
OFF_VB = OFF_KB + HB * HEAD_DIM
N_MAIN = OFF_VB + HB * HEAD_DIM
_SPLIT_EDGES = [0] + [int(s) for s in np.cumsum(SPLIT_SIZES)]
_PACK_RUNS = ((OFF_GA, _SPLIT_EDGES[10], 2 * D_MODEL),
              (OFF_QA, _SPLIT_EDGES[0], OFF_QI + H_IDX * D_IDX - OFF_QA),
              (OFF_QB, _SPLIT_EDGES[6], N_MAIN - OFF_QB))
_SEGMENTS = ((OFF_GA, OFF_QA, "gate"), (OFF_QA, OFF_KA, "q_a"), (OFF_KA, OFF_VA, "k_a"),
             (OFF_VA, OFF_QI, "v_a"), (OFF_QI, OFF_QI + H_IDX * D_IDX, "q_i"),
             (OFF_QB, OFF_KB, "q_b"), (OFF_KB, OFF_VB, "k_b"), (OFF_VB, N_MAIN, "v_b"))


def _tile_patterns():
    def group(col):
        for lo, hi, kind in _SEGMENTS:
            if lo <= col < hi:
                assert (col - lo) % MXU_N == 0 and col + MXU_N <= hi
                return kind, (col - lo) // MXU_N
        return None
    by_pattern = {}
    for tile in range(N_MAIN // TN_IN):
        pattern = tuple(group(tile * TN_IN + g * MXU_N) for g in range(TN_IN // MXU_N))
        pattern = tuple(None if grp is None else (grp[0], grp[1] if grp[0] in ("v_a", "v_b") else 0)
                        for grp in pattern)
        by_pattern.setdefault(pattern, []).append(tile)
    return tuple(by_pattern.items())


_TILE_PATTERNS = _tile_patterns()
LANE_KI = 0
LANE_WI = D_IDX
LANE_FB = D_IDX + H_IDX

NEG = -1e30
VMEM_LIMIT = 60 * 1024 * 1024
LOG2E = 1.4426950408889634
ATTN_SCALE = HEAD_DIM ** -0.5 * LOG2E
VT_ROWS = HEAD_DIM + 16
HEADS_PER_STEP_B = 4
BISECT_FIRST = 12
BISECT_ROUND = 4
BISECT_MAX_ITERS = 48


def _cparams(sem):
    return pltpu.CompilerParams(dimension_semantics=sem, vmem_limit_bytes=VMEM_LIMIT)


def _rms(x, g):
    return x * lax.rsqrt(jnp.mean(x * x, axis=-1, keepdims=True) + EPS) * g


def _resident(shape):
    nd = len(shape)
    return pl.BlockSpec(shape, lambda *_: (0,) * nd, pipeline_mode=pl.Buffered(1))


def _rope128(a, c, sa, sb, shift):
    outs = []
    for q in range(a.shape[1] // 128):
        aq = a[:, q * 128:(q + 1) * 128]
        outs.append(aq * c + pltpu.roll(aq, 128 - shift, 1) * sa + pltpu.roll(aq, shift, 1) * sb)
    return outs[0] if len(outs) == 1 else jnp.concatenate(outs, axis=1)


def _inproj_kernel(x_ref, pos_ref, g_ref, w_ref, ws_ref, bf_ref, inv_ref,
                   o_ref, kie_ref, kio_ref, wlf_ref, vta_ref, vtb_ref,
                   h_scr, ca_scr, saa_scr, sab_scr, ci_scr, sia_scr, sib_scr):
    j = pl.program_id(1)
    tm = x_ref.shape[0]

    @pl.when(j == 0)
    def _():
        hb = _rms(x_ref[...], g_ref[...]).astype(BF16)
        h_scr[...] = hb
        pos = pos_ref[...].astype(F32)
        lane = lax.broadcasted_iota(jnp.int32, (tm, 128), 1)
        ha, hi = ROT_A // 2, ROT_IDX // 2
        ang = pos * inv_ref[...]
        cs = jnp.cos(ang)
        sn = jnp.sin(ang)

        def at(tab, src, dst):
            return pltpu.roll(tab, (dst - src) % 128, 1)

        def lanes(lo, n):
            return (lane >= lo) & (lane < lo + n)

        ca_scr[...] = jnp.where(lanes(0, ha), cs, jnp.where(lanes(ha, ha), at(cs, 0, ha), 1.0))
        saa_scr[...] = jnp.where(lanes(0, ha), -sn, 0.0)
        sab_scr[...] = jnp.where(lanes(ha, ha), at(sn, 0, ha), 0.0)
        ci = jnp.ones((tm, 128), F32)
        sia = jnp.zeros((tm, 128), F32)
        sib = jnp.zeros((tm, 128), F32)
        for head0 in (0, D_IDX):
            ci = jnp.where(lanes(head0, hi), at(cs, ha, head0), ci)
            ci = jnp.where(lanes(head0 + hi, hi), at(cs, ha, head0 + hi), ci)
            sia = jnp.where(lanes(head0, hi), -at(sn, ha, head0), sia)
            sib = jnp.where(lanes(head0 + hi, hi), at(sn, ha, head0 + hi), sib)
        ci_scr[...] = ci
        sia_scr[...] = sia
        sib_scr[...] = sib
        small = lax.dot_general(hb, ws_ref[...].astype(BF16), (((1,), (1,)), ((), ())),
                                preferred_element_type=F32)
        kr = _rope128(small, ci, sia, sib, ROT_IDX // 2)
        ke = jnp.where(lane < D_IDX, kr, 0.0)
        kie_ref[...] = ke.astype(BF16)
        kio_ref[...] = pltpu.roll(ke, D_IDX, 1).astype(BF16)
        is_w = (lane >= LANE_WI) & (lane < LANE_WI + H_IDX)
        is_f = (lane >= LANE_FB) & (lane < LANE_FB + HB)
        logf = jax.nn.log_sigmoid(small + bf_ref[...])
        wlf_ref[...] = jnp.where(is_w, small * (H_IDX ** -0.5 * D_IDX ** -0.5), jnp.where(is_f, logf, 0.0))

    t = TN_IN
    groups = t // MXU_N
    heads_per_group = MXU_N // HEAD_DIM

    def project(epilogues):
        for c, epilogue in enumerate(epilogues):
            if epilogue is None:
                continue
            cols = slice(c * MXU_N, (c + 1) * MXU_N)
            w_t = w_ref[cols, :].astype(BF16)
            acc = lax.dot_general(h_scr[...], w_t, (((1,), (1,)), ((), ())), preferred_element_type=F32)
            o_ref[:, cols] = epilogue(acc).astype(BF16)

    def rope_a(acc):
        return _rope128(acc, ca_scr[...], saa_scr[...], sab_scr[...], ROT_A // 2)

    def rope_i(acc):
        return _rope128(acc, ci_scr[...], sia_scr[...], sib_scr[...], ROT_IDX // 2)

    def values(vt_ref, c):
        def epilogue(acc):
            acc_t = acc.T
            for hh in range(heads_per_group):
                h = (c * heads_per_group + hh) % vt_ref.shape[0]
                vt_ref[h, 0:HEAD_DIM, :] = acc_t[hh * HEAD_DIM:(hh + 1) * HEAD_DIM, :].astype(BF16)
                vt_ref[h, HEAD_DIM:VT_ROWS, :] = jnp.ones((VT_ROWS - HEAD_DIM, tm), BF16)
            return acc
        return epilogue

    def epilogue_of(kind, c):
        return {"gate": lambda: jax.nn.sigmoid,
                "q_a": lambda: (lambda acc: rope_a(acc) * ATTN_SCALE),
                "k_a": lambda: rope_a,
                "v_a": lambda: values(vta_ref, c),
                "q_i": lambda: rope_i,
                "q_b": lambda: (lambda acc: acc * ATTN_SCALE),
                "k_b": lambda: (lambda acc: acc),
                "v_b": lambda: values(vtb_ref, c)}[kind]()

    for pattern, tiles in _TILE_PATTERNS:
        pred = functools.reduce(lambda a, b: a | b, [j == tile for tile in tiles])

        @pl.when(pred)
        def _(pattern=pattern):
            project([None if grp is None else epilogue_of(*grp) for grp in pattern])


def _w_in_row(j):
    row = jnp.int32(0)
    for out0, src0, _ in _PACK_RUNS:
        row = jnp.where(j >= out0 // TN_IN, src0 + TN_IN * (j - out0 // TN_IN), row)
    return pl.multiple_of(row, 8)


def _inproj(x2, pos2, g, w_t, w_small, bf_row, inv_freq, tm, b, t_len):
    m = x2.shape[0]
    for out0, src0, width in _PACK_RUNS:
        assert out0 % TN_IN == 0 and src0 % 8 == 0
        assert src0 + -(-width // TN_IN) * TN_IN <= w_t.shape[0]
    assert list(_PACK_RUNS) == sorted(_PACK_RUNS)
    grid = (m // tm, N_MAIN // TN_IN)
    nt = t_len // tm
    j_vb = OFF_VB // TN_IN
    heads_per_tile = min(TN_IN // HEAD_DIM, HB)
    vb_tiles = HB // heads_per_tile
    row128 = pl.BlockSpec((1, 128), lambda i, j: (0, 0))
    return pl.pallas_call(
        _inproj_kernel,
        grid=grid,
        in_specs=[
            pl.BlockSpec((tm, D_MODEL), lambda i, j: (i, 0)),
            pl.BlockSpec((tm, 1), lambda i, j: (i, 0)),
            pl.BlockSpec((1, D_MODEL), lambda i, j: (0, 0)),
            pl.BlockSpec((pl.Element(TN_IN), pl.Element(D_MODEL)), lambda i, j: (_w_in_row(j), 0)),
            pl.BlockSpec((128, D_MODEL), lambda i, j: (0, 0)),
            row128, row128,
        ],
        out_specs=[
            pl.BlockSpec((tm, TN_IN), lambda i, j: (i, j)),
            pl.BlockSpec((tm, 128), lambda i, j: (i, 0)),
            pl.BlockSpec((tm, 128), lambda i, j: (i, 0)),
            pl.BlockSpec((tm, 128), lambda i, j: (i, 0)),
            pl.BlockSpec((None, HA_KV, VT_ROWS, tm), lambda i, j: (i // nt, 0, 0, i % nt)),
            pl.BlockSpec((None, heads_per_tile, VT_ROWS, tm),
                         lambda i, j: (i // nt, jnp.clip(j - j_vb, 0, vb_tiles - 1), 0, i % nt)),
        ],
        out_shape=[
            jax.ShapeDtypeStruct((m, N_MAIN), BF16),
            jax.ShapeDtypeStruct((m, 128), BF16),
            jax.ShapeDtypeStruct((m, 128), BF16),
            jax.ShapeDtypeStruct((m, 128), F32),
            jax.ShapeDtypeStruct((b, HA_KV, VT_ROWS, t_len), BF16),
            jax.ShapeDtypeStruct((b, HB, VT_ROWS, t_len), BF16),
        ],
        scratch_shapes=[pltpu.VMEM((tm, D_MODEL), BF16)] + [pltpu.VMEM((tm, 128), F32)] * 6,
        compiler_params=_cparams(("arbitrary", "arbitrary")),
        name="inproj",
    )(x2, pos2, g, w_t, w_small, bf_row, inv_freq)


def _cumsum_kernel(wlf_ref, o_ref, carry_scr, *, tc):
    @pl.when(pl.program_id(1) == 0)
    def _():
        carry_scr[...] = jnp.zeros_like(carry_scr)

    lf = wlf_ref[...]
    r = lax.broadcasted_iota(jnp.int32, (tc, tc), 0)
    c = lax.broadcasted_iota(jnp.int32, (tc, tc), 1)
    tri = jnp.where(c <= r, 1.0, 0.0).astype(BF16)
    hi = lf.astype(BF16)
    r1 = lf - hi.astype(F32)
    mid = r1.astype(BF16)
    lo = (r1 - mid.astype(F32)).astype(BF16)
    cs = (jnp.dot(tri, hi, preferred_element_type=F32) + jnp.dot(tri, mid, preferred_element_type=F32)
          + jnp.dot(tri, lo, preferred_element_type=F32)) + carry_scr[0:1, :]
    carry_scr[...] = jnp.broadcast_to(cs[tc - 1:tc, :], carry_scr.shape)
    cs2 = cs * LOG2E
    for h in range(HB):
        o_ref[h] = jnp.broadcast_to(cs2[:, LANE_FB + h:LANE_FB + h + 1], (tc, 128))


def _cumsum(wlf, b, t_len):
    tc = 256
    nt = t_len // tc
    return pl.pallas_call(
        functools.partial(_cumsum_kernel, tc=tc),
        grid=(b, nt),
        in_specs=[pl.BlockSpec((tc, 128), lambda bi, ti: (bi * nt + ti, 0))],
        out_specs=pl.BlockSpec((None, HB, tc, 128), lambda bi, ti: (bi, 0, ti, 0)),
        out_shape=jax.ShapeDtypeStruct((b, HB, t_len, 128), F32),
        scratch_shapes=[pltpu.VMEM((8, 128), F32)],
        compiler_params=_cparams(("arbitrary", "arbitrary")),
        name="fcumsum",
    )(wlf)


def _attn_a_kernel(qa_ref, qi0_ref, qi1_ref, ka_ref, vt_ref, kie_ref, kio_ref, wlf_ref, o_ref,
                   sc_scr, qs_scr, qg_scr, lt_scr, acc_scr, *, tk, top_k, n_iter):
    i = pl.program_id(1)
    nq = Q_BLOCK
    group = HA // HA_KV
    gw = group * nq
    nchunks = ((i + 1) * nq + tk - 1) // tk
    q_pos = i * nq + lax.broadcasted_iota(jnp.int32, (1, nq), 1)
    wt = wlf_ref[...].T
    half = H_IDX // 4
    for p in range(H_IDX // 2):
        src = qi0_ref if p < half else qi1_ref
        qs_scr[p * nq:(p + 1) * nq, :] = src[:, (p % half) * 128:(p % half + 1) * 128]

    def score_chunk(c, carry):
        rmin, rmax = carry
        off = pl.multiple_of(c * tk, tk)
        ke = kie_ref[pl.ds(off, tk), :]
        ko = kio_ref[pl.ds(off, tk), :]
        s = jnp.zeros((tk, nq), F32)
        pairs_per_dot = MXU_N // nq
        for pg in range(H_IDX // 2 // pairs_per_dot):
            qs = qs_scr[pg * MXU_N:(pg + 1) * MXU_N, :]
            de = lax.dot_general(ke, qs, (((1,), (1,)), ((), ())), preferred_element_type=F32)
            do = lax.dot_general(ko, qs, (((1,), (1,)), ((), ())), preferred_element_type=F32)
            for pp in range(pairs_per_dot):
                p = pg * pairs_per_dot + pp
                cols = slice(pp * nq, (pp + 1) * nq)
                s = s + jnp.maximum(de[:, cols], 0.0) * wt[LANE_WI + 2 * p:LANE_WI + 2 * p + 1, :]
                s = s + jnp.maximum(do[:, cols], 0.0) * wt[LANE_WI + 2 * p + 1:LANE_WI + 2 * p + 2, :]
        kpos = off + lax.broadcasted_iota(jnp.int32, (tk, 1), 0)
        valid = kpos <= q_pos
        sc_scr[pl.ds(off, tk), :] = jnp.where(valid, s, NEG)
        rmin = jnp.minimum(rmin, jnp.min(jnp.where(valid, s, -NEG), axis=0, keepdims=True))
        rmax = jnp.maximum(rmax, jnp.max(jnp.where(valid, s, NEG), axis=0, keepdims=True))
        return rmin, rmax

    lo, hi = lax.fori_loop(0, nchunks, score_chunk,
                           (jnp.full((1, nq), -NEG, F32), jnp.full((1, nq), NEG, F32)))

    def bisect(_, carry):
        lo, hi = carry
        mid = 0.5 * (lo + hi)

        def count_chunk(c, part):
            off = pl.multiple_of(c * tk, tk)
            ind = jnp.where(sc_scr[pl.ds(off, tk), :] >= mid, 1.0, 0.0)
            return part + jnp.sum(ind.reshape(tk // 64, 64, nq), axis=0)

        part = lax.fori_loop(0, nchunks, count_chunk, jnp.zeros((64, nq), F32))
        cnt = jnp.sum(part, axis=0, keepdims=True)
        ge = cnt >= top_k
        return jnp.where(ge, mid, lo), jnp.where(cnt == top_k, mid, jnp.where(ge, hi, mid))

    searching = q_pos >= top_k

    def bisect_round(state):
        it, lo, hi, _ = state
        lo, hi = lax.fori_loop(0, BISECT_ROUND, bisect, (lo, hi))
        pending = jnp.sum(jnp.where(searching & (lo < hi), 1.0, 0.0))
        return it + BISECT_ROUND, lo, hi, pending

    lo, hi = lax.fori_loop(0, BISECT_FIRST, bisect, (lo, hi))
    _, thr, _, _ = lax.while_loop(lambda st: (st[0] < n_iter) & (st[3] > 0.0), bisect_round,
                                  (jnp.int32(BISECT_FIRST), lo, hi, jnp.float32(1.0)))

    tka = tk // 2
    last = (nchunks - 1) * tk
    for n in range(HA_KV):
        for g in range(group):
            h = n * group + g
            qg_scr[n, g * nq:(g + 1) * nq, :] = qa_ref[:, h * HEAD_DIM:(h + 1) * HEAD_DIM]
    acc_scr[...] = jnp.zeros(acc_scr.shape, F32)

    def qk(off, slot):
        msk = jnp.where(sc_scr[pl.ds(off, tka), :] >= thr, 0.0, NEG)
        msk = jnp.concatenate([msk] * group, axis=1)
        cmax = []
        for n in range(HA_KV):
            k = ka_ref[pl.ds(off, tka), n * HEAD_DIM:(n + 1) * HEAD_DIM]
            s = lax.dot_general(k, qg_scr[n], (((1,), (1,)), ((), ())), preferred_element_type=F32) + msk
            lt_scr[n, slot] = s
            cmax.append(jnp.max(s, axis=0, keepdims=True))
        return tuple(cmax)

    def softmax_pv(off, slot, m_old, cmax):
        new = []
        for n in range(HA_KV):
            m_new = jnp.maximum(m_old[n], cmax[n])
            alpha = jnp.exp2(m_old[n] - m_new)
            p = jnp.exp2(lt_scr[n, slot] - m_new).astype(BF16)
            acc_scr[n] = alpha * acc_scr[n] + jnp.dot(vt_ref[n, :, pl.ds(off, tka)], p,
                                                      preferred_element_type=F32)
            new.append(m_new)
        return tuple(new)

    def pair(j, carry):
        m, cmax0 = carry
        off0 = pl.multiple_of(j * tk, tk)
        off1 = pl.multiple_of(off0 + tka, tka)
        off2 = pl.multiple_of(jnp.minimum(off0 + tk, last), tk)
        cmax1 = qk(off1, 1)
        m = softmax_pv(off0, 0, m, cmax0)
        cmax0 = qk(off2, 0)
        return softmax_pv(off1, 1, m, cmax1), cmax0

    lax.fori_loop(0, nchunks, pair, ((jnp.full((1, gw), NEG, F32),) * HA_KV, qk(0, 0)))
    for n in range(HA_KV):
        out_t = acc_scr[n, 0:HEAD_DIM, :] / acc_scr[n, HEAD_DIM:HEAD_DIM + 1, :]
        for g in range(group):
            h = n * group + g
            o_ref[:, h * HEAD_DIM:(h + 1) * HEAD_DIM] = out_t[:, g * nq:(g + 1) * nq].T.astype(BF16)


def _attn_a(proj, vt_a, kie, kio, wlf, b, t_len, top_k):
    nb = t_len // Q_BLOCK
    wa = HA * HEAD_DIM
    wkv = HA_KV * HEAD_DIM
    wqi = H_IDX * D_IDX // 2
    tk = min(512, t_len)
    gw = HA // HA_KV * Q_BLOCK
    kern = functools.partial(_attn_a_kernel, tk=tk, top_k=top_k, n_iter=BISECT_MAX_ITERS)
    return pl.pallas_call(
        kern,
        grid=(b, nb),
        in_specs=[
            pl.BlockSpec((Q_BLOCK, wa), lambda bi, i: (bi * nb + i, OFF_QA // wa)),
            pl.BlockSpec((Q_BLOCK, wqi), lambda bi, i: (bi * nb + i, OFF_QI // wqi)),
            pl.BlockSpec((Q_BLOCK, wqi), lambda bi, i: (bi * nb + i, OFF_QI // wqi + 1)),
            pl.BlockSpec((t_len, wkv), lambda bi, i: (bi, OFF_KA // wkv)),
            pl.BlockSpec((None, HA_KV, VT_ROWS, t_len), lambda bi, i: (bi, 0, 0, 0)),
            pl.BlockSpec((t_len, 128), lambda bi, i: (bi, 0)),
            pl.BlockSpec((t_len, 128), lambda bi, i: (bi, 0)),
            pl.BlockSpec((Q_BLOCK, 128), lambda bi, i: (bi * nb + i, 0)),
        ],
        out_specs=pl.BlockSpec((Q_BLOCK, wa), lambda bi, i: (bi * nb + i, 0)),
        out_shape=jax.ShapeDtypeStruct((b * t_len, wa), BF16),
        scratch_shapes=[
            pltpu.VMEM((t_len, Q_BLOCK), F32),
            pltpu.VMEM((H_IDX // 2 * Q_BLOCK, 128), BF16),
            pltpu.VMEM((HA_KV, gw, HEAD_DIM), BF16),
            pltpu.VMEM((HA_KV, 2, tk // 2, gw), F32),
            pltpu.VMEM((HA_KV, VT_ROWS, gw), F32),
        ],
        compiler_params=_cparams(("arbitrary", "arbitrary")),
        name="attn_a",
    )(proj, proj, proj, proj, vt_a, kie, kio, wlf)


def _attn_b_kernel(q_ref, k_ref, vt_ref, cb_ref, o_ref, lt_scr, acc_scr, *, tq):
    qi = pl.program_id(2)
    tk = tq // 2
    heads = HEADS_PER_STEP_B
    q0 = pl.multiple_of(qi * tq, tq)
    acc_scr[...] = jnp.zeros(acc_scr.shape, F32)

    def qk(off, slot, key0=None):
        cmax = []
        for n in range(heads):
            cols = slice(n * HEAD_DIM, (n + 1) * HEAD_DIM)
            bias = cb_ref[n, pl.ds(q0, 1), :] - cb_ref[n, pl.ds(off, tk), :]
            s = lax.dot_general(k_ref[pl.ds(off, tk), cols], q_ref[:, cols], (((1,), (1,)), ((), ())),
                                preferred_element_type=F32) + jnp.concatenate([bias] * (tq // 128), axis=1)
            if key0 is not None:
                key = key0 + lax.broadcasted_iota(jnp.int32, (tk, 1), 0)
                qry = lax.broadcasted_iota(jnp.int32, (1, tq), 1)
                s = jnp.where(key <= qry, s, NEG)
            lt_scr[n, slot] = s
            cmax.append(jnp.max(s, axis=0, keepdims=True))
        return tuple(cmax)

    def softmax_pv(off, slot, m_old, cmax):
        new = []
        for n in range(heads):
            m_new = jnp.maximum(m_old[n], cmax[n])
            alpha = jnp.exp2(m_old[n] - m_new)
            p = jnp.exp2(lt_scr[n, slot] - m_new).astype(BF16)
            acc_scr[n] = alpha * acc_scr[n] + jnp.dot(vt_ref[n, :, pl.ds(off, tk)], p,
                                                      preferred_element_type=F32)
            new.append(m_new)
        return tuple(new)

    q1 = pl.multiple_of(q0 + tk, tk)
    m = (jnp.full((1, tq), NEG, F32),) * heads
    cmax0 = qk(q0, 0, key0=0)
    cmax1 = qk(q1, 1, key0=tk)
    m = softmax_pv(q0, 0, m, cmax0)
    cmax0 = qk(0, 0)
    m = softmax_pv(q1, 1, m, cmax1)
    last = jnp.maximum(qi - 1, 0) * tq

    def pair(j, carry):
        m, cmax0 = carry
        off0 = pl.multiple_of(j * tq, tq)
        off1 = pl.multiple_of(off0 + tk, tk)
        off2 = pl.multiple_of(jnp.minimum(off0 + tq, last), tq)
        cmax1 = qk(off1, 1)
        m = softmax_pv(off0, 0, m, cmax0)
        cmax0 = qk(off2, 0)
        return softmax_pv(off1, 1, m, cmax1), cmax0

    lax.fori_loop(0, qi, pair, (m, cmax0))
    for n in range(heads):
        out_t = acc_scr[n, 0:HEAD_DIM, :] / acc_scr[n, HEAD_DIM:HEAD_DIM + 1, :]
        for g in range(tq // 128):
            o_ref[g * 128:(g + 1) * 128, n * HEAD_DIM:(n + 1) * HEAD_DIM] = (
                out_t[:, g * 128:(g + 1) * 128].T.astype(BF16))


def _attn_b(proj, vt_b, cb, b, t_len):
    tq = 512
    nq = t_len // tq
    hps = HEADS_PER_STEP_B
    wh = hps * HEAD_DIM
    return pl.pallas_call(
        functools.partial(_attn_b_kernel, tq=tq),
        grid=(b, HB // hps, nq),
        in_specs=[
            pl.BlockSpec((tq, wh), lambda bi, h, i: (bi * nq + i, OFF_QB // wh + h)),
            pl.BlockSpec((t_len, wh), lambda bi, h, i: (bi, OFF_KB // wh + h)),
            pl.BlockSpec((None, hps, VT_ROWS, t_len), lambda bi, h, i: (bi, h, 0, 0)),
            pl.BlockSpec((None, hps, t_len, 128), lambda bi, h, i: (bi, h, 0, 0)),
        ],
        out_specs=pl.BlockSpec((tq, wh), lambda bi, h, i: (bi * nq + i, h)),
        out_shape=jax.ShapeDtypeStruct((b * t_len, HB * HEAD_DIM), BF16),
        scratch_shapes=[pltpu.VMEM((hps, 2, tq // 2, tq), F32), pltpu.VMEM((hps, VT_ROWS, tq), F32)],
        compiler_params=_cparams(("arbitrary", "arbitrary", "arbitrary")),
        name="attn_b",
    )(proj, proj, vt_b, cb)


def _merge_kernel(x_ref, oa_ref, ob_ref, ga_ref, gb_ref, woa_ref, wob_ref, wout_ref, o_ref):
    ya = jnp.dot(oa_ref[...], woa_ref[...], preferred_element_type=F32)
    yb = jnp.dot(ob_ref[...], wob_ref[...], preferred_element_type=F32)
    mixed = ga_ref[...].astype(F32) * ya + gb_ref[...].astype(F32) * yb
    o_ref[...] = x_ref[...] + jnp.dot(mixed.astype(BF16), wout_ref[...], preferred_element_type=F32)


def _merge(x2, out_a, out_b, proj, w_oa, w_ob, w_out, tm):
    m = x2.shape[0]
    wa = HA * HEAD_DIM
    wb = HB * HEAD_DIM
    return pl.pallas_call(
        _merge_kernel,
        grid=(m // tm,),
        in_specs=[
            pl.BlockSpec((tm, D_MODEL), lambda i: (i, 0)),
            pl.BlockSpec((tm, wa), lambda i: (i, 0)),
            pl.BlockSpec((tm, wb), lambda i: (i, 0)),
            pl.BlockSpec((tm, D_MODEL), lambda i: (i, OFF_GA // D_MODEL)),
            pl.BlockSpec((tm, D_MODEL), lambda i: (i, OFF_GB // D_MODEL)),
            _resident((wa, D_MODEL)),
            _resident((wb, D_MODEL)),
            _resident((D_MODEL, D_MODEL)),
        ],
        out_specs=pl.BlockSpec((tm, D_MODEL), lambda i: (i, 0)),
        out_shape=jax.ShapeDtypeStruct((m, D_MODEL), F32),
        compiler_params=_cparams(("arbitrary",)),
        name="merge",
    )(x2, out_a, out_b, proj, proj, w_oa, w_ob, w_out)


def _ffn_kernel(x_ref, g_ref, wg_ref, wu_ref, wd_ref, o_ref, h_scr):
    @pl.when(pl.program_id(1) == 0)
    def _():
        x = x_ref[...]
        h_scr[...] = _rms(x, g_ref[...]).astype(BF16)
        o_ref[...] = x

    h = h_scr[...]
    gate = jnp.dot(h, wg_ref[...].astype(BF16), preferred_element_type=F32)
    up = jnp.dot(h, wu_ref[...].astype(BF16), preferred_element_type=F32)
    act = (jax.nn.silu(gate) * up).astype(BF16)
    o_ref[...] += jnp.dot(act, wd_ref[...].astype(BF16), preferred_element_type=F32)


def _ffn(x2, g, w_gate, w_up, w_down, tm, tf):
    m = x2.shape[0]
    return pl.pallas_call(
        _ffn_kernel,
        grid=(m // tm, D_FF // tf),
        in_specs=[
            pl.BlockSpec((tm, D_MODEL), lambda i, f: (i, 0)),
            pl.BlockSpec((1, D_MODEL), lambda i, f: (0, 0)),
            pl.BlockSpec((D_MODEL, tf), lambda i, f: (0, f)),
            pl.BlockSpec((D_MODEL, tf), lambda i, f: (0, f)),
            pl.BlockSpec((tf, D_MODEL), lambda i, f: (f, 0)),
        ],
        out_specs=pl.BlockSpec((tm, D_MODEL), lambda i, f: (i, 0)),
        out_shape=jax.ShapeDtypeStruct((m, D_MODEL), F32),
        scratch_shapes=[pltpu.VMEM((tm, D_MODEL), BF16)],
        compiler_params=_cparams(("arbitrary", "arbitrary")),
        name="ffn",
    )(x2, g, w_gate, w_up, w_down)


def _ple_kernel(x_ref, p_ref, g_ref, gf_ref, wg_ref, wp_ref, o_ref, *, final_norm):
    x = x_ref[...]
    h = _rms(x, g_ref[...]).astype(BF16)
    gate = jax.nn.sigmoid(jnp.dot(h, wg_ref[...], preferred_element_type=F32))
    emb = jnp.dot(p_ref[...].astype(BF16), wp_ref[...], preferred_element_type=F32)
    y = x + gate * emb
    o_ref[...] = _rms(y, gf_ref[...]) if final_norm else y


def _ple(x2, p2, g, g_final, w_gate, w_proj, tm, final_norm):
    m = x2.shape[0]
    return pl.pallas_call(
        functools.partial(_ple_kernel, final_norm=final_norm),
        grid=(m // tm,),
        in_specs=[
            pl.BlockSpec((tm, D_MODEL), lambda i: (i, 0)),
            pl.BlockSpec((tm, D_PLE), lambda i: (i, 0)),
            pl.BlockSpec((1, D_MODEL), lambda i: (0, 0)),
            pl.BlockSpec((1, D_MODEL), lambda i: (0, 0)),
            _resident((D_MODEL, D_MODEL)),
            _resident((D_PLE, D_MODEL)),
        ],
        out_specs=pl.BlockSpec((tm, D_MODEL), lambda i: (i, 0)),
        out_shape=jax.ShapeDtypeStruct((m, D_MODEL), F32),
        compiler_params=_cparams(("arbitrary",)),
        name="ple",
    )(x2, p2, g, g_final, w_gate, w_proj)


def _rope_lane_freqs():
    def inv(rot):
        half = rot // 2
        return ROPE_THETA ** (-jnp.arange(half, dtype=F32) / half)
    used = ROT_A // 2 + ROT_IDX // 2
    return jnp.concatenate([inv(ROT_A), inv(ROT_IDX), jnp.zeros((128 - used,), F32)])[None, :]


def _w_in_views(w):
    w_t = jnp.swapaxes(w, 0, 1)
    rows = [w_t[_SPLIT_EDGES[k]:_SPLIT_EDGES[k + 1]] for k in (4, 5, 9)]
    pad = jnp.zeros((128 - D_IDX - H_IDX - HB, w_t.shape[1]), w_t.dtype)
    return w_t, jnp.concatenate(rows + [pad], axis=0)


def kernel(x, p, positions, g_mix, w_in, b_f, w_o_a, w_o_b, w_out, g_ffn, w_ffn_gate, w_ffn_up,
           w_ffn_down, g_ple, w_ple_gate, w_ple_proj, g_final):
    b, t_len, d = x.shape
    depth = w_in.shape[0]
    m = b * t_len
    top_k = min(TOPK_MAX, t_len // 4)
    inv_freq = _rope_lane_freqs()
    pos2 = positions.reshape(m, 1)
    x2 = x.reshape(m, d)
    tm_in = min(1024, m)
    tm = min(512, m)
    for i in range(depth):
        w_t, w_small = _w_in_views(w_in[i])
        bf_row = jnp.zeros((1, 128), F32).at[0, LANE_FB:LANE_FB + HB].set(b_f[i].astype(F32))
        proj, kie, kio, wlf, vt_a, vt_b = _inproj(x2, pos2, g_mix[i][None, :], w_t, w_small, bf_row,
                                                  inv_freq, min(tm_in, t_len), b, t_len)
        cb = _cumsum(wlf, b, t_len)
        out_a = _attn_a(proj, vt_a, kie, kio, wlf, b, t_len, top_k)
        out_b = _attn_b(proj, vt_b, cb, b, t_len)
        x2 = _merge(x2, out_a, out_b, proj, w_o_a[i].astype(BF16), w_o_b[i].astype(BF16),
                    w_out[i].astype(BF16), tm)
        x2 = _ffn(x2, g_ffn[i][None, :], w_ffn_gate[i], w_ffn_up[i], w_ffn_down[i], tm_in, 256)
        x2 = _ple(x2, p[i].reshape(m, D_PLE), g_ple[i][None, :], g_final[None, :], w_ple_gate[i].astype(BF16),
                  w_ple_proj[i].astype(BF16), tm, final_norm=(i + 1 == depth))
    return x2.reshape(b, t_len, d)
```

```python
import functools

import jax
import jax.numpy as jnp
import numpy as np
from jax import lax
from jax.experimental import pallas as pl
from jax.experimental.pallas import tpu as pltpu

F32 = jnp.float32
BF16 = jnp.bfloat16

D_MODEL = 2048
HEAD_DIM = 128
HA = 8
HA_KV = 2
H_IDX = 16
D_IDX = 64
TOPK_MAX = 256
HB = 8
ROPE_THETA = 500000.0
ROT_A = HEAD_DIM // 4
ROT_IDX = D_IDX // 4
Q_BLOCK = 128
D_FF = -(-8 * D_MODEL // (3 * 256)) * 256
D_PLE = 256
EPS = 1e-6

SPLIT_SIZES = (HA * HEAD_DIM, HA_KV * HEAD_DIM, HA_KV * HEAD_DIM, H_IDX * D_IDX, D_IDX, H_IDX,
               HB * HEAD_DIM, HB * HEAD_DIM, HB * HEAD_DIM, HB, D_MODEL, D_MODEL)

OFF_GA = 0
OFF_GB = OFF_GA + D_MODEL
OFF_QA = OFF_GB + D_MODEL
OFF_KA = OFF_QA + HA * HEAD_DIM
OFF_VA = OFF_KA + HA_KV * HEAD_DIM
OFF_QI = OFF_VA + HA_KV * HEAD_DIM
TN_IN = 1024
MXU_N = 256
OFF_QB = -(-(OFF_QI + H_IDX * D_IDX) // TN_IN) * TN_IN
OFF_KB = OFF_QB + HB * HEAD_DIM
OFF_VB = OFF_KB + HB * HEAD_DIM
N_MAIN = OFF_VB + HB * HEAD_DIM
_SPLIT_EDGES = [0] + [int(s) for s in np.cumsum(SPLIT_SIZES)]
_PACK_RUNS = ((OFF_GA, _SPLIT_EDGES[10], 2 * D_MODEL),
              (OFF_QA, _SPLIT_EDGES[0], OFF_QI + H_IDX * D_IDX - OFF_QA),
              (OFF_QB, _SPLIT_EDGES[6], N_MAIN - OFF_QB))
_SEGMENTS = ((OFF_GA, OFF_QA, "gate"), (OFF_QA, OFF_KA, "q_a"), (OFF_KA, OFF_VA, "k_a"),
             (OFF_VA, OFF_QI, "v_a"), (OFF_QI, OFF_QI + H_IDX * D_IDX, "q_i"),
             (OFF_QB, OFF_KB, "q_b"), (OFF_KB, OFF_VB, "k_b"), (OFF_VB, N_MAIN, "v_b"))


def _tile_patterns():
    def group(col):
        for lo, hi, kind in _SEGMENTS:
            if lo <= col < hi:
                assert (col - lo) % MXU_N == 0 and col + MXU_N <= hi
                return kind, (col - lo) // MXU_N
        return None
    by_pattern = {}
    for tile in range(N_MAIN // TN_IN):
        pattern = tuple(group(tile * TN_IN + g * MXU_N) for g in range(TN_IN // MXU_N))
        pattern = tuple(None if grp is None else (grp[0], grp[1] if grp[0] in ("v_a", "v_b") else 0)
                        for grp in pattern)
        by_pattern.setdefault(pattern, []).append(tile)
    return tuple(by_pattern.items())


_TILE_PATTERNS = _tile_patterns()
LANE_KI = 0
LANE_WI = D_IDX
LANE_FB = D_IDX + H_IDX

NEG = -1e30
VMEM_LIMIT = 60 * 1024 * 1024
LOG2E = 1.4426950408889634
ATTN_SCALE = HEAD_DIM ** -0.5 * LOG2E
VT_ROWS = HEAD_DIM + 16
TRIG_PACK = 4
HEADS_PER_STEP_B = 4
BISECT_FIRST = 12
BISECT_ROUND = 4
BISECT_MAX_ITERS = 48


def _cparams(sem):
    return pltpu.CompilerParams(dimension_semantics=sem, vmem_limit_bytes=VMEM_LIMIT)


def _rms(x, g):
    return x * lax.rsqrt(jnp.mean(x * x, axis=-1, keepdims=True) + EPS) * g


def _resident(shape):
    nd = len(shape)
    return pl.BlockSpec(shape, lambda *_: (0,) * nd, pipeline_mode=pl.Buffered(1))


def _rope128(a, c, sa, sb, shift):
    outs = []
    for q in range(a.shape[1] // 128):
        aq = a[:, q * 128:(q + 1) * 128]
        outs.append(aq * c + pltpu.roll(aq, 128 - shift, 1) * sa + pltpu.roll(aq, shift, 1) * sb)
    return outs[0] if len(outs) == 1 else jnp.concatenate(outs, axis=1)


def _inproj_kernel(x_ref, pos_ref, g_ref, w_ref, ws_ref, bf_ref, inv_ref,
                   o_ref, kie_ref, kio_ref, wlf_ref, vta_ref, vtb_ref,
                   h_scr, ca_scr, saa_scr, sab_scr, ci_scr, sia_scr, sib_scr):
    j = pl.program_id(1)
    tm = x_ref.shape[0]

    @pl.when(j == 0)
    def _():
        hb = _rms(x_ref[...], g_ref[...]).astype(BF16)
        h_scr[...] = hb
        pos = pos_ref[...].astype(F32)
        lane = lax.broadcasted_iota(jnp.int32, (tm, 128), 1)
        ha, hi = ROT_A // 2, ROT_IDX // 2
        span, rq = 128 // TRIG_PACK, tm // TRIG_PACK
        lane_q = lax.broadcasted_iota(jnp.int32, (rq, 128), 1)
        pos_p = jnp.zeros((rq, 128), F32)
        for q in range(TRIG_PACK):
            pos_p = jnp.where((lane_q >= q * span) & (lane_q < (q + 1) * span), pos[q * rq:(q + 1) * rq, :], pos_p)
        ang = pos_p * inv_ref[...]
        cs_p = jnp.cos(ang)
        sn_p = jnp.sin(ang)
        cs = jnp.concatenate([pltpu.roll(cs_p, (128 - q * span) % 128, 1) for q in range(TRIG_PACK)], axis=0)
        sn = jnp.concatenate([pltpu.roll(sn_p, (128 - q * span) % 128, 1) for q in range(TRIG_PACK)], axis=0)

        def at(tab, src, dst):
            return pltpu.roll(tab, (dst - src) % 128, 1)

        def lanes(lo, n):
            return (lane >= lo) & (lane < lo + n)

        ca_scr[...] = jnp.where(lanes(0, ha), cs, jnp.where(lanes(ha, ha), at(cs, 0, ha), 1.0))
        saa_scr[...] = jnp.where(lanes(0, ha), -sn, 0.0)
        sab_scr[...] = jnp.where(lanes(ha, ha), at(sn, 0, ha), 0.0)
        ci = jnp.ones((tm, 128), F32)
        sia = jnp.zeros((tm, 128), F32)
        sib = jnp.zeros((tm, 128), F32)
        for head0 in (0, D_IDX):
            ci = jnp.where(lanes(head0, hi), at(cs, ha, head0), ci)
            ci = jnp.where(lanes(head0 + hi, hi), at(cs, ha, head0 + hi), ci)
            sia = jnp.where(lanes(head0, hi), -at(sn, ha, head0), sia)
            sib = jnp.where(lanes(head0 + hi, hi), at(sn, ha, head0 + hi), sib)
        ci_scr[...] = ci
        sia_scr[...] = sia
        sib_scr[...] = sib
        small = lax.dot_general(hb, ws_ref[...].astype(BF16), (((1,), (1,)), ((), ())),
                                preferred_element_type=F32)
        kr = _rope128(small, ci, sia, sib, ROT_IDX // 2)
        ke = jnp.where(lane < D_IDX, kr, 0.0)
        kie_ref[...] = ke.astype(BF16)
        kio_ref[...] = pltpu.roll(ke, D_IDX, 1).astype(BF16)
        is_w = (lane >= LANE_WI) & (lane < LANE_WI + H_IDX)
        is_f = (lane >= LANE_FB) & (lane < LANE_FB + HB)
        logf = jax.nn.log_sigmoid(small + bf_ref[...])
        wlf_ref[...] = jnp.where(is_w, small * (H_IDX ** -0.5 * D_IDX ** -0.5), jnp.where(is_f, logf, 0.0))

    t = TN_IN
    groups = t // MXU_N
    heads_per_group = MXU_N // HEAD_DIM

    def project(epilogues):
        for c, epilogue in enumerate(epilogues):
            if epilogue is None:
                continue
            cols = slice(c * MXU_N, (c + 1) * MXU_N)
            w_t = w_ref[cols, :].astype(BF16)
            acc = lax.dot_general(h_scr[...], w_t, (((1,), (1,)), ((), ())), preferred_element_type=F32)
            o_ref[:, cols] = epilogue(acc).astype(BF16)

    def rope_a(acc):
        return _rope128(acc, ca_scr[...], saa_scr[...], sab_scr[...], ROT_A // 2)

    def rope_i(acc):
        return _rope128(acc, ci_scr[...], sia_scr[...], sib_scr[...], ROT_IDX // 2)

    def values(vt_ref, c):
        def epilogue(acc):
            acc_t = acc.T
            for hh in range(heads_per_group):
                h = (c * heads_per_group + hh) % vt_ref.shape[0]
                vt_ref[h, 0:HEAD_DIM, :] = acc_t[hh * HEAD_DIM:(hh + 1) * HEAD_DIM, :].astype(BF16)
                vt_ref[h, HEAD_DIM:VT_ROWS, :] = jnp.ones((VT_ROWS - HEAD_DIM, tm), BF16)
            return acc
        return epilogue

    def epilogue_of(kind, c):
        return {"gate": lambda: jax.nn.sigmoid,
                "q_a": lambda: (lambda acc: rope_a(acc) * ATTN_SCALE),
                "k_a": lambda: rope_a,
                "v_a": lambda: values(vta_ref, c),
                "q_i": lambda: rope_i,
                "q_b": lambda: (lambda acc: acc * ATTN_SCALE),
                "k_b": lambda: (lambda acc: acc),
                "v_b": lambda: values(vtb_ref, c)}[kind]()

    for pattern, tiles in _TILE_PATTERNS:
        pred = functools.reduce(lambda a, b: a | b, [j == tile for tile in tiles])

        @pl.when(pred)
        def _(pattern=pattern):
            project([None if grp is None else epilogue_of(*grp) for grp in pattern])


def _w_in_row(j):
    row = jnp.int32(0)
    for out0, src0, _ in _PACK_RUNS:
        row = jnp.where(j >= out0 // TN_IN, src0 + TN_IN * (j - out0 // TN_IN), row)
    return pl.multiple_of(row, 8)


def _inproj(x2, pos2, g, w_t, w_small, bf_row, inv_freq, tm, b, t_len):
    m = x2.shape[0]
    for out0, src0, width in _PACK_RUNS:
        assert out0 % TN_IN == 0 and src0 % 8 == 0
        assert src0 + -(-width // TN_IN) * TN_IN <= w_t.shape[0]
    assert list(_PACK_RUNS) == sorted(_PACK_RUNS)
    grid = (m // tm, N_MAIN // TN_IN)
    nt = t_len // tm
    j_vb = OFF_VB // TN_IN
    heads_per_tile = min(TN_IN // HEAD_DIM, HB)
    vb_tiles = HB // heads_per_tile
    row128 = pl.BlockSpec((1, 128), lambda i, j: (0, 0))
    return pl.pallas_call(
        _inproj_kernel,
        grid=grid,
        in_specs=[
            pl.BlockSpec((tm, D_MODEL), lambda i, j: (i, 0)),
            pl.BlockSpec((tm, 1), lambda i, j: (i, 0)),
            pl.BlockSpec((1, D_MODEL), lambda i, j: (0, 0)),
            pl.BlockSpec((pl.Element(TN_IN), pl.Element(D_MODEL)), lambda i, j: (_w_in_row(j), 0)),
            pl.BlockSpec((128, D_MODEL), lambda i, j: (0, 0)),
            row128, row128,
        ],
        out_specs=[
            pl.BlockSpec((tm, TN_IN), lambda i, j: (i, j)),
            pl.BlockSpec((tm, 128), lambda i, j: (i, 0)),
            pl.BlockSpec((tm, 128), lambda i, j: (i, 0)),
            pl.BlockSpec((tm, 128), lambda i, j: (i, 0)),
            pl.BlockSpec((None, HA_KV, VT_ROWS, tm), lambda i, j: (i // nt, 0, 0, i % nt)),
            pl.BlockSpec((None, heads_per_tile, VT_ROWS, tm),
                         lambda i, j: (i // nt, jnp.clip(j - j_vb, 0, vb_tiles - 1), 0, i % nt)),
        ],
        out_shape=[
            jax.ShapeDtypeStruct((m, N_MAIN), BF16),
            jax.ShapeDtypeStruct((m, 128), BF16),
            jax.ShapeDtypeStruct((m, 128), BF16),
            jax.ShapeDtypeStruct((m, 128), F32),
            jax.ShapeDtypeStruct((b, HA_KV, VT_ROWS, t_len), BF16),
            jax.ShapeDtypeStruct((b, HB, VT_ROWS, t_len), BF16),
        ],
        scratch_shapes=[pltpu.VMEM((tm, D_MODEL), BF16)] + [pltpu.VMEM((tm, 128), F32)] * 6,
        compiler_params=_cparams(("arbitrary", "arbitrary")),
        name="inproj",
    )(x2, pos2, g, w_t, w_small, bf_row, inv_freq)


def _cumsum_kernel(wlf_ref, o_ref, carry_scr, *, tc):
    @pl.when(pl.program_id(1) == 0)
    def _():
        carry_scr[...] = jnp.zeros_like(carry_scr)

    lf = wlf_ref[...]
    r = lax.broadcasted_iota(jnp.int32, (tc, tc), 0)
    c = lax.broadcasted_iota(jnp.int32, (tc, tc), 1)
    tri = jnp.where(c <= r, 1.0, 0.0).astype(BF16)
    hi = lf.astype(BF16)
    r1 = lf - hi.astype(F32)
    mid = r1.astype(BF16)
    lo = (r1 - mid.astype(F32)).astype(BF16)
    cs = (jnp.dot(tri, hi, preferred_element_type=F32) + jnp.dot(tri, mid, preferred_element_type=F32)
          + jnp.dot(tri, lo, preferred_element_type=F32)) + carry_scr[0:1, :]
    carry_scr[...] = jnp.broadcast_to(cs[tc - 1:tc, :], carry_scr.shape)
    cs2 = cs * LOG2E
    for grp in range(HB // HEADS_PER_STEP_B):
        o_ref[grp] = pltpu.roll(cs2, (128 - LANE_FB - grp * HEADS_PER_STEP_B) % 128, 1)


def _cumsum(wlf, b, t_len):
    tc = 256
    nt = t_len // tc
    return pl.pallas_call(
        functools.partial(_cumsum_kernel, tc=tc),
        grid=(b, nt),
        in_specs=[pl.BlockSpec((tc, 128), lambda bi, ti: (bi * nt + ti, 0))],
        out_specs=pl.BlockSpec((None, HB // HEADS_PER_STEP_B, tc, 128), lambda bi, ti: (bi, 0, ti, 0)),
        out_shape=jax.ShapeDtypeStruct((b, HB // HEADS_PER_STEP_B, t_len, 128), F32),
        scratch_shapes=[pltpu.VMEM((8, 128), F32)],
        compiler_params=_cparams(("arbitrary", "arbitrary")),
        name="fcumsum",
    )(wlf)


def _attn_a_kernel(qa_ref, qi0_ref, qi1_ref, ka_ref, vt_ref, kie_ref, kio_ref, wlf_ref, o_ref,
                   sc_scr, qs_scr, qg_scr, lt_scr, acc_scr, *, tk, top_k, n_iter):
    i = pl.program_id(1)
    nq = Q_BLOCK
    group = HA // HA_KV
    gw = group * nq
    nchunks = ((i + 1) * nq + tk - 1) // tk
    q_pos = i * nq + lax.broadcasted_iota(jnp.int32, (1, nq), 1)
    wt = wlf_ref[...].T
    half = H_IDX // 4
    for p in range(H_IDX // 2):
        src = qi0_ref if p < half else qi1_ref
        qs_scr[p * nq:(p + 1) * nq, :] = src[:, (p % half) * 128:(p % half + 1) * 128]

    def score_chunk(c, carry):
        rmin, rmax = carry
        off = pl.multiple_of(c * tk, tk)
        ke = kie_ref[pl.ds(off, tk), :]
        ko = kio_ref[pl.ds(off, tk), :]
        s = jnp.zeros((tk, nq), F32)
        pairs_per_dot = MXU_N // nq
        for pg in range(H_IDX // 2 // pairs_per_dot):
            qs = qs_scr[pg * MXU_N:(pg + 1) * MXU_N, :]
            de = lax.dot_general(ke, qs, (((1,), (1,)), ((), ())), preferred_element_type=F32)
            do = lax.dot_general(ko, qs, (((1,), (1,)), ((), ())), preferred_element_type=F32)
            for pp in range(pairs_per_dot):
                p = pg * pairs_per_dot + pp
                cols = slice(pp * nq, (pp + 1) * nq)
                s = s + jnp.maximum(de[:, cols], 0.0) * wt[LANE_WI + 2 * p:LANE_WI + 2 * p + 1, :]
                s = s + jnp.maximum(do[:, cols], 0.0) * wt[LANE_WI + 2 * p + 1:LANE_WI + 2 * p + 2, :]
        kpos = off + lax.broadcasted_iota(jnp.int32, (tk, 1), 0)
        valid = kpos <= q_pos
        sc_scr[pl.ds(off, tk), :] = jnp.where(valid, s, NEG)
        rmin = jnp.minimum(rmin, jnp.min(jnp.where(valid, s, -NEG), axis=0, keepdims=True))
        rmax = jnp.maximum(rmax, jnp.max(jnp.where(valid, s, NEG), axis=0, keepdims=True))
        return rmin, rmax

    lo, hi = lax.fori_loop(0, nchunks, score_chunk,
                           (jnp.full((1, nq), -NEG, F32), jnp.full((1, nq), NEG, F32)))

    def bisect(_, carry):
        lo, hi = carry
        mid = 0.5 * (lo + hi)

        def count_chunk(c, part):
            off = pl.multiple_of(c * tk, tk)
            ind = jnp.where(sc_scr[pl.ds(off, tk), :] >= mid, 1.0, 0.0)
            return part + jnp.sum(ind.reshape(tk // 64, 64, nq), axis=0)

        part = lax.fori_loop(0, nchunks, count_chunk, jnp.zeros((64, nq), F32))
        cnt = jnp.sum(part, axis=0, keepdims=True)
        ge = cnt >= top_k
        return jnp.where(ge, mid, lo), jnp.where(cnt == top_k, mid, jnp.where(ge, hi, mid))

    searching = q_pos >= top_k

    def bisect_round(state):
        it, lo, hi, _ = state
        lo, hi = lax.fori_loop(0, BISECT_ROUND, bisect, (lo, hi))
        pending = jnp.sum(jnp.where(searching & (lo < hi), 1.0, 0.0))
        return it + BISECT_ROUND, lo, hi, pending

    lo, hi = lax.fori_loop(0, BISECT_FIRST, bisect, (lo, hi))
    _, thr, _, _ = lax.while_loop(lambda st: (st[0] < n_iter) & (st[3] > 0.0), bisect_round,
                                  (jnp.int32(BISECT_FIRST), lo, hi, jnp.float32(1.0)))

    tka = tk // 2
    last = (nchunks - 1) * tk
    for n in range(HA_KV):
        for g in range(group):
            h = n * group + g
            qg_scr[n, g * nq:(g + 1) * nq, :] = qa_ref[:, h * HEAD_DIM:(h + 1) * HEAD_DIM]
    acc_scr[...] = jnp.zeros(acc_scr.shape, F32)

    def qk(off, slot):
        msk = jnp.where(sc_scr[pl.ds(off, tka), :] >= thr, 0.0, NEG)
        msk = jnp.concatenate([msk] * group, axis=1)
        cmax = []
        for n in range(HA_KV):
            k = ka_ref[pl.ds(off, tka), n * HEAD_DIM:(n + 1) * HEAD_DIM]
            s = lax.dot_general(k, qg_scr[n], (((1,), (1,)), ((), ())), preferred_element_type=F32) + msk
            lt_scr[n, slot] = s
            cmax.append(jnp.max(s, axis=0, keepdims=True))
        return tuple(cmax)

    def softmax_pv(off, slot, m_old, cmax):
        new = []
        for n in range(HA_KV):
            m_new = jnp.maximum(m_old[n], cmax[n])
            alpha = jnp.exp2(m_old[n] - m_new)
            p = jnp.exp2(lt_scr[n, slot] - m_new).astype(BF16)
            acc_scr[n] = alpha * acc_scr[n] + jnp.dot(vt_ref[n, :, pl.ds(off, tka)], p,
                                                      preferred_element_type=F32)
            new.append(m_new)
        return tuple(new)

    def pair(j, carry):
        m, cmax0 = carry
        off0 = pl.multiple_of(j * tk, tk)
        off1 = pl.multiple_of(off0 + tka, tka)
        off2 = pl.multiple_of(jnp.minimum(off0 + tk, last), tk)
        cmax1 = qk(off1, 1)
        m = softmax_pv(off0, 0, m, cmax0)
        cmax0 = qk(off2, 0)
        return softmax_pv(off1, 1, m, cmax1), cmax0

    lax.fori_loop(0, nchunks, pair, ((jnp.full((1, gw), NEG, F32),) * HA_KV, qk(0, 0)))
    for n in range(HA_KV):
        out_t = acc_scr[n, 0:HEAD_DIM, :] / acc_scr[n, HEAD_DIM:HEAD_DIM + 1, :]
        for g in range(group):
            h = n * group + g
            o_ref[:, h * HEAD_DIM:(h + 1) * HEAD_DIM] = out_t[:, g * nq:(g + 1) * nq].T.astype(BF16)


def _attn_a(proj, vt_a, kie, kio, wlf, b, t_len, top_k):
    nb = t_len // Q_BLOCK
    wa = HA * HEAD_DIM
    wkv = HA_KV * HEAD_DIM
    wqi = H_IDX * D_IDX // 2
    tk = min(512, t_len)
    gw = HA // HA_KV * Q_BLOCK
    kern = functools.partial(_attn_a_kernel, tk=tk, top_k=top_k, n_iter=BISECT_MAX_ITERS)
    return pl.pallas_call(
        kern,
        grid=(b, nb),
        in_specs=[
            pl.BlockSpec((Q_BLOCK, wa), lambda bi, i: (bi * nb + i, OFF_QA // wa)),
            pl.BlockSpec((Q_BLOCK, wqi), lambda bi, i: (bi * nb + i, OFF_QI // wqi)),
            pl.BlockSpec((Q_BLOCK, wqi), lambda bi, i: (bi * nb + i, OFF_QI // wqi + 1)),
            pl.BlockSpec((t_len, wkv), lambda bi, i: (bi, OFF_KA // wkv)),
            pl.BlockSpec((None, HA_KV, VT_ROWS, t_len), lambda bi, i: (bi, 0, 0, 0)),
            pl.BlockSpec((t_len, 128), lambda bi, i: (bi, 0)),
            pl.BlockSpec((t_len, 128), lambda bi, i: (bi, 0)),
            pl.BlockSpec((Q_BLOCK, 128), lambda bi, i: (bi * nb + i, 0)),
        ],
        out_specs=pl.BlockSpec((Q_BLOCK, wa), lambda bi, i: (bi * nb + i, 0)),
        out_shape=jax.ShapeDtypeStruct((b * t_len, wa), BF16),
        scratch_shapes=[
            pltpu.VMEM((t_len, Q_BLOCK), F32),
            pltpu.VMEM((H_IDX // 2 * Q_BLOCK, 128), BF16),
            pltpu.VMEM((HA_KV, gw, HEAD_DIM), BF16),
            pltpu.VMEM((HA_KV, 2, tk // 2, gw), F32),
            pltpu.VMEM((HA_KV, VT_ROWS, gw), F32),
        ],
        compiler_params=_cparams(("arbitrary", "arbitrary")),
        name="attn_a",
    )(proj, proj, proj, proj, vt_a, kie, kio, wlf)


def _attn_b_kernel(q_ref, k_ref, vt_ref, cb_ref, o_ref, lt_scr, acc_scr, *, tq):
    qi = pl.program_id(2)
    tk = tq // 2
    heads = HEADS_PER_STEP_B
    q0 = pl.multiple_of(qi * tq, tq)
    acc_scr[...] = jnp.zeros(acc_scr.shape, F32)

    def qk(off, slot, key0=None):
        cmax = []
        for n in range(heads):
            cols = slice(n * HEAD_DIM, (n + 1) * HEAD_DIM)
            bias = (cb_ref[pl.ds(q0, 1), :] - cb_ref[pl.ds(off, tk), :])[:, n:n + 1]
            s = lax.dot_general(k_ref[pl.ds(off, tk), cols], q_ref[:, cols], (((1,), (1,)), ((), ())),
                                preferred_element_type=F32) + bias
            if key0 is not None:
                key = key0 + lax.broadcasted_iota(jnp.int32, (tk, 1), 0)
                qry = lax.broadcasted_iota(jnp.int32, (1, tq), 1)
                s = jnp.where(key <= qry, s, NEG)
            lt_scr[n, slot] = s
            cmax.append(jnp.max(s, axis=0, keepdims=True))
        return tuple(cmax)

    def softmax_pv(off, slot, m_old, cmax):
        new = []
        for n in range(heads):
            m_new = jnp.maximum(m_old[n], cmax[n])
            alpha = jnp.exp2(m_old[n] - m_new)
            p = jnp.exp2(lt_scr[n, slot] - m_new).astype(BF16)
            acc_scr[n] = alpha * acc_scr[n] + jnp.dot(vt_ref[n, :, pl.ds(off, tk)], p,
                                                      preferred_element_type=F32)
            new.append(m_new)
        return tuple(new)

    q1 = pl.multiple_of(q0 + tk, tk)
    m = (jnp.full((1, tq), NEG, F32),) * heads
    cmax0 = qk(q0, 0, key0=0)
    cmax1 = qk(q1, 1, key0=tk)
    m = softmax_pv(q0, 0, m, cmax0)
    cmax0 = qk(0, 0)
    m = softmax_pv(q1, 1, m, cmax1)
    last = jnp.maximum(qi - 1, 0) * tq

    def pair(j, carry):
        m, cmax0 = carry
        off0 = pl.multiple_of(j * tq, tq)
        off1 = pl.multiple_of(off0 + tk, tk)
        off2 = pl.multiple_of(jnp.minimum(off0 + tq, last), tq)
        cmax1 = qk(off1, 1)
        m = softmax_pv(off0, 0, m, cmax0)
        cmax0 = qk(off2, 0)
        return softmax_pv(off1, 1, m, cmax1), cmax0

    lax.fori_loop(0, qi, pair, (m, cmax0))
    for n in range(heads):
        out_t = acc_scr[n, 0:HEAD_DIM, :] / acc_scr[n, HEAD_DIM:HEAD_DIM + 1, :]
        for g in range(tq // 128):
            o_ref[g * 128:(g + 1) * 128, n * HEAD_DIM:(n + 1) * HEAD_DIM] = (
                out_t[:, g * 128:(g + 1) * 128].T.astype(BF16))


def _attn_b(proj, vt_b, cb, b, t_len):
    tq = 512
    nq = t_len // tq
    hps = HEADS_PER_STEP_B
    wh = hps * HEAD_DIM
    return pl.pallas_call(
        functools.partial(_attn_b_kernel, tq=tq),
        grid=(b, HB // hps, nq),
        in_specs=[
            pl.BlockSpec((tq, wh), lambda bi, h, i: (bi * nq + i, OFF_QB // wh + h)),
            pl.BlockSpec((t_len, wh), lambda bi, h, i: (bi, OFF_KB // wh + h)),
            pl.BlockSpec((None, hps, VT_ROWS, t_len), lambda bi, h, i: (bi, h, 0, 0)),
            pl.BlockSpec((None, None, t_len, 128), lambda bi, h, i: (bi, h, 0, 0)),
        ],
        out_specs=pl.BlockSpec((tq, wh), lambda bi, h, i: (bi * nq + i, h)),
        out_shape=jax.ShapeDtypeStruct((b * t_len, HB * HEAD_DIM), BF16),
        scratch_shapes=[pltpu.VMEM((hps, 2, tq // 2, tq), F32), pltpu.VMEM((hps, VT_ROWS, tq), F32)],
        compiler_params=_cparams(("arbitrary", "arbitrary", "arbitrary")),
        name="attn_b",
    )(proj, proj, vt_b, cb)


def _merge_kernel(x_ref, oa_ref, ob_ref, ga_ref, gb_ref, woa_ref, wob_ref, wout_ref, o_ref):
    ya = jnp.dot(oa_ref[...], woa_ref[...], preferred_element_type=F32)
    yb = jnp.dot(ob_ref[...], wob_ref[...], preferred_element_type=F32)
    mixed = ga_ref[...].astype(F32) * ya + gb_ref[...].astype(F32) * yb
    o_ref[...] = x_ref[...] + jnp.dot(mixed.astype(BF16), wout_ref[...], preferred_element_type=F32)


def _merge(x2, out_a, out_b, proj, w_oa, w_ob, w_out, tm):
    m = x2.shape[0]
    wa = HA * HEAD_DIM
    wb = HB * HEAD_DIM
    return pl.pallas_call(
        _merge_kernel,
        grid=(m // tm,),
        in_specs=[
            pl.BlockSpec((tm, D_MODEL), lambda i: (i, 0)),
            pl.BlockSpec((tm, wa), lambda i: (i, 0)),
            pl.BlockSpec((tm, wb), lambda i: (i, 0)),
            pl.BlockSpec((tm, D_MODEL), lambda i: (i, OFF_GA // D_MODEL)),
            pl.BlockSpec((tm, D_MODEL), lambda i: (i, OFF_GB // D_MODEL)),
            _resident((wa, D_MODEL)),
            _resident((wb, D_MODEL)),
            _resident((D_MODEL, D_MODEL)),
        ],
        out_specs=pl.BlockSpec((tm, D_MODEL), lambda i: (i, 0)),
        out_shape=jax.ShapeDtypeStruct((m, D_MODEL), F32),
        compiler_params=_cparams(("arbitrary",)),
        name="merge",
    )(x2, out_a, out_b, proj, proj, w_oa, w_ob, w_out)


def _ffn_kernel(x_ref, g_ref, wg_ref, wu_ref, wd_ref, o_ref, h_scr):
    @pl.when(pl.program_id(1) == 0)
    def _():
        x = x_ref[...]
        h_scr[...] = _rms(x, g_ref[...]).astype(BF16)
        o_ref[...] = x

    h = h_scr[...]
    gate = jnp.dot(h, wg_ref[...].astype(BF16), preferred_element_type=F32)
    up = jnp.dot(h, wu_ref[...].astype(BF16), preferred_element_type=F32)
    act = (jax.nn.silu(gate) * up).astype(BF16)
    o_ref[...] += jnp.dot(act, wd_ref[...].astype(BF16), preferred_element_type=F32)


def _ffn(x2, g, w_gate, w_up, w_down, tm, tf):
    m = x2.shape[0]
    return pl.pallas_call(
        _ffn_kernel,
        grid=(m // tm, D_FF // tf),
        in_specs=[
            pl.BlockSpec((tm, D_MODEL), lambda i, f: (i, 0)),
            pl.BlockSpec((1, D_MODEL), lambda i, f: (0, 0)),
            pl.BlockSpec((D_MODEL, tf), lambda i, f: (0, f)),
            pl.BlockSpec((D_MODEL, tf), lambda i, f: (0, f)),
            pl.BlockSpec((tf, D_MODEL), lambda i, f: (f, 0)),
        ],
        out_specs=pl.BlockSpec((tm, D_MODEL), lambda i, f: (i, 0)),
        out_shape=jax.ShapeDtypeStruct((m, D_MODEL), F32),
        scratch_shapes=[pltpu.VMEM((tm, D_MODEL), BF16)],
        compiler_params=_cparams(("arbitrary", "arbitrary")),
        name="ffn",
    )(x2, g, w_gate, w_up, w_down)


def _ple_kernel(x_ref, p_ref, g_ref, gf_ref, wg_ref, wp_ref, o_ref, *, final_norm):
    x = x_ref[...]
    h = _rms(x, g_ref[...]).astype(BF16)
    gate = jax.nn.sigmoid(jnp.dot(h, wg_ref[...], preferred_element_type=F32))
    emb = jnp.dot(p_ref[...].astype(BF16), wp_ref[...], preferred_element_type=F32)
    y = x + gate * emb
    o_ref[...] = _rms(y, gf_ref[...]) if final_norm else y


def _ple(x2, p2, g, g_final, w_gate, w_proj, tm, final_norm):
    m = x2.shape[0]
    return pl.pallas_call(
        functools.partial(_ple_kernel, final_norm=final_norm),
        grid=(m // tm,),
        in_specs=[
            pl.BlockSpec((tm, D_MODEL), lambda i: (i, 0)),
            pl.BlockSpec((tm, D_PLE), lambda i: (i, 0)),
            pl.BlockSpec((1, D_MODEL), lambda i: (0, 0)),
            pl.BlockSpec((1, D_MODEL), lambda i: (0, 0)),
            _resident((D_MODEL, D_MODEL)),
            _resident((D_PLE, D_MODEL)),
        ],
        out_specs=pl.BlockSpec((tm, D_MODEL), lambda i: (i, 0)),
        out_shape=jax.ShapeDtypeStruct((m, D_MODEL), F32),
        compiler_params=_cparams(("arbitrary",)),
        name="ple",
    )(x2, p2, g, g_final, w_gate, w_proj)


def _rope_lane_freqs():
    def inv(rot):
        half = rot // 2
        return ROPE_THETA ** (-jnp.arange(half, dtype=F32) / half)
    span = 128 // TRIG_PACK
    used = ROT_A // 2 + ROT_IDX // 2
    assert used <= span
    one = jnp.concatenate([inv(ROT_A), inv(ROT_IDX), jnp.zeros((span - used,), F32)])
    return jnp.tile(one, TRIG_PACK)[None, :]


def _w_in_views(w):
    w_t = jnp.swapaxes(w, 0, 1)
    rows = [w_t[_SPLIT_EDGES[k]:_SPLIT_EDGES[k + 1]] for k in (4, 5, 9)]
    pad = jnp.zeros((128 - D_IDX - H_IDX - HB, w_t.shape[1]), w_t.dtype)
    return w_t, jnp.concatenate(rows + [pad], axis=0)


def kernel(x, p, positions, g_mix, w_in, b_f, w_o_a, w_o_b, w_out, g_ffn, w_ffn_gate, w_ffn_up,
           w_ffn_down, g_ple, w_ple_gate, w_ple_proj, g_final):
    b, t_len, d = x.shape
    depth = w_in.shape[0]
    m = b * t_len
    top_k = min(TOPK_MAX, t_len // 4)
    inv_freq = _rope_lane_freqs()
    pos2 = positions.reshape(m, 1)
    x2 = x.reshape(m, d)
    tm_in = min(1024, m)
    tm = min(512, m)
    for i in range(depth):
        w_t, w_small = _w_in_views(w_in[i])
        bf_row = jnp.zeros((1, 128), F32).at[0, LANE_FB:LANE_FB + HB].set(b_f[i].astype(F32))
        proj, kie, kio, wlf, vt_a, vt_b = _inproj(x2, pos2, g_mix[i][None, :], w_t, w_small, bf_row,
                                                  inv_freq, min(tm_in, t_len), b, t_len)
        cb = _cumsum(wlf, b, t_len)
        out_a = _attn_a(proj, vt_a, kie, kio, wlf, b, t_len, top_k)
        out_b = _attn_b(proj, vt_b, cb, b, t_len)
        x2 = _merge(x2, out_a, out_b, proj, w_o_a[i].astype(BF16), w_o_b[i].astype(BF16),
                    w_out[i].astype(BF16), tm)
        x2 = _ffn(x2, g_ffn[i][None, :], w_ffn_gate[i], w_ffn_up[i], w_ffn_down[i], tm_in, 256)
        x2 = _ple(x2, p[i].reshape(m, D_PLE), g_ple[i][None, :], g_final[None, :], w_ple_gate[i].astype(BF16),
                  w_ple_proj[i].astype(BF16), tm_in, final_norm=(i + 1 == depth))
    return x2.reshape(b, t_len, d)
```

```python
import functools

import jax
import jax.numpy as jnp
import numpy as np
from jax import lax
from jax.experimental import pallas as pl
from jax.experimental.pallas import tpu as pltpu

F32 = jnp.float32
BF16 = jnp.bfloat16

D_MODEL = 2048
HEAD_DIM = 128
HA = 8
HA_KV = 2
H_IDX = 16
D_IDX = 64
TOPK_MAX = 256
HB = 8
ROPE_THETA = 500000.0
ROT_A = HEAD_DIM // 4
ROT_IDX = D_IDX // 4
Q_BLOCK = 128
D_FF = -(-8 * D_MODEL // (3 * 256)) * 256
D_PLE = 256
EPS = 1e-6

SPLIT_SIZES = (HA * HEAD_DIM, HA_KV * HEAD_DIM, HA_KV * HEAD_DIM, H_IDX * D_IDX, D_IDX, H_IDX,
               HB * HEAD_DIM, HB * HEAD_DIM, HB * HEAD_DIM, HB, D_MODEL, D_MODEL)

OFF_GA = 0
OFF_GB = OFF_GA + D_MODEL
OFF_QA = OFF_GB + D_MODEL
OFF_KA = OFF_QA + HA * HEAD_DIM
OFF_VA = OFF_KA + HA_KV * HEAD_DIM
OFF_QI = OFF_VA + HA_KV * HEAD_DIM
TN_IN = 1024
MXU_N = 256
OFF_QB = -(-(OFF_QI + H_IDX * D_IDX) // TN_IN) * TN_IN
OFF_KB = OFF_QB + HB * HEAD_DIM
OFF_VB = OFF_KB + HB * HEAD_DIM
N_MAIN = OFF_VB + HB * HEAD_DIM
_SPLIT_EDGES = [0] + [int(s) for s in np.cumsum(SPLIT_SIZES)]
_PACK_RUNS = ((OFF_GA, _SPLIT_EDGES[10], 2 * D_MODEL),
              (OFF_QA, _SPLIT_EDGES[0], OFF_QI + H_IDX * D_IDX - OFF_QA),
              (OFF_QB, _SPLIT_EDGES[6], N_MAIN - OFF_QB))
_SEGMENTS = ((OFF_GA, OFF_QA, "gate"), (OFF_QA, OFF_KA, "q_a"), (OFF_KA, OFF_VA, "k_a"),
             (OFF_VA, OFF_QI, "v_a"), (OFF_QI, OFF_QI + H_IDX * D_IDX, "q_i"),
             (OFF_QB, OFF_KB, "q_b"), (OFF_KB, OFF_VB, "k_b"), (OFF_VB, N_MAIN, "v_b"))


def _tile_patterns():
    def group(col):
        for lo, hi, kind in _SEGMENTS:
            if lo <= col < hi:
                assert (col - lo) % MXU_N == 0 and col + MXU_N <= hi
                return kind, (col - lo) // MXU_N
        return None
    by_pattern = {}
    for tile in range(N_MAIN // TN_IN):
        pattern = tuple(group(tile * TN_IN + g * MXU_N) for g in range(TN_IN // MXU_N))
        pattern = tuple(None if grp is None else (grp[0], grp[1] if grp[0] in ("v_a", "v_b") else 0)
                        for grp in pattern)
        by_pattern.setdefault(pattern, []).append(tile)
    return tuple(by_pattern.items())


_TILE_PATTERNS = _tile_patterns()
LANE_KI = 0
LANE_WI = D_IDX
LANE_FB = D_IDX + H_IDX

NEG = -1e30
VMEM_LIMIT = 60 * 1024 * 1024
LOG2E = 1.4426950408889634
ATTN_SCALE = HEAD_DIM ** -0.5 * LOG2E
VT_ROWS = HEAD_DIM + 16
TRIG_PACK = 4
HEADS_PER_STEP_B = 4
BISECT_FIRST = 16
BISECT_ROUND = 2
BISECT_MAX_ITERS = 48


def _cparams(sem):
    return pltpu.CompilerParams(dimension_semantics=sem, vmem_limit_bytes=VMEM_LIMIT)


def _rms(x, g):
    return x * lax.rsqrt(jnp.mean(x * x, axis=-1, keepdims=True) + EPS) * g


def _resident(shape):
    nd = len(shape)
    return pl.BlockSpec(shape, lambda *_: (0,) * nd, pipeline_mode=pl.Buffered(1))


def _rope128(a, c, sa, sb, shift):
    outs = []
    for q in range(a.shape[1] // 128):
        aq = a[:, q * 128:(q + 1) * 128]
        outs.append(aq * c + pltpu.roll(aq, 128 - shift, 1) * sa + pltpu.roll(aq, shift, 1) * sb)
    return outs[0] if len(outs) == 1 else jnp.concatenate(outs, axis=1)


def _inproj_kernel(x_ref, pos_ref, g_ref, w_ref, ws_ref, bf_ref, inv_ref,
                   o_ref, kie_ref, kio_ref, wlf_ref, vta_ref, vtb_ref,
                   h_scr, ca_scr, saa_scr, sab_scr, ci_scr, sia_scr, sib_scr):
    j = pl.program_id(1)
    tm = x_ref.shape[0]

    @pl.when(j == 0)
    def _():
        hb = _rms(x_ref[...], g_ref[...]).astype(BF16)
        h_scr[...] = hb
        pos = pos_ref[...].astype(F32)
        lane = lax.broadcasted_iota(jnp.int32, (tm, 128), 1)
        ha, hi = ROT_A // 2, ROT_IDX // 2
        span, rq = 128 // TRIG_PACK, tm // TRIG_PACK
        lane_q = lax.broadcasted_iota(jnp.int32, (rq, 128), 1)
        pos_p = jnp.zeros((rq, 128), F32)
        for q in range(TRIG_PACK):
            pos_p = jnp.where((lane_q >= q * span) & (lane_q < (q + 1) * span), pos[q * rq:(q + 1) * rq, :], pos_p)
        ang = pos_p * inv_ref[...]
        cs_p = jnp.cos(ang)
        sn_p = jnp.sin(ang)
        cs = jnp.concatenate([pltpu.roll(cs_p, (128 - q * span) % 128, 1) for q in range(TRIG_PACK)], axis=0)
        sn = jnp.concatenate([pltpu.roll(sn_p, (128 - q * span) % 128, 1) for q in range(TRIG_PACK)], axis=0)

        def at(tab, src, dst):
            return pltpu.roll(tab, (dst - src) % 128, 1)

        def lanes(lo, n):
            return (lane >= lo) & (lane < lo + n)

        ca_scr[...] = jnp.where(lanes(0, ha), cs, jnp.where(lanes(ha, ha), at(cs, 0, ha), 1.0))
        saa_scr[...] = jnp.where(lanes(0, ha), -sn, 0.0)
        sab_scr[...] = jnp.where(lanes(ha, ha), at(sn, 0, ha), 0.0)
        ci = jnp.ones((tm, 128), F32)
        sia = jnp.zeros((tm, 128), F32)
        sib = jnp.zeros((tm, 128), F32)
        for head0 in (0, D_IDX):
            ci = jnp.where(lanes(head0, hi), at(cs, ha, head0), ci)
            ci = jnp.where(lanes(head0 + hi, hi), at(cs, ha, head0 + hi), ci)
            sia = jnp.where(lanes(head0, hi), -at(sn, ha, head0), sia)
            sib = jnp.where(lanes(head0 + hi, hi), at(sn, ha, head0 + hi), sib)
        ci_scr[...] = ci
        sia_scr[...] = sia
        sib_scr[...] = sib
        small = lax.dot_general(hb, ws_ref[...].astype(BF16), (((1,), (1,)), ((), ())),
                                preferred_element_type=F32)
        kr = _rope128(small, ci, sia, sib, ROT_IDX // 2)
        ke = jnp.where(lane < D_IDX, kr, 0.0)
        kie_ref[...] = ke.astype(BF16)
        kio_ref[...] = pltpu.roll(ke, D_IDX, 1).astype(BF16)
        is_w = (lane >= LANE_WI) & (lane < LANE_WI + H_IDX)
        is_f = (lane >= LANE_FB) & (lane < LANE_FB + HB)
        logf = jax.nn.log_sigmoid(small + bf_ref[...])
        wlf_ref[...] = jnp.where(is_w, small * (H_IDX ** -0.5 * D_IDX ** -0.5), jnp.where(is_f, logf, 0.0))

    t = TN_IN
    groups = t // MXU_N
    heads_per_group = MXU_N // HEAD_DIM

    def project(epilogues):
        for c, epilogue in enumerate(epilogues):
            cols = slice(c * MXU_N, (c + 1) * MXU_N)
            if epilogue is None:
                o_ref[:, cols] = jnp.zeros((tm, MXU_N), BF16)
                continue
            w_t = w_ref[cols, :].astype(BF16)
            acc = lax.dot_general(h_scr[...], w_t, (((1,), (1,)), ((), ())), preferred_element_type=F32)
            o_ref[:, cols] = epilogue(acc).astype(BF16)

    def rope_a(acc):
        return _rope128(acc, ca_scr[...], saa_scr[...], sab_scr[...], ROT_A // 2)

    def rope_i(acc):
        return _rope128(acc, ci_scr[...], sia_scr[...], sib_scr[...], ROT_IDX // 2)

    def values(vt_ref, c):
        def epilogue(acc):
            acc_t = acc.T
            for hh in range(heads_per_group):
                h = (c * heads_per_group + hh) % vt_ref.shape[0]
                vt_ref[h, 0:HEAD_DIM, :] = acc_t[hh * HEAD_DIM:(hh + 1) * HEAD_DIM, :].astype(BF16)
                vt_ref[h, HEAD_DIM:VT_ROWS, :] = jnp.ones((VT_ROWS - HEAD_DIM, tm), BF16)
            return acc
        return epilogue

    def epilogue_of(kind, c):
        return {"gate": lambda: jax.nn.sigmoid,
                "q_a": lambda: (lambda acc: rope_a(acc) * ATTN_SCALE),
                "k_a": lambda: rope_a,
                "v_a": lambda: values(vta_ref, c),
                "q_i": lambda: rope_i,
                "q_b": lambda: (lambda acc: acc * ATTN_SCALE),
                "k_b": lambda: (lambda acc: acc),
                "v_b": lambda: values(vtb_ref, c)}[kind]()

    for pattern, tiles in _TILE_PATTERNS:
        pred = functools.reduce(lambda a, b: a | b, [j == tile for tile in tiles])

        @pl.when(pred)
        def _(pattern=pattern):
            project([None if grp is None else epilogue_of(*grp) for grp in pattern])


def _w_in_row(j):
    row = jnp.int32(0)
    for out0, src0, _ in _PACK_RUNS:
        row = jnp.where(j >= out0 // TN_IN, src0 + TN_IN * (j - out0 // TN_IN), row)
    return pl.multiple_of(row, 8)


def _inproj(x2, pos2, g, w_t, w_small, bf_row, inv_freq, tm, b, t_len):
    m = x2.shape[0]
    for out0, src0, width in _PACK_RUNS:
        assert out0 % TN_IN == 0 and src0 % 8 == 0
        assert src0 + -(-width // TN_IN) * TN_IN <= w_t.shape[0]
    assert list(_PACK_RUNS) == sorted(_PACK_RUNS)
    grid = (m // tm, N_MAIN // TN_IN)
    nt = t_len // tm
    j_vb = OFF_VB // TN_IN
    heads_per_tile = min(TN_IN // HEAD_DIM, HB)
    vb_tiles = HB // heads_per_tile
    row128 = pl.BlockSpec((1, 128), lambda i, j: (0, 0))
    return pl.pallas_call(
        _inproj_kernel,
        grid=grid,
        in_specs=[
            pl.BlockSpec((tm, D_MODEL), lambda i, j: (i, 0)),
            pl.BlockSpec((tm, 1), lambda i, j: (i, 0)),
            pl.BlockSpec((1, D_MODEL), lambda i, j: (0, 0)),
            pl.BlockSpec((pl.Element(TN_IN), pl.Element(D_MODEL)), lambda i, j: (_w_in_row(j), 0)),
            pl.BlockSpec((128, D_MODEL), lambda i, j: (0, 0)),
            row128, row128,
        ],
        out_specs=[
            pl.BlockSpec((tm, TN_IN), lambda i, j: (i, j)),
            pl.BlockSpec((tm, 128), lambda i, j: (i, 0)),
            pl.BlockSpec((tm, 128), lambda i, j: (i, 0)),
            pl.BlockSpec((tm, 128), lambda i, j: (i, 0)),
            pl.BlockSpec((None, HA_KV, VT_ROWS, tm), lambda i, j: (i // nt, 0, 0, i % nt)),
            pl.BlockSpec((None, heads_per_tile, VT_ROWS, tm),
                         lambda i, j: (i // nt, jnp.clip(j - j_vb, 0, vb_tiles - 1), 0, i % nt)),
        ],
        out_shape=[
            jax.ShapeDtypeStruct((m, N_MAIN), BF16),
            jax.ShapeDtypeStruct((m, 128), BF16),
            jax.ShapeDtypeStruct((m, 128), BF16),
            jax.ShapeDtypeStruct((m, 128), F32),
            jax.ShapeDtypeStruct((b, HA_KV, VT_ROWS, t_len), BF16),
            jax.ShapeDtypeStruct((b, HB, VT_ROWS, t_len), BF16),
        ],
        scratch_shapes=[pltpu.VMEM((tm, D_MODEL), BF16)] + [pltpu.VMEM((tm, 128), F32)] * 6,
        compiler_params=_cparams(("arbitrary", "arbitrary")),
        name="inproj",
    )(x2, pos2, g, w_t, w_small, bf_row, inv_freq)


def _cumsum_kernel(wlf_ref, o_ref, carry_scr, *, tc):
    @pl.when(pl.program_id(1) == 0)
    def _():
        carry_scr[...] = jnp.zeros_like(carry_scr)

    lf = wlf_ref[...]
    r = lax.broadcasted_iota(jnp.int32, (tc, tc), 0)
    c = lax.broadcasted_iota(jnp.int32, (tc, tc), 1)
    tri = jnp.where(c <= r, 1.0, 0.0).astype(BF16)
    hi = lf.astype(BF16)
    r1 = lf - hi.astype(F32)
    mid = r1.astype(BF16)
    lo = (r1 - mid.astype(F32)).astype(BF16)
    cs = (jnp.dot(tri, hi, preferred_element_type=F32) + jnp.dot(tri, mid, preferred_element_type=F32)
          + jnp.dot(tri, lo, preferred_element_type=F32)) + carry_scr[0:1, :]
    carry_scr[...] = jnp.broadcast_to(cs[tc - 1:tc, :], carry_scr.shape)
    cs2 = cs * LOG2E
    for grp in range(HB // HEADS_PER_STEP_B):
        o_ref[grp] = pltpu.roll(cs2, (128 - LANE_FB - grp * HEADS_PER_STEP_B) % 128, 1)


def _cumsum(wlf, b, t_len):
    tc = 256
    nt = t_len // tc
    return pl.pallas_call(
        functools.partial(_cumsum_kernel, tc=tc),
        grid=(b, nt),
        in_specs=[pl.BlockSpec((tc, 128), lambda bi, ti: (bi * nt + ti, 0))],
        out_specs=pl.BlockSpec((None, HB // HEADS_PER_STEP_B, tc, 128), lambda bi, ti: (bi, 0, ti, 0)),
        out_shape=jax.ShapeDtypeStruct((b, HB // HEADS_PER_STEP_B, t_len, 128), F32),
        scratch_shapes=[pltpu.VMEM((8, 128), F32)],
        compiler_params=_cparams(("arbitrary", "arbitrary")),
        name="fcumsum",
    )(wlf)


def _attn_a_kernel(qa_ref, qi0_ref, qi1_ref, ka_ref, vt_ref, kie_ref, kio_ref, wlf_ref, o_ref,
                   sc_scr, qs_scr, qg_scr, lt_scr, acc_scr, *, tk, top_k, n_iter):
    i = pl.program_id(1)
    nq = Q_BLOCK
    group = HA // HA_KV
    gw = group * nq
    nchunks = ((i + 1) * nq + tk - 1) // tk
    q_pos = i * nq + lax.broadcasted_iota(jnp.int32, (1, nq), 1)
    wt = wlf_ref[...].T
    half = H_IDX // 4
    for p in range(H_IDX // 2):
        src = qi0_ref if p < half else qi1_ref
        qs_scr[p * nq:(p + 1) * nq, :] = src[:, (p % half) * 128:(p % half + 1) * 128]

    def score_chunk(c, carry):
        rmin, rmax = carry
        off = pl.multiple_of(c * tk, tk)
        ke = kie_ref[pl.ds(off, tk), :]
        ko = kio_ref[pl.ds(off, tk), :]
        s = jnp.zeros((tk, nq), F32)
        pairs_per_dot = MXU_N // nq
        for pg in range(H_IDX // 2 // pairs_per_dot):
            qs = qs_scr[pg * MXU_N:(pg + 1) * MXU_N, :]
            de = lax.dot_general(ke, qs, (((1,), (1,)), ((), ())), preferred_element_type=F32)
            do = lax.dot_general(ko, qs, (((1,), (1,)), ((), ())), preferred_element_type=F32)
            for pp in range(pairs_per_dot):
                p = pg * pairs_per_dot + pp
                cols = slice(pp * nq, (pp + 1) * nq)
                s = s + jnp.maximum(de[:, cols], 0.0) * wt[LANE_WI + 2 * p:LANE_WI + 2 * p + 1, :]
                s = s + jnp.maximum(do[:, cols], 0.0) * wt[LANE_WI + 2 * p + 1:LANE_WI + 2 * p + 2, :]
        kpos = off + lax.broadcasted_iota(jnp.int32, (tk, 1), 0)
        valid = kpos <= q_pos
        sc_scr[pl.ds(off, tk), :] = jnp.where(valid, s, NEG)
        rmin = jnp.minimum(rmin, jnp.min(jnp.where(valid, s, -NEG), axis=0, keepdims=True))
        rmax = jnp.maximum(rmax, jnp.max(jnp.where(valid, s, NEG), axis=0, keepdims=True))
        return rmin, rmax

    lo, hi = lax.fori_loop(0, nchunks, score_chunk,
                           (jnp.full((1, nq), -NEG, F32), jnp.full((1, nq), NEG, F32)))

    def bisect(_, carry):
        lo, hi = carry
        mid = 0.5 * (lo + hi)

        def count_chunk(c, part):
            off = pl.multiple_of(c * tk, tk)
            ind = jnp.where(sc_scr[pl.ds(off, tk), :] >= mid, 1.0, 0.0)
            return part + jnp.sum(ind.reshape(tk // 64, 64, nq), axis=0)

        part = lax.fori_loop(0, nchunks, count_chunk, jnp.zeros((64, nq), F32))
        cnt = jnp.sum(part, axis=0, keepdims=True)
        ge = cnt >= top_k
        return jnp.where(ge, mid, lo), jnp.where(cnt == top_k, mid, jnp.where(ge, hi, mid))

    searching = q_pos >= top_k

    def bisect_round(state):
        it, lo, hi, _ = state
        lo, hi = lax.fori_loop(0, BISECT_ROUND, bisect, (lo, hi))
        pending = jnp.sum(jnp.where(searching & (lo < hi), 1.0, 0.0))
        return it + BISECT_ROUND, lo, hi, pending

    lo, hi = lax.fori_loop(0, BISECT_FIRST, bisect, (lo, hi))
    _, thr, _, _ = lax.while_loop(lambda st: (st[0] < n_iter) & (st[3] > 0.0), bisect_round,
                                  (jnp.int32(BISECT_FIRST), lo, hi, jnp.float32(1.0)))

    tka = tk // 2
    last = (nchunks - 1) * tk
    for n in range(HA_KV):
        for g in range(group):
            h = n * group + g
            qg_scr[n, g * nq:(g + 1) * nq, :] = qa_ref[:, h * HEAD_DIM:(h + 1) * HEAD_DIM]
    acc_scr[...] = jnp.zeros(acc_scr.shape, F32)

    def qk(off, slot):
        msk = jnp.where(sc_scr[pl.ds(off, tka), :] >= thr, 0.0, NEG)
        msk = jnp.concatenate([msk] * group, axis=1)
        cmax = []
        for n in range(HA_KV):
            k = ka_ref[pl.ds(off, tka), n * HEAD_DIM:(n + 1) * HEAD_DIM]
            s = lax.dot_general(k, qg_scr[n], (((1,), (1,)), ((), ())), preferred_element_type=F32) + msk
            lt_scr[n, slot] = s
            cmax.append(jnp.max(s, axis=0, keepdims=True))
        return tuple(cmax)

    def softmax_pv(off, slot, m_old, cmax):
        new = []
        for n in range(HA_KV):
            m_new = jnp.maximum(m_old[n], cmax[n])
            alpha = jnp.exp2(m_old[n] - m_new)
            p = jnp.exp2(lt_scr[n, slot] - m_new).astype(BF16)
            acc_scr[n] = alpha * acc_scr[n] + jnp.dot(vt_ref[n, :, pl.ds(off, tka)], p,
                                                      preferred_element_type=F32)
            new.append(m_new)
        return tuple(new)

    def pair(j, carry):
        m, cmax0 = carry
        off0 = pl.multiple_of(j * tk, tk)
        off1 = pl.multiple_of(off0 + tka, tka)
        off2 = pl.multiple_of(jnp.minimum(off0 + tk, last), tk)
        cmax1 = qk(off1, 1)
        m = softmax_pv(off0, 0, m, cmax0)
        cmax0 = qk(off2, 0)
        return softmax_pv(off1, 1, m, cmax1), cmax0

    lax.fori_loop(0, nchunks, pair, ((jnp.full((1, gw), NEG, F32),) * HA_KV, qk(0, 0)))
    for n in range(HA_KV):
        out_t = acc_scr[n, 0:HEAD_DIM, :] / acc_scr[n, HEAD_DIM:HEAD_DIM + 1, :]
        for g in range(group):
            h = n * group + g
            o_ref[:, h * HEAD_DIM:(h + 1) * HEAD_DIM] = out_t[:, g * nq:(g + 1) * nq].T.astype(BF16)


def _attn_a(proj, vt_a, kie, kio, wlf, b, t_len, top_k):
    nb = t_len // Q_BLOCK
    wa = HA * HEAD_DIM
    wkv = HA_KV * HEAD_DIM
    wqi = H_IDX * D_IDX // 2
    tk = min(512, t_len)
    gw = HA // HA_KV * Q_BLOCK
    kern = functools.partial(_attn_a_kernel, tk=tk, top_k=top_k, n_iter=BISECT_MAX_ITERS)
    return pl.pallas_call(
        kern,
        grid=(b, nb),
        in_specs=[
            pl.BlockSpec((Q_BLOCK, wa), lambda bi, i: (bi * nb + i, OFF_QA // wa)),
            pl.BlockSpec((Q_BLOCK, wqi), lambda bi, i: (bi * nb + i, OFF_QI // wqi)),
            pl.BlockSpec((Q_BLOCK, wqi), lambda bi, i: (bi * nb + i, OFF_QI // wqi + 1)),
            pl.BlockSpec((t_len, wkv), lambda bi, i: (bi, OFF_KA // wkv)),
            pl.BlockSpec((None, HA_KV, VT_ROWS, t_len), lambda bi, i: (bi, 0, 0, 0)),
            pl.BlockSpec((t_len, 128), lambda bi, i: (bi, 0)),
            pl.BlockSpec((t_len, 128), lambda bi, i: (bi, 0)),
            pl.BlockSpec((Q_BLOCK, 128), lambda bi, i: (bi * nb + i, 0)),
        ],
        out_specs=pl.BlockSpec((Q_BLOCK, wa), lambda bi, i: (bi * nb + i, 0)),
        out_shape=jax.ShapeDtypeStruct((b * t_len, wa), BF16),
        scratch_shapes=[
            pltpu.VMEM((t_len, Q_BLOCK), F32),
            pltpu.VMEM((H_IDX // 2 * Q_BLOCK, 128), BF16),
            pltpu.VMEM((HA_KV, gw, HEAD_DIM), BF16),
            pltpu.VMEM((HA_KV, 2, tk // 2, gw), F32),
            pltpu.VMEM((HA_KV, VT_ROWS, gw), F32),
        ],
        compiler_params=_cparams(("arbitrary", "arbitrary")),
        name="attn_a",
    )(proj, proj, proj, proj, vt_a, kie, kio, wlf)


def _attn_b_kernel(q_ref, k_ref, vt_ref, cb_ref, o_ref, lt_scr, acc_scr, *, tq):
    qi = pl.program_id(2)
    tk = tq // 2
    heads = HEADS_PER_STEP_B
    q0 = pl.multiple_of(qi * tq, tq)
    acc_scr[...] = jnp.zeros(acc_scr.shape, F32)

    def qk(off, slot, key0=None):
        cmax = []
        for n in range(heads):
            cols = slice(n * HEAD_DIM, (n + 1) * HEAD_DIM)
            bias = (cb_ref[pl.ds(q0, 1), :] - cb_ref[pl.ds(off, tk), :])[:, n:n + 1]
            s = lax.dot_general(k_ref[pl.ds(off, tk), cols], q_ref[:, cols], (((1,), (1,)), ((), ())),
                                preferred_element_type=F32) + bias
            if key0 is not None:
                key = key0 + lax.broadcasted_iota(jnp.int32, (tk, 1), 0)
                qry = lax.broadcasted_iota(jnp.int32, (1, tq), 1)
                s = jnp.where(key <= qry, s, NEG)
            lt_scr[n, slot] = s
            cmax.append(jnp.max(s, axis=0, keepdims=True))
        return tuple(cmax)

    def softmax_pv(off, slot, m_old, cmax):
        new = []
        for n in range(heads):
            m_new = jnp.maximum(m_old[n], cmax[n])
            alpha = jnp.exp2(m_old[n] - m_new)
            p = jnp.exp2(lt_scr[n, slot] - m_new).astype(BF16)
            acc_scr[n] = alpha * acc_scr[n] + jnp.dot(vt_ref[n, :, pl.ds(off, tk)], p,
                                                      preferred_element_type=F32)
            new.append(m_new)
        return tuple(new)

    q1 = pl.multiple_of(q0 + tk, tk)
    m = (jnp.full((1, tq), NEG, F32),) * heads
    cmax0 = qk(q0, 0, key0=0)
    cmax1 = qk(q1, 1, key0=tk)
    m = softmax_pv(q0, 0, m, cmax0)
    cmax0 = qk(0, 0)
    m = softmax_pv(q1, 1, m, cmax1)
    last = jnp.maximum(qi - 1, 0) * tq

    def pair(j, carry):
        m, cmax0 = carry
        off0 = pl.multiple_of(j * tq, tq)
        off1 = pl.multiple_of(off0 + tk, tk)
        off2 = pl.multiple_of(jnp.minimum(off0 + tq, last), tq)
        cmax1 = qk(off1, 1)
        m = softmax_pv(off0, 0, m, cmax0)
        cmax0 = qk(off2, 0)
        return softmax_pv(off1, 1, m, cmax1), cmax0

    lax.fori_loop(0, qi, pair, (m, cmax0))
    for n in range(heads):
        out_t = acc_scr[n, 0:HEAD_DIM, :] / acc_scr[n, HEAD_DIM:HEAD_DIM + 1, :]
        for g in range(tq // 128):
            o_ref[g * 128:(g + 1) * 128, n * HEAD_DIM:(n + 1) * HEAD_DIM] = (
                out_t[:, g * 128:(g + 1) * 128].T.astype(BF16))


def _attn_b(proj, vt_b, cb, b, t_len):
    tq = 512
    nq = t_len // tq
    hps = HEADS_PER_STEP_B
    wh = hps * HEAD_DIM
    return pl.pallas_call(
        functools.partial(_attn_b_kernel, tq=tq),
        grid=(b, HB // hps, nq),
        in_specs=[
            pl.BlockSpec((tq, wh), lambda bi, h, i: (bi * nq + i, OFF_QB // wh + h)),
            pl.BlockSpec((t_len, wh), lambda bi, h, i: (bi, OFF_KB // wh + h)),
            pl.BlockSpec((None, hps, VT_ROWS, t_len), lambda bi, h, i: (bi, h, 0, 0)),
            pl.BlockSpec((None, None, t_len, 128), lambda bi, h, i: (bi, h, 0, 0)),
        ],
        out_specs=pl.BlockSpec((tq, wh), lambda bi, h, i: (bi * nq + i, h)),
        out_shape=jax.ShapeDtypeStruct((b * t_len, HB * HEAD_DIM), BF16),
        scratch_shapes=[pltpu.VMEM((hps, 2, tq // 2, tq), F32), pltpu.VMEM((hps, VT_ROWS, tq), F32)],
        compiler_params=_cparams(("arbitrary", "arbitrary", "arbitrary")),
        name="attn_b",
    )(proj, proj, vt_b, cb)


def _merge_kernel(x_ref, oa_ref, ob_ref, ga_ref, gb_ref, woa_ref, wob_ref, wout_ref, gn_ref, o_ref, h_ref):
    ya = jnp.dot(oa_ref[...], woa_ref[...], preferred_element_type=F32)
    yb = jnp.dot(ob_ref[...], wob_ref[...], preferred_element_type=F32)
    mixed = ga_ref[...].astype(F32) * ya + gb_ref[...].astype(F32) * yb
    x1 = x_ref[...] + jnp.dot(mixed.astype(BF16), wout_ref[...], preferred_element_type=F32)
    o_ref[...] = x1
    h_ref[...] = _rms(x1, gn_ref[...]).astype(BF16)


def _merge(x2, out_a, out_b, proj, w_oa, w_ob, w_out, g_next, tm):
    m = x2.shape[0]
    wa = HA * HEAD_DIM
    wb = HB * HEAD_DIM
    return pl.pallas_call(
        _merge_kernel,
        grid=(m // tm,),
        in_specs=[
            pl.BlockSpec((tm, D_MODEL), lambda i: (i, 0)),
            pl.BlockSpec((tm, wa), lambda i: (i, 0)),
            pl.BlockSpec((tm, wb), lambda i: (i, 0)),
            pl.BlockSpec((tm, D_MODEL), lambda i: (i, OFF_GA // D_MODEL)),
            pl.BlockSpec((tm, D_MODEL), lambda i: (i, OFF_GB // D_MODEL)),
            _resident((wa, D_MODEL)),
            _resident((wb, D_MODEL)),
            _resident((D_MODEL, D_MODEL)),
            pl.BlockSpec((1, D_MODEL), lambda i: (0, 0)),
        ],
        out_specs=[pl.BlockSpec((tm, D_MODEL), lambda i: (i, 0)), pl.BlockSpec((tm, D_MODEL), lambda i: (i, 0))],
        out_shape=[jax.ShapeDtypeStruct((m, D_MODEL), F32), jax.ShapeDtypeStruct((m, D_MODEL), BF16)],
        compiler_params=_cparams(("arbitrary",)),
        name="merge",
    )(x2, out_a, out_b, proj, proj, w_oa, w_ob, w_out, g_next)


def _ffn_kernel(h_ref, wg_ref, wu_ref, wd_ref, o_ref):
    @pl.when(pl.program_id(1) == 0)
    def _():
        o_ref[...] = jnp.zeros(o_ref.shape, F32)

    h = h_ref[...]
    gate = jnp.dot(h, wg_ref[...].astype(BF16), preferred_element_type=F32)
    up = jnp.dot(h, wu_ref[...].astype(BF16), preferred_element_type=F32)
    act = (jax.nn.silu(gate) * up).astype(BF16)
    o_ref[...] += jnp.dot(act, wd_ref[...].astype(BF16), preferred_element_type=F32)


def _ffn(h, w_gate, w_up, w_down, tm, tf):
    m = h.shape[0]
    return pl.pallas_call(
        _ffn_kernel,
        grid=(m // tm, D_FF // tf),
        in_specs=[
            pl.BlockSpec((tm, D_MODEL), lambda i, f: (i, 0)),
            pl.BlockSpec((D_MODEL, tf), lambda i, f: (0, f)),
            pl.BlockSpec((D_MODEL, tf), lambda i, f: (0, f)),
            pl.BlockSpec((tf, D_MODEL), lambda i, f: (f, 0)),
        ],
        out_specs=pl.BlockSpec((tm, D_MODEL), lambda i, f: (i, 0)),
        out_shape=jax.ShapeDtypeStruct((m, D_MODEL), F32),
        compiler_params=_cparams(("arbitrary", "arbitrary")),
        name="ffn",
    )(h, w_gate, w_up, w_down)


def _ple_kernel(x_ref, y_ref, p_ref, g_ref, gf_ref, wg_ref, wp_ref, o_ref, *, final_norm):
    x = x_ref[...] + y_ref[...]
    h = _rms(x, g_ref[...]).astype(BF16)
    gate = jax.nn.sigmoid(jnp.dot(h, wg_ref[...], preferred_element_type=F32))
    emb = jnp.dot(p_ref[...].astype(BF16), wp_ref[...], preferred_element_type=F32)
    y = x + gate * emb
    o_ref[...] = _rms(y, gf_ref[...]) if final_norm else y


def _ple(x2, y2, p2, g, g_final, w_gate, w_proj, tm, final_norm):
    m = x2.shape[0]
    return pl.pallas_call(
        functools.partial(_ple_kernel, final_norm=final_norm),
        grid=(m // tm,),
        in_specs=[
            pl.BlockSpec((tm, D_MODEL), lambda i: (i, 0)),
            pl.BlockSpec((tm, D_MODEL), lambda i: (i, 0)),
            pl.BlockSpec((tm, D_PLE), lambda i: (i, 0)),
            pl.BlockSpec((1, D_MODEL), lambda i: (0, 0)),
            pl.BlockSpec((1, D_MODEL), lambda i: (0, 0)),
            _resident((D_MODEL, D_MODEL)),
            _resident((D_PLE, D_MODEL)),
        ],
        out_specs=pl.BlockSpec((tm, D_MODEL), lambda i: (i, 0)),
        out_shape=jax.ShapeDtypeStruct((m, D_MODEL), F32),
        compiler_params=_cparams(("arbitrary",)),
        name="ple",
    )(x2, y2, p2, g, g_final, w_gate, w_proj)


def _rope_lane_freqs():
    def inv(rot):
        half = rot // 2
        return ROPE_THETA ** (-jnp.arange(half, dtype=F32) / half)
    span = 128 // TRIG_PACK
    used = ROT_A // 2 + ROT_IDX // 2
    assert used <= span
    one = jnp.concatenate([inv(ROT_A), inv(ROT_IDX), jnp.zeros((span - used,), F32)])
    return jnp.tile(one, TRIG_PACK)[None, :]


def _w_in_views(w):
    w_t = jnp.swapaxes(w, 0, 1)
    rows = [w_t[_SPLIT_EDGES[k]:_SPLIT_EDGES[k + 1]] for k in (4, 5, 9)]
    pad = jnp.zeros((128 - D_IDX - H_IDX - HB, w_t.shape[1]), w_t.dtype)
    return w_t, jnp.concatenate(rows + [pad], axis=0)


def kernel(x, p, positions, g_mix, w_in, b_f, w_o_a, w_o_b, w_out, g_ffn, w_ffn_gate, w_ffn_up,
           w_ffn_down, g_ple, w_ple_gate, w_ple_proj, g_final):
    b, t_len, d = x.shape
    depth = w_in.shape[0]
    m = b * t_len
    top_k = min(TOPK_MAX, t_len // 4)
    inv_freq = _rope_lane_freqs()
    pos2 = positions.reshape(m, 1)
    x2 = x.reshape(m, d)
    tm_in = min(1024, m)
    tm = min(512, m)
    for i in range(depth):
        w_t, w_small = _w_in_views(w_in[i])
        bf_row = jnp.zeros((1, 128), F32).at[0, LANE_FB:LANE_FB + HB].set(b_f[i].astype(F32))
        proj, kie, kio, wlf, vt_a, vt_b = _inproj(x2, pos2, g_mix[i][None, :], w_t, w_small, bf_row,
                                                  inv_freq, min(tm_in, t_len), b, t_len)
        cb = _cumsum(wlf, b, t_len)
        out_a = _attn_a(proj, vt_a, kie, kio, wlf, b, t_len, top_k)
        out_b = _attn_b(proj, vt_b, cb, b, t_len)
        x2, h2 = _merge(x2, out_a, out_b, proj, w_o_a[i].astype(BF16), w_o_b[i].astype(BF16),
                        w_out[i].astype(BF16), g_ffn[i][None, :], tm)
        y2 = _ffn(h2, w_ffn_gate[i], w_ffn_up[i], w_ffn_down[i], tm_in, 512)
        x2 = _ple(x2, y2, p[i].reshape(m, D_PLE), g_ple[i][None, :], g_final[None, :],
                  w_ple_gate[i].astype(BF16), w_ple_proj[i].astype(BF16), tm, final_norm=(i + 1 == depth))
    return x2.reshape(b, t_len, d)
```

```python
import functools

import jax
import jax.numpy as jnp
import numpy as np
from jax import lax
from jax.experimental import pallas as pl
from jax.experimental.pallas import tpu as pltpu

F32 = jnp.float32
BF16 = jnp.bfloat16

D_MODEL = 2048
HEAD_DIM = 128
HA = 8
HA_KV = 2
H_IDX = 16
D_IDX = 64
TOPK_MAX = 256
HB = 8
ROPE_THETA = 500000.0
ROT_A = HEAD_DIM // 4
ROT_IDX = D_IDX // 4
Q_BLOCK = 128
D_FF = -(-8 * D_MODEL // (3 * 256)) * 256
D_PLE = 256
EPS = 1e-6

SPLIT_SIZES = (HA * HEAD_DIM, HA_KV * HEAD_DIM, HA_KV * HEAD_DIM, H_IDX * D_IDX, D_IDX, H_IDX,
               HB * HEAD_DIM, HB * HEAD_DIM, HB * HEAD_DIM, HB, D_MODEL, D_MODEL)

OFF_GA = 0
OFF_GB = OFF_GA + D_MODEL
OFF_QA = OFF_GB + D_MODEL
OFF_KA = OFF_QA + HA * HEAD_DIM
OFF_VA = OFF_KA + HA_KV * HEAD_DIM
OFF_QI = OFF_VA + HA_KV * HEAD_DIM
TN_IN = 1024
MXU_N = 256
OFF_QB = -(-(OFF_QI + H_IDX * D_IDX) // TN_IN) * TN_IN
OFF_KB = OFF_QB + HB * HEAD_DIM
OFF_VB = OFF_KB + HB * HEAD_DIM
N_MAIN = OFF_VB + HB * HEAD_DIM
_SPLIT_EDGES = [0] + [int(s) for s in np.cumsum(SPLIT_SIZES)]
_PACK_RUNS = ((OFF_GA, _SPLIT_EDGES[10], 2 * D_MODEL),
              (OFF_QA, _SPLIT_EDGES[0], OFF_QI + H_IDX * D_IDX - OFF_QA),
              (OFF_QB, _SPLIT_EDGES[6], N_MAIN - OFF_QB))
_SEGMENTS = ((OFF_GA, OFF_QA, "gate"), (OFF_QA, OFF_KA, "q_a"), (OFF_KA, OFF_VA, "k_a"),
             (OFF_VA, OFF_QI, "v_a"), (OFF_QI, OFF_QI + H_IDX * D_IDX, "q_i"),
             (OFF_QB, OFF_KB, "q_b"), (OFF_KB, OFF_VB, "k_b"), (OFF_VB, N_MAIN, "v_b"))


def _tile_patterns():
    def group(col):
        for lo, hi, kind in _SEGMENTS:
            if lo <= col < hi:
                assert (col - lo) % MXU_N == 0 and col + MXU_N <= hi
                return kind, (col - lo) // MXU_N
        return None
    by_pattern = {}
    for tile in range(N_MAIN // TN_IN):
        pattern = tuple(group(tile * TN_IN + g * MXU_N) for g in range(TN_IN // MXU_N))
        pattern = tuple(None if grp is None else (grp[0], grp[1] if grp[0] in ("v_a", "v_b") else 0)
                        for grp in pattern)
        by_pattern.setdefault(pattern, []).append(tile)
    return tuple(by_pattern.items())


_TILE_PATTERNS = _tile_patterns()
LANE_WI = D_IDX
LANE_FB = D_IDX + H_IDX

NEG = -1e30
VMEM_LIMIT = 60 * 1024 * 1024
LOG2E = 1.4426950408889634
ATTN_SCALE = HEAD_DIM ** -0.5 * LOG2E
VT_ROWS = HEAD_DIM + 16
TRIG_PACK = 4
HEADS_PER_STEP_B = 4
KEY_CHUNK_A = 512
Q_BLOCK_B = 512
CUMSUM_ROWS = 256
BISECT_FIRST = 16
BISECT_ROUND = 2
BISECT_MAX_ITERS = 48


def _row_tiles(m, t_len):
    return min(1024, m, t_len), min(512, m)


FFN_COLS = 512


def _cparams(sem):
    return pltpu.CompilerParams(dimension_semantics=sem, vmem_limit_bytes=VMEM_LIMIT)


def _rms(x, g):
    return x * lax.rsqrt(jnp.mean(x * x, axis=-1, keepdims=True) + EPS) * g


def _resident(shape):
    nd = len(shape)
    return pl.BlockSpec(shape, lambda *_: (0,) * nd, pipeline_mode=pl.Buffered(1))


def _rope128(a, c, sa, sb, shift):
    outs = []
    for q in range(a.shape[1] // 128):
        aq = a[:, q * 128:(q + 1) * 128]
        outs.append(aq * c + pltpu.roll(aq, 128 - shift, 1) * sa + pltpu.roll(aq, shift, 1) * sb)
    return outs[0] if len(outs) == 1 else jnp.concatenate(outs, axis=1)


def _inproj_kernel(x_ref, pos_ref, g_ref, w_ref, ws_ref, bf_ref, inv_ref,
                   o_ref, kie_ref, kio_ref, wlf_ref, vta_ref, vtb_ref,
                   h_scr, ca_scr, saa_scr, sab_scr, ci_scr, sia_scr, sib_scr):
    j = pl.program_id(1)
    tm = x_ref.shape[0]

    @pl.when(j == 0)
    def _():
        hb = _rms(x_ref[...], g_ref[...]).astype(BF16)
        h_scr[...] = hb
        pos = pos_ref[...].astype(F32)
        lane = lax.broadcasted_iota(jnp.int32, (tm, 128), 1)
        ha, hi = ROT_A // 2, ROT_IDX // 2
        span, rq = 128 // TRIG_PACK, tm // TRIG_PACK
        lane_q = lax.broadcasted_iota(jnp.int32, (rq, 128), 1)
        pos_p = jnp.zeros((rq, 128), F32)
        for q in range(TRIG_PACK):
            pos_p = jnp.where((lane_q >= q * span) & (lane_q < (q + 1) * span), pos[q * rq:(q + 1) * rq, :], pos_p)
        ang = pos_p * inv_ref[...]
        cs_p = jnp.cos(ang)
        sn_p = jnp.sin(ang)
        cs = jnp.concatenate([pltpu.roll(cs_p, (128 - q * span) % 128, 1) for q in range(TRIG_PACK)], axis=0)
        sn = jnp.concatenate([pltpu.roll(sn_p, (128 - q * span) % 128, 1) for q in range(TRIG_PACK)], axis=0)

        def at(tab, src, dst):
            return pltpu.roll(tab, (dst - src) % 128, 1)

        def lanes(lo, n):
            return (lane >= lo) & (lane < lo + n)

        ca_scr[...] = jnp.where(lanes(0, ha), cs, jnp.where(lanes(ha, ha), at(cs, 0, ha), 1.0))
        saa_scr[...] = jnp.where(lanes(0, ha), -sn, 0.0)
        sab_scr[...] = jnp.where(lanes(ha, ha), at(sn, 0, ha), 0.0)
        ci = jnp.ones((tm, 128), F32)
        sia = jnp.zeros((tm, 128), F32)
        sib = jnp.zeros((tm, 128), F32)
        for head0 in (0, D_IDX):
            ci = jnp.where(lanes(head0, hi), at(cs, ha, head0), ci)
            ci = jnp.where(lanes(head0 + hi, hi), at(cs, ha, head0 + hi), ci)
            sia = jnp.where(lanes(head0, hi), -at(sn, ha, head0), sia)
            sib = jnp.where(lanes(head0 + hi, hi), at(sn, ha, head0 + hi), sib)
        ci_scr[...] = ci
        sia_scr[...] = sia
        sib_scr[...] = sib
        small = lax.dot_general(hb, ws_ref[...].astype(BF16), (((1,), (1,)), ((), ())),
                                preferred_element_type=F32)
        kr = _rope128(small, ci, sia, sib, ROT_IDX // 2)
        ke = jnp.where(lane < D_IDX, kr, 0.0)
        kie_ref[...] = ke.astype(BF16)
        kio_ref[...] = pltpu.roll(ke, D_IDX, 1).astype(BF16)
        is_w = (lane >= LANE_WI) & (lane < LANE_WI + H_IDX)
        is_f = (lane >= LANE_FB) & (lane < LANE_FB + HB)
        logf = jax.nn.log_sigmoid(small + bf_ref[...])
        wlf_ref[...] = jnp.where(is_w, small * (H_IDX ** -0.5 * D_IDX ** -0.5), jnp.where(is_f, logf, 0.0))

    t = TN_IN
    groups = t // MXU_N
    heads_per_group = MXU_N // HEAD_DIM

    def project(epilogues):
        for c, epilogue in enumerate(epilogues):
            cols = slice(c * MXU_N, (c + 1) * MXU_N)
            if epilogue is None:
                o_ref[:, cols] = jnp.zeros((tm, MXU_N), BF16)
                continue
            w_t = w_ref[cols, :].astype(BF16)
            acc = lax.dot_general(h_scr[...], w_t, (((1,), (1,)), ((), ())), preferred_element_type=F32)
            o_ref[:, cols] = epilogue(acc).astype(BF16)

    def rope_a(acc):
        return _rope128(acc, ca_scr[...], saa_scr[...], sab_scr[...], ROT_A // 2)

    def rope_i(acc):
        return _rope128(acc, ci_scr[...], sia_scr[...], sib_scr[...], ROT_IDX // 2)

    def values(vt_ref, c):
        def epilogue(acc):
            acc_t = acc.T
            for hh in range(heads_per_group):
                h = (c * heads_per_group + hh) % vt_ref.shape[0]
                vt_ref[h, 0:HEAD_DIM, :] = acc_t[hh * HEAD_DIM:(hh + 1) * HEAD_DIM, :].astype(BF16)
                vt_ref[h, HEAD_DIM:VT_ROWS, :] = jnp.ones((VT_ROWS - HEAD_DIM, tm), BF16)
            return acc
        return epilogue

    def epilogue_of(kind, c):
        return {"gate": lambda: jax.nn.sigmoid,
                "q_a": lambda: (lambda acc: rope_a(acc) * ATTN_SCALE),
                "k_a": lambda: rope_a,
                "v_a": lambda: values(vta_ref, c),
                "q_i": lambda: rope_i,
                "q_b": lambda: (lambda acc: acc * ATTN_SCALE),
                "k_b": lambda: (lambda acc: acc),
                "v_b": lambda: values(vtb_ref, c)}[kind]()

    for pattern, tiles in _TILE_PATTERNS:
        pred = functools.reduce(lambda a, b: a | b, [j == tile for tile in tiles])

        @pl.when(pred)
        def _(pattern=pattern):
            project([None if grp is None else epilogue_of(*grp) for grp in pattern])


def _w_in_row(j):
    row = jnp.int32(0)
    for out0, src0, _ in _PACK_RUNS:
        row = jnp.where(j >= out0 // TN_IN, src0 + TN_IN * (j - out0 // TN_IN), row)
    return pl.multiple_of(row, 8)


def _inproj(x2, pos2, g, w_t, w_small, bf_row, inv_freq, tm, b, t_len):
    m = x2.shape[0]
    for out0, src0, width in _PACK_RUNS:
        assert out0 % TN_IN == 0 and src0 % 8 == 0
        assert src0 + -(-width // TN_IN) * TN_IN <= w_t.shape[0]
    assert list(_PACK_RUNS) == sorted(_PACK_RUNS)
    grid = (m // tm, N_MAIN // TN_IN)
    nt = t_len // tm
    j_vb = OFF_VB // TN_IN
    heads_per_tile = min(TN_IN // HEAD_DIM, HB)
    vb_tiles = HB // heads_per_tile
    row128 = pl.BlockSpec((1, 128), lambda i, j: (0, 0))
    return pl.pallas_call(
        _inproj_kernel,
        grid=grid,
        in_specs=[
            pl.BlockSpec((tm, D_MODEL), lambda i, j: (i, 0)),
            pl.BlockSpec((tm, 1), lambda i, j: (i, 0)),
            pl.BlockSpec((1, D_MODEL), lambda i, j: (0, 0)),
            pl.BlockSpec((pl.Element(TN_IN), pl.Element(D_MODEL)), lambda i, j: (_w_in_row(j), 0)),
            pl.BlockSpec((128, D_MODEL), lambda i, j: (0, 0)),
            row128, row128,
        ],
        out_specs=[
            pl.BlockSpec((tm, TN_IN), lambda i, j: (i, j)),
            pl.BlockSpec((tm, 128), lambda i, j: (i, 0)),
            pl.BlockSpec((tm, 128), lambda i, j: (i, 0)),
            pl.BlockSpec((tm, 128), lambda i, j: (i, 0)),
            pl.BlockSpec((None, HA_KV, VT_ROWS, tm), lambda i, j: (i // nt, 0, 0, i % nt)),
            pl.BlockSpec((None, heads_per_tile, VT_ROWS, tm),
                         lambda i, j: (i // nt, jnp.clip(j - j_vb, 0, vb_tiles - 1), 0, i % nt)),
        ],
        out_shape=[
            jax.ShapeDtypeStruct((m, N_MAIN), BF16),
            jax.ShapeDtypeStruct((m, 128), BF16),
            jax.ShapeDtypeStruct((m, 128), BF16),
            jax.ShapeDtypeStruct((m, 128), F32),
            jax.ShapeDtypeStruct((b, HA_KV, VT_ROWS, t_len), BF16),
            jax.ShapeDtypeStruct((b, HB, VT_ROWS, t_len), BF16),
        ],
        scratch_shapes=[pltpu.VMEM((tm, D_MODEL), BF16)] + [pltpu.VMEM((tm, 128), F32)] * 6,
        compiler_params=_cparams(("arbitrary", "arbitrary")),
        name="inproj",
    )(x2, pos2, g, w_t, w_small, bf_row, inv_freq)


def _cumsum_kernel(wlf_ref, o_ref, carry_scr, *, tc):
    @pl.when(pl.program_id(1) == 0)
    def _():
        carry_scr[...] = jnp.zeros_like(carry_scr)

    lf = wlf_ref[...]
    r = lax.broadcasted_iota(jnp.int32, (tc, tc), 0)
    c = lax.broadcasted_iota(jnp.int32, (tc, tc), 1)
    tri = jnp.where(c <= r, 1.0, 0.0).astype(BF16)
    hi = lf.astype(BF16)
    r1 = lf - hi.astype(F32)
    mid = r1.astype(BF16)
    lo = (r1 - mid.astype(F32)).astype(BF16)
    cs = (jnp.dot(tri, hi, preferred_element_type=F32) + jnp.dot(tri, mid, preferred_element_type=F32)
          + jnp.dot(tri, lo, preferred_element_type=F32)) + carry_scr[0:1, :]
    carry_scr[...] = jnp.broadcast_to(cs[tc - 1:tc, :], carry_scr.shape)
    cs2 = cs * LOG2E
    for grp in range(HB // HEADS_PER_STEP_B):
        o_ref[grp] = pltpu.roll(cs2, (128 - LANE_FB - grp * HEADS_PER_STEP_B) % 128, 1)


def _cumsum(wlf, b, t_len):
    tc = CUMSUM_ROWS
    nt = t_len // tc
    return pl.pallas_call(
        functools.partial(_cumsum_kernel, tc=tc),
        grid=(b, nt),
        in_specs=[pl.BlockSpec((tc, 128), lambda bi, ti: (bi * nt + ti, 0))],
        out_specs=pl.BlockSpec((None, HB // HEADS_PER_STEP_B, tc, 128), lambda bi, ti: (bi, 0, ti, 0)),
        out_shape=jax.ShapeDtypeStruct((b, HB // HEADS_PER_STEP_B, t_len, 128), F32),
        scratch_shapes=[pltpu.VMEM((8, 128), F32)],
        compiler_params=_cparams(("arbitrary", "arbitrary")),
        name="fcumsum",
    )(wlf)


def _attn_a_kernel(qa_ref, qi0_ref, qi1_ref, ka_ref, vt_ref, kie_ref, kio_ref, wlf_ref, o_ref,
                   sc_scr, qs_scr, qg_scr, lt_scr, acc_scr, *, tk, top_k, n_iter):
    i = pl.program_id(1)
    nq = Q_BLOCK
    group = HA // HA_KV
    gw = group * nq
    nchunks = ((i + 1) * nq + tk - 1) // tk
    q_pos = i * nq + lax.broadcasted_iota(jnp.int32, (1, nq), 1)
    wt = wlf_ref[...].T
    half = H_IDX // 4
    for p in range(H_IDX // 2):
        src = qi0_ref if p < half else qi1_ref
        qs_scr[p * nq:(p + 1) * nq, :] = src[:, (p % half) * 128:(p % half + 1) * 128]

    def score_chunk(c, carry):
        rmin, rmax = carry
        off = pl.multiple_of(c * tk, tk)
        ke = kie_ref[pl.ds(off, tk), :]
        ko = kio_ref[pl.ds(off, tk), :]
        s = jnp.zeros((tk, nq), F32)
        pairs_per_dot = MXU_N // nq
        for pg in range(H_IDX // 2 // pairs_per_dot):
            qs = qs_scr[pg * MXU_N:(pg + 1) * MXU_N, :]
            de = lax.dot_general(ke, qs, (((1,), (1,)), ((), ())), preferred_element_type=F32)
            do = lax.dot_general(ko, qs, (((1,), (1,)), ((), ())), preferred_element_type=F32)
            for pp in range(pairs_per_dot):
                p = pg * pairs_per_dot + pp
                cols = slice(pp * nq, (pp + 1) * nq)
                s = s + jnp.maximum(de[:, cols], 0.0) * wt[LANE_WI + 2 * p:LANE_WI + 2 * p + 1, :]
                s = s + jnp.maximum(do[:, cols], 0.0) * wt[LANE_WI + 2 * p + 1:LANE_WI + 2 * p + 2, :]
        kpos = off + lax.broadcasted_iota(jnp.int32, (tk, 1), 0)
        valid = kpos <= q_pos
        sc_scr[pl.ds(off, tk), :] = jnp.where(valid, s, NEG)
        rmin = jnp.minimum(rmin, jnp.min(jnp.where(valid, s, -NEG), axis=0, keepdims=True))
        rmax = jnp.maximum(rmax, jnp.max(jnp.where(valid, s, NEG), axis=0, keepdims=True))
        return rmin, rmax

    lo, hi = lax.fori_loop(0, nchunks, score_chunk,
                           (jnp.full((1, nq), -NEG, F32), jnp.full((1, nq), NEG, F32)))

    def bisect(_, carry):
        lo, hi = carry
        mid = 0.5 * (lo + hi)

        def count_chunk(c, part):
            off = pl.multiple_of(c * tk, tk)
            ind = jnp.where(sc_scr[pl.ds(off, tk), :] >= mid, 1.0, 0.0)
            return part + jnp.sum(ind.reshape(tk // 64, 64, nq), axis=0)

        part = lax.fori_loop(0, nchunks, count_chunk, jnp.zeros((64, nq), F32))
        cnt = jnp.sum(part, axis=0, keepdims=True)
        ge = cnt >= top_k
        return jnp.where(ge, mid, lo), jnp.where(cnt == top_k, mid, jnp.where(ge, hi, mid))

    searching = q_pos >= top_k

    def bisect_round(state):
        it, lo, hi, _ = state
        lo, hi = lax.fori_loop(0, BISECT_ROUND, bisect, (lo, hi))
        pending = jnp.sum(jnp.where(searching & (lo < hi), 1.0, 0.0))
        return it + BISECT_ROUND, lo, hi, pending

    lo, hi = lax.fori_loop(0, BISECT_FIRST, bisect, (lo, hi))
    _, thr, _, _ = lax.while_loop(lambda st: (st[0] < n_iter) & (st[3] > 0.0), bisect_round,
                                  (jnp.int32(BISECT_FIRST), lo, hi, jnp.float32(1.0)))

    tka = tk // 2
    last = (nchunks - 1) * tk
    for n in range(HA_KV):
        for g in range(group):
            h = n * group + g
            qg_scr[n, g * nq:(g + 1) * nq, :] = qa_ref[:, h * HEAD_DIM:(h + 1) * HEAD_DIM]
    acc_scr[...] = jnp.zeros(acc_scr.shape, F32)

    def qk(off, slot):
        msk = jnp.where(sc_scr[pl.ds(off, tka), :] >= thr, 0.0, NEG)
        msk = jnp.concatenate([msk] * group, axis=1)
        cmax = []
        for n in range(HA_KV):
            k = ka_ref[pl.ds(off, tka), n * HEAD_DIM:(n + 1) * HEAD_DIM]
            s = lax.dot_general(k, qg_scr[n], (((1,), (1,)), ((), ())), preferred_element_type=F32) + msk
            lt_scr[n, slot] = s
            cmax.append(jnp.max(s, axis=0, keepdims=True))
        return tuple(cmax)

    def softmax_pv(off, slot, m_old, cmax):
        new = []
        for n in range(HA_KV):
            m_new = jnp.maximum(m_old[n], cmax[n])
            alpha = jnp.exp2(m_old[n] - m_new)
            p = jnp.exp2(lt_scr[n, slot] - m_new).astype(BF16)
            acc_scr[n] = alpha * acc_scr[n] + jnp.dot(vt_ref[n, :, pl.ds(off, tka)], p,
                                                      preferred_element_type=F32)
            new.append(m_new)
        return tuple(new)

    def pair(j, carry):
        m, cmax0 = carry
        off0 = pl.multiple_of(j * tk, tk)
        off1 = pl.multiple_of(off0 + tka, tka)
        off2 = pl.multiple_of(jnp.minimum(off0 + tk, last), tk)
        cmax1 = qk(off1, 1)
        m = softmax_pv(off0, 0, m, cmax0)
        cmax0 = qk(off2, 0)
        return softmax_pv(off1, 1, m, cmax1), cmax0

    lax.fori_loop(0, nchunks, pair, ((jnp.full((1, gw), NEG, F32),) * HA_KV, qk(0, 0)))
    for n in range(HA_KV):
        out_t = acc_scr[n, 0:HEAD_DIM, :] / acc_scr[n, HEAD_DIM:HEAD_DIM + 1, :]
        for g in range(group):
            h = n * group + g
            o_ref[:, h * HEAD_DIM:(h + 1) * HEAD_DIM] = out_t[:, g * nq:(g + 1) * nq].T.astype(BF16)


def _attn_a(proj, vt_a, kie, kio, wlf, b, t_len, top_k):
    nb = t_len // Q_BLOCK
    wa = HA * HEAD_DIM
    wkv = HA_KV * HEAD_DIM
    wqi = H_IDX * D_IDX // 2
    tk = min(KEY_CHUNK_A, t_len)
    gw = HA // HA_KV * Q_BLOCK
    kern = functools.partial(_attn_a_kernel, tk=tk, top_k=top_k, n_iter=BISECT_MAX_ITERS)
    return pl.pallas_call(
        kern,
        grid=(b, nb),
        in_specs=[
            pl.BlockSpec((Q_BLOCK, wa), lambda bi, i: (bi * nb + i, OFF_QA // wa)),
            pl.BlockSpec((Q_BLOCK, wqi), lambda bi, i: (bi * nb + i, OFF_QI // wqi)),
            pl.BlockSpec((Q_BLOCK, wqi), lambda bi, i: (bi * nb + i, OFF_QI // wqi + 1)),
            pl.BlockSpec((t_len, wkv), lambda bi, i: (bi, OFF_KA // wkv)),
            pl.BlockSpec((None, HA_KV, VT_ROWS, t_len), lambda bi, i: (bi, 0, 0, 0)),
            pl.BlockSpec((t_len, 128), lambda bi, i: (bi, 0)),
            pl.BlockSpec((t_len, 128), lambda bi, i: (bi, 0)),
            pl.BlockSpec((Q_BLOCK, 128), lambda bi, i: (bi * nb + i, 0)),
        ],
        out_specs=pl.BlockSpec((Q_BLOCK, wa), lambda bi, i: (bi * nb + i, 0)),
        out_shape=jax.ShapeDtypeStruct((b * t_len, wa), BF16),
        scratch_shapes=[
            pltpu.VMEM((t_len, Q_BLOCK), F32),
            pltpu.VMEM((H_IDX // 2 * Q_BLOCK, 128), BF16),
            pltpu.VMEM((HA_KV, gw, HEAD_DIM), BF16),
            pltpu.VMEM((HA_KV, 2, tk // 2, gw), F32),
            pltpu.VMEM((HA_KV, VT_ROWS, gw), F32),
        ],
        compiler_params=_cparams(("arbitrary", "arbitrary")),
        name="attn_a",
    )(proj, proj, proj, proj, vt_a, kie, kio, wlf)


def _attn_b_kernel(q_ref, k_ref, vt_ref, cb_ref, o_ref, lt_scr, acc_scr, *, tq):
    qi = pl.program_id(2)
    tk = tq // 2
    heads = HEADS_PER_STEP_B
    q0 = pl.multiple_of(qi * tq, tq)
    acc_scr[...] = jnp.zeros(acc_scr.shape, F32)

    def qk(off, slot, key0=None):
        cmax = []
        for n in range(heads):
            cols = slice(n * HEAD_DIM, (n + 1) * HEAD_DIM)
            bias = (cb_ref[pl.ds(q0, 1), :] - cb_ref[pl.ds(off, tk), :])[:, n:n + 1]
            s = lax.dot_general(k_ref[pl.ds(off, tk), cols], q_ref[:, cols], (((1,), (1,)), ((), ())),
                                preferred_element_type=F32) + bias
            if key0 is not None:
                key = key0 + lax.broadcasted_iota(jnp.int32, (tk, 1), 0)
                qry = lax.broadcasted_iota(jnp.int32, (1, tq), 1)
                s = jnp.where(key <= qry, s, NEG)
            lt_scr[n, slot] = s
            cmax.append(jnp.max(s, axis=0, keepdims=True))
        return tuple(cmax)

    def softmax_pv(off, slot, m_old, cmax):
        new = []
        for n in range(heads):
            m_new = jnp.maximum(m_old[n], cmax[n])
            alpha = jnp.exp2(m_old[n] - m_new)
            p = jnp.exp2(lt_scr[n, slot] - m_new).astype(BF16)
            acc_scr[n] = alpha * acc_scr[n] + jnp.dot(vt_ref[n, :, pl.ds(off, tk)], p,
                                                      preferred_element_type=F32)
            new.append(m_new)
        return tuple(new)

    q1 = pl.multiple_of(q0 + tk, tk)
    m = (jnp.full((1, tq), NEG, F32),) * heads
    cmax0 = qk(q0, 0, key0=0)
    cmax1 = qk(q1, 1, key0=tk)
    m = softmax_pv(q0, 0, m, cmax0)
    cmax0 = qk(0, 0)
    m = softmax_pv(q1, 1, m, cmax1)
    last = jnp.maximum(qi - 1, 0) * tq

    def pair(j, carry):
        m, cmax0 = carry
        off0 = pl.multiple_of(j * tq, tq)
        off1 = pl.multiple_of(off0 + tk, tk)
        off2 = pl.multiple_of(jnp.minimum(off0 + tq, last), tq)
        cmax1 = qk(off1, 1)
        m = softmax_pv(off0, 0, m, cmax0)
        cmax0 = qk(off2, 0)
        return softmax_pv(off1, 1, m, cmax1), cmax0

    lax.fori_loop(0, qi, pair, (m, cmax0))
    for n in range(heads):
        out_t = acc_scr[n, 0:HEAD_DIM, :] / acc_scr[n, HEAD_DIM:HEAD_DIM + 1, :]
        for g in range(tq // 128):
            o_ref[g * 128:(g + 1) * 128, n * HEAD_DIM:(n + 1) * HEAD_DIM] = (
                out_t[:, g * 128:(g + 1) * 128].T.astype(BF16))


def _attn_b(proj, vt_b, cb, b, t_len):
    tq = Q_BLOCK_B
    nq = t_len // tq
    hps = HEADS_PER_STEP_B
    wh = hps * HEAD_DIM
    return pl.pallas_call(
        functools.partial(_attn_b_kernel, tq=tq),
        grid=(b, HB // hps, nq),
        in_specs=[
            pl.BlockSpec((tq, wh), lambda bi, h, i: (bi * nq + i, OFF_QB // wh + h)),
            pl.BlockSpec((t_len, wh), lambda bi, h, i: (bi, OFF_KB // wh + h)),
            pl.BlockSpec((None, hps, VT_ROWS, t_len), lambda bi, h, i: (bi, h, 0, 0)),
            pl.BlockSpec((None, None, t_len, 128), lambda bi, h, i: (bi, h, 0, 0)),
        ],
        out_specs=pl.BlockSpec((tq, wh), lambda bi, h, i: (bi * nq + i, h)),
        out_shape=jax.ShapeDtypeStruct((b * t_len, HB * HEAD_DIM), BF16),
        scratch_shapes=[pltpu.VMEM((hps, 2, tq // 2, tq), F32), pltpu.VMEM((hps, VT_ROWS, tq), F32)],
        compiler_params=_cparams(("arbitrary", "arbitrary", "arbitrary")),
        name="attn_b",
    )(proj, proj, vt_b, cb)


def _merge_kernel(x_ref, oa_ref, ob_ref, ga_ref, gb_ref, woa_ref, wob_ref, wout_ref, gn_ref, o_ref, h_ref):
    ya = jnp.dot(oa_ref[...], woa_ref[...], preferred_element_type=F32)
    yb = jnp.dot(ob_ref[...], wob_ref[...], preferred_element_type=F32)
    mixed = ga_ref[...].astype(F32) * ya + gb_ref[...].astype(F32) * yb
    x1 = x_ref[...] + jnp.dot(mixed.astype(BF16), wout_ref[...], preferred_element_type=F32)
    o_ref[...] = x1
    h_ref[...] = _rms(x1, gn_ref[...]).astype(BF16)


def _merge(x2, out_a, out_b, proj, w_oa, w_ob, w_out, g_next, tm):
    m = x2.shape[0]
    wa = HA * HEAD_DIM
    wb = HB * HEAD_DIM
    return pl.pallas_call(
        _merge_kernel,
        grid=(m // tm,),
        in_specs=[
            pl.BlockSpec((tm, D_MODEL), lambda i: (i, 0)),
            pl.BlockSpec((tm, wa), lambda i: (i, 0)),
            pl.BlockSpec((tm, wb), lambda i: (i, 0)),
            pl.BlockSpec((tm, D_MODEL), lambda i: (i, OFF_GA // D_MODEL)),
            pl.BlockSpec((tm, D_MODEL), lambda i: (i, OFF_GB // D_MODEL)),
            _resident((wa, D_MODEL)),
            _resident((wb, D_MODEL)),
            _resident((D_MODEL, D_MODEL)),
            pl.BlockSpec((1, D_MODEL), lambda i: (0, 0)),
        ],
        out_specs=[pl.BlockSpec((tm, D_MODEL), lambda i: (i, 0)), pl.BlockSpec((tm, D_MODEL), lambda i: (i, 0))],
        out_shape=[jax.ShapeDtypeStruct((m, D_MODEL), F32), jax.ShapeDtypeStruct((m, D_MODEL), BF16)],
        compiler_params=_cparams(("arbitrary",)),
        name="merge",
    )(x2, out_a, out_b, proj, proj, w_oa, w_ob, w_out, g_next)


def _ffn_kernel(h_ref, wg_ref, wu_ref, wd_ref, o_ref):
    @pl.when(pl.program_id(1) == 0)
    def _():
        o_ref[...] = jnp.zeros(o_ref.shape, F32)

    h = h_ref[...]
    gate = jnp.dot(h, wg_ref[...].astype(BF16), preferred_element_type=F32)
    up = jnp.dot(h, wu_ref[...].astype(BF16), preferred_element_type=F32)
    act = (jax.nn.silu(gate) * up).astype(BF16)
    o_ref[...] += jnp.dot(act, wd_ref[...].astype(BF16), preferred_element_type=F32)


def _ffn(h, w_gate, w_up, w_down, tm, tf):
    m = h.shape[0]
    return pl.pallas_call(
        _ffn_kernel,
        grid=(m // tm, D_FF // tf),
        in_specs=[
            pl.BlockSpec((tm, D_MODEL), lambda i, f: (i, 0)),
            pl.BlockSpec((D_MODEL, tf), lambda i, f: (0, f)),
            pl.BlockSpec((D_MODEL, tf), lambda i, f: (0, f)),
            pl.BlockSpec((tf, D_MODEL), lambda i, f: (f, 0)),
        ],
        out_specs=pl.BlockSpec((tm, D_MODEL), lambda i, f: (i, 0)),
        out_shape=jax.ShapeDtypeStruct((m, D_MODEL), F32),
        compiler_params=_cparams(("arbitrary", "arbitrary")),
        name="ffn",
    )(h, w_gate, w_up, w_down)


def _ple_kernel(x_ref, y_ref, p_ref, g_ref, gf_ref, wg_ref, wp_ref, o_ref, *, final_norm):
    x = x_ref[...] + y_ref[...]
    h = _rms(x, g_ref[...]).astype(BF16)
    gate = jax.nn.sigmoid(jnp.dot(h, wg_ref[...], preferred_element_type=F32))
    emb = jnp.dot(p_ref[...].astype(BF16), wp_ref[...], preferred_element_type=F32)
    y = x + gate * emb
    o_ref[...] = _rms(y, gf_ref[...]) if final_norm else y


def _ple(x2, y2, p2, g, g_final, w_gate, w_proj, tm, final_norm):
    m = x2.shape[0]
    return pl.pallas_call(
        functools.partial(_ple_kernel, final_norm=final_norm),
        grid=(m // tm,),
        in_specs=[
            pl.BlockSpec((tm, D_MODEL), lambda i: (i, 0)),
            pl.BlockSpec((tm, D_MODEL), lambda i: (i, 0)),
            pl.BlockSpec((tm, D_PLE), lambda i: (i, 0)),
            pl.BlockSpec((1, D_MODEL), lambda i: (0, 0)),
            pl.BlockSpec((1, D_MODEL), lambda i: (0, 0)),
            _resident((D_MODEL, D_MODEL)),
            _resident((D_PLE, D_MODEL)),
        ],
        out_specs=pl.BlockSpec((tm, D_MODEL), lambda i: (i, 0)),
        out_shape=jax.ShapeDtypeStruct((m, D_MODEL), F32),
        compiler_params=_cparams(("arbitrary",)),
        name="ple",
    )(x2, y2, p2, g, g_final, w_gate, w_proj)


def _rope_lane_freqs():
    def inv(rot):
        half = rot // 2
        return ROPE_THETA ** (-jnp.arange(half, dtype=F32) / half)
    span = 128 // TRIG_PACK
    used = ROT_A // 2 + ROT_IDX // 2
    assert used <= span
    one = jnp.concatenate([inv(ROT_A), inv(ROT_IDX), jnp.zeros((span - used,), F32)])
    return jnp.tile(one, TRIG_PACK)[None, :]


def _w_in_views(w):
    w_t = jnp.swapaxes(w, 0, 1)
    rows = [w_t[_SPLIT_EDGES[k]:_SPLIT_EDGES[k + 1]] for k in (4, 5, 9)]
    pad = jnp.zeros((128 - D_IDX - H_IDX - HB, w_t.shape[1]), w_t.dtype)
    return w_t, jnp.concatenate(rows + [pad], axis=0)


def kernel(x, p, positions, g_mix, w_in, b_f, w_o_a, w_o_b, w_out, g_ffn, w_ffn_gate, w_ffn_up,
           w_ffn_down, g_ple, w_ple_gate, w_ple_proj, g_final):
    b, t_len, d = x.shape
    depth = w_in.shape[0]
    m = b * t_len
    top_k = min(TOPK_MAX, t_len // 4)
    inv_freq = _rope_lane_freqs()
    pos2 = positions.reshape(m, 1)
    x2 = x.reshape(m, d)
    tm_wide, tm = _row_tiles(m, t_len)
    for i in range(depth):
        w_t, w_small = _w_in_views(w_in[i])
        bf_row = jnp.zeros((1, 128), F32).at[0, LANE_FB:LANE_FB + HB].set(b_f[i].astype(F32))
        proj, kie, kio, wlf, vt_a, vt_b = _inproj(x2, pos2, g_mix[i][None, :], w_t, w_small, bf_row,
                                                  inv_freq, tm_wide, b, t_len)
        cb = _cumsum(wlf, b, t_len)
        out_a = _attn_a(proj, vt_a, kie, kio, wlf, b, t_len, top_k)
        out_b = _attn_b(proj, vt_b, cb, b, t_len)
        x2, h2 = _merge(x2, out_a, out_b, proj, w_o_a[i].astype(BF16), w_o_b[i].astype(BF16),
                        w_out[i].astype(BF16), g_ffn[i][None, :], tm)
        y2 = _ffn(h2, w_ffn_gate[i], w_ffn_up[i], w_ffn_down[i], tm_wide, FFN_COLS)
        x2 = _ple(x2, y2, p[i].reshape(m, D_PLE), g_ple[i][None, :], g_final[None, :],
                  w_ple_gate[i].astype(BF16), w_ple_proj[i].astype(BF16), tm, final_norm=(i + 1 == depth))
    return x2.reshape(b, t_len, d)
```

```python
import functools

import jax
import jax.numpy as jnp
import numpy as np
from jax import lax
from jax.experimental import pallas as pl
from jax.experimental.pallas import tpu as pltpu

F32 = jnp.float32
BF16 = jnp.bfloat16

D_MODEL = 2048
HEAD_DIM = 128
HA = 8
HA_KV = 2
H_IDX = 16
D_IDX = 64
TOPK_MAX = 256
HB = 8
ROPE_THETA = 500000.0
ROT_A = HEAD_DIM // 4
ROT_IDX = D_IDX // 4
Q_BLOCK = 128
D_FF = -(-8 * D_MODEL // (3 * 256)) * 256
D_PLE = 256
EPS = 1e-6

SPLIT_SIZES = (HA * HEAD_DIM, HA_KV * HEAD_DIM, HA_KV * HEAD_DIM, H_IDX * D_IDX, D_IDX, H_IDX,
               HB * HEAD_DIM, HB * HEAD_DIM, HB * HEAD_DIM, HB, D_MODEL, D_MODEL)

OFF_GA = 0
OFF_GB = OFF_GA + D_MODEL
OFF_QA = OFF_GB + D_MODEL
OFF_KA = OFF_QA + HA * HEAD_DIM
OFF_VA = OFF_KA + HA_KV * HEAD_DIM
OFF_QI = OFF_VA + HA_KV * HEAD_DIM
TN_IN = 1024
MXU_N = 256
OFF_QB = -(-(OFF_QI + H_IDX * D_IDX) // TN_IN) * TN_IN
OFF_KB = OFF_QB + HB * HEAD_DIM
OFF_VB = OFF_KB + HB * HEAD_DIM
N_MAIN = OFF_VB + HB * HEAD_DIM
_SPLIT_EDGES = [0] + [int(s) for s in np.cumsum(SPLIT_SIZES)]
_PACK_RUNS = ((OFF_GA, _SPLIT_EDGES[10], 2 * D_MODEL),
              (OFF_QA, _SPLIT_EDGES[0], OFF_QI + H_IDX * D_IDX - OFF_QA),
              (OFF_QB, _SPLIT_EDGES[6], N_MAIN - OFF_QB))
_SEGMENTS = ((OFF_GA, OFF_QA, "gate"), (OFF_QA, OFF_KA, "q_a"), (OFF_KA, OFF_VA, "k_a"),
             (OFF_VA, OFF_QI, "v_a"), (OFF_QI, OFF_QI + H_IDX * D_IDX, "q_i"),
             (OFF_QB, OFF_KB, "q_b"), (OFF_KB, OFF_VB, "k_b"), (OFF_VB, N_MAIN, "v_b"))


def _tile_patterns():
    def group(col):
        for lo, hi, kind in _SEGMENTS:
            if lo <= col < hi:
                assert (col - lo) % MXU_N == 0 and col + MXU_N <= hi
                return kind, (col - lo) // MXU_N
        return None
    by_pattern = {}
    for tile in range(N_MAIN // TN_IN):
        pattern = tuple(group(tile * TN_IN + g * MXU_N) for g in range(TN_IN // MXU_N))
        pattern = tuple(None if grp is None else (grp[0], grp[1] if grp[0] in ("v_a", "v_b") else 0)
                        for grp in pattern)
        by_pattern.setdefault(pattern, []).append(tile)
    return tuple(by_pattern.items())


_TILE_PATTERNS = _tile_patterns()
LANE_WI = D_IDX
LANE_FB = D_IDX + H_IDX

NEG = -1e30
VMEM_LIMIT = 60 * 1024 * 1024
LOG2E = 1.4426950408889634
ATTN_SCALE = HEAD_DIM ** -0.5 * LOG2E
VT_ROWS = HEAD_DIM + 16
TRIG_PACK = 4
HEADS_PER_STEP_B = 4
KEY_CHUNK_A = 512
Q_BLOCK_B = 512
CUMSUM_ROWS = 256
BISECT_COARSE = 10
BISECT_FIRST = 8
BISECT_ROUND = 2
BISECT_MAX_ITERS = 48


def _row_tiles(m, t_len):
    return min(1024, m, t_len), min(512, m)


FFN_COLS = 512


def _cparams(sem):
    return pltpu.CompilerParams(dimension_semantics=sem, vmem_limit_bytes=VMEM_LIMIT)


def _rms(x, g):
    return x * lax.rsqrt(jnp.mean(x * x, axis=-1, keepdims=True) + EPS) * g


def _resident(shape):
    nd = len(shape)
    return pl.BlockSpec(shape, lambda *_: (0,) * nd, pipeline_mode=pl.Buffered(1))


def _rope128(a, c, sa, sb, shift):
    outs = []
    for q in range(a.shape[1] // 128):
        aq = a[:, q * 128:(q + 1) * 128]
        outs.append(aq * c + pltpu.roll(aq, 128 - shift, 1) * sa + pltpu.roll(aq, shift, 1) * sb)
    return outs[0] if len(outs) == 1 else jnp.concatenate(outs, axis=1)


def _inproj_kernel(x_ref, pos_ref, g_ref, w_ref, ws_ref, bf_ref, inv_ref,
                   o_ref, kie_ref, kio_ref, wlf_ref, vta_ref, vtb_ref,
                   h_scr, ca_scr, saa_scr, sab_scr, ci_scr, sia_scr, sib_scr):
    j = pl.program_id(1)
    tm = x_ref.shape[0]

    @pl.when(j == 0)
    def _():
        hb = _rms(x_ref[...], g_ref[...]).astype(BF16)
        h_scr[...] = hb
        pos = pos_ref[...].astype(F32)
        lane = lax.broadcasted_iota(jnp.int32, (tm, 128), 1)
        ha, hi = ROT_A // 2, ROT_IDX // 2
        span, rq = 128 // TRIG_PACK, tm // TRIG_PACK
        lane_q = lax.broadcasted_iota(jnp.int32, (rq, 128), 1)
        pos_p = jnp.zeros((rq, 128), F32)
        for q in range(TRIG_PACK):
            pos_p = jnp.where((lane_q >= q * span) & (lane_q < (q + 1) * span), pos[q * rq:(q + 1) * rq, :], pos_p)
        ang = pos_p * inv_ref[...]
        cs_p = jnp.cos(ang)
        sn_p = jnp.sin(ang)
        cs = jnp.concatenate([pltpu.roll(cs_p, (128 - q * span) % 128, 1) for q in range(TRIG_PACK)], axis=0)
        sn = jnp.concatenate([pltpu.roll(sn_p, (128 - q * span) % 128, 1) for q in range(TRIG_PACK)], axis=0)

        def at(tab, src, dst):
            return pltpu.roll(tab, (dst - src) % 128, 1)

        def lanes(lo, n):
            return (lane >= lo) & (lane < lo + n)

        ca_scr[...] = jnp.where(lanes(0, ha), cs, jnp.where(lanes(ha, ha), at(cs, 0, ha), 1.0))
        saa_scr[...] = jnp.where(lanes(0, ha), -sn, 0.0)
        sab_scr[...] = jnp.where(lanes(ha, ha), at(sn, 0, ha), 0.0)
        ci = jnp.ones((tm, 128), F32)
        sia = jnp.zeros((tm, 128), F32)
        sib = jnp.zeros((tm, 128), F32)
        for head0 in (0, D_IDX):
            ci = jnp.where(lanes(head0, hi), at(cs, ha, head0), ci)
            ci = jnp.where(lanes(head0 + hi, hi), at(cs, ha, head0 + hi), ci)
            sia = jnp.where(lanes(head0, hi), -at(sn, ha, head0), sia)
            sib = jnp.where(lanes(head0 + hi, hi), at(sn, ha, head0 + hi), sib)
        ci_scr[...] = ci
        sia_scr[...] = sia
        sib_scr[...] = sib
        small = lax.dot_general(hb, ws_ref[...].astype(BF16), (((1,), (1,)), ((), ())),
                                preferred_element_type=F32)
        kr = _rope128(small, ci, sia, sib, ROT_IDX // 2)
        ke = jnp.where(lane < D_IDX, kr, 0.0)
        kie_ref[...] = ke.astype(BF16)
        kio_ref[...] = pltpu.roll(ke, D_IDX, 1).astype(BF16)
        is_w = (lane >= LANE_WI) & (lane < LANE_WI + H_IDX)
        is_f = (lane >= LANE_FB) & (lane < LANE_FB + HB)
        logf = jax.nn.log_sigmoid(small + bf_ref[...])
        wlf_ref[...] = jnp.where(is_w, small * (H_IDX ** -0.5 * D_IDX ** -0.5), jnp.where(is_f, logf, 0.0))

    t = TN_IN
    groups = t // MXU_N
    heads_per_group = MXU_N // HEAD_DIM

    def project(epilogues):
        for c, epilogue in enumerate(epilogues):
            cols = slice(c * MXU_N, (c + 1) * MXU_N)
            if epilogue is None:
                o_ref[:, cols] = jnp.zeros((tm, MXU_N), BF16)
                continue
            w_t = w_ref[cols, :].astype(BF16)
            acc = lax.dot_general(h_scr[...], w_t, (((1,), (1,)), ((), ())), preferred_element_type=F32)
            o_ref[:, cols] = epilogue(acc).astype(BF16)

    def rope_a(acc):
        return _rope128(acc, ca_scr[...], saa_scr[...], sab_scr[...], ROT_A // 2)

    def rope_i(acc):
        return _rope128(acc, ci_scr[...], sia_scr[...], sib_scr[...], ROT_IDX // 2)

    def values(vt_ref, c):
        def epilogue(acc):
            acc_t = acc.T
            for hh in range(heads_per_group):
                h = (c * heads_per_group + hh) % vt_ref.shape[0]
                vt_ref[h, 0:HEAD_DIM, :] = acc_t[hh * HEAD_DIM:(hh + 1) * HEAD_DIM, :].astype(BF16)
                vt_ref[h, HEAD_DIM:VT_ROWS, :] = jnp.ones((VT_ROWS - HEAD_DIM, tm), BF16)
            return acc
        return epilogue

    def epilogue_of(kind, c):
        return {"gate": lambda: jax.nn.sigmoid,
                "q_a": lambda: (lambda acc: rope_a(acc) * ATTN_SCALE),
                "k_a": lambda: rope_a,
                "v_a": lambda: values(vta_ref, c),
                "q_i": lambda: rope_i,
                "q_b": lambda: (lambda acc: acc * ATTN_SCALE),
                "k_b": lambda: (lambda acc: acc),
                "v_b": lambda: values(vtb_ref, c)}[kind]()

    for pattern, tiles in _TILE_PATTERNS:
        pred = functools.reduce(lambda a, b: a | b, [j == tile for tile in tiles])

        @pl.when(pred)
        def _(pattern=pattern):
            project([None if grp is None else epilogue_of(*grp) for grp in pattern])


def _w_in_row(j):
    row = jnp.int32(0)
    for out0, src0, _ in _PACK_RUNS:
        row = jnp.where(j >= out0 // TN_IN, src0 + TN_IN * (j - out0 // TN_IN), row)
    return pl.multiple_of(row, 8)


def _inproj(x2, pos2, g, w_t, w_small, bf_row, inv_freq, tm, b, t_len):
    m = x2.shape[0]
    for out0, src0, width in _PACK_RUNS:
        assert out0 % TN_IN == 0 and src0 % 8 == 0
        assert src0 + -(-width // TN_IN) * TN_IN <= w_t.shape[0]
    assert list(_PACK_RUNS) == sorted(_PACK_RUNS)
    grid = (m // tm, N_MAIN // TN_IN)
    nt = t_len // tm
    j_vb = OFF_VB // TN_IN
    heads_per_tile = min(TN_IN // HEAD_DIM, HB)
    vb_tiles = HB // heads_per_tile
    row128 = pl.BlockSpec((1, 128), lambda i, j: (0, 0))
    return pl.pallas_call(
        _inproj_kernel,
        grid=grid,
        in_specs=[
            pl.BlockSpec((tm, D_MODEL), lambda i, j: (i, 0)),
            pl.BlockSpec((tm, 1), lambda i, j: (i, 0)),
            pl.BlockSpec((1, D_MODEL), lambda i, j: (0, 0)),
            pl.BlockSpec((pl.Element(TN_IN), pl.Element(D_MODEL)), lambda i, j: (_w_in_row(j), 0)),
            pl.BlockSpec((128, D_MODEL), lambda i, j: (0, 0)),
            row128, row128,
        ],
        out_specs=[
            pl.BlockSpec((tm, TN_IN), lambda i, j: (i, j)),
            pl.BlockSpec((tm, 128), lambda i, j: (i, 0)),
            pl.BlockSpec((tm, 128), lambda i, j: (i, 0)),
            pl.BlockSpec((tm, 128), lambda i, j: (i, 0)),
            pl.BlockSpec((None, HA_KV, VT_ROWS, tm), lambda i, j: (i // nt, 0, 0, i % nt)),
            pl.BlockSpec((None, heads_per_tile, VT_ROWS, tm),
                         lambda i, j: (i // nt, jnp.clip(j - j_vb, 0, vb_tiles - 1), 0, i % nt)),
        ],
        out_shape=[
            jax.ShapeDtypeStruct((m, N_MAIN), BF16),
            jax.ShapeDtypeStruct((m, 128), BF16),
            jax.ShapeDtypeStruct((m, 128), BF16),
            jax.ShapeDtypeStruct((m, 128), F32),
            jax.ShapeDtypeStruct((b, HA_KV, VT_ROWS, t_len), BF16),
            jax.ShapeDtypeStruct((b, HB, VT_ROWS, t_len), BF16),
        ],
        scratch_shapes=[pltpu.VMEM((tm, D_MODEL), BF16)] + [pltpu.VMEM((tm, 128), F32)] * 6,
        compiler_params=_cparams(("arbitrary", "arbitrary")),
        name="inproj",
    )(x2, pos2, g, w_t, w_small, bf_row, inv_freq)


def _cumsum_kernel(wlf_ref, o_ref, carry_scr, *, tc):
    @pl.when(pl.program_id(1) == 0)
    def _():
        carry_scr[...] = jnp.zeros_like(carry_scr)

    lf = wlf_ref[...]
    r = lax.broadcasted_iota(jnp.int32, (tc, tc), 0)
    c = lax.broadcasted_iota(jnp.int32, (tc, tc), 1)
    tri = jnp.where(c <= r, 1.0, 0.0).astype(BF16)
    hi = lf.astype(BF16)
    r1 = lf - hi.astype(F32)
    mid = r1.astype(BF16)
    lo = (r1 - mid.astype(F32)).astype(BF16)
    cs = (jnp.dot(tri, hi, preferred_element_type=F32) + jnp.dot(tri, mid, preferred_element_type=F32)
          + jnp.dot(tri, lo, preferred_element_type=F32)) + carry_scr[0:1, :]
    carry_scr[...] = jnp.broadcast_to(cs[tc - 1:tc, :], carry_scr.shape)
    cs2 = cs * LOG2E
    for grp in range(HB // HEADS_PER_STEP_B):
        o_ref[grp] = pltpu.roll(cs2, (128 - LANE_FB - grp * HEADS_PER_STEP_B) % 128, 1)


def _cumsum(wlf, b, t_len):
    tc = CUMSUM_ROWS
    nt = t_len // tc
    return pl.pallas_call(
        functools.partial(_cumsum_kernel, tc=tc),
        grid=(b, nt),
        in_specs=[pl.BlockSpec((tc, 128), lambda bi, ti: (bi * nt + ti, 0))],
        out_specs=pl.BlockSpec((None, HB // HEADS_PER_STEP_B, tc, 128), lambda bi, ti: (bi, 0, ti, 0)),
        out_shape=jax.ShapeDtypeStruct((b, HB // HEADS_PER_STEP_B, t_len, 128), F32),
        scratch_shapes=[pltpu.VMEM((8, 128), F32)],
        compiler_params=_cparams(("arbitrary", "arbitrary")),
        name="fcumsum",
    )(wlf)


def _attn_a_kernel(qa_ref, qi0_ref, qi1_ref, ka_ref, vt_ref, kie_ref, kio_ref, wlf_ref, o_ref,
                   sc_scr, scb_scr, qs_scr, qg_scr, lt_scr, acc_scr, *, tk, top_k, n_iter):
    i = pl.program_id(1)
    nq = Q_BLOCK
    group = HA // HA_KV
    gw = group * nq
    nchunks = ((i + 1) * nq + tk - 1) // tk
    q_pos = i * nq + lax.broadcasted_iota(jnp.int32, (1, nq), 1)
    wt = wlf_ref[...].T
    half = H_IDX // 4
    for p in range(H_IDX // 2):
        src = qi0_ref if p < half else qi1_ref
        qs_scr[p * nq:(p + 1) * nq, :] = src[:, (p % half) * 128:(p % half + 1) * 128]

    def score_chunk(c, carry):
        rmin, rmax = carry
        off = pl.multiple_of(c * tk, tk)
        ke = kie_ref[pl.ds(off, tk), :]
        ko = kio_ref[pl.ds(off, tk), :]
        s = jnp.zeros((tk, nq), F32)
        pairs_per_dot = MXU_N // nq
        for pg in range(H_IDX // 2 // pairs_per_dot):
            qs = qs_scr[pg * MXU_N:(pg + 1) * MXU_N, :]
            de = lax.dot_general(ke, qs, (((1,), (1,)), ((), ())), preferred_element_type=F32)
            do = lax.dot_general(ko, qs, (((1,), (1,)), ((), ())), preferred_element_type=F32)
            for pp in range(pairs_per_dot):
                p = pg * pairs_per_dot + pp
                cols = slice(pp * nq, (pp + 1) * nq)
                s = s + jnp.maximum(de[:, cols], 0.0) * wt[LANE_WI + 2 * p:LANE_WI + 2 * p + 1, :]
                s = s + jnp.maximum(do[:, cols], 0.0) * wt[LANE_WI + 2 * p + 1:LANE_WI + 2 * p + 2, :]
        kpos = off + lax.broadcasted_iota(jnp.int32, (tk, 1), 0)
        valid = kpos <= q_pos
        s_masked = jnp.where(valid, s, NEG)
        sc_scr[pl.ds(off, tk), :] = s_masked
        scb_scr[pl.ds(off, tk), :] = s_masked.astype(BF16)
        rmin = jnp.minimum(rmin, jnp.min(jnp.where(valid, s, -NEG), axis=0, keepdims=True))
        rmax = jnp.maximum(rmax, jnp.max(jnp.where(valid, s, NEG), axis=0, keepdims=True))
        return rmin, rmax

    lo, hi = lax.fori_loop(0, nchunks, score_chunk,
                           (jnp.full((1, nq), -NEG, F32), jnp.full((1, nq), NEG, F32)))

    def bisect(_, carry):
        lo, hi = carry
        mid = 0.5 * (lo + hi)

        def count_chunk(c, part):
            off = pl.multiple_of(c * tk, tk)
            ind = jnp.where(sc_scr[pl.ds(off, tk), :] >= mid, 1.0, 0.0)
            return part + jnp.sum(ind.reshape(tk // 64, 64, nq), axis=0)

        part = lax.fori_loop(0, nchunks, count_chunk, jnp.zeros((64, nq), F32))
        cnt = jnp.sum(part, axis=0, keepdims=True)
        ge = cnt >= top_k
        return jnp.where(ge, mid, lo), jnp.where(cnt == top_k, mid, jnp.where(ge, hi, mid))

    searching = q_pos >= top_k

    def bisect_coarse(_, carry):
        lo, hi = carry
        g = (0.5 * (lo + hi)).astype(BF16)

        def count_chunk(c, part):
            off = pl.multiple_of(c * tk, tk)
            ind = jnp.where(scb_scr[pl.ds(off, tk), :] >= g, jnp.ones((), BF16), jnp.zeros((), BF16))
            for r in range(tk // 128):
                part = part + ind[r * 128:(r + 1) * 128, :]
            return part

        part = lax.fori_loop(0, nchunks, count_chunk, jnp.zeros((128, nq), BF16))
        cnt = jnp.sum(part.astype(F32), axis=0, keepdims=True)
        gf = g.astype(F32)
        slack = jnp.maximum(jnp.abs(gf) * (2.0 ** -8), 1e-30)
        ge = cnt >= top_k
        return (jnp.where(ge, jnp.maximum(lo, gf - slack), lo),
                jnp.where(ge, hi, jnp.minimum(hi, gf + slack)))

    assert sc_scr.shape[0] // 128 <= 256
    lo, hi = lax.fori_loop(0, BISECT_COARSE, bisect_coarse, (lo, hi))

    def bisect_round(state):
        it, lo, hi, _ = state
        lo, hi = lax.fori_loop(0, BISECT_ROUND, bisect, (lo, hi))
        pending = jnp.sum(jnp.where(searching & (lo < hi), 1.0, 0.0))
        return it + BISECT_ROUND, lo, hi, pending

    lo, hi = lax.fori_loop(0, BISECT_FIRST, bisect, (lo, hi))
    _, thr, _, _ = lax.while_loop(lambda st: (st[0] < n_iter) & (st[3] > 0.0), bisect_round,
                                  (jnp.int32(BISECT_FIRST), lo, hi, jnp.float32(1.0)))

    tka = tk // 2
    last = (nchunks - 1) * tk
    for n in range(HA_KV):
        for g in range(group):
            h = n * group + g
            qg_scr[n, g * nq:(g + 1) * nq, :] = qa_ref[:, h * HEAD_DIM:(h + 1) * HEAD_DIM]
    acc_scr[...] = jnp.zeros(acc_scr.shape, F32)

    def qk(off, slot):
        msk = jnp.where(sc_scr[pl.ds(off, tka), :] >= thr, 0.0, NEG)
        msk = jnp.concatenate([msk] * group, axis=1)
        cmax = []
        for n in range(HA_KV):
            k = ka_ref[pl.ds(off, tka), n * HEAD_DIM:(n + 1) * HEAD_DIM]
            s = lax.dot_general(k, qg_scr[n], (((1,), (1,)), ((), ())), preferred_element_type=F32) + msk
            lt_scr[n, slot] = s
            cmax.append(jnp.max(s, axis=0, keepdims=True))
        return tuple(cmax)

    def softmax_pv(off, slot, m_old, cmax):
        new = []
        for n in range(HA_KV):
            m_new = jnp.maximum(m_old[n], cmax[n])
            alpha = jnp.exp2(m_old[n] - m_new)
            p = jnp.exp2(lt_scr[n, slot] - m_new).astype(BF16)
            acc_scr[n] = alpha * acc_scr[n] + jnp.dot(vt_ref[n, :, pl.ds(off, tka)], p,
                                                      preferred_element_type=F32)
            new.append(m_new)
        return tuple(new)

    def pair(j, carry):
        m, cmax0 = carry
        off0 = pl.multiple_of(j * tk, tk)
        off1 = pl.multiple_of(off0 + tka, tka)
        off2 = pl.multiple_of(jnp.minimum(off0 + tk, last), tk)
        cmax1 = qk(off1, 1)
        m = softmax_pv(off0, 0, m, cmax0)
        cmax0 = qk(off2, 0)
        return softmax_pv(off1, 1, m, cmax1), cmax0

    lax.fori_loop(0, nchunks, pair, ((jnp.full((1, gw), NEG, F32),) * HA_KV, qk(0, 0)))
    for n in range(HA_KV):
        out_t = acc_scr[n, 0:HEAD_DIM, :] / acc_scr[n, HEAD_DIM:HEAD_DIM + 1, :]
        for g in range(group):
            h = n * group + g
            o_ref[:, h * HEAD_DIM:(h + 1) * HEAD_DIM] = out_t[:, g * nq:(g + 1) * nq].T.astype(BF16)


def _attn_a(proj, vt_a, kie, kio, wlf, b, t_len, top_k):
    nb = t_len // Q_BLOCK
    wa = HA * HEAD_DIM
    wkv = HA_KV * HEAD_DIM
    wqi = H_IDX * D_IDX // 2
    tk = min(KEY_CHUNK_A, t_len)
    gw = HA // HA_KV * Q_BLOCK
    kern = functools.partial(_attn_a_kernel, tk=tk, top_k=top_k, n_iter=BISECT_MAX_ITERS)
    return pl.pallas_call(
        kern,
        grid=(b, nb),
        in_specs=[
            pl.BlockSpec((Q_BLOCK, wa), lambda bi, i: (bi * nb + i, OFF_QA // wa)),
            pl.BlockSpec((Q_BLOCK, wqi), lambda bi, i: (bi * nb + i, OFF_QI // wqi)),
            pl.BlockSpec((Q_BLOCK, wqi), lambda bi, i: (bi * nb + i, OFF_QI // wqi + 1)),
            pl.BlockSpec((t_len, wkv), lambda bi, i: (bi, OFF_KA // wkv)),
            pl.BlockSpec((None, HA_KV, VT_ROWS, t_len), lambda bi, i: (bi, 0, 0, 0)),
            pl.BlockSpec((t_len, 128), lambda bi, i: (bi, 0)),
            pl.BlockSpec((t_len, 128), lambda bi, i: (bi, 0)),
            pl.BlockSpec((Q_BLOCK, 128), lambda bi, i: (bi * nb + i, 0)),
        ],
        out_specs=pl.BlockSpec((Q_BLOCK, wa), lambda bi, i: (bi * nb + i, 0)),
        out_shape=jax.ShapeDtypeStruct((b * t_len, wa), BF16),
        scratch_shapes=[
            pltpu.VMEM((t_len, Q_BLOCK), F32),
            pltpu.VMEM((t_len, Q_BLOCK), BF16),
            pltpu.VMEM((H_IDX // 2 * Q_BLOCK, 128), BF16),
            pltpu.VMEM((HA_KV, gw, HEAD_DIM), BF16),
            pltpu.VMEM((HA_KV, 2, tk // 2, gw), F32),
            pltpu.VMEM((HA_KV, VT_ROWS, gw), F32),
        ],
        compiler_params=_cparams(("arbitrary", "arbitrary")),
        name="attn_a",
    )(proj, proj, proj, proj, vt_a, kie, kio, wlf)


def _attn_b_kernel(q_ref, k_ref, vt_ref, cb_ref, o_ref, lt_scr, acc_scr, *, tq):
    qi = pl.program_id(2)
    tk = tq // 2
    heads = HEADS_PER_STEP_B
    q0 = pl.multiple_of(qi * tq, tq)
    acc_scr[...] = jnp.zeros(acc_scr.shape, F32)

    def qk(off, slot, key0=None):
        cmax = []
        for n in range(heads):
            cols = slice(n * HEAD_DIM, (n + 1) * HEAD_DIM)
            bias = (cb_ref[pl.ds(q0, 1), :] - cb_ref[pl.ds(off, tk), :])[:, n:n + 1]
            s = lax.dot_general(k_ref[pl.ds(off, tk), cols], q_ref[:, cols], (((1,), (1,)), ((), ())),
                                preferred_element_type=F32) + bias
            if key0 is not None:
                key = key0 + lax.broadcasted_iota(jnp.int32, (tk, 1), 0)
                qry = lax.broadcasted_iota(jnp.int32, (1, tq), 1)
                s = jnp.where(key <= qry, s, NEG)
            lt_scr[n, slot] = s
            cmax.append(jnp.max(s, axis=0, keepdims=True))
        return tuple(cmax)

    def softmax_pv(off, slot, m_old, cmax):
        new = []
        for n in range(heads):
            m_new = jnp.maximum(m_old[n], cmax[n])
            alpha = jnp.exp2(m_old[n] - m_new)
            p = jnp.exp2(lt_scr[n, slot] - m_new).astype(BF16)
            acc_scr[n] = alpha * acc_scr[n] + jnp.dot(vt_ref[n, :, pl.ds(off, tk)], p,
                                                      preferred_element_type=F32)
            new.append(m_new)
        return tuple(new)

    q1 = pl.multiple_of(q0 + tk, tk)
    m = (jnp.full((1, tq), NEG, F32),) * heads
    cmax0 = qk(q0, 0, key0=0)
    cmax1 = qk(q1, 1, key0=tk)
    m = softmax_pv(q0, 0, m, cmax0)
    cmax0 = qk(0, 0)
    m = softmax_pv(q1, 1, m, cmax1)
    last = jnp.maximum(qi - 1, 0) * tq

    def pair(j, carry):
        m, cmax0 = carry
        off0 = pl.multiple_of(j * tq, tq)
        off1 = pl.multiple_of(off0 + tk, tk)
        off2 = pl.multiple_of(jnp.minimum(off0 + tq, last), tq)
        cmax1 = qk(off1, 1)
        m = softmax_pv(off0, 0, m, cmax0)
        cmax0 = qk(off2, 0)
        return softmax_pv(off1, 1, m, cmax1), cmax0

    lax.fori_loop(0, qi, pair, (m, cmax0))
    for n in range(heads):
        out_t = acc_scr[n, 0:HEAD_DIM, :] / acc_scr[n, HEAD_DIM:HEAD_DIM + 1, :]
        for g in range(tq // 128):
            o_ref[g * 128:(g + 1) * 128, n * HEAD_DIM:(n + 1) * HEAD_DIM] = (
                out_t[:, g * 128:(g + 1) * 128].T.astype(BF16))


def _attn_b(proj, vt_b, cb, b, t_len):
    tq = Q_BLOCK_B
    nq = t_len // tq
    hps = HEADS_PER_STEP_B
    wh = hps * HEAD_DIM
    return pl.pallas_call(
        functools.partial(_attn_b_kernel, tq=tq),
        grid=(b, HB // hps, nq),
        in_specs=[
            pl.BlockSpec((tq, wh), lambda bi, h, i: (bi * nq + i, OFF_QB // wh + h)),
            pl.BlockSpec((t_len, wh), lambda bi, h, i: (bi, OFF_KB // wh + h)),
            pl.BlockSpec((None, hps, VT_ROWS, t_len), lambda bi, h, i: (bi, h, 0, 0)),
            pl.BlockSpec((None, None, t_len, 128), lambda bi, h, i: (bi, h, 0, 0)),
        ],
        out_specs=pl.BlockSpec((tq, wh), lambda bi, h, i: (bi * nq + i, h)),
        out_shape=jax.ShapeDtypeStruct((b * t_len, HB * HEAD_DIM), BF16),
        scratch_shapes=[pltpu.VMEM((hps, 2, tq // 2, tq), F32), pltpu.VMEM((hps, VT_ROWS, tq), F32)],
        compiler_params=_cparams(("arbitrary", "arbitrary", "arbitrary")),
        name="attn_b",
    )(proj, proj, vt_b, cb)


def _merge_kernel(x_ref, oa_ref, ob_ref, ga_ref, gb_ref, woa_ref, wob_ref, wout_ref, gn_ref, o_ref, h_ref):
    ya = jnp.dot(oa_ref[...], woa_ref[...], preferred_element_type=F32)
    yb = jnp.dot(ob_ref[...], wob_ref[...], preferred_element_type=F32)
    mixed = ga_ref[...].astype(F32) * ya + gb_ref[...].astype(F32) * yb
    x1 = x_ref[...] + jnp.dot(mixed.astype(BF16), wout_ref[...], preferred_element_type=F32)
    o_ref[...] = x1
    h_ref[...] = _rms(x1, gn_ref[...]).astype(BF16)


def _merge(x2, out_a, out_b, proj, w_oa, w_ob, w_out, g_next, tm):
    m = x2.shape[0]
    wa = HA * HEAD_DIM
    wb = HB * HEAD_DIM
    return pl.pallas_call(
        _merge_kernel,
        grid=(m // tm,),
        in_specs=[
            pl.BlockSpec((tm, D_MODEL), lambda i: (i, 0)),
            pl.BlockSpec((tm, wa), lambda i: (i, 0)),
            pl.BlockSpec((tm, wb), lambda i: (i, 0)),
            pl.BlockSpec((tm, D_MODEL), lambda i: (i, OFF_GA // D_MODEL)),
            pl.BlockSpec((tm, D_MODEL), lambda i: (i, OFF_GB // D_MODEL)),
            _resident((wa, D_MODEL)),
            _resident((wb, D_MODEL)),
            _resident((D_MODEL, D_MODEL)),
            pl.BlockSpec((1, D_MODEL), lambda i: (0, 0)),
        ],
        out_specs=[pl.BlockSpec((tm, D_MODEL), lambda i: (i, 0)), pl.BlockSpec((tm, D_MODEL), lambda i: (i, 0))],
        out_shape=[jax.ShapeDtypeStruct((m, D_MODEL), F32), jax.ShapeDtypeStruct((m, D_MODEL), BF16)],
        compiler_params=_cparams(("arbitrary",)),
        name="merge",
    )(x2, out_a, out_b, proj, proj, w_oa, w_ob, w_out, g_next)


def _ffn_kernel(h_ref, wg_ref, wu_ref, wd_ref, o_ref):
    @pl.when(pl.program_id(1) == 0)
    def _():
        o_ref[...] = jnp.zeros(o_ref.shape, F32)

    h = h_ref[...]
    gate = jnp.dot(h, wg_ref[...].astype(BF16), preferred_element_type=F32)
    up = jnp.dot(h, wu_ref[...].astype(BF16), preferred_element_type=F32)
    act = (jax.nn.silu(gate) * up).astype(BF16)
    o_ref[...] += jnp.dot(act, wd_ref[...].astype(BF16), preferred_element_type=F32)


def _ffn(h, w_gate, w_up, w_down, tm, tf):
    m = h.shape[0]
    return pl.pallas_call(
        _ffn_kernel,
        grid=(m // tm, D_FF // tf),
        in_specs=[
            pl.BlockSpec((tm, D_MODEL), lambda i, f: (i, 0)),
            pl.BlockSpec((D_MODEL, tf), lambda i, f: (0, f)),
            pl.BlockSpec((D_MODEL, tf), lambda i, f: (0, f)),
            pl.BlockSpec((tf, D_MODEL), lambda i, f: (f, 0)),
        ],
        out_specs=pl.BlockSpec((tm, D_MODEL), lambda i, f: (i, 0)),
        out_shape=jax.ShapeDtypeStruct((m, D_MODEL), F32),
        compiler_params=_cparams(("arbitrary", "arbitrary")),
        name="ffn",
    )(h, w_gate, w_up, w_down)


def _ple_kernel(x_ref, y_ref, p_ref, g_ref, gf_ref, wg_ref, wp_ref, o_ref, *, final_norm):
    x = x_ref[...] + y_ref[...]
    h = _rms(x, g_ref[...]).astype(BF16)
    gate = jax.nn.sigmoid(jnp.dot(h, wg_ref[...], preferred_element_type=F32))
    emb = jnp.dot(p_ref[...].astype(BF16), wp_ref[...], preferred_element_type=F32)
    y = x + gate * emb
    o_ref[...] = _rms(y, gf_ref[...]) if final_norm else y


def _ple(x2, y2, p2, g, g_final, w_gate, w_proj, tm, final_norm):
    m = x2.shape[0]
    return pl.pallas_call(
        functools.partial(_ple_kernel, final_norm=final_norm),
        grid=(m // tm,),
        in_specs=[
            pl.BlockSpec((tm, D_MODEL), lambda i: (i, 0)),
            pl.BlockSpec((tm, D_MODEL), lambda i: (i, 0)),
            pl.BlockSpec((tm, D_PLE), lambda i: (i, 0)),
            pl.BlockSpec((1, D_MODEL), lambda i: (0, 0)),
            pl.BlockSpec((1, D_MODEL), lambda i: (0, 0)),
            _resident((D_MODEL, D_MODEL)),
            _resident((D_PLE, D_MODEL)),
        ],
        out_specs=pl.BlockSpec((tm, D_MODEL), lambda i: (i, 0)),
        out_shape=jax.ShapeDtypeStruct((m, D_MODEL), F32),
        compiler_params=_cparams(("arbitrary",)),
        name="ple",
    )(x2, y2, p2, g, g_final, w_gate, w_proj)


def _rope_lane_freqs():
    def inv(rot):
        half = rot // 2
        return ROPE_THETA ** (-jnp.arange(half, dtype=F32) / half)
    span = 128 // TRIG_PACK
    used = ROT_A // 2 + ROT_IDX // 2
    assert used <= span
    one = jnp.concatenate([inv(ROT_A), inv(ROT_IDX), jnp.zeros((span - used,), F32)])
    return jnp.tile(one, TRIG_PACK)[None, :]


def _w_in_views(w):
    w_t = jnp.swapaxes(w, 0, 1)
    rows = [w_t[_SPLIT_EDGES[k]:_SPLIT_EDGES[k + 1]] for k in (4, 5, 9)]
    pad = jnp.zeros((128 - D_IDX - H_IDX - HB, w_t.shape[1]), w_t.dtype)
    return w_t, jnp.concatenate(rows + [pad], axis=0)


def kernel(x, p, positions, g_mix, w_in, b_f, w_o_a, w_o_b, w_out, g_ffn, w_ffn_gate, w_ffn_up,
           w_ffn_down, g_ple, w_ple_gate, w_ple_proj, g_final):
    b, t_len, d = x.shape
    depth = w_in.shape[0]
    m = b * t_len
    top_k = min(TOPK_MAX, t_len // 4)
    inv_freq = _rope_lane_freqs()
    pos2 = positions.reshape(m, 1)
    x2 = x.reshape(m, d)
    tm_wide, tm = _row_tiles(m, t_len)
    for i in range(depth):
        w_t, w_small = _w_in_views(w_in[i])
        bf_row = jnp.zeros((1, 128), F32).at[0, LANE_FB:LANE_FB + HB].set(b_f[i].astype(F32))
        proj, kie, kio, wlf, vt_a, vt_b = _inproj(x2, pos2, g_mix[i][None, :], w_t, w_small, bf_row,
                                                  inv_freq, tm_wide, b, t_len)
        cb = _cumsum(wlf, b, t_len)
        out_a = _attn_a(proj, vt_a, kie, kio, wlf, b, t_len, top_k)
        out_b = _attn_b(proj, vt_b, cb, b, t_len)
        x2, h2 = _merge(x2, out_a, out_b, proj, w_o_a[i].astype(BF16), w_o_b[i].astype(BF16),
                        w_out[i].astype(BF16), g_ffn[i][None, :], tm)
        y2 = _ffn(h2, w_ffn_gate[i], w_ffn_up[i], w_ffn_down[i], tm_wide, FFN_COLS)
        x2 = _ple(x2, y2, p[i].reshape(m, D_PLE), g_ple[i][None, :], g_final[None, :],
                  w_ple_gate[i].astype(BF16), w_ple_proj[i].astype(BF16), tm, final_norm=(i + 1 == depth))
    return x2.reshape(b, t_len, d)
```

```python
import functools

import jax
import jax.numpy as jnp
import numpy as np
from jax import lax
from jax.experimental import pallas as pl
from jax.experimental.pallas import tpu as pltpu

F32 = jnp.float32
BF16 = jnp.bfloat16

D_MODEL = 2048
HEAD_DIM = 128
HA = 8
HA_KV = 2
H_IDX = 16
D_IDX = 64
TOPK_MAX = 256
HB = 8
ROPE_THETA = 500000.0
ROT_A = HEAD_DIM // 4
ROT_IDX = D_IDX // 4
Q_BLOCK = 128
D_FF = -(-8 * D_MODEL // (3 * 256)) * 256
D_PLE = 256
EPS = 1e-6

SPLIT_SIZES = (HA * HEAD_DIM, HA_KV * HEAD_DIM, HA_KV * HEAD_DIM, H_IDX * D_IDX, D_IDX, H_IDX,
               HB * HEAD_DIM, HB * HEAD_DIM, HB * HEAD_DIM, HB, D_MODEL, D_MODEL)

OFF_GA = 0
OFF_GB = OFF_GA + D_MODEL
OFF_QA = OFF_GB + D_MODEL
OFF_KA = OFF_QA + HA * HEAD_DIM
OFF_VA = OFF_KA + HA_KV * HEAD_DIM
OFF_QI = OFF_VA + HA_KV * HEAD_DIM
TN_IN = 1024
MXU_N = 256
OFF_QB = -(-(OFF_QI + H_IDX * D_IDX) // TN_IN) * TN_IN
OFF_KB = OFF_QB + HB * HEAD_DIM
OFF_VB = OFF_KB + HB * HEAD_DIM
N_MAIN = OFF_VB + HB * HEAD_DIM
_SPLIT_EDGES = [0] + [int(s) for s in np.cumsum(SPLIT_SIZES)]
_PACK_RUNS = ((OFF_GA, _SPLIT_EDGES[10], 2 * D_MODEL),
              (OFF_QA, _SPLIT_EDGES[0], OFF_QI + H_IDX * D_IDX - OFF_QA),
              (OFF_QB, _SPLIT_EDGES[6], N_MAIN - OFF_QB))
_SEGMENTS = ((OFF_GA, OFF_QA, "gate"), (OFF_QA, OFF_KA, "q_a"), (OFF_KA, OFF_VA, "k_a"),
             (OFF_VA, OFF_QI, "v_a"), (OFF_QI, OFF_QI + H_IDX * D_IDX, "q_i"),
             (OFF_QB, OFF_KB, "q_b"), (OFF_KB, OFF_VB, "k_b"), (OFF_VB, N_MAIN, "v_b"))


def _tile_patterns():
    def group(col):
        for lo, hi, kind in _SEGMENTS:
            if lo <= col < hi:
                assert (col - lo) % MXU_N == 0 and col + MXU_N <= hi
                return kind, (col - lo) // MXU_N
        return None
    by_pattern = {}
    for tile in range(N_MAIN // TN_IN):
        pattern = tuple(group(tile * TN_IN + g * MXU_N) for g in range(TN_IN // MXU_N))
        pattern = tuple(None if grp is None else (grp[0], grp[1] if grp[0] in ("v_a", "v_b") else 0)
                        for grp in pattern)
        by_pattern.setdefault(pattern, []).append(tile)
    return tuple(by_pattern.items())


_TILE_PATTERNS = _tile_patterns()
LANE_WI = D_IDX
LANE_FB = D_IDX + H_IDX

NEG = -1e30
VMEM_LIMIT = 60 * 1024 * 1024
LOG2E = 1.4426950408889634
ATTN_SCALE = HEAD_DIM ** -0.5 * LOG2E
VT_ROWS = HEAD_DIM + 16
TRIG_PACK = 4
HEADS_PER_STEP_B = 4
KEY_CHUNK_A = 512
Q_BLOCK_B = 512
CUMSUM_ROWS = 256
BISECT_FIRST = 18
BISECT_ROUND = 2
BISECT_MAX_ITERS = 48


def _row_tiles(m, t_len):
    return min(1024, m, t_len), min(512, m)


FFN_COLS = 512


def _cparams(sem):
    return pltpu.CompilerParams(dimension_semantics=sem, vmem_limit_bytes=VMEM_LIMIT)


def _rms(x, g):
    return x * lax.rsqrt(jnp.mean(x * x, axis=-1, keepdims=True) + EPS) * g


def _resident(shape):
    nd = len(shape)
    return pl.BlockSpec(shape, lambda *_: (0,) * nd, pipeline_mode=pl.Buffered(1))


def _rope128(a, c, sa, sb, shift):
    outs = []
    for q in range(a.shape[1] // 128):
        aq = a[:, q * 128:(q + 1) * 128]
        outs.append(aq * c + pltpu.roll(aq, 128 - shift, 1) * sa + pltpu.roll(aq, shift, 1) * sb)
    return outs[0] if len(outs) == 1 else jnp.concatenate(outs, axis=1)


def _inproj_kernel(x_ref, pos_ref, g_ref, w_ref, ws_ref, bf_ref, inv_ref,
                   o_ref, kie_ref, kio_ref, wlf_ref, vta_ref, vtb_ref,
                   h_scr, ca_scr, saa_scr, sab_scr, ci_scr, sia_scr, sib_scr):
    j = pl.program_id(1)
    tm = x_ref.shape[0]

    @pl.when(j == 0)
    def _():
        hb = _rms(x_ref[...], g_ref[...]).astype(BF16)
        h_scr[...] = hb
        pos = pos_ref[...].astype(F32)
        lane = lax.broadcasted_iota(jnp.int32, (tm, 128), 1)
        ha, hi = ROT_A // 2, ROT_IDX // 2
        span, rq = 128 // TRIG_PACK, tm // TRIG_PACK
        lane_q = lax.broadcasted_iota(jnp.int32, (rq, 128), 1)
        pos_p = jnp.zeros((rq, 128), F32)
        for q in range(TRIG_PACK):
            pos_p = jnp.where((lane_q >= q * span) & (lane_q < (q + 1) * span), pos[q * rq:(q + 1) * rq, :], pos_p)
        ang = pos_p * inv_ref[...]
        cs_p = jnp.cos(ang)
        sn_p = jnp.sin(ang)
        cs = jnp.concatenate([pltpu.roll(cs_p, (128 - q * span) % 128, 1) for q in range(TRIG_PACK)], axis=0)
        sn = jnp.concatenate([pltpu.roll(sn_p, (128 - q * span) % 128, 1) for q in range(TRIG_PACK)], axis=0)

        def at(tab, src, dst):
            return pltpu.roll(tab, (dst - src) % 128, 1)

        def lanes(lo, n):
            return (lane >= lo) & (lane < lo + n)

        ca_scr[...] = jnp.where(lanes(0, ha), cs, jnp.where(lanes(ha, ha), at(cs, 0, ha), 1.0))
        saa_scr[...] = jnp.where(lanes(0, ha), -sn, 0.0)
        sab_scr[...] = jnp.where(lanes(ha, ha), at(sn, 0, ha), 0.0)
        ci = jnp.ones((tm, 128), F32)
        sia = jnp.zeros((tm, 128), F32)
        sib = jnp.zeros((tm, 128), F32)
        for head0 in (0, D_IDX):
            ci = jnp.where(lanes(head0, hi), at(cs, ha, head0), ci)
            ci = jnp.where(lanes(head0 + hi, hi), at(cs, ha, head0 + hi), ci)
            sia = jnp.where(lanes(head0, hi), -at(sn, ha, head0), sia)
            sib = jnp.where(lanes(head0 + hi, hi), at(sn, ha, head0 + hi), sib)
        ci_scr[...] = ci
        sia_scr[...] = sia
        sib_scr[...] = sib
        small = lax.dot_general(hb, ws_ref[...].astype(BF16), (((1,), (1,)), ((), ())),
                                preferred_element_type=F32)
        kr = _rope128(small, ci, sia, sib, ROT_IDX // 2)
        ke = jnp.where(lane < D_IDX, kr, 0.0)
        kie_ref[...] = ke.astype(BF16)
        kio_ref[...] = pltpu.roll(ke, D_IDX, 1).astype(BF16)
        is_w = (lane >= LANE_WI) & (lane < LANE_WI + H_IDX)
        is_f = (lane >= LANE_FB) & (lane < LANE_FB + HB)
        logf = jax.nn.log_sigmoid(small + bf_ref[...])
        wlf_ref[...] = jnp.where(is_w, small * (H_IDX ** -0.5 * D_IDX ** -0.5), jnp.where(is_f, logf, 0.0))

    t = TN_IN
    groups = t // MXU_N
    heads_per_group = MXU_N // HEAD_DIM

    def project(epilogues):
        for c, epilogue in enumerate(epilogues):
            cols = slice(c * MXU_N, (c + 1) * MXU_N)
            if epilogue is None:
                o_ref[:, cols] = jnp.zeros((tm, MXU_N), BF16)
                continue
            w_t = w_ref[cols, :].astype(BF16)
            acc = lax.dot_general(h_scr[...], w_t, (((1,), (1,)), ((), ())), preferred_element_type=F32)
            o_ref[:, cols] = epilogue(acc).astype(BF16)

    def rope_a(acc):
        return _rope128(acc, ca_scr[...], saa_scr[...], sab_scr[...], ROT_A // 2)

    def rope_i(acc):
        return _rope128(acc, ci_scr[...], sia_scr[...], sib_scr[...], ROT_IDX // 2)

    def values(vt_ref, c):
        def epilogue(acc):
            acc_t = acc.T
            for hh in range(heads_per_group):
                h = (c * heads_per_group + hh) % vt_ref.shape[0]
                vt_ref[h, 0:HEAD_DIM, :] = acc_t[hh * HEAD_DIM:(hh + 1) * HEAD_DIM, :].astype(BF16)
                vt_ref[h, HEAD_DIM:VT_ROWS, :] = jnp.ones((VT_ROWS - HEAD_DIM, tm), BF16)
            return acc
        return epilogue

    def epilogue_of(kind, c):
        return {"gate": lambda: jax.nn.sigmoid,
                "q_a": lambda: (lambda acc: rope_a(acc) * ATTN_SCALE),
                "k_a": lambda: rope_a,
                "v_a": lambda: values(vta_ref, c),
                "q_i": lambda: rope_i,
                "q_b": lambda: (lambda acc: acc * ATTN_SCALE),
                "k_b": lambda: (lambda acc: acc),
                "v_b": lambda: values(vtb_ref, c)}[kind]()

    for pattern, tiles in _TILE_PATTERNS:
        pred = functools.reduce(lambda a, b: a | b, [j == tile for tile in tiles])

        @pl.when(pred)
        def _(pattern=pattern):
            project([None if grp is None else epilogue_of(*grp) for grp in pattern])


def _w_in_row(j):
    row = jnp.int32(0)
    for out0, src0, _ in _PACK_RUNS:
        row = jnp.where(j >= out0 // TN_IN, src0 + TN_IN * (j - out0 // TN_IN), row)
    return pl.multiple_of(row, 8)


def _inproj(x2, pos2, g, w_t, w_small, bf_row, inv_freq, tm, b, t_len):
    m = x2.shape[0]
    for out0, src0, width in _PACK_RUNS:
        assert out0 % TN_IN == 0 and src0 % 8 == 0
        assert src0 + -(-width // TN_IN) * TN_IN <= w_t.shape[0]
    assert list(_PACK_RUNS) == sorted(_PACK_RUNS)
    grid = (m // tm, N_MAIN // TN_IN)
    nt = t_len // tm
    j_vb = OFF_VB // TN_IN
    heads_per_tile = min(TN_IN // HEAD_DIM, HB)
    vb_tiles = HB // heads_per_tile
    row128 = pl.BlockSpec((1, 128), lambda i, j: (0, 0))
    return pl.pallas_call(
        _inproj_kernel,
        grid=grid,
        in_specs=[
            pl.BlockSpec((tm, D_MODEL), lambda i, j: (i, 0)),
            pl.BlockSpec((tm, 1), lambda i, j: (i, 0)),
            pl.BlockSpec((1, D_MODEL), lambda i, j: (0, 0)),
            pl.BlockSpec((pl.Element(TN_IN), pl.Element(D_MODEL)), lambda i, j: (_w_in_row(j), 0)),
            pl.BlockSpec((128, D_MODEL), lambda i, j: (0, 0)),
            row128, row128,
        ],
        out_specs=[
            pl.BlockSpec((tm, TN_IN), lambda i, j: (i, j)),
            pl.BlockSpec((tm, 128), lambda i, j: (i, 0)),
            pl.BlockSpec((tm, 128), lambda i, j: (i, 0)),
            pl.BlockSpec((tm, 128), lambda i, j: (i, 0)),
            pl.BlockSpec((None, HA_KV, VT_ROWS, tm), lambda i, j: (i // nt, 0, 0, i % nt)),
            pl.BlockSpec((None, heads_per_tile, VT_ROWS, tm),
                         lambda i, j: (i // nt, jnp.clip(j - j_vb, 0, vb_tiles - 1), 0, i % nt)),
        ],
        out_shape=[
            jax.ShapeDtypeStruct((m, N_MAIN), BF16),
            jax.ShapeDtypeStruct((m, 128), BF16),
            jax.ShapeDtypeStruct((m, 128), BF16),
            jax.ShapeDtypeStruct((m, 128), F32),
            jax.ShapeDtypeStruct((b, HA_KV, VT_ROWS, t_len), BF16),
            jax.ShapeDtypeStruct((b, HB, VT_ROWS, t_len), BF16),
        ],
        scratch_shapes=[pltpu.VMEM((tm, D_MODEL), BF16)] + [pltpu.VMEM((tm, 128), F32)] * 6,
        compiler_params=_cparams(("arbitrary", "arbitrary")),
        name="inproj",
    )(x2, pos2, g, w_t, w_small, bf_row, inv_freq)


def _cumsum_kernel(wlf_ref, o_ref, carry_scr, *, tc):
    @pl.when(pl.program_id(1) == 0)
    def _():
        carry_scr[...] = jnp.zeros_like(carry_scr)

    lf = wlf_ref[...]
    r = lax.broadcasted_iota(jnp.int32, (tc, tc), 0)
    c = lax.broadcasted_iota(jnp.int32, (tc, tc), 1)
    tri = jnp.where(c <= r, 1.0, 0.0).astype(BF16)
    hi = lf.astype(BF16)
    r1 = lf - hi.astype(F32)
    mid = r1.astype(BF16)
    lo = (r1 - mid.astype(F32)).astype(BF16)
    cs = (jnp.dot(tri, hi, preferred_element_type=F32) + jnp.dot(tri, mid, preferred_element_type=F32)
          + jnp.dot(tri, lo, preferred_element_type=F32)) + carry_scr[0:1, :]
    carry_scr[...] = jnp.broadcast_to(cs[tc - 1:tc, :], carry_scr.shape)
    cs2 = cs * LOG2E
    for grp in range(HB // HEADS_PER_STEP_B):
        o_ref[grp] = pltpu.roll(cs2, (128 - LANE_FB - grp * HEADS_PER_STEP_B) % 128, 1)


def _cumsum(wlf, b, t_len):
    tc = CUMSUM_ROWS
    nt = t_len // tc
    return pl.pallas_call(
        functools.partial(_cumsum_kernel, tc=tc),
        grid=(b, nt),
        in_specs=[pl.BlockSpec((tc, 128), lambda bi, ti: (bi * nt + ti, 0))],
        out_specs=pl.BlockSpec((None, HB // HEADS_PER_STEP_B, tc, 128), lambda bi, ti: (bi, 0, ti, 0)),
        out_shape=jax.ShapeDtypeStruct((b, HB // HEADS_PER_STEP_B, t_len, 128), F32),
        scratch_shapes=[pltpu.VMEM((8, 128), F32)],
        compiler_params=_cparams(("arbitrary", "arbitrary")),
        name="fcumsum",
    )(wlf)


def _attn_a_kernel(qa_ref, qi0_ref, qi1_ref, ka_ref, vt_ref, kie_ref, kio_ref, wlf_ref, o_ref,
                   sc_scr, qs_scr, qg_scr, lt_scr, acc_scr, *, tk, top_k, n_iter):
    i = pl.program_id(1)
    nq = Q_BLOCK
    group = HA // HA_KV
    gw = group * nq
    nchunks = ((i + 1) * nq + tk - 1) // tk
    q_pos = i * nq + lax.broadcasted_iota(jnp.int32, (1, nq), 1)
    wt = wlf_ref[...].T
    half = H_IDX // 4
    for p in range(H_IDX // 2):
        src = qi0_ref if p < half else qi1_ref
        qs_scr[p * nq:(p + 1) * nq, :] = src[:, (p % half) * 128:(p % half + 1) * 128]

    def score_chunk(c, carry):
        rmin, rmax = carry
        off = pl.multiple_of(c * tk, tk)
        ke = kie_ref[pl.ds(off, tk), :]
        ko = kio_ref[pl.ds(off, tk), :]
        s = jnp.zeros((tk, nq), F32)
        pairs_per_dot = MXU_N // nq
        for pg in range(H_IDX // 2 // pairs_per_dot):
            qs = qs_scr[pg * MXU_N:(pg + 1) * MXU_N, :]
            de = lax.dot_general(ke, qs, (((1,), (1,)), ((), ())), preferred_element_type=F32)
            do = lax.dot_general(ko, qs, (((1,), (1,)), ((), ())), preferred_element_type=F32)
            for pp in range(pairs_per_dot):
                p = pg * pairs_per_dot + pp
                cols = slice(pp * nq, (pp + 1) * nq)
                s = s + jnp.maximum(de[:, cols], 0.0) * wt[LANE_WI + 2 * p:LANE_WI + 2 * p + 1, :]
                s = s + jnp.maximum(do[:, cols], 0.0) * wt[LANE_WI + 2 * p + 1:LANE_WI + 2 * p + 2, :]
        kpos = off + lax.broadcasted_iota(jnp.int32, (tk, 1), 0)
        valid = kpos <= q_pos
        sc_scr[pl.ds(off, tk), :] = jnp.where(valid, s, NEG)
        rmin = jnp.minimum(rmin, jnp.min(jnp.where(valid, s, -NEG), axis=0, keepdims=True))
        rmax = jnp.maximum(rmax, jnp.max(jnp.where(valid, s, NEG), axis=0, keepdims=True))
        return rmin, rmax

    lo, hi = lax.fori_loop(0, nchunks, score_chunk,
                           (jnp.full((1, nq), -NEG, F32), jnp.full((1, nq), NEG, F32)))

    def bisect(_, carry):
        lo, hi = carry
        mid = 0.5 * (lo + hi)

        def count_chunk(c, part):
            off = pl.multiple_of(c * tk, tk)
            ind = jnp.where(sc_scr[pl.ds(off, tk), :] >= mid, 1.0, 0.0)
            return part + jnp.sum(ind.reshape(tk // 64, 64, nq), axis=0)

        part = lax.fori_loop(0, nchunks, count_chunk, jnp.zeros((64, nq), F32))
        cnt = jnp.sum(part, axis=0, keepdims=True)
        ge = cnt >= top_k
        return jnp.where(ge, mid, lo), jnp.where(cnt == top_k, mid, jnp.where(ge, hi, mid))

    searching = q_pos >= top_k

    def bisect_round(state):
        it, lo, hi, _ = state
        lo, hi = lax.fori_loop(0, BISECT_ROUND, bisect, (lo, hi))
        pending = jnp.sum(jnp.where(searching & (lo < hi), 1.0, 0.0))
        return it + BISECT_ROUND, lo, hi, pending

    lo, hi = lax.fori_loop(0, BISECT_FIRST, bisect, (lo, hi))
    _, thr, _, _ = lax.while_loop(lambda st: (st[0] < n_iter) & (st[3] > 0.0), bisect_round,
                                  (jnp.int32(BISECT_FIRST), lo, hi, jnp.float32(1.0)))

    tka = tk // 2
    last = (nchunks - 1) * tk
    for n in range(HA_KV):
        for g in range(group):
            h = n * group + g
            qg_scr[n, g * nq:(g + 1) * nq, :] = qa_ref[:, h * HEAD_DIM:(h + 1) * HEAD_DIM]
    acc_scr[...] = jnp.zeros(acc_scr.shape, F32)

    def qk(off, slot):
        msk = jnp.where(sc_scr[pl.ds(off, tka), :] >= thr, 0.0, NEG)
        msk = jnp.concatenate([msk] * group, axis=1)
        cmax = []
        for n in range(HA_KV):
            k = ka_ref[pl.ds(off, tka), n * HEAD_DIM:(n + 1) * HEAD_DIM]
            s = lax.dot_general(k, qg_scr[n], (((1,), (1,)), ((), ())), preferred_element_type=F32) + msk
            lt_scr[n, slot] = s
            cmax.append(jnp.max(s, axis=0, keepdims=True))
        return tuple(cmax)

    def softmax_pv(off, slot, m_old, cmax):
        new = []
        for n in range(HA_KV):
            m_new = jnp.maximum(m_old[n], cmax[n])
            alpha = jnp.exp2(m_old[n] - m_new)
            p = jnp.exp2(lt_scr[n, slot] - m_new).astype(BF16)
            acc_scr[n] = alpha * acc_scr[n] + jnp.dot(vt_ref[n, :, pl.ds(off, tka)], p,
                                                      preferred_element_type=F32)
            new.append(m_new)
        return tuple(new)

    def pair(j, carry):
        m, cmax0 = carry
        off0 = pl.multiple_of(j * tk, tk)
        off1 = pl.multiple_of(off0 + tka, tka)
        off2 = pl.multiple_of(jnp.minimum(off0 + tk, last), tk)
        cmax1 = qk(off1, 1)
        m = softmax_pv(off0, 0, m, cmax0)
        cmax0 = qk(off2, 0)
        return softmax_pv(off1, 1, m, cmax1), cmax0

    lax.fori_loop(0, nchunks, pair, ((jnp.full((1, gw), NEG, F32),) * HA_KV, qk(0, 0)))
    for n in range(HA_KV):
        out_t = acc_scr[n, 0:HEAD_DIM, :] / acc_scr[n, HEAD_DIM:HEAD_DIM + 1, :]
        for g in range(group):
            h = n * group + g
            o_ref[:, h * HEAD_DIM:(h + 1) * HEAD_DIM] = out_t[:, g * nq:(g + 1) * nq].T.astype(BF16)


def _attn_a(proj, vt_a, kie, kio, wlf, b, t_len, top_k):
    nb = t_len // Q_BLOCK
    wa = HA * HEAD_DIM
    wkv = HA_KV * HEAD_DIM
    wqi = H_IDX * D_IDX // 2
    tk = min(KEY_CHUNK_A, t_len)
    gw = HA // HA_KV * Q_BLOCK
    kern = functools.partial(_attn_a_kernel, tk=tk, top_k=top_k, n_iter=BISECT_MAX_ITERS)
    return pl.pallas_call(
        kern,
        grid=(b, nb),
        in_specs=[
            pl.BlockSpec((Q_BLOCK, wa), lambda bi, i: (bi * nb + i, OFF_QA // wa)),
            pl.BlockSpec((Q_BLOCK, wqi), lambda bi, i: (bi * nb + i, OFF_QI // wqi)),
            pl.BlockSpec((Q_BLOCK, wqi), lambda bi, i: (bi * nb + i, OFF_QI // wqi + 1)),
            pl.BlockSpec((t_len, wkv), lambda bi, i: (bi, OFF_KA // wkv)),
            pl.BlockSpec((None, HA_KV, VT_ROWS, t_len), lambda bi, i: (bi, 0, 0, 0)),
            pl.BlockSpec((t_len, 128), lambda bi, i: (bi, 0)),
            pl.BlockSpec((t_len, 128), lambda bi, i: (bi, 0)),
            pl.BlockSpec((Q_BLOCK, 128), lambda bi, i: (bi * nb + i, 0)),
        ],
        out_specs=pl.BlockSpec((Q_BLOCK, wa), lambda bi, i: (bi * nb + i, 0)),
        out_shape=jax.ShapeDtypeStruct((b * t_len, wa), BF16),
        scratch_shapes=[
            pltpu.VMEM((t_len, Q_BLOCK), F32),
            pltpu.VMEM((H_IDX // 2 * Q_BLOCK, 128), BF16),
            pltpu.VMEM((HA_KV, gw, HEAD_DIM), BF16),
            pltpu.VMEM((HA_KV, 2, tk // 2, gw), F32),
            pltpu.VMEM((HA_KV, VT_ROWS, gw), F32),
        ],
        compiler_params=_cparams(("arbitrary", "arbitrary")),
        name="attn_a",
    )(proj, proj, proj, proj, vt_a, kie, kio, wlf)


def _attn_b_kernel(q_ref, k_ref, vt_ref, cb_ref, o_ref, lt_scr, acc_scr, *, tq):
    qi = pl.program_id(2)
    tk = tq // 2
    heads = HEADS_PER_STEP_B
    q0 = pl.multiple_of(qi * tq, tq)
    acc_scr[...] = jnp.zeros(acc_scr.shape, F32)

    def qk(off, slot, key0=None):
        cmax = []
        for n in range(heads):
            cols = slice(n * HEAD_DIM, (n + 1) * HEAD_DIM)
            bias = (cb_ref[pl.ds(q0, 1), :] - cb_ref[pl.ds(off, tk), :])[:, n:n + 1]
            s = lax.dot_general(k_ref[pl.ds(off, tk), cols], q_ref[:, cols], (((1,), (1,)), ((), ())),
                                preferred_element_type=F32) + bias
            if key0 is not None:
                key = key0 + lax.broadcasted_iota(jnp.int32, (tk, 1), 0)
                qry = lax.broadcasted_iota(jnp.int32, (1, tq), 1)
                s = jnp.where(key <= qry, s, NEG)
            lt_scr[n, slot] = s
            cmax.append(jnp.max(s, axis=0, keepdims=True))
        return tuple(cmax)

    def softmax_pv(off, slot, m_old, cmax):
        new = []
        for n in range(heads):
            m_new = jnp.maximum(m_old[n], cmax[n])
            alpha = jnp.exp2(m_old[n] - m_new)
            p = jnp.exp2(lt_scr[n, slot] - m_new).astype(BF16)
            acc_scr[n] = alpha * acc_scr[n] + jnp.dot(vt_ref[n, :, pl.ds(off, tk)], p,
                                                      preferred_element_type=F32)
            new.append(m_new)
        return tuple(new)

    q1 = pl.multiple_of(q0 + tk, tk)
    m = (jnp.full((1, tq), NEG, F32),) * heads
    cmax0 = qk(q0, 0, key0=0)
    cmax1 = qk(q1, 1, key0=tk)
    m = softmax_pv(q0, 0, m, cmax0)
    cmax0 = qk(0, 0)
    m = softmax_pv(q1, 1, m, cmax1)
    last = jnp.maximum(qi - 1, 0) * tq

    def pair(j, carry):
        m, cmax0 = carry
        off0 = pl.multiple_of(j * tq, tq)
        off1 = pl.multiple_of(off0 + tk, tk)
        off2 = pl.multiple_of(jnp.minimum(off0 + tq, last), tq)
        cmax1 = qk(off1, 1)
        m = softmax_pv(off0, 0, m, cmax0)
        cmax0 = qk(off2, 0)
        return softmax_pv(off1, 1, m, cmax1), cmax0

    lax.fori_loop(0, qi, pair, (m, cmax0))
    for n in range(heads):
        out_t = acc_scr[n, 0:HEAD_DIM, :] / acc_scr[n, HEAD_DIM:HEAD_DIM + 1, :]
        for g in range(tq // 128):
            o_ref[g * 128:(g + 1) * 128, n * HEAD_DIM:(n + 1) * HEAD_DIM] = (
                out_t[:, g * 128:(g + 1) * 128].T.astype(BF16))


def _attn_b(proj, vt_b, cb, b, t_len):
    tq = Q_BLOCK_B
    nq = t_len // tq
    hps = HEADS_PER_STEP_B
    wh = hps * HEAD_DIM
    return pl.pallas_call(
        functools.partial(_attn_b_kernel, tq=tq),
        grid=(b, HB // hps, nq),
        in_specs=[
            pl.BlockSpec((tq, wh), lambda bi, h, i: (bi * nq + i, OFF_QB // wh + h)),
            pl.BlockSpec((t_len, wh), lambda bi, h, i: (bi, OFF_KB // wh + h)),
            pl.BlockSpec((None, hps, VT_ROWS, t_len), lambda bi, h, i: (bi, h, 0, 0)),
            pl.BlockSpec((None, None, t_len, 128), lambda bi, h, i: (bi, h, 0, 0)),
        ],
        out_specs=pl.BlockSpec((tq, wh), lambda bi, h, i: (bi * nq + i, h)),
        out_shape=jax.ShapeDtypeStruct((b * t_len, HB * HEAD_DIM), BF16),
        scratch_shapes=[pltpu.VMEM((hps, 2, tq // 2, tq), F32), pltpu.VMEM((hps, VT_ROWS, tq), F32)],
        compiler_params=_cparams(("arbitrary", "arbitrary", "arbitrary")),
        name="attn_b",
    )(proj, proj, vt_b, cb)


def _merge_kernel(x_ref, oa_ref, ob_ref, ga_ref, gb_ref, woa_ref, wob_ref, wout_ref, gn_ref, o_ref, h_ref):
    ya = jnp.dot(oa_ref[...], woa_ref[...], preferred_element_type=F32)
    yb = jnp.dot(ob_ref[...], wob_ref[...], preferred_element_type=F32)
    mixed = ga_ref[...].astype(F32) * ya + gb_ref[...].astype(F32) * yb
    x1 = x_ref[...] + jnp.dot(mixed.astype(BF16), wout_ref[...], preferred_element_type=F32)
    o_ref[...] = x1
    h_ref[...] = _rms(x1, gn_ref[...]).astype(BF16)


def _merge(x2, out_a, out_b, proj, w_oa, w_ob, w_out, g_next, tm):
    m = x2.shape[0]
    wa = HA * HEAD_DIM
    wb = HB * HEAD_DIM
    return pl.pallas_call(
        _merge_kernel,
        grid=(m // tm,),
        in_specs=[
            pl.BlockSpec((tm, D_MODEL), lambda i: (i, 0)),
            pl.BlockSpec((tm, wa), lambda i: (i, 0)),
            pl.BlockSpec((tm, wb), lambda i: (i, 0)),
            pl.BlockSpec((tm, D_MODEL), lambda i: (i, OFF_GA // D_MODEL)),
            pl.BlockSpec((tm, D_MODEL), lambda i: (i, OFF_GB // D_MODEL)),
            _resident((wa, D_MODEL)),
            _resident((wb, D_MODEL)),
            _resident((D_MODEL, D_MODEL)),
            pl.BlockSpec((1, D_MODEL), lambda i: (0, 0)),
        ],
        out_specs=[pl.BlockSpec((tm, D_MODEL), lambda i: (i, 0)), pl.BlockSpec((tm, D_MODEL), lambda i: (i, 0))],
        out_shape=[jax.ShapeDtypeStruct((m, D_MODEL), F32), jax.ShapeDtypeStruct((m, D_MODEL), BF16)],
        compiler_params=_cparams(("arbitrary",)),
        name="merge",
    )(x2, out_a, out_b, proj, proj, w_oa, w_ob, w_out, g_next)


def _ffn_kernel(h_ref, wg_ref, wu_ref, wd_ref, o_ref):
    @pl.when(pl.program_id(1) == 0)
    def _():
        o_ref[...] = jnp.zeros(o_ref.shape, F32)

    h = h_ref[...]
    gate = jnp.dot(h, wg_ref[...].astype(BF16), preferred_element_type=F32)
    up = jnp.dot(h, wu_ref[...].astype(BF16), preferred_element_type=F32)
    act = (jax.nn.silu(gate) * up).astype(BF16)
    o_ref[...] += jnp.dot(act, wd_ref[...].astype(BF16), preferred_element_type=F32)


def _ffn(h, w_gate, w_up, w_down, tm, tf):
    m = h.shape[0]
    return pl.pallas_call(
        _ffn_kernel,
        grid=(m // tm, D_FF // tf),
        in_specs=[
            pl.BlockSpec((tm, D_MODEL), lambda i, f: (i, 0)),
            pl.BlockSpec((D_MODEL, tf), lambda i, f: (0, f)),
            pl.BlockSpec((D_MODEL, tf), lambda i, f: (0, f)),
            pl.BlockSpec((tf, D_MODEL), lambda i, f: (f, 0)),
        ],
        out_specs=pl.BlockSpec((tm, D_MODEL), lambda i, f: (i, 0)),
        out_shape=jax.ShapeDtypeStruct((m, D_MODEL), F32),
        compiler_params=_cparams(("arbitrary", "arbitrary")),
        name="ffn",
    )(h, w_gate, w_up, w_down)


def _ple_kernel(x_ref, y_ref, p_ref, g_ref, gf_ref, wg_ref, wp_ref, o_ref, *, final_norm):
    x = x_ref[...] + y_ref[...]
    h = _rms(x, g_ref[...]).astype(BF16)
    gate = jax.nn.sigmoid(jnp.dot(h, wg_ref[...], preferred_element_type=F32))
    emb = jnp.dot(p_ref[...].astype(BF16), wp_ref[...], preferred_element_type=F32)
    y = x + gate * emb
    o_ref[...] = _rms(y, gf_ref[...]) if final_norm else y


def _ple(x2, y2, p2, g, g_final, w_gate, w_proj, tm, final_norm):
    m = x2.shape[0]
    return pl.pallas_call(
        functools.partial(_ple_kernel, final_norm=final_norm),
        grid=(m // tm,),
        in_specs=[
            pl.BlockSpec((tm, D_MODEL), lambda i: (i, 0)),
            pl.BlockSpec((tm, D_MODEL), lambda i: (i, 0)),
            pl.BlockSpec((tm, D_PLE), lambda i: (i, 0)),
            pl.BlockSpec((1, D_MODEL), lambda i: (0, 0)),
            pl.BlockSpec((1, D_MODEL), lambda i: (0, 0)),
            _resident((D_MODEL, D_MODEL)),
            _resident((D_PLE, D_MODEL)),
        ],
        out_specs=pl.BlockSpec((tm, D_MODEL), lambda i: (i, 0)),
        out_shape=jax.ShapeDtypeStruct((m, D_MODEL), F32),
        compiler_params=_cparams(("arbitrary",)),
        name="ple",
    )(x2, y2, p2, g, g_final, w_gate, w_proj)


def _rope_lane_freqs():
    def inv(rot):
        half = rot // 2
        return ROPE_THETA ** (-jnp.arange(half, dtype=F32) / half)
    span = 128 // TRIG_PACK
    used = ROT_A // 2 + ROT_IDX // 2
    assert used <= span
    one = jnp.concatenate([inv(ROT_A), inv(ROT_IDX), jnp.zeros((span - used,), F32)])
    return jnp.tile(one, TRIG_PACK)[None, :]


def _w_in_views(w):
    w_t = jnp.swapaxes(w, 0, 1)
    rows = [w_t[_SPLIT_EDGES[k]:_SPLIT_EDGES[k + 1]] for k in (4, 5, 9)]
    pad = jnp.zeros((128 - D_IDX - H_IDX - HB, w_t.shape[1]), w_t.dtype)
    return w_t, jnp.concatenate(rows + [pad], axis=0)


def kernel(x, p, positions, g_mix, w_in, b_f, w_o_a, w_o_b, w_out, g_ffn, w_ffn_gate, w_ffn_up,
           w_ffn_down, g_ple, w_ple_gate, w_ple_proj, g_final):
    b, t_len, d = x.shape
    depth = w_in.shape[0]
    m = b * t_len
    top_k = min(TOPK_MAX, t_len // 4)
    inv_freq = _rope_lane_freqs()
    pos2 = positions.reshape(m, 1)
    x2 = x.reshape(m, d)
    tm_wide, tm = _row_tiles(m, t_len)
    for i in range(depth):
        w_t, w_small = _w_in_views(w_in[i])
        bf_row = jnp.zeros((1, 128), F32).at[0, LANE_FB:LANE_FB + HB].set(b_f[i].astype(F32))
        proj, kie, kio, wlf, vt_a, vt_b = _inproj(x2, pos2, g_mix[i][None, :], w_t, w_small, bf_row,
                                                  inv_freq, tm_wide, b, t_len)
        cb = _cumsum(wlf, b, t_len)
        out_a = _attn_a(proj, vt_a, kie, kio, wlf, b, t_len, top_k)
        out_b = _attn_b(proj, vt_b, cb, b, t_len)
        x2, h2 = _merge(x2, out_a, out_b, proj, w_o_a[i].astype(BF16), w_o_b[i].astype(BF16),
                        w_out[i].astype(BF16), g_ffn[i][None, :], tm)
        y2 = _ffn(h2, w_ffn_gate[i], w_ffn_up[i], w_ffn_down[i], tm_wide, FFN_COLS)
        x2 = _ple(x2, y2, p[i].reshape(m, D_PLE), g_ple[i][None, :], g_final[None, :],
                  w_ple_gate[i].astype(BF16), w_ple_proj[i].astype(BF16), tm, final_norm=(i + 1 == depth))
    return x2.reshape(b, t_len, d)
```

```python
import functools

import jax
import jax.numpy as jnp
import numpy as np
from jax import lax
from jax.experimental import pallas as pl
from jax.experimental.pallas import tpu as pltpu

F32 = jnp.float32
BF16 = jnp.bfloat16

D_MODEL = 2048
HEAD_DIM = 128
HA = 8
HA_KV = 2
H_IDX = 16
D_IDX = 64
TOPK_MAX = 256
HB = 8
ROPE_THETA = 500000.0
ROT_A = HEAD_DIM // 4
ROT_IDX = D_IDX // 4
Q_BLOCK = 128
D_FF = -(-8 * D_MODEL // (3 * 256)) * 256
D_PLE = 256
EPS = 1e-6

SPLIT_SIZES = (HA * HEAD_DIM, HA_KV * HEAD_DIM, HA_KV * HEAD_DIM, H_IDX * D_IDX, D_IDX, H_IDX,
               HB * HEAD_DIM, HB * HEAD_DIM, HB * HEAD_DIM, HB, D_MODEL, D_MODEL)

OFF_GA = 0
OFF_GB = OFF_GA + D_MODEL
OFF_QA = OFF_GB + D_MODEL
OFF_KA = OFF_QA + HA * HEAD_DIM
OFF_VA = OFF_KA + HA_KV * HEAD_DIM
OFF_QI = OFF_VA + HA_KV * HEAD_DIM
TN_IN = 1024
MXU_N = 256
OFF_QB = -(-(OFF_QI + H_IDX * D_IDX) // TN_IN) * TN_IN
OFF_KB = OFF_QB + HB * HEAD_DIM
OFF_VB = OFF_KB + HB * HEAD_DIM
N_MAIN = OFF_VB + HB * HEAD_DIM
_SPLIT_EDGES = [0] + [int(s) for s in np.cumsum(SPLIT_SIZES)]
_PACK_RUNS = ((OFF_GA, _SPLIT_EDGES[10], 2 * D_MODEL),
              (OFF_QA, _SPLIT_EDGES[0], OFF_QI + H_IDX * D_IDX - OFF_QA),
              (OFF_QB, _SPLIT_EDGES[6], N_MAIN - OFF_QB))
_SEGMENTS = ((OFF_GA, OFF_QA, "gate"), (OFF_QA, OFF_KA, "q_a"), (OFF_KA, OFF_VA, "k_a"),
             (OFF_VA, OFF_QI, "v_a"), (OFF_QI, OFF_QI + H_IDX * D_IDX, "q_i"),
             (OFF_QB, OFF_KB, "q_b"), (OFF_KB, OFF_VB, "k_b"), (OFF_VB, N_MAIN, "v_b"))


def _tile_patterns():
    def group(col):
        for lo, hi, kind in _SEGMENTS:
            if lo <= col < hi:
                assert (col - lo) % MXU_N == 0 and col + MXU_N <= hi
                return kind, (col - lo) // MXU_N
        return None
    by_pattern = {}
    for tile in range(N_MAIN // TN_IN):
        pattern = tuple(group(tile * TN_IN + g * MXU_N) for g in range(TN_IN // MXU_N))
        pattern = tuple(None if grp is None else (grp[0], grp[1] if grp[0] in ("v_a", "v_b") else 0)
                        for grp in pattern)
        by_pattern.setdefault(pattern, []).append(tile)
    return tuple(by_pattern.items())


_TILE_PATTERNS = _tile_patterns()
LANE_WI = D_IDX
LANE_FB = D_IDX + H_IDX

NEG = -1e30
VMEM_LIMIT = 60 * 1024 * 1024
LOG2E = 1.4426950408889634
ATTN_SCALE = HEAD_DIM ** -0.5 * LOG2E
VT_ROWS = HEAD_DIM + 16
TRIG_PACK = 4
HEADS_PER_STEP_B = 4
KEY_CHUNK_A = 512
Q_BLOCK_B = 512
CUMSUM_ROWS = 256
BISECT_FIRST = 16
BISECT_ROUND = 2
BISECT_MAX_ITERS = 48


def _row_tiles(m, t_len):
    return min(1024, m, t_len), min(512, m)


FFN_COLS = 512


def _cparams(sem):
    return pltpu.CompilerParams(dimension_semantics=sem, vmem_limit_bytes=VMEM_LIMIT)


def _rms(x, g):
    return x * lax.rsqrt(jnp.mean(x * x, axis=-1, keepdims=True) + EPS) * g


def _resident(shape):
    nd = len(shape)
    return pl.BlockSpec(shape, lambda *_: (0,) * nd, pipeline_mode=pl.Buffered(1))


def _rope128(a, c, sa, sb, shift):
    outs = []
    for q in range(a.shape[1] // 128):
        aq = a[:, q * 128:(q + 1) * 128]
        outs.append(aq * c + pltpu.roll(aq, 128 - shift, 1) * sa + pltpu.roll(aq, shift, 1) * sb)
    return outs[0] if len(outs) == 1 else jnp.concatenate(outs, axis=1)


def _inproj_kernel(x_ref, pos_ref, g_ref, w_ref, ws_ref, bf_ref, inv_ref,
                   o_ref, kie_ref, kio_ref, wlf_ref, vta_ref, vtb_ref,
                   h_scr, ca_scr, saa_scr, sab_scr, ci_scr, sia_scr, sib_scr):
    j = pl.program_id(1)
    tm = x_ref.shape[0]

    @pl.when(j == 0)
    def _():
        hb = _rms(x_ref[...], g_ref[...]).astype(BF16)
        h_scr[...] = hb
        pos = pos_ref[...].astype(F32)
        lane = lax.broadcasted_iota(jnp.int32, (tm, 128), 1)
        ha, hi = ROT_A // 2, ROT_IDX // 2
        span, rq = 128 // TRIG_PACK, tm // TRIG_PACK
        lane_q = lax.broadcasted_iota(jnp.int32, (rq, 128), 1)
        pos_p = jnp.zeros((rq, 128), F32)
        for q in range(TRIG_PACK):
            pos_p = jnp.where((lane_q >= q * span) & (lane_q < (q + 1) * span), pos[q * rq:(q + 1) * rq, :], pos_p)
        ang = pos_p * inv_ref[...]
        cs_p = jnp.cos(ang)
        sn_p = jnp.sin(ang)
        cs = jnp.concatenate([pltpu.roll(cs_p, (128 - q * span) % 128, 1) for q in range(TRIG_PACK)], axis=0)
        sn = jnp.concatenate([pltpu.roll(sn_p, (128 - q * span) % 128, 1) for q in range(TRIG_PACK)], axis=0)

        def at(tab, src, dst):
            return pltpu.roll(tab, (dst - src) % 128, 1)

        def lanes(lo, n):
            return (lane >= lo) & (lane < lo + n)

        ca_scr[...] = jnp.where(lanes(0, ha), cs, jnp.where(lanes(ha, ha), at(cs, 0, ha), 1.0))
        saa_scr[...] = jnp.where(lanes(0, ha), -sn, 0.0)
        sab_scr[...] = jnp.where(lanes(ha, ha), at(sn, 0, ha), 0.0)
        ci = jnp.ones((tm, 128), F32)
        sia = jnp.zeros((tm, 128), F32)
        sib = jnp.zeros((tm, 128), F32)
        for head0 in (0, D_IDX):
            ci = jnp.where(lanes(head0, hi), at(cs, ha, head0), ci)
            ci = jnp.where(lanes(head0 + hi, hi), at(cs, ha, head0 + hi), ci)
            sia = jnp.where(lanes(head0, hi), -at(sn, ha, head0), sia)
            sib = jnp.where(lanes(head0 + hi, hi), at(sn, ha, head0 + hi), sib)
        ci_scr[...] = ci
        sia_scr[...] = sia
        sib_scr[...] = sib
        small = lax.dot_general(hb, ws_ref[...].astype(BF16), (((1,), (1,)), ((), ())),
                                preferred_element_type=F32)
        kr = _rope128(small, ci, sia, sib, ROT_IDX // 2)
        ke = jnp.where(lane < D_IDX, kr, 0.0)
        kie_ref[...] = ke.astype(BF16)
        kio_ref[...] = pltpu.roll(ke, D_IDX, 1).astype(BF16)
        is_w = (lane >= LANE_WI) & (lane < LANE_WI + H_IDX)
        is_f = (lane >= LANE_FB) & (lane < LANE_FB + HB)
        logf = jax.nn.log_sigmoid(small + bf_ref[...])
        wlf_ref[...] = jnp.where(is_w, small * (H_IDX ** -0.5 * D_IDX ** -0.5), jnp.where(is_f, logf, 0.0))

    t = TN_IN
    groups = t // MXU_N
    heads_per_group = MXU_N // HEAD_DIM

    def project(epilogues):
        for c, epilogue in enumerate(epilogues):
            cols = slice(c * MXU_N, (c + 1) * MXU_N)
            if epilogue is None:
                o_ref[:, cols] = jnp.zeros((tm, MXU_N), BF16)
                continue
            w_t = w_ref[cols, :].astype(BF16)
            acc = lax.dot_general(h_scr[...], w_t, (((1,), (1,)), ((), ())), preferred_element_type=F32)
            o_ref[:, cols] = epilogue(acc).astype(BF16)

    def rope_a(acc):
        return _rope128(acc, ca_scr[...], saa_scr[...], sab_scr[...], ROT_A // 2)

    def rope_i(acc):
        return _rope128(acc, ci_scr[...], sia_scr[...], sib_scr[...], ROT_IDX // 2)

    def values(vt_ref, c):
        def epilogue(acc):
            acc_t = acc.T
            for hh in range(heads_per_group):
                h = (c * heads_per_group + hh) % vt_ref.shape[0]
                vt_ref[h, 0:HEAD_DIM, :] = acc_t[hh * HEAD_DIM:(hh + 1) * HEAD_DIM, :].astype(BF16)
                vt_ref[h, HEAD_DIM:VT_ROWS, :] = jnp.ones((VT_ROWS - HEAD_DIM, tm), BF16)
            return acc
        return epilogue

    def epilogue_of(kind, c):
        return {"gate": lambda: jax.nn.sigmoid,
                "q_a": lambda: (lambda acc: rope_a(acc) * ATTN_SCALE),
                "k_a": lambda: rope_a,
                "v_a": lambda: values(vta_ref, c),
                "q_i": lambda: rope_i,
                "q_b": lambda: (lambda acc: acc * ATTN_SCALE),
                "k_b": lambda: (lambda acc: acc),
                "v_b": lambda: values(vtb_ref, c)}[kind]()

    for pattern, tiles in _TILE_PATTERNS:
        pred = functools.reduce(lambda a, b: a | b, [j == tile for tile in tiles])

        @pl.when(pred)
        def _(pattern=pattern):
            project([None if grp is None else epilogue_of(*grp) for grp in pattern])


def _w_in_row(j):
    row = jnp.int32(0)
    for out0, src0, _ in _PACK_RUNS:
        row = jnp.where(j >= out0 // TN_IN, src0 + TN_IN * (j - out0 // TN_IN), row)
    return pl.multiple_of(row, 8)


def _inproj(x2, pos2, g, w_t, w_small, bf_row, inv_freq, tm, b, t_len):
    m = x2.shape[0]
    for out0, src0, width in _PACK_RUNS:
        assert out0 % TN_IN == 0 and src0 % 8 == 0
        assert src0 + -(-width // TN_IN) * TN_IN <= w_t.shape[0]
    assert list(_PACK_RUNS) == sorted(_PACK_RUNS)
    grid = (m // tm, N_MAIN // TN_IN)
    nt = t_len // tm
    j_vb = OFF_VB // TN_IN
    heads_per_tile = min(TN_IN // HEAD_DIM, HB)
    vb_tiles = HB // heads_per_tile
    row128 = pl.BlockSpec((1, 128), lambda i, j: (0, 0))
    return pl.pallas_call(
        _inproj_kernel,
        grid=grid,
        in_specs=[
            pl.BlockSpec((tm, D_MODEL), lambda i, j: (i, 0)),
            pl.BlockSpec((tm, 1), lambda i, j: (i, 0)),
            pl.BlockSpec((1, D_MODEL), lambda i, j: (0, 0)),
            pl.BlockSpec((pl.Element(TN_IN), pl.Element(D_MODEL)), lambda i, j: (_w_in_row(j), 0)),
            pl.BlockSpec((128, D_MODEL), lambda i, j: (0, 0)),
            row128, row128,
        ],
        out_specs=[
            pl.BlockSpec((tm, TN_IN), lambda i, j: (i, j)),
            pl.BlockSpec((tm, 128), lambda i, j: (i, 0)),
            pl.BlockSpec((tm, 128), lambda i, j: (i, 0)),
            pl.BlockSpec((tm, 128), lambda i, j: (i, 0)),
            pl.BlockSpec((None, HA_KV, VT_ROWS, tm), lambda i, j: (i // nt, 0, 0, i % nt)),
            pl.BlockSpec((None, heads_per_tile, VT_ROWS, tm),
                         lambda i, j: (i // nt, jnp.clip(j - j_vb, 0, vb_tiles - 1), 0, i % nt)),
        ],
        out_shape=[
            jax.ShapeDtypeStruct((m, N_MAIN), BF16),
            jax.ShapeDtypeStruct((m, 128), BF16),
            jax.ShapeDtypeStruct((m, 128), BF16),
            jax.ShapeDtypeStruct((m, 128), F32),
            jax.ShapeDtypeStruct((b, HA_KV, VT_ROWS, t_len), BF16),
            jax.ShapeDtypeStruct((b, HB, VT_ROWS, t_len), BF16),
        ],
        scratch_shapes=[pltpu.VMEM((tm, D_MODEL), BF16)] + [pltpu.VMEM((tm, 128), F32)] * 6,
        compiler_params=_cparams(("arbitrary", "arbitrary")),
        name="inproj",
    )(x2, pos2, g, w_t, w_small, bf_row, inv_freq)


def _attn_a_kernel(qa_ref, qi0_ref, qi1_ref, ka_ref, vt_ref, kie_ref, kio_ref, wlf_ref, o_ref,
                   sc_scr, qs_scr, qg_scr, lt_scr, acc_scr, *, tk, top_k, n_iter):
    i = pl.program_id(1)
    nq = Q_BLOCK
    group = HA // HA_KV
    gw = group * nq
    nchunks = ((i + 1) * nq + tk - 1) // tk
    q_pos = i * nq + lax.broadcasted_iota(jnp.int32, (1, nq), 1)
    wt = wlf_ref[...].T
    half = H_IDX // 4
    for p in range(H_IDX // 2):
        src = qi0_ref if p < half else qi1_ref
        qs_scr[p * nq:(p + 1) * nq, :] = src[:, (p % half) * 128:(p % half + 1) * 128]

    def score_chunk(c, carry):
        rmin, rmax = carry
        off = pl.multiple_of(c * tk, tk)
        ke = kie_ref[pl.ds(off, tk), :]
        ko = kio_ref[pl.ds(off, tk), :]
        s = jnp.zeros((tk, nq), F32)
        pairs_per_dot = MXU_N // nq
        for pg in range(H_IDX // 2 // pairs_per_dot):
            qs = qs_scr[pg * MXU_N:(pg + 1) * MXU_N, :]
            de = lax.dot_general(ke, qs, (((1,), (1,)), ((), ())), preferred_element_type=F32)
            do = lax.dot_general(ko, qs, (((1,), (1,)), ((), ())), preferred_element_type=F32)
            for pp in range(pairs_per_dot):
                p = pg * pairs_per_dot + pp
                cols = slice(pp * nq, (pp + 1) * nq)
                s = s + jnp.maximum(de[:, cols], 0.0) * wt[LANE_WI + 2 * p:LANE_WI + 2 * p + 1, :]
                s = s + jnp.maximum(do[:, cols], 0.0) * wt[LANE_WI + 2 * p + 1:LANE_WI + 2 * p + 2, :]
        kpos = off + lax.broadcasted_iota(jnp.int32, (tk, 1), 0)
        valid = kpos <= q_pos
        sc_scr[pl.ds(off, tk), :] = jnp.where(valid, s, NEG)
        rmin = jnp.minimum(rmin, jnp.min(jnp.where(valid, s, -NEG), axis=0, keepdims=True))
        rmax = jnp.maximum(rmax, jnp.max(jnp.where(valid, s, NEG), axis=0, keepdims=True))
        return rmin, rmax

    lo, hi = lax.fori_loop(0, nchunks, score_chunk,
                           (jnp.full((1, nq), -NEG, F32), jnp.full((1, nq), NEG, F32)))

    def bisect(_, carry):
        lo, hi = carry
        mid = 0.5 * (lo + hi)

        def count_chunk(c, part):
            off = pl.multiple_of(c * tk, tk)
            ind = jnp.where(sc_scr[pl.ds(off, tk), :] >= mid, 1.0, 0.0)
            return part + jnp.sum(ind.reshape(tk // 64, 64, nq), axis=0)

        part = lax.fori_loop(0, nchunks, count_chunk, jnp.zeros((64, nq), F32))
        cnt = jnp.sum(part, axis=0, keepdims=True)
        ge = cnt >= top_k
        return jnp.where(ge, mid, lo), jnp.where(cnt == top_k, mid, jnp.where(ge, hi, mid))

    searching = q_pos >= top_k

    def bisect_round(state):
        it, lo, hi, _ = state
        lo, hi = lax.fori_loop(0, BISECT_ROUND, bisect, (lo, hi))
        pending = jnp.sum(jnp.where(searching & (lo < hi), 1.0, 0.0))
        return it + BISECT_ROUND, lo, hi, pending

    lo, hi = lax.fori_loop(0, BISECT_FIRST, bisect, (lo, hi))
    _, thr, _, _ = lax.while_loop(lambda st: (st[0] < n_iter) & (st[3] > 0.0), bisect_round,
                                  (jnp.int32(BISECT_FIRST), lo, hi, jnp.float32(1.0)))

    tka = tk // 2
    last = (nchunks - 1) * tk
    for n in range(HA_KV):
        for g in range(group):
            h = n * group + g
            qg_scr[n, g * nq:(g + 1) * nq, :] = qa_ref[:, h * HEAD_DIM:(h + 1) * HEAD_DIM]
    acc_scr[...] = jnp.zeros(acc_scr.shape, F32)

    def qk(off, slot):
        msk = jnp.where(sc_scr[pl.ds(off, tka), :] >= thr, 0.0, NEG)
        msk = jnp.concatenate([msk] * group, axis=1)
        cmax = []
        for n in range(HA_KV):
            k = ka_ref[pl.ds(off, tka), n * HEAD_DIM:(n + 1) * HEAD_DIM]
            s = lax.dot_general(k, qg_scr[n], (((1,), (1,)), ((), ())), preferred_element_type=F32) + msk
            lt_scr[n, slot] = s
            cmax.append(jnp.max(s, axis=0, keepdims=True))
        return tuple(cmax)

    def softmax_pv(off, slot, m_old, cmax):
        new = []
        for n in range(HA_KV):
            m_new = jnp.maximum(m_old[n], cmax[n])
            alpha = jnp.exp2(m_old[n] - m_new)
            p = jnp.exp2(lt_scr[n, slot] - m_new).astype(BF16)
            acc_scr[n] = alpha * acc_scr[n] + jnp.dot(vt_ref[n, :, pl.ds(off, tka)], p,
                                                      preferred_element_type=F32)
            new.append(m_new)
        return tuple(new)

    def pair(j, carry):
        m, cmax0 = carry
        off0 = pl.multiple_of(j * tk, tk)
        off1 = pl.multiple_of(off0 + tka, tka)
        off2 = pl.multiple_of(jnp.minimum(off0 + tk, last), tk)
        cmax1 = qk(off1, 1)
        m = softmax_pv(off0, 0, m, cmax0)
        cmax0 = qk(off2, 0)
        return softmax_pv(off1, 1, m, cmax1), cmax0

    lax.fori_loop(0, nchunks, pair, ((jnp.full((1, gw), NEG, F32),) * HA_KV, qk(0, 0)))
    for n in range(HA_KV):
        out_t = acc_scr[n, 0:HEAD_DIM, :] / acc_scr[n, HEAD_DIM:HEAD_DIM + 1, :]
        for g in range(group):
            h = n * group + g
            o_ref[:, h * HEAD_DIM:(h + 1) * HEAD_DIM] = out_t[:, g * nq:(g + 1) * nq].T.astype(BF16)


def _attn_a(proj, vt_a, kie, kio, wlf, b, t_len, top_k):
    nb = t_len // Q_BLOCK
    wa = HA * HEAD_DIM
    wkv = HA_KV * HEAD_DIM
    wqi = H_IDX * D_IDX // 2
    tk = min(KEY_CHUNK_A, t_len)
    gw = HA // HA_KV * Q_BLOCK
    kern = functools.partial(_attn_a_kernel, tk=tk, top_k=top_k, n_iter=BISECT_MAX_ITERS)
    return pl.pallas_call(
        kern,
        grid=(b, nb),
        in_specs=[
            pl.BlockSpec((Q_BLOCK, wa), lambda bi, i: (bi * nb + i, OFF_QA // wa)),
            pl.BlockSpec((Q_BLOCK, wqi), lambda bi, i: (bi * nb + i, OFF_QI // wqi)),
            pl.BlockSpec((Q_BLOCK, wqi), lambda bi, i: (bi * nb + i, OFF_QI // wqi + 1)),
            pl.BlockSpec((t_len, wkv), lambda bi, i: (bi, OFF_KA // wkv)),
            pl.BlockSpec((None, HA_KV, VT_ROWS, t_len), lambda bi, i: (bi, 0, 0, 0)),
            pl.BlockSpec((t_len, 128), lambda bi, i: (bi, 0)),
            pl.BlockSpec((t_len, 128), lambda bi, i: (bi, 0)),
            pl.BlockSpec((Q_BLOCK, 128), lambda bi, i: (bi * nb + i, 0)),
        ],
        out_specs=pl.BlockSpec((Q_BLOCK, wa), lambda bi, i: (bi * nb + i, 0)),
        out_shape=jax.ShapeDtypeStruct((b * t_len, wa), BF16),
        scratch_shapes=[
            pltpu.VMEM((t_len, Q_BLOCK), F32),
            pltpu.VMEM((H_IDX // 2 * Q_BLOCK, 128), BF16),
            pltpu.VMEM((HA_KV, gw, HEAD_DIM), BF16),
            pltpu.VMEM((HA_KV, 2, tk // 2, gw), F32),
            pltpu.VMEM((HA_KV, VT_ROWS, gw), F32),
        ],
        compiler_params=_cparams(("arbitrary", "arbitrary")),
        name="attn_a",
    )(proj, proj, proj, proj, vt_a, kie, kio, wlf)


def _attn_b_kernel(q_ref, k_ref, vt_ref, wlf_ref, o_ref, lt_scr, acc_scr, cb_ref, *, tq):
    qi = pl.program_id(2)
    tk = tq // 2
    heads = HEADS_PER_STEP_B
    q0 = pl.multiple_of(qi * tq, tq)
    acc_scr[...] = jnp.zeros(acc_scr.shape, F32)

    @pl.when(qi == 0)
    def _():
        tc = CUMSUM_ROWS
        r = lax.broadcasted_iota(jnp.int32, (tc, tc), 0)
        c = lax.broadcasted_iota(jnp.int32, (tc, tc), 1)
        tri = jnp.where(c <= r, 1.0, 0.0).astype(BF16)
        grp = pl.program_id(1)

        def block(t, carry):
            rows = pl.ds(pl.multiple_of(t * tc, tc), tc)
            lf = wlf_ref[rows, :]
            hi = lf.astype(BF16)
            r1 = lf - hi.astype(F32)
            mid = r1.astype(BF16)
            lo = (r1 - mid.astype(F32)).astype(BF16)
            cs = (jnp.dot(tri, hi, preferred_element_type=F32) + jnp.dot(tri, mid, preferred_element_type=F32)
                  + jnp.dot(tri, lo, preferred_element_type=F32)) + carry
            cs2 = cs * LOG2E
            rolled = [pltpu.roll(cs2, (128 - LANE_FB - g * heads) % 128, 1) for g in range(HB // heads)]
            out = rolled[0]
            for g in range(1, HB // heads):
                out = jnp.where(grp == g, rolled[g], out)
            cb_ref[rows, :] = out
            return cs[tc - 1:tc, :]

        lax.fori_loop(0, cb_ref.shape[0] // tc, block, jnp.zeros((1, 128), F32))

    def qk(off, slot, key0=None):
        cmax = []
        for n in range(heads):
            cols = slice(n * HEAD_DIM, (n + 1) * HEAD_DIM)
            bias = (cb_ref[pl.ds(q0, 1), :] - cb_ref[pl.ds(off, tk), :])[:, n:n + 1]
            s = lax.dot_general(k_ref[pl.ds(off, tk), cols], q_ref[:, cols], (((1,), (1,)), ((), ())),
                                preferred_element_type=F32) + bias
            if key0 is not None:
                key = key0 + lax.broadcasted_iota(jnp.int32, (tk, 1), 0)
                qry = lax.broadcasted_iota(jnp.int32, (1, tq), 1)
                s = jnp.where(key <= qry, s, NEG)
            lt_scr[n, slot] = s
            cmax.append(jnp.max(s, axis=0, keepdims=True))
        return tuple(cmax)

    def softmax_pv(off, slot, m_old, cmax):
        new = []
        for n in range(heads):
            m_new = jnp.maximum(m_old[n], cmax[n])
            alpha = jnp.exp2(m_old[n] - m_new)
            p = jnp.exp2(lt_scr[n, slot] - m_new).astype(BF16)
            acc_scr[n] = alpha * acc_scr[n] + jnp.dot(vt_ref[n, :, pl.ds(off, tk)], p,
                                                      preferred_element_type=F32)
            new.append(m_new)
        return tuple(new)

    q1 = pl.multiple_of(q0 + tk, tk)
    m = (jnp.full((1, tq), NEG, F32),) * heads
    cmax0 = qk(q0, 0, key0=0)
    cmax1 = qk(q1, 1, key0=tk)
    m = softmax_pv(q0, 0, m, cmax0)
    cmax0 = qk(0, 0)
    m = softmax_pv(q1, 1, m, cmax1)
    last = jnp.maximum(qi - 1, 0) * tq

    def pair(j, carry):
        m, cmax0 = carry
        off0 = pl.multiple_of(j * tq, tq)
        off1 = pl.multiple_of(off0 + tk, tk)
        off2 = pl.multiple_of(jnp.minimum(off0 + tq, last), tq)
        cmax1 = qk(off1, 1)
        m = softmax_pv(off0, 0, m, cmax0)
        cmax0 = qk(off2, 0)
        return softmax_pv(off1, 1, m, cmax1), cmax0

    lax.fori_loop(0, qi, pair, (m, cmax0))
    for n in range(heads):
        out_t = acc_scr[n, 0:HEAD_DIM, :] / acc_scr[n, HEAD_DIM:HEAD_DIM + 1, :]
        for g in range(tq // 128):
            o_ref[g * 128:(g + 1) * 128, n * HEAD_DIM:(n + 1) * HEAD_DIM] = (
                out_t[:, g * 128:(g + 1) * 128].T.astype(BF16))


def _attn_b(proj, vt_b, wlf, b, t_len):
    tq = Q_BLOCK_B
    nq = t_len // tq
    hps = HEADS_PER_STEP_B
    wh = hps * HEAD_DIM
    return pl.pallas_call(
        functools.partial(_attn_b_kernel, tq=tq),
        grid=(b, HB // hps, nq),
        in_specs=[
            pl.BlockSpec((tq, wh), lambda bi, h, i: (bi * nq + i, OFF_QB // wh + h)),
            pl.BlockSpec((t_len, wh), lambda bi, h, i: (bi, OFF_KB // wh + h)),
            pl.BlockSpec((None, hps, VT_ROWS, t_len), lambda bi, h, i: (bi, h, 0, 0)),
            pl.BlockSpec((t_len, 128), lambda bi, h, i: (bi, 0)),
        ],
        out_specs=pl.BlockSpec((tq, wh), lambda bi, h, i: (bi * nq + i, h)),
        out_shape=jax.ShapeDtypeStruct((b * t_len, HB * HEAD_DIM), BF16),
        scratch_shapes=[pltpu.VMEM((hps, 2, tq // 2, tq), F32), pltpu.VMEM((hps, VT_ROWS, tq), F32),
                        pltpu.VMEM((t_len, 128), F32)],
        compiler_params=_cparams(("arbitrary", "arbitrary", "arbitrary")),
        name="attn_b",
    )(proj, proj, vt_b, wlf)


def _merge_kernel(x_ref, oa_ref, ob_ref, ga_ref, gb_ref, woa_ref, wob_ref, wout_ref, gn_ref, o_ref, h_ref):
    ya = jnp.dot(oa_ref[...], woa_ref[...], preferred_element_type=F32)
    yb = jnp.dot(ob_ref[...], wob_ref[...], preferred_element_type=F32)
    mixed = ga_ref[...].astype(F32) * ya + gb_ref[...].astype(F32) * yb
    x1 = x_ref[...] + jnp.dot(mixed.astype(BF16), wout_ref[...], preferred_element_type=F32)
    o_ref[...] = x1
    h_ref[...] = _rms(x1, gn_ref[...]).astype(BF16)


def _merge(x2, out_a, out_b, proj, w_oa, w_ob, w_out, g_next, tm):
    m = x2.shape[0]
    wa = HA * HEAD_DIM
    wb = HB * HEAD_DIM
    return pl.pallas_call(
        _merge_kernel,
        grid=(m // tm,),
        in_specs=[
            pl.BlockSpec((tm, D_MODEL), lambda i: (i, 0)),
            pl.BlockSpec((tm, wa), lambda i: (i, 0)),
            pl.BlockSpec((tm, wb), lambda i: (i, 0)),
            pl.BlockSpec((tm, D_MODEL), lambda i: (i, OFF_GA // D_MODEL)),
            pl.BlockSpec((tm, D_MODEL), lambda i: (i, OFF_GB // D_MODEL)),
            _resident((wa, D_MODEL)),
            _resident((wb, D_MODEL)),
            _resident((D_MODEL, D_MODEL)),
            pl.BlockSpec((1, D_MODEL), lambda i: (0, 0)),
        ],
        out_specs=[pl.BlockSpec((tm, D_MODEL), lambda i: (i, 0)), pl.BlockSpec((tm, D_MODEL), lambda i: (i, 0))],
        out_shape=[jax.ShapeDtypeStruct((m, D_MODEL), F32), jax.ShapeDtypeStruct((m, D_MODEL), BF16)],
        compiler_params=_cparams(("arbitrary",)),
        name="merge",
    )(x2, out_a, out_b, proj, proj, w_oa, w_ob, w_out, g_next)


def _ffn_kernel(h_ref, wg_ref, wu_ref, wd_ref, o_ref):
    @pl.when(pl.program_id(1) == 0)
    def _():
        o_ref[...] = jnp.zeros(o_ref.shape, F32)

    h = h_ref[...]
    gate = jnp.dot(h, wg_ref[...].astype(BF16), preferred_element_type=F32)
    up = jnp.dot(h, wu_ref[...].astype(BF16), preferred_element_type=F32)
    act = (jax.nn.silu(gate) * up).astype(BF16)
    o_ref[...] += jnp.dot(act, wd_ref[...].astype(BF16), preferred_element_type=F32)


def _ffn(h, w_gate, w_up, w_down, tm, tf):
    m = h.shape[0]
    return pl.pallas_call(
        _ffn_kernel,
        grid=(m // tm, D_FF // tf),
        in_specs=[
            pl.BlockSpec((tm, D_MODEL), lambda i, f: (i, 0)),
            pl.BlockSpec((D_MODEL, tf), lambda i, f: (0, f)),
            pl.BlockSpec((D_MODEL, tf), lambda i, f: (0, f)),
            pl.BlockSpec((tf, D_MODEL), lambda i, f: (f, 0)),
        ],
        out_specs=pl.BlockSpec((tm, D_MODEL), lambda i, f: (i, 0)),
        out_shape=jax.ShapeDtypeStruct((m, D_MODEL), F32),
        compiler_params=_cparams(("arbitrary", "arbitrary")),
        name="ffn",
    )(h, w_gate, w_up, w_down)


def _ple_kernel(x_ref, y_ref, p_ref, g_ref, gf_ref, wg_ref, wp_ref, o_ref, *, final_norm):
    x = x_ref[...] + y_ref[...]
    h = _rms(x, g_ref[...]).astype(BF16)
    gate = jax.nn.sigmoid(jnp.dot(h, wg_ref[...], preferred_element_type=F32))
    emb = jnp.dot(p_ref[...].astype(BF16), wp_ref[...], preferred_element_type=F32)
    y = x + gate * emb
    o_ref[...] = _rms(y, gf_ref[...]) if final_norm else y


def _ple(x2, y2, p2, g, g_final, w_gate, w_proj, tm, final_norm):
    m = x2.shape[0]
    return pl.pallas_call(
        functools.partial(_ple_kernel, final_norm=final_norm),
        grid=(m // tm,),
        in_specs=[
            pl.BlockSpec((tm, D_MODEL), lambda i: (i, 0)),
            pl.BlockSpec((tm, D_MODEL), lambda i: (i, 0)),
            pl.BlockSpec((tm, D_PLE), lambda i: (i, 0)),
            pl.BlockSpec((1, D_MODEL), lambda i: (0, 0)),
            pl.BlockSpec((1, D_MODEL), lambda i: (0, 0)),
            _resident((D_MODEL, D_MODEL)),
            _resident((D_PLE, D_MODEL)),
        ],
        out_specs=pl.BlockSpec((tm, D_MODEL), lambda i: (i, 0)),
        out_shape=jax.ShapeDtypeStruct((m, D_MODEL), F32),
        compiler_params=_cparams(("arbitrary",)),
        name="ple",
    )(x2, y2, p2, g, g_final, w_gate, w_proj)


def _rope_lane_freqs():
    def inv(rot):
        half = rot // 2
        return ROPE_THETA ** (-jnp.arange(half, dtype=F32) / half)
    span = 128 // TRIG_PACK
    used = ROT_A // 2 + ROT_IDX // 2
    assert used <= span
    one = jnp.concatenate([inv(ROT_A), inv(ROT_IDX), jnp.zeros((span - used,), F32)])
    return jnp.tile(one, TRIG_PACK)[None, :]


def _w_in_views(w):
    w_t = jnp.swapaxes(w, 0, 1)
    rows = [w_t[_SPLIT_EDGES[k]:_SPLIT_EDGES[k + 1]] for k in (4, 5, 9)]
    pad = jnp.zeros((128 - D_IDX - H_IDX - HB, w_t.shape[1]), w_t.dtype)
    return w_t, jnp.concatenate(rows + [pad], axis=0)


def kernel(x, p, positions, g_mix, w_in, b_f, w_o_a, w_o_b, w_out, g_ffn, w_ffn_gate, w_ffn_up,
           w_ffn_down, g_ple, w_ple_gate, w_ple_proj, g_final):
    b, t_len, d = x.shape
    depth = w_in.shape[0]
    m = b * t_len
    top_k = min(TOPK_MAX, t_len // 4)
    inv_freq = _rope_lane_freqs()
    pos2 = positions.reshape(m, 1)
    x2 = x.reshape(m, d)
    tm_wide, tm = _row_tiles(m, t_len)
    for i in range(depth):
        w_t, w_small = _w_in_views(w_in[i])
        bf_row = jnp.zeros((1, 128), F32).at[0, LANE_FB:LANE_FB + HB].set(b_f[i].astype(F32))
        proj, kie, kio, wlf, vt_a, vt_b = _inproj(x2, pos2, g_mix[i][None, :], w_t, w_small, bf_row,
                                                  inv_freq, tm_wide, b, t_len)
        out_a = _attn_a(proj, vt_a, kie, kio, wlf, b, t_len, top_k)
        out_b = _attn_b(proj, vt_b, wlf, b, t_len)
        x2, h2 = _merge(x2, out_a, out_b, proj, w_o_a[i].astype(BF16), w_o_b[i].astype(BF16),
                        w_out[i].astype(BF16), g_ffn[i][None, :], tm)
        y2 = _ffn(h2, w_ffn_gate[i], w_ffn_up[i], w_ffn_down[i], tm_wide, FFN_COLS)
        x2 = _ple(x2, y2, p[i].reshape(m, D_PLE), g_ple[i][None, :], g_final[None, :],
                  w_ple_gate[i].astype(BF16), w_ple_proj[i].astype(BF16), tm, final_norm=(i + 1 == depth))
    return x2.reshape(b, t_len, d)
```

```python
import functools

import jax
import jax.numpy as jnp
import numpy as np
from jax import lax
from jax.experimental import pallas as pl
from jax.experimental.pallas import tpu as pltpu

F32 = jnp.float32
BF16 = jnp.bfloat16

D_MODEL = 2048
HEAD_DIM = 128
HA = 8
HA_KV = 2
H_IDX = 16
D_IDX = 64
TOPK_MAX = 256
HB = 8
ROPE_THETA = 500000.0
ROT_A = HEAD_DIM // 4
ROT_IDX = D_IDX // 4
Q_BLOCK = 128
D_FF = -(-8 * D_MODEL // (3 * 256)) * 256
D_PLE = 256
EPS = 1e-6

SPLIT_SIZES = (HA * HEAD_DIM, HA_KV * HEAD_DIM, HA_KV * HEAD_DIM, H_IDX * D_IDX, D_IDX, H_IDX,
               HB * HEAD_DIM, HB * HEAD_DIM, HB * HEAD_DIM, HB, D_MODEL, D_MODEL)

OFF_GA = 0
OFF_GB = OFF_GA + D_MODEL
OFF_QA = OFF_GB + D_MODEL
OFF_KA = OFF_QA + HA * HEAD_DIM
OFF_VA = OFF_KA + HA_KV * HEAD_DIM
OFF_QI = OFF_VA + HA_KV * HEAD_DIM
TN_IN = 1024
MXU_N = 256
OFF_QB = -(-(OFF_QI + H_IDX * D_IDX) // TN_IN) * TN_IN
OFF_KB = OFF_QB + HB * HEAD_DIM
OFF_VB = OFF_KB + HB * HEAD_DIM
N_MAIN = OFF_VB + HB * HEAD_DIM
_SPLIT_EDGES = [0] + [int(s) for s in np.cumsum(SPLIT_SIZES)]
_PACK_RUNS = ((OFF_GA, _SPLIT_EDGES[10], 2 * D_MODEL),
              (OFF_QA, _SPLIT_EDGES[0], OFF_QI + H_IDX * D_IDX - OFF_QA),
              (OFF_QB, _SPLIT_EDGES[6], N_MAIN - OFF_QB))
_SEGMENTS = ((OFF_GA, OFF_QA, "gate"), (OFF_QA, OFF_KA, "q_a"), (OFF_KA, OFF_VA, "k_a"),
             (OFF_VA, OFF_QI, "v_a"), (OFF_QI, OFF_QI + H_IDX * D_IDX, "q_i"),
             (OFF_QB, OFF_KB, "q_b"), (OFF_KB, OFF_VB, "k_b"), (OFF_VB, N_MAIN, "v_b"))


def _tile_patterns():
    def group(col):
        for lo, hi, kind in _SEGMENTS:
            if lo <= col < hi:
                assert (col - lo) % MXU_N == 0 and col + MXU_N <= hi
                return kind, (col - lo) // MXU_N
        return None
    by_pattern = {}
    for tile in range(N_MAIN // TN_IN):
        pattern = tuple(group(tile * TN_IN + g * MXU_N) for g in range(TN_IN // MXU_N))
        pattern = tuple(None if grp is None else (grp[0], grp[1] if grp[0] in ("v_a", "v_b") else 0)
                        for grp in pattern)
        by_pattern.setdefault(pattern, []).append(tile)
    return tuple(by_pattern.items())


_TILE_PATTERNS = _tile_patterns()
LANE_WI = D_IDX
LANE_FB = D_IDX + H_IDX

NEG = -1e30
VMEM_LIMIT = 60 * 1024 * 1024
LOG2E = 1.4426950408889634
ATTN_SCALE = HEAD_DIM ** -0.5 * LOG2E
VT_ROWS = HEAD_DIM + 16
TRIG_PACK = 4
HEADS_PER_STEP_B = 4
KEY_CHUNK_A = 512
Q_BLOCK_B = 512
CUMSUM_ROWS = 256
BISECT_FIRST = 16
BISECT_ROUND = 2
BISECT_MAX_ITERS = 48


def _row_tiles(m, t_len):
    return min(1024, m, t_len), min(512, m)


FFN_COLS = 512


def _cparams(sem):
    return pltpu.CompilerParams(dimension_semantics=sem, vmem_limit_bytes=VMEM_LIMIT)


def _rms(x, g):
    return x * lax.rsqrt(jnp.mean(x * x, axis=-1, keepdims=True) + EPS) * g


def _resident(shape):
    nd = len(shape)
    return pl.BlockSpec(shape, lambda *_: (0,) * nd, pipeline_mode=pl.Buffered(1))


def _rope128(a, c, sa, sb, shift):
    outs = []
    for q in range(a.shape[1] // 128):
        aq = a[:, q * 128:(q + 1) * 128]
        outs.append(aq * c + pltpu.roll(aq, 128 - shift, 1) * sa + pltpu.roll(aq, shift, 1) * sb)
    return outs[0] if len(outs) == 1 else jnp.concatenate(outs, axis=1)


def _inproj_kernel(x_ref, pos_ref, g_ref, w_ref, ws_ref, bf_ref, inv_ref,
                   o_ref, kie_ref, kio_ref, wlf_ref, vta_ref, vtb_ref,
                   h_scr, ca_scr, saa_scr, sab_scr, ci_scr, sia_scr, sib_scr):
    j = pl.program_id(1)
    tm = x_ref.shape[0]

    @pl.when(j == 0)
    def _():
        hb = _rms(x_ref[...], g_ref[...]).astype(BF16)
        h_scr[...] = hb
        pos = pos_ref[...].astype(F32)
        lane = lax.broadcasted_iota(jnp.int32, (tm, 128), 1)
        ha, hi = ROT_A // 2, ROT_IDX // 2
        span, rq = 128 // TRIG_PACK, tm // TRIG_PACK
        lane_q = lax.broadcasted_iota(jnp.int32, (rq, 128), 1)
        pos_p = jnp.zeros((rq, 128), F32)
        for q in range(TRIG_PACK):
            pos_p = jnp.where((lane_q >= q * span) & (lane_q < (q + 1) * span), pos[q * rq:(q + 1) * rq, :], pos_p)
        ang = pos_p * inv_ref[...]
        cs_p = jnp.cos(ang)
        sn_p = jnp.sin(ang)
        cs = jnp.concatenate([pltpu.roll(cs_p, (128 - q * span) % 128, 1) for q in range(TRIG_PACK)], axis=0)
        sn = jnp.concatenate([pltpu.roll(sn_p, (128 - q * span) % 128, 1) for q in range(TRIG_PACK)], axis=0)

        def at(tab, src, dst):
            return pltpu.roll(tab, (dst - src) % 128, 1)

        def lanes(lo, n):
            return (lane >= lo) & (lane < lo + n)

        ca_scr[...] = jnp.where(lanes(0, ha), cs, jnp.where(lanes(ha, ha), at(cs, 0, ha), 1.0))
        saa_scr[...] = jnp.where(lanes(0, ha), -sn, 0.0)
        sab_scr[...] = jnp.where(lanes(ha, ha), at(sn, 0, ha), 0.0)
        ci = jnp.ones((tm, 128), F32)
        sia = jnp.zeros((tm, 128), F32)
        sib = jnp.zeros((tm, 128), F32)
        for head0 in (0, D_IDX):
            ci = jnp.where(lanes(head0, hi), at(cs, ha, head0), ci)
            ci = jnp.where(lanes(head0 + hi, hi), at(cs, ha, head0 + hi), ci)
            sia = jnp.where(lanes(head0, hi), -at(sn, ha, head0), sia)
            sib = jnp.where(lanes(head0 + hi, hi), at(sn, ha, head0 + hi), sib)
        ci_scr[...] = ci
        sia_scr[...] = sia
        sib_scr[...] = sib
        small = lax.dot_general(hb, ws_ref[...].astype(BF16), (((1,), (1,)), ((), ())),
                                preferred_element_type=F32)
        kr = _rope128(small, ci, sia, sib, ROT_IDX // 2)
        ke = jnp.where(lane < D_IDX, kr, 0.0)
        kie_ref[...] = ke.astype(BF16)
        kio_ref[...] = pltpu.roll(ke, D_IDX, 1).astype(BF16)
        is_w = (lane >= LANE_WI) & (lane < LANE_WI + H_IDX)
        is_f = (lane >= LANE_FB) & (lane < LANE_FB + HB)
        logf = jax.nn.log_sigmoid(small + bf_ref[...])
        wlf_ref[...] = jnp.where(is_w, small * (H_IDX ** -0.5 * D_IDX ** -0.5), jnp.where(is_f, logf, 0.0))

    t = TN_IN
    groups = t // MXU_N
    heads_per_group = MXU_N // HEAD_DIM

    def project(epilogues):
        for c, epilogue in enumerate(epilogues):
            cols = slice(c * MXU_N, (c + 1) * MXU_N)
            if epilogue is None:
                o_ref[:, cols] = jnp.zeros((tm, MXU_N), BF16)
                continue
            w_t = w_ref[cols, :].astype(BF16)
            acc = lax.dot_general(h_scr[...], w_t, (((1,), (1,)), ((), ())), preferred_element_type=F32)
            o_ref[:, cols] = epilogue(acc).astype(BF16)

    def rope_a(acc):
        return _rope128(acc, ca_scr[...], saa_scr[...], sab_scr[...], ROT_A // 2)

    def rope_i(acc):
        return _rope128(acc, ci_scr[...], sia_scr[...], sib_scr[...], ROT_IDX // 2)

    def values(vt_ref, c):
        def epilogue(acc):
            acc_t = acc.T
            for hh in range(heads_per_group):
                h = (c * heads_per_group + hh) % vt_ref.shape[0]
                vt_ref[h, 0:HEAD_DIM, :] = acc_t[hh * HEAD_DIM:(hh + 1) * HEAD_DIM, :].astype(BF16)
                vt_ref[h, HEAD_DIM:VT_ROWS, :] = jnp.ones((VT_ROWS - HEAD_DIM, tm), BF16)
            return acc
        return epilogue

    def epilogue_of(kind, c):
        return {"gate": lambda: jax.nn.sigmoid,
                "q_a": lambda: (lambda acc: rope_a(acc) * ATTN_SCALE),
                "k_a": lambda: rope_a,
                "v_a": lambda: values(vta_ref, c),
                "q_i": lambda: rope_i,
                "q_b": lambda: (lambda acc: acc * ATTN_SCALE),
                "k_b": lambda: (lambda acc: acc),
                "v_b": lambda: values(vtb_ref, c)}[kind]()

    for pattern, tiles in _TILE_PATTERNS:
        pred = functools.reduce(lambda a, b: a | b, [j == tile for tile in tiles])

        @pl.when(pred)
        def _(pattern=pattern):
            project([None if grp is None else epilogue_of(*grp) for grp in pattern])


def _w_in_row(j):
    row = jnp.int32(0)
    for out0, src0, _ in _PACK_RUNS:
        row = jnp.where(j >= out0 // TN_IN, src0 + TN_IN * (j - out0 // TN_IN), row)
    return pl.multiple_of(row, 8)


def _inproj(x2, pos2, g, w_t, w_small, bf_row, inv_freq, tm, b, t_len):
    m = x2.shape[0]
    for out0, src0, width in _PACK_RUNS:
        assert out0 % TN_IN == 0 and src0 % 8 == 0
        assert src0 + -(-width // TN_IN) * TN_IN <= w_t.shape[0]
    assert list(_PACK_RUNS) == sorted(_PACK_RUNS)
    grid = (m // tm, N_MAIN // TN_IN)
    nt = t_len // tm
    j_vb = OFF_VB // TN_IN
    heads_per_tile = min(TN_IN // HEAD_DIM, HB)
    vb_tiles = HB // heads_per_tile
    row128 = pl.BlockSpec((1, 128), lambda i, j: (0, 0))
    return pl.pallas_call(
        _inproj_kernel,
        grid=grid,
        in_specs=[
            pl.BlockSpec((tm, D_MODEL), lambda i, j: (i, 0)),
            pl.BlockSpec((tm, 1), lambda i, j: (i, 0)),
            pl.BlockSpec((1, D_MODEL), lambda i, j: (0, 0)),
            pl.BlockSpec((pl.Element(TN_IN), pl.Element(D_MODEL)), lambda i, j: (_w_in_row(j), 0)),
            pl.BlockSpec((128, D_MODEL), lambda i, j: (0, 0)),
            row128, row128,
        ],
        out_specs=[
            pl.BlockSpec((tm, TN_IN), lambda i, j: (i, j)),
            pl.BlockSpec((tm, 128), lambda i, j: (i, 0)),
            pl.BlockSpec((tm, 128), lambda i, j: (i, 0)),
            pl.BlockSpec((tm, 128), lambda i, j: (i, 0)),
            pl.BlockSpec((None, HA_KV, VT_ROWS, tm), lambda i, j: (i // nt, 0, 0, i % nt)),
            pl.BlockSpec((None, heads_per_tile, VT_ROWS, tm),
                         lambda i, j: (i // nt, jnp.clip(j - j_vb, 0, vb_tiles - 1), 0, i % nt)),
        ],
        out_shape=[
            jax.ShapeDtypeStruct((m, N_MAIN), BF16),
            jax.ShapeDtypeStruct((m, 128), BF16),
            jax.ShapeDtypeStruct((m, 128), BF16),
            jax.ShapeDtypeStruct((m, 128), F32),
            jax.ShapeDtypeStruct((b, HA_KV, VT_ROWS, t_len), BF16),
            jax.ShapeDtypeStruct((b, HB, VT_ROWS, t_len), BF16),
        ],
        scratch_shapes=[pltpu.VMEM((tm, D_MODEL), BF16)] + [pltpu.VMEM((tm, 128), F32)] * 6,
        compiler_params=_cparams(("arbitrary", "arbitrary")),
        name="inproj",
    )(x2, pos2, g, w_t, w_small, bf_row, inv_freq)


def _attn_a_kernel(qa_ref, qi0_ref, qi1_ref, ka_ref, vt_ref, kie_ref, kio_ref, wlf_ref, o_ref,
                   sc_scr, qs_scr, qg_scr, lt_scr, acc_scr, *, tk, top_k, n_iter):
    i = pl.program_id(1)
    nq = Q_BLOCK
    group = HA // HA_KV
    gw = group * nq
    nchunks = ((i + 1) * nq + tk - 1) // tk
    q_pos = i * nq + lax.broadcasted_iota(jnp.int32, (1, nq), 1)
    wt = wlf_ref[...].T
    half = H_IDX // 4
    for p in range(H_IDX // 2):
        src = qi0_ref if p < half else qi1_ref
        qs_scr[p * nq:(p + 1) * nq, :] = src[:, (p % half) * 128:(p % half + 1) * 128]

    def score_chunk(c, carry):
        rmin, rmax = carry
        off = pl.multiple_of(c * tk, tk)
        ke = kie_ref[pl.ds(off, tk), :]
        ko = kio_ref[pl.ds(off, tk), :]
        s = jnp.zeros((tk, nq), F32)
        pairs_per_dot = MXU_N // nq
        for pg in range(H_IDX // 2 // pairs_per_dot):
            qs = qs_scr[pg * MXU_N:(pg + 1) * MXU_N, :]
            de = lax.dot_general(ke, qs, (((1,), (1,)), ((), ())), preferred_element_type=F32)
            do = lax.dot_general(ko, qs, (((1,), (1,)), ((), ())), preferred_element_type=F32)
            for pp in range(pairs_per_dot):
                p = pg * pairs_per_dot + pp
                cols = slice(pp * nq, (pp + 1) * nq)
                s = s + jnp.maximum(de[:, cols], 0.0) * wt[LANE_WI + 2 * p:LANE_WI + 2 * p + 1, :]
                s = s + jnp.maximum(do[:, cols], 0.0) * wt[LANE_WI + 2 * p + 1:LANE_WI + 2 * p + 2, :]
        kpos = off + lax.broadcasted_iota(jnp.int32, (tk, 1), 0)
        valid = kpos <= q_pos
        sc_scr[pl.ds(off, tk), :] = jnp.where(valid, s, NEG)
        rmin = jnp.minimum(rmin, jnp.min(jnp.where(valid, s, -NEG), axis=0, keepdims=True))
        rmax = jnp.maximum(rmax, jnp.max(jnp.where(valid, s, NEG), axis=0, keepdims=True))
        return rmin, rmax

    lo, hi = lax.fori_loop(0, nchunks, score_chunk,
                           (jnp.full((1, nq), -NEG, F32), jnp.full((1, nq), NEG, F32)))

    def bisect(_, carry):
        lo, hi = carry
        mid = 0.5 * (lo + hi)

        def count_chunk(c, part):
            off = pl.multiple_of(c * tk, tk)
            ind = jnp.where(sc_scr[pl.ds(off, tk), :] >= mid, 1.0, 0.0)
            return part + jnp.sum(ind.reshape(tk // 64, 64, nq), axis=0)

        part = lax.fori_loop(0, nchunks, count_chunk, jnp.zeros((64, nq), F32))
        cnt = jnp.sum(part, axis=0, keepdims=True)
        ge = cnt >= top_k
        return jnp.where(ge, mid, lo), jnp.where(cnt == top_k, mid, jnp.where(ge, hi, mid))

    searching = q_pos >= top_k

    def bisect_round(state):
        it, lo, hi, _ = state
        lo, hi = lax.fori_loop(0, BISECT_ROUND, bisect, (lo, hi))
        pending = jnp.sum(jnp.where(searching & (lo < hi), 1.0, 0.0))
        return it + BISECT_ROUND, lo, hi, pending

    lo, hi = lax.fori_loop(0, BISECT_FIRST, bisect, (lo, hi))
    _, thr, _, _ = lax.while_loop(lambda st: (st[0] < n_iter) & (st[3] > 0.0), bisect_round,
                                  (jnp.int32(BISECT_FIRST), lo, hi, jnp.float32(1.0)))

    tka = tk // 2
    last = (nchunks - 1) * tk
    for n in range(HA_KV):
        for g in range(group):
            h = n * group + g
            qg_scr[n, g * nq:(g + 1) * nq, :] = qa_ref[:, h * HEAD_DIM:(h + 1) * HEAD_DIM]
    acc_scr[...] = jnp.zeros(acc_scr.shape, F32)

    def qk(off, slot):
        msk = jnp.where(sc_scr[pl.ds(off, tka), :] >= thr, 0.0, NEG)
        msk = jnp.concatenate([msk] * group, axis=1)
        cmax = []
        for n in range(HA_KV):
            k = ka_ref[pl.ds(off, tka), n * HEAD_DIM:(n + 1) * HEAD_DIM]
            s = lax.dot_general(k, qg_scr[n], (((1,), (1,)), ((), ())), preferred_element_type=F32) + msk
            lt_scr[n, slot] = s
            cmax.append(jnp.max(s, axis=0, keepdims=True))
        return tuple(cmax)

    def softmax_pv(off, slot, m_old, cmax):
        new = []
        for n in range(HA_KV):
            m_new = jnp.maximum(m_old[n], cmax[n])
            alpha = jnp.exp2(m_old[n] - m_new)
            p = jnp.exp2(lt_scr[n, slot] - m_new).astype(BF16)
            acc_scr[n] = alpha * acc_scr[n] + jnp.dot(vt_ref[n, :, pl.ds(off, tka)], p,
                                                      preferred_element_type=F32)
            new.append(m_new)
        return tuple(new)

    def pair(j, carry):
        m, cmax0 = carry
        off0 = pl.multiple_of(j * tk, tk)
        off1 = pl.multiple_of(off0 + tka, tka)
        off2 = pl.multiple_of(jnp.minimum(off0 + tk, last), tk)
        cmax1 = qk(off1, 1)
        m = softmax_pv(off0, 0, m, cmax0)
        cmax0 = qk(off2, 0)
        return softmax_pv(off1, 1, m, cmax1), cmax0

    lax.fori_loop(0, nchunks, pair, ((jnp.full((1, gw), NEG, F32),) * HA_KV, qk(0, 0)))
    for n in range(HA_KV):
        out_t = acc_scr[n, 0:HEAD_DIM, :] / acc_scr[n, HEAD_DIM:HEAD_DIM + 1, :]
        for g in range(group):
            h = n * group + g
            o_ref[:, h * HEAD_DIM:(h + 1) * HEAD_DIM] = out_t[:, g * nq:(g + 1) * nq].T.astype(BF16)


def _attn_a(proj, vt_a, kie, kio, wlf, b, t_len, top_k):
    nb = t_len // Q_BLOCK
    wa = HA * HEAD_DIM
    wkv = HA_KV * HEAD_DIM
    wqi = H_IDX * D_IDX // 2
    tk = min(KEY_CHUNK_A, t_len)
    gw = HA // HA_KV * Q_BLOCK
    kern = functools.partial(_attn_a_kernel, tk=tk, top_k=top_k, n_iter=BISECT_MAX_ITERS)
    return pl.pallas_call(
        kern,
        grid=(b, nb),
        in_specs=[
            pl.BlockSpec((Q_BLOCK, wa), lambda bi, i: (bi * nb + i, OFF_QA // wa)),
            pl.BlockSpec((Q_BLOCK, wqi), lambda bi, i: (bi * nb + i, OFF_QI // wqi)),
            pl.BlockSpec((Q_BLOCK, wqi), lambda bi, i: (bi * nb + i, OFF_QI // wqi + 1)),
            pl.BlockSpec((t_len, wkv), lambda bi, i: (bi, OFF_KA // wkv)),
            pl.BlockSpec((None, HA_KV, VT_ROWS, t_len), lambda bi, i: (bi, 0, 0, 0)),
            pl.BlockSpec((t_len, 128), lambda bi, i: (bi, 0)),
            pl.BlockSpec((t_len, 128), lambda bi, i: (bi, 0)),
            pl.BlockSpec((Q_BLOCK, 128), lambda bi, i: (bi * nb + i, 0)),
        ],
        out_specs=pl.BlockSpec((Q_BLOCK, wa), lambda bi, i: (bi * nb + i, 0)),
        out_shape=jax.ShapeDtypeStruct((b * t_len, wa), BF16),
        scratch_shapes=[
            pltpu.VMEM((t_len, Q_BLOCK), F32),
            pltpu.VMEM((H_IDX // 2 * Q_BLOCK, 128), BF16),
            pltpu.VMEM((HA_KV, gw, HEAD_DIM), BF16),
            pltpu.VMEM((HA_KV, 2, tk // 2, gw), F32),
            pltpu.VMEM((HA_KV, VT_ROWS, gw), F32),
        ],
        compiler_params=_cparams(("arbitrary", "arbitrary")),
        name="attn_a",
    )(proj, proj, proj, proj, vt_a, kie, kio, wlf)


def _attn_b_kernel(q_ref, k_ref, vt_ref, wlf_ref, o_ref, lt_scr, acc_scr, cb_ref, *, tq):
    qi = pl.program_id(2)
    tk = tq // 2
    heads = HEADS_PER_STEP_B
    q0 = pl.multiple_of(qi * tq, tq)
    acc_scr[...] = jnp.zeros(acc_scr.shape, F32)

    grp = pl.program_id(1)

    @pl.when((qi == 0) & (grp == 0))
    def _():
        tc = CUMSUM_ROWS
        r = lax.broadcasted_iota(jnp.int32, (tc, tc), 0)
        c = lax.broadcasted_iota(jnp.int32, (tc, tc), 1)
        tri = jnp.where(c <= r, 1.0, 0.0).astype(BF16)

        def block(t, carry):
            rows = pl.ds(pl.multiple_of(t * tc, tc), tc)
            lf = wlf_ref[rows, :]
            hi = lf.astype(BF16)
            r1 = lf - hi.astype(F32)
            mid = r1.astype(BF16)
            lo = (r1 - mid.astype(F32)).astype(BF16)
            cs = (jnp.dot(tri, hi, preferred_element_type=F32) + jnp.dot(tri, mid, preferred_element_type=F32)
                  + jnp.dot(tri, lo, preferred_element_type=F32)) + carry
            cs2 = cs * LOG2E
            for g in range(HB // heads):
                cb_ref[g, rows, :] = pltpu.roll(cs2, (128 - LANE_FB - g * heads) % 128, 1)
            return cs[tc - 1:tc, :]

        lax.fori_loop(0, cb_ref.shape[1] // tc, block, jnp.zeros((1, 128), F32))

    def qk(off, slot, key0=None):
        cmax = []
        for n in range(heads):
            cols = slice(n * HEAD_DIM, (n + 1) * HEAD_DIM)
            bias = (cb_ref[grp, pl.ds(q0, 1), :] - cb_ref[grp, pl.ds(off, tk), :])[:, n:n + 1]
            s = lax.dot_general(k_ref[pl.ds(off, tk), cols], q_ref[:, cols], (((1,), (1,)), ((), ())),
                                preferred_element_type=F32) + bias
            if key0 is not None:
                key = key0 + lax.broadcasted_iota(jnp.int32, (tk, 1), 0)
                qry = lax.broadcasted_iota(jnp.int32, (1, tq), 1)
                s = jnp.where(key <= qry, s, NEG)
            lt_scr[n, slot] = s
            cmax.append(jnp.max(s, axis=0, keepdims=True))
        return tuple(cmax)

    def softmax_pv(off, slot, m_old, cmax):
        new = []
        for n in range(heads):
            m_new = jnp.maximum(m_old[n], cmax[n])
            alpha = jnp.exp2(m_old[n] - m_new)
            p = jnp.exp2(lt_scr[n, slot] - m_new).astype(BF16)
            acc_scr[n] = alpha * acc_scr[n] + jnp.dot(vt_ref[n, :, pl.ds(off, tk)], p,
                                                      preferred_element_type=F32)
            new.append(m_new)
        return tuple(new)

    q1 = pl.multiple_of(q0 + tk, tk)
    m = (jnp.full((1, tq), NEG, F32),) * heads
    cmax0 = qk(q0, 0, key0=0)
    cmax1 = qk(q1, 1, key0=tk)
    m = softmax_pv(q0, 0, m, cmax0)
    cmax0 = qk(0, 0)
    m = softmax_pv(q1, 1, m, cmax1)
    last = jnp.maximum(qi - 1, 0) * tq

    def pair(j, carry):
        m, cmax0 = carry
        off0 = pl.multiple_of(j * tq, tq)
        off1 = pl.multiple_of(off0 + tk, tk)
        off2 = pl.multiple_of(jnp.minimum(off0 + tq, last), tq)
        cmax1 = qk(off1, 1)
        m = softmax_pv(off0, 0, m, cmax0)
        cmax0 = qk(off2, 0)
        return softmax_pv(off1, 1, m, cmax1), cmax0

    lax.fori_loop(0, qi, pair, (m, cmax0))
    for n in range(heads):
        out_t = acc_scr[n, 0:HEAD_DIM, :] / acc_scr[n, HEAD_DIM:HEAD_DIM + 1, :]
        for g in range(tq // 128):
            o_ref[g * 128:(g + 1) * 128, n * HEAD_DIM:(n + 1) * HEAD_DIM] = (
                out_t[:, g * 128:(g + 1) * 128].T.astype(BF16))


def _attn_b(proj, vt_b, wlf, b, t_len):
    tq = Q_BLOCK_B
    nq = t_len // tq
    hps = HEADS_PER_STEP_B
    wh = hps * HEAD_DIM
    return pl.pallas_call(
        functools.partial(_attn_b_kernel, tq=tq),
        grid=(b, HB // hps, nq),
        in_specs=[
            pl.BlockSpec((tq, wh), lambda bi, h, i: (bi * nq + i, OFF_QB // wh + h)),
            pl.BlockSpec((t_len, wh), lambda bi, h, i: (bi, OFF_KB // wh + h)),
            pl.BlockSpec((None, hps, VT_ROWS, t_len), lambda bi, h, i: (bi, h, 0, 0)),
            pl.BlockSpec((t_len, 128), lambda bi, h, i: (bi, 0)),
        ],
        out_specs=pl.BlockSpec((tq, wh), lambda bi, h, i: (bi * nq + i, h)),
        out_shape=jax.ShapeDtypeStruct((b * t_len, HB * HEAD_DIM), BF16),
        scratch_shapes=[pltpu.VMEM((hps, 2, tq // 2, tq), F32), pltpu.VMEM((hps, VT_ROWS, tq), F32),
                        pltpu.VMEM((HB // hps, t_len, 128), F32)],
        compiler_params=_cparams(("arbitrary", "arbitrary", "arbitrary")),
        name="attn_b",
    )(proj, proj, vt_b, wlf)


def _merge_kernel(x_ref, oa_ref, ob_ref, ga_ref, gb_ref, woa_ref, wob_ref, wout_ref, gn_ref, o_ref, h_ref):
    ya = jnp.dot(oa_ref[...], woa_ref[...], preferred_element_type=F32)
    yb = jnp.dot(ob_ref[...], wob_ref[...], preferred_element_type=F32)
    mixed = ga_ref[...].astype(F32) * ya + gb_ref[...].astype(F32) * yb
    x1 = x_ref[...] + jnp.dot(mixed.astype(BF16), wout_ref[...], preferred_element_type=F32)
    o_ref[...] = x1
    h_ref[...] = _rms(x1, gn_ref[...]).astype(BF16)


def _merge(x2, out_a, out_b, proj, w_oa, w_ob, w_out, g_next, tm):
    m = x2.shape[0]
    wa = HA * HEAD_DIM
    wb = HB * HEAD_DIM
    return pl.pallas_call(
        _merge_kernel,
        grid=(m // tm,),
        in_specs=[
            pl.BlockSpec((tm, D_MODEL), lambda i: (i, 0)),
            pl.BlockSpec((tm, wa), lambda i: (i, 0)),
            pl.BlockSpec((tm, wb), lambda i: (i, 0)),
            pl.BlockSpec((tm, D_MODEL), lambda i: (i, OFF_GA // D_MODEL)),
            pl.BlockSpec((tm, D_MODEL), lambda i: (i, OFF_GB // D_MODEL)),
            _resident((wa, D_MODEL)),
            _resident((wb, D_MODEL)),
            _resident((D_MODEL, D_MODEL)),
            pl.BlockSpec((1, D_MODEL), lambda i: (0, 0)),
        ],
        out_specs=[pl.BlockSpec((tm, D_MODEL), lambda i: (i, 0)), pl.BlockSpec((tm, D_MODEL), lambda i: (i, 0))],
        out_shape=[jax.ShapeDtypeStruct((m, D_MODEL), F32), jax.ShapeDtypeStruct((m, D_MODEL), BF16)],
        compiler_params=_cparams(("arbitrary",)),
        name="merge",
    )(x2, out_a, out_b, proj, proj, w_oa, w_ob, w_out, g_next)


def _ffn_kernel(h_ref, wg_ref, wu_ref, wd_ref, o_ref):
    @pl.when(pl.program_id(1) == 0)
    def _():
        o_ref[...] = jnp.zeros(o_ref.shape, F32)

    h = h_ref[...]
    gate = jnp.dot(h, wg_ref[...].astype(BF16), preferred_element_type=F32)
    up = jnp.dot(h, wu_ref[...].astype(BF16), preferred_element_type=F32)
    act = (jax.nn.silu(gate) * up).astype(BF16)
    o_ref[...] += jnp.dot(act, wd_ref[...].astype(BF16), preferred_element_type=F32)


def _ffn(h, w_gate, w_up, w_down, tm, tf):
    m = h.shape[0]
    return pl.pallas_call(
        _ffn_kernel,
        grid=(m // tm, D_FF // tf),
        in_specs=[
            pl.BlockSpec((tm, D_MODEL), lambda i, f: (i, 0)),
            pl.BlockSpec((D_MODEL, tf), lambda i, f: (0, f)),
            pl.BlockSpec((D_MODEL, tf), lambda i, f: (0, f)),
            pl.BlockSpec((tf, D_MODEL), lambda i, f: (f, 0)),
        ],
        out_specs=pl.BlockSpec((tm, D_MODEL), lambda i, f: (i, 0)),
        out_shape=jax.ShapeDtypeStruct((m, D_MODEL), F32),
        compiler_params=_cparams(("arbitrary", "arbitrary")),
        name="ffn",
    )(h, w_gate, w_up, w_down)


def _ple_kernel(x_ref, y_ref, p_ref, g_ref, gf_ref, wg_ref, wp_ref, o_ref, *, final_norm):
    x = x_ref[...] + y_ref[...]
    h = _rms(x, g_ref[...]).astype(BF16)
    gate = jax.nn.sigmoid(jnp.dot(h, wg_ref[...], preferred_element_type=F32))
    emb = jnp.dot(p_ref[...].astype(BF16), wp_ref[...], preferred_element_type=F32)
    y = x + gate * emb
    o_ref[...] = _rms(y, gf_ref[...]) if final_norm else y


def _ple(x2, y2, p2, g, g_final, w_gate, w_proj, tm, final_norm):
    m = x2.shape[0]
    return pl.pallas_call(
        functools.partial(_ple_kernel, final_norm=final_norm),
        grid=(m // tm,),
        in_specs=[
            pl.BlockSpec((tm, D_MODEL), lambda i: (i, 0)),
            pl.BlockSpec((tm, D_MODEL), lambda i: (i, 0)),
            pl.BlockSpec((tm, D_PLE), lambda i: (i, 0)),
            pl.BlockSpec((1, D_MODEL), lambda i: (0, 0)),
            pl.BlockSpec((1, D_MODEL), lambda i: (0, 0)),
            _resident((D_MODEL, D_MODEL)),
            _resident((D_PLE, D_MODEL)),
        ],
        out_specs=pl.BlockSpec((tm, D_MODEL), lambda i: (i, 0)),
        out_shape=jax.ShapeDtypeStruct((m, D_MODEL), F32),
        compiler_params=_cparams(("arbitrary",)),
        name="ple",
    )(x2, y2, p2, g, g_final, w_gate, w_proj)


def _rope_lane_freqs():
    def inv(rot):
        half = rot // 2
        return ROPE_THETA ** (-jnp.arange(half, dtype=F32) / half)
    span = 128 // TRIG_PACK
    used = ROT_A // 2 + ROT_IDX // 2
    assert used <= span
    one = jnp.concatenate([inv(ROT_A), inv(ROT_IDX), jnp.zeros((span - used,), F32)])
    return jnp.tile(one, TRIG_PACK)[None, :]


def _w_in_views(w):
    w_t = jnp.swapaxes(w, 0, 1)
    rows = [w_t[_SPLIT_EDGES[k]:_SPLIT_EDGES[k + 1]] for k in (4, 5, 9)]
    pad = jnp.zeros((128 - D_IDX - H_IDX - HB, w_t.shape[1]), w_t.dtype)
    return w_t, jnp.concatenate(rows + [pad], axis=0)


def kernel(x, p, positions, g_mix, w_in, b_f, w_o_a, w_o_b, w_out, g_ffn, w_ffn_gate, w_ffn_up,
           w_ffn_down, g_ple, w_ple_gate, w_ple_proj, g_final):
    b, t_len, d = x.shape
    depth = w_in.shape[0]
    m = b * t_len
    top_k = min(TOPK_MAX, t_len // 4)
    inv_freq = _rope_lane_freqs()
    pos2 = positions.reshape(m, 1)
    x2 = x.reshape(m, d)
    tm_wide, tm = _row_tiles(m, t_len)
    for i in range(depth):
        w_t, w_small = _w_in_views(w_in[i])
        bf_row = jnp.zeros((1, 128), F32).at[0, LANE_FB:LANE_FB + HB].set(b_f[i].astype(F32))
        proj, kie, kio, wlf, vt_a, vt_b = _inproj(x2, pos2, g_mix[i][None, :], w_t, w_small, bf_row,
                                                  inv_freq, tm_wide, b, t_len)
        out_a = _attn_a(proj, vt_a, kie, kio, wlf, b, t_len, top_k)
        out_b = _attn_b(proj, vt_b, wlf, b, t_len)
        x2, h2 = _merge(x2, out_a, out_b, proj, w_o_a[i].astype(BF16), w_o_b[i].astype(BF16),
                        w_out[i].astype(BF16), g_ffn[i][None, :], tm)
        y2 = _ffn(h2, w_ffn_gate[i], w_ffn_up[i], w_ffn_down[i], tm_wide, FFN_COLS)
        x2 = _ple(x2, y2, p[i].reshape(m, D_PLE), g_ple[i][None, :], g_final[None, :],
                  w_ple_gate[i].astype(BF16), w_ple_proj[i].astype(BF16), tm, final_norm=(i + 1 == depth))
    return x2.reshape(b, t_len, d)
```
